```python
import jax, jax.numpy as jnp
from jax import lax
import numpy as np

D_MODEL = 1024
BATCH = 8
SEQ = 8192
DEPTH = 2

MIX_WIDTH = D_MODEL
WIDTH_A = MIX_WIDTH // 2
WIDTH_B = MIX_WIDTH - WIDTH_A
HEADS_A = 4
HEAD_DIM_A = WIDTH_A // HEADS_A
GROUPS_B = 4
CHUNK = 128
CONV_K = 3
PLE_DIM = 256
EPS = 1e-6
SPLITS = [WIDTH_A, WIDTH_A, WIDTH_A, WIDTH_B, WIDTH_B, WIDTH_B, WIDTH_B]
PROJ_WIDTH = sum(SPLITS)

kernel_name = "hybrid_sgu_shortconv_ple_trunk"


def rmsnorm(x, g):
    xf = x.astype(jnp.float32)
    y = xf * lax.rsqrt(jnp.mean(xf * xf, axis=-1, keepdims=True) + EPS)
    return (y * g.astype(jnp.float32)).astype(x.dtype)


def layernorm(x, g, b):
    xf = x.astype(jnp.float32)
    mu = jnp.mean(xf, axis=-1, keepdims=True)
    xc = xf - mu
    var = jnp.mean(xc * xc, axis=-1, keepdims=True)
    y = xc * lax.rsqrt(var + EPS)
    return (y * g.astype(jnp.float32) + b.astype(jnp.float32)).astype(x.dtype)


def spatial_gating(u, v, ln_g, ln_b, w_s, b_s):
    bsz, s_len, _ = v.shape
    n_chunks = s_len // CHUNK
    v = layernorm(v, ln_g, ln_b)
    vh = v.reshape(bsz, n_chunks, CHUNK, HEADS_A, HEAD_DIM_A)
    mask = jnp.tril(jnp.ones((CHUNK, CHUNK), dtype=w_s.dtype))
    ws = w_s * mask[None]
    mixed = jnp.einsum('hts,bnshd->bnthd', ws, vh)
    mixed = mixed + jnp.transpose(b_s)[None, None, :, :, None]
    return u * mixed.reshape(bsz, s_len, WIDTH_A)


def gated_short_conv(h, gate_b, gate_c, conv_w):
    s_len = h.shape[1]
    xc = gate_c * h
    xp = jnp.pad(xc, ((0, 0), (CONV_K - 1, 0), (0, 0)))
    y = xp[:, 0:s_len] * conv_w[:, 0]
    for k in range(1, CONV_K):
        y = y + xp[:, k:k + s_len] * conv_w[:, k]
    return gate_b * y


def _fwd_setup_inputs(seed: int = 0) -> dict:
    key = jax.random.key(seed)
    ks = jax.random.split(key, 16)
    f32 = jnp.float32
    x = jax.random.normal(ks[0], (BATCH, SEQ, D_MODEL), f32)
    p = jax.random.normal(ks[1], (DEPTH, BATCH, SEQ, PLE_DIM), f32)
    norm_g = 1.0 + 0.02 * jax.random.normal(ks[2], (DEPTH, D_MODEL), f32)
    w_in = jax.random.normal(ks[3], (DEPTH, D_MODEL, PROJ_WIDTH), f32) * D_MODEL ** -0.5
    ln_v_g = 1.0 + 0.02 * jax.random.normal(ks[4], (DEPTH, WIDTH_A), f32)
    ln_v_b = 0.02 * jax.random.normal(ks[5], (DEPTH, WIDTH_A), f32)
    w_s = 0.5 * jax.random.normal(ks[6], (DEPTH, HEADS_A, CHUNK, CHUNK), f32) * CHUNK ** -0.5
    b_s = 1.0 + 0.1 * jax.random.normal(ks[7], (DEPTH, HEADS_A, CHUNK), f32)
    conv_w = jax.random.normal(ks[8], (DEPTH, WIDTH_B, CONV_K), f32) * CONV_K ** -0.5
    w_out = jax.random.normal(ks[9], (DEPTH, MIX_WIDTH, D_MODEL), f32) * MIX_WIDTH ** -0.5
    ple_norm_g = 1.0 + 0.02 * jax.random.normal(ks[10], (DEPTH, D_MODEL), f32)
    w_ple_gate = jax.random.normal(ks[11], (DEPTH, D_MODEL, D_MODEL), f32) * D_MODEL ** -0.5
    w_ple_proj = 0.5 * jax.random.normal(ks[12], (DEPTH, PLE_DIM, D_MODEL), f32) * PLE_DIM ** -0.5
    final_g = 1.0 + 0.02 * jax.random.normal(ks[13], (D_MODEL,), f32)
    return {"x": x, "p": p, "norm_g": norm_g, "w_in": w_in, "ln_v_g": ln_v_g,
            "ln_v_b": ln_v_b, "w_s": w_s, "b_s": b_s, "conv_w": conv_w,
            "w_out": w_out, "ple_norm_g": ple_norm_g, "w_ple_gate": w_ple_gate,
            "w_ple_proj": w_ple_proj, "final_g": final_g}


def _fwd_reference(x, p, norm_g, w_in, ln_v_g, ln_v_b, w_s, b_s, conv_w, w_out,
              ple_norm_g, w_ple_gate, w_ple_proj, final_g):
    split_idx = list(np.cumsum(SPLITS)[:-1])
    for i in range(DEPTH):
        hn = rmsnorm(x, norm_g[i])
        proj = hn @ w_in[i]
        u_a, v_a, z_a, h_b, gb, gc, z_b = jnp.split(proj, split_idx, axis=-1)
        out_a = spatial_gating(u_a, v_a, ln_v_g[i], ln_v_b[i], w_s[i], b_s[i]) * jax.nn.silu(z_a)
        out_b = gated_short_conv(h_b, gb, gc, conv_w[i]) * jax.nn.silu(z_b)
        x = x + jnp.concatenate([out_a, out_b], axis=-1) @ w_out[i]
        gate = jax.nn.sigmoid(rmsnorm(x, ple_norm_g[i]) @ w_ple_gate[i])
        x = x + gate * (p[i] @ w_ple_proj[i])
    return rmsnorm(x, final_g)


import jax as _jax
import jax.numpy as _jnp

TWIN_FORMAT = 'train_step'
FWD_PARAMS = ['x', 'p', 'norm_g', 'w_in', 'ln_v_g', 'ln_v_b', 'w_s', 'b_s', 'conv_w', 'w_out', 'ple_norm_g', 'w_ple_gate', 'w_ple_proj', 'final_g']
TWIN_WEIGHTS = ['norm_g', 'w_in', 'ln_v_g', 'ln_v_b', 'w_s', 'b_s', 'conv_w', 'w_out', 'ple_norm_g', 'w_ple_gate', 'w_ple_proj', 'final_g']
TWIN_DIFF_INPUT = 'x'
TWIN_INPUTS = ['x', 'p', 'norm_g', 'w_in', 'ln_v_g', 'ln_v_b', 'w_s', 'b_s', 'conv_w', 'w_out', 'ple_norm_g', 'w_ple_gate', 'w_ple_proj', 'final_g', 'loss_target', 'm_norm_g', 'm_w_in', 'm_ln_v_g', 'm_ln_v_b', 'm_w_s', 'm_b_s', 'm_conv_w', 'm_w_out', 'm_ple_norm_g', 'm_w_ple_gate', 'm_w_ple_proj', 'm_final_g', 'v_norm_g', 'v_w_in', 'v_ln_v_g', 'v_ln_v_b', 'v_w_s', 'v_b_s', 'v_conv_w', 'v_w_out', 'v_ple_norm_g', 'v_w_ple_gate', 'v_w_ple_proj', 'v_final_g']
TWIN_OUTPUTS = ['loss', 'grad_x', 'grad_norm_g', 'grad_w_in', 'grad_ln_v_g', 'grad_ln_v_b', 'grad_w_s', 'grad_b_s', 'grad_conv_w', 'grad_w_out', 'grad_ple_norm_g', 'grad_w_ple_gate', 'grad_w_ple_proj', 'grad_final_g', 'delta_norm_g', 'delta_w_in', 'delta_ln_v_g', 'delta_ln_v_b', 'delta_w_s', 'delta_b_s', 'delta_conv_w', 'delta_w_out', 'delta_ple_norm_g', 'delta_w_ple_gate', 'delta_w_ple_proj', 'delta_final_g', 'new_m_norm_g', 'new_m_w_in', 'new_m_ln_v_g', 'new_m_ln_v_b', 'new_m_w_s', 'new_m_b_s', 'new_m_conv_w', 'new_m_w_out', 'new_m_ple_norm_g', 'new_m_w_ple_gate', 'new_m_w_ple_proj', 'new_m_final_g', 'new_v_norm_g', 'new_v_w_in', 'new_v_ln_v_g', 'new_v_ln_v_b', 'new_v_w_s', 'new_v_b_s', 'new_v_conv_w', 'new_v_w_out', 'new_v_ple_norm_g', 'new_v_w_ple_gate', 'new_v_w_ple_proj', 'new_v_final_g']
TWIN_LEAF_KINDS = {'loss': 'loss', 'grad_x': 'grad_x', 'grad_norm_g': 'grad_w', 'grad_w_in': 'grad_w', 'grad_ln_v_g': 'grad_w', 'grad_ln_v_b': 'grad_w', 'grad_w_s': 'grad_w', 'grad_b_s': 'grad_w', 'grad_conv_w': 'grad_w', 'grad_w_out': 'grad_w', 'grad_ple_norm_g': 'grad_w', 'grad_w_ple_gate': 'grad_w', 'grad_w_ple_proj': 'grad_w', 'grad_final_g': 'grad_w', 'delta_norm_g': 'delta_w', 'delta_w_in': 'delta_w', 'delta_ln_v_g': 'delta_w', 'delta_ln_v_b': 'delta_w', 'delta_w_s': 'delta_w', 'delta_b_s': 'delta_w', 'delta_conv_w': 'delta_w', 'delta_w_out': 'delta_w', 'delta_ple_norm_g': 'delta_w', 'delta_w_ple_gate': 'delta_w', 'delta_w_ple_proj': 'delta_w', 'delta_final_g': 'delta_w', 'new_m_norm_g': 'new_m', 'new_m_w_in': 'new_m', 'new_m_ln_v_g': 'new_m', 'new_m_ln_v_b': 'new_m', 'new_m_w_s': 'new_m', 'new_m_b_s': 'new_m', 'new_m_conv_w': 'new_m', 'new_m_w_out': 'new_m', 'new_m_ple_norm_g': 'new_m', 'new_m_w_ple_gate': 'new_m', 'new_m_w_ple_proj': 'new_m', 'new_m_final_g': 'new_m', 'new_v_norm_g': 'new_v', 'new_v_w_in': 'new_v', 'new_v_ln_v_g': 'new_v', 'new_v_ln_v_b': 'new_v', 'new_v_w_s': 'new_v', 'new_v_b_s': 'new_v', 'new_v_conv_w': 'new_v', 'new_v_w_out': 'new_v', 'new_v_ple_norm_g': 'new_v', 'new_v_w_ple_gate': 'new_v', 'new_v_w_ple_proj': 'new_v', 'new_v_final_g': 'new_v'}


def _forward(args):
    return _fwd_reference(*[args[k] for k in FWD_PARAMS])


def _output_shape():
    out = _jax.eval_shape(lambda: _forward(_fwd_setup_inputs(0)))
    return out.shape, out.dtype

N_MICROBATCH = 1
ADAM_LR = 0.001
ADAM_B1 = 0.9
ADAM_B2 = 0.999
ADAM_EPS = 1e-08
ADAM_WD = 0.01
ADAM_STEP = 10
PER_EXAMPLE_BATCH_AXIS = {'x': 0, 'p': 1, 'loss_target': 0}
SHARED_INPUTS = []
_WEIGHT_DTYPES = {'norm_g': _jnp.float32, 'w_in': _jnp.float32, 'ln_v_g': _jnp.float32, 'ln_v_b': _jnp.float32, 'w_s': _jnp.float32, 'b_s': _jnp.float32, 'conv_w': _jnp.float32, 'w_out': _jnp.float32, 'ple_norm_g': _jnp.float32, 'w_ple_gate': _jnp.float32, 'w_ple_proj': _jnp.float32, 'final_g': _jnp.float32}
MOMENT_SCALE = {'norm_g': 2.392153e-01, 'w_in': 1.232597e-01, 'ln_v_g': 4.604306e-02, 'ln_v_b': 4.851639e-02, 'w_s': 9.177303e-02, 'b_s': 1.300649e-01, 'conv_w': 1.398117e-01, 'w_out': 1.315421e-01, 'ple_norm_g': 2.476696e-02, 'w_ple_gate': 2.287830e-02, 'w_ple_proj': 1.174788e-01, 'final_g': 6.403494e+01}


def _to_microbatches(a, axis):
    t = _jnp.moveaxis(a, axis, 0)
    t = t.reshape((N_MICROBATCH, t.shape[0] // N_MICROBATCH) + t.shape[1:])
    return _jnp.moveaxis(t, 1, axis + 1)


def setup_inputs(seed: int = 0) -> dict:
    inp = _fwd_setup_inputs(seed)
    key = _jax.random.fold_in(_jax.random.key(seed), 7919)
    shape, _ = _output_shape()
    out = dict(inp)
    out["loss_target"] = _jax.random.normal(_jax.random.fold_in(key, 0), shape, _jnp.float32)
    for i, name in enumerate(TWIN_WEIGHTS):
        w = inp[name].astype(_jnp.float32)
        if MOMENT_SCALE is None:
            s = _jnp.sqrt(_jnp.mean(_jnp.square(w)) + 1e-30)
        else:
            s = MOMENT_SCALE[name]
        km, kv = _jax.random.split(_jax.random.fold_in(key, i + 1))
        out[name] = w
        out["m_" + name] = s * _jax.random.normal(km, w.shape, _jnp.float32)
        out["v_" + name] = (s * s) * _jax.random.uniform(kv, w.shape, _jnp.float32, 0.5, 1.5)
    if N_MICROBATCH > 1:
        for name, axis in PER_EXAMPLE_BATCH_AXIS.items():
            out[name] = _to_microbatches(out[name], axis)
    return {'x': out['x'], 'p': out['p'], 'norm_g': out['norm_g'], 'w_in': out['w_in'], 'ln_v_g': out['ln_v_g'], 'ln_v_b': out['ln_v_b'], 'w_s': out['w_s'], 'b_s': out['b_s'], 'conv_w': out['conv_w'], 'w_out': out['w_out'], 'ple_norm_g': out['ple_norm_g'], 'w_ple_gate': out['w_ple_gate'], 'w_ple_proj': out['w_ple_proj'], 'final_g': out['final_g'], 'loss_target': out['loss_target'], 'm_norm_g': out['m_norm_g'], 'm_w_in': out['m_w_in'], 'm_ln_v_g': out['m_ln_v_g'], 'm_ln_v_b': out['m_ln_v_b'], 'm_w_s': out['m_w_s'], 'm_b_s': out['m_b_s'], 'm_conv_w': out['m_conv_w'], 'm_w_out': out['m_w_out'], 'm_ple_norm_g': out['m_ple_norm_g'], 'm_w_ple_gate': out['m_w_ple_gate'], 'm_w_ple_proj': out['m_w_ple_proj'], 'm_final_g': out['m_final_g'], 'v_norm_g': out['v_norm_g'], 'v_w_in': out['v_w_in'], 'v_ln_v_g': out['v_ln_v_g'], 'v_ln_v_b': out['v_ln_v_b'], 'v_w_s': out['v_w_s'], 'v_b_s': out['v_b_s'], 'v_conv_w': out['v_conv_w'], 'v_w_out': out['v_w_out'], 'v_ple_norm_g': out['v_ple_norm_g'], 'v_w_ple_gate': out['v_w_ple_gate'], 'v_w_ple_proj': out['v_w_ple_proj'], 'v_final_g': out['v_final_g']}


def _loss(weights, diff, rest, loss_target):
    with _jax.named_scope("forward"):
        args = {**rest, TWIN_DIFF_INPUT: diff, **{k: w.astype(_WEIGHT_DTYPES[k]) for k, w in weights.items()}}
        y = _forward(args)
    with _jax.named_scope("loss_head"):
        err = _jnp.square(y.astype(_jnp.float32) - loss_target)
        return 0.5 * _jnp.sum(_jnp.mean(err, axis=-1)) if err.ndim else 0.5 * err


def _adamw(w, g, m, v):
    m = ADAM_B1 * m + (1.0 - ADAM_B1) * g
    v = ADAM_B2 * v + (1.0 - ADAM_B2) * _jnp.square(g)
    m_hat = m / (1.0 - ADAM_B1 ** ADAM_STEP)
    v_hat = v / (1.0 - ADAM_B2 ** ADAM_STEP)
    delta = -ADAM_LR * (m_hat / (_jnp.sqrt(v_hat) + ADAM_EPS) + ADAM_WD * w)
    return delta, m, v


def reference(x, p, norm_g, w_in, ln_v_g, ln_v_b, w_s, b_s, conv_w, w_out, ple_norm_g, w_ple_gate, w_ple_proj, final_g, loss_target, m_norm_g, m_w_in, m_ln_v_g, m_ln_v_b, m_w_s, m_b_s, m_conv_w, m_w_out, m_ple_norm_g, m_w_ple_gate, m_w_ple_proj, m_final_g, v_norm_g, v_w_in, v_ln_v_g, v_ln_v_b, v_w_s, v_b_s, v_conv_w, v_w_out, v_ple_norm_g, v_w_ple_gate, v_w_ple_proj, v_final_g):
    given = dict(x=x, p=p, norm_g=norm_g, w_in=w_in, ln_v_g=ln_v_g, ln_v_b=ln_v_b, w_s=w_s, b_s=b_s, conv_w=conv_w, w_out=w_out, ple_norm_g=ple_norm_g, w_ple_gate=w_ple_gate, w_ple_proj=w_ple_proj, final_g=final_g, loss_target=loss_target, m_norm_g=m_norm_g, m_w_in=m_w_in, m_ln_v_g=m_ln_v_g, m_ln_v_b=m_ln_v_b, m_w_s=m_w_s, m_b_s=m_b_s, m_conv_w=m_conv_w, m_w_out=m_w_out, m_ple_norm_g=m_ple_norm_g, m_w_ple_gate=m_w_ple_gate, m_w_ple_proj=m_w_ple_proj, m_final_g=m_final_g, v_norm_g=v_norm_g, v_w_in=v_w_in, v_ln_v_g=v_ln_v_g, v_ln_v_b=v_ln_v_b, v_w_s=v_w_s, v_b_s=v_b_s, v_conv_w=v_conv_w, v_w_out=v_w_out, v_ple_norm_g=v_ple_norm_g, v_w_ple_gate=v_w_ple_gate, v_w_ple_proj=v_w_ple_proj, v_final_g=v_final_g)
    weights = {n: given[n] for n in TWIN_WEIGHTS}
    shared = {n: given[n] for n in SHARED_INPUTS}
    per_example = {n: given[n] for n in ['x', 'p']}
    grad_fn = _jax.value_and_grad(_loss, argnums=(0, 1))

    def one_microbatch(ex, loss_target):
        ex = dict(ex)
        diff = ex.pop(TWIN_DIFF_INPUT)
        return grad_fn(weights, diff, {**shared, **ex}, loss_target)

    if N_MICROBATCH == 1:
        loss, (grad_w, grad_x) = one_microbatch(per_example, given["loss_target"])
    else:
        def body(carry, xs):
            loss_sum, grad_sum = carry
            l_k, (gw_k, gx_k) = one_microbatch(xs[0], xs[1])
            with _jax.named_scope("update"):
                return (loss_sum + l_k, _jax.tree.map(_jnp.add, grad_sum, gw_k)), gx_k

        init = (_jnp.zeros((), _jnp.float32), _jax.tree.map(_jnp.zeros_like, weights))
        (loss, grad_w), grad_x = _jax.lax.scan(body, init, (per_example, given["loss_target"]))
    with _jax.named_scope("update"):
        delta_w, new_m, new_v = {}, {}, {}
        for n in TWIN_WEIGHTS:
            delta_w[n], new_m[n], new_v[n] = _adamw(weights[n], grad_w[n], given["m_" + n], given["v_" + n])
    return (loss, grad_x, *[grad_w[n] for n in TWIN_WEIGHTS], *[delta_w[n] for n in TWIN_WEIGHTS],
            *[new_m[n] for n in TWIN_WEIGHTS], *[new_v[n] for n in TWIN_WEIGHTS])
```

```python
import functools

import jax
import jax.numpy as jnp
from jax import lax
from jax.experimental import pallas as pl
from jax.experimental.pallas import tpu as pltpu

F32 = jnp.float32
BF16 = jnp.bfloat16

SEQ = 8192
D_MODEL = 1024
WIDTH = 512
PROJ = 7 * WIDTH
N_CHIPS = 4
COL_BLK = PROJ // N_CHIPS
PLE = 256
HEADS = 4
CHUNK = 128
DEPTH = 2
EPS = 1e-6
TILE = 256
N_TILES = SEQ // TILE
HALO = 8
VMEM_LIMIT = 60 * 1024 * 1024

ADAM_LR, ADAM_B1, ADAM_B2, ADAM_EPS, ADAM_WD, ADAM_STEP = 0.001, 0.9, 0.999, 1e-08, 0.01, 10
ADAM_C1 = 1.0 - ADAM_B1**ADAM_STEP
ADAM_C2 = 1.0 - ADAM_B2**ADAM_STEP

MESH = pl.DeviceIdType.MESH
ANY = pl.BlockSpec(memory_space=pl.ANY)


def _mm(a, b):
    return lax.dot_general(a, b, (((1,), (0,)), ((), ())), preferred_element_type=F32)


def _mm_nt(a, b):
    return lax.dot_general(a, b, (((1,), (1,)), ((), ())), preferred_element_type=F32)


def _mm_rows(a, w_ref):
    blk = w_ref.shape[1]
    acc = _mm(a[:, 0:blk], w_ref[0])
    for k in range(1, N_CHIPS):
        acc = acc + _mm(a[:, k * blk : (k + 1) * blk], w_ref[k])
    return acc


def _mm_nt_rows(a, w_ref):
    return jnp.concatenate([_mm_nt(a, w_ref[k]) for k in range(N_CHIPS)], axis=-1)


def _sigmoid(z):
    return 1.0 / (1.0 + jnp.exp(-z))


def _rms_stats(x):
    r = lax.rsqrt(jnp.mean(x * x, axis=-1, keepdims=True) + EPS)
    return r, x * r


def _rms_bwd(dyg, xh, r):
    return r * (dyg - xh * jnp.mean(dyg * xh, axis=-1, keepdims=True))


def _ln_stats(v):
    mu = jnp.mean(v, axis=-1, keepdims=True)
    vc = v - mu
    rs = lax.rsqrt(jnp.mean(vc * vc, axis=-1, keepdims=True) + EPS)
    return rs, vc * rs


def _mesh_pos():
    x, y, c = lax.axis_index("x"), lax.axis_index("y"), lax.axis_index("c")
    return x, y, c, 2 * x + y


def _peer(x, y, c, r):
    return ((1 - x) if (r >> 1) else x, (1 - y) if (r & 1) else y, c)


def _full(shape):
    return pl.BlockSpec(shape, lambda *_: (0,) * len(shape))


def _const(shape, pos):
    return pl.BlockSpec((None,) * len(pos) + tuple(shape), lambda *_: tuple(pos) + (0,) * len(shape))


def gather_weights(shards):
    n = len(shards)

    def body(*refs):
        src, dst = refs[:n], refs[n : 2 * n]
        loc_sem, ici_s, ici_r, fwd_s, fwd_r = refs[2 * n :]
        x, y, c, own = _mesh_pos()
        sib = (x, y, 1 - c)
        local = [pltpu.make_async_copy(src[a], dst[a].at[own], loc_sem.at[a]) for a in range(n)]
        for cp in local:
            cp.start()
        ici, fwd = {}, {}
        for a in range(n):
            for r in (1, 2, 3):
                k = 3 * a + r - 1
                ici[a, r] = pltpu.make_async_remote_copy(
                    src_ref=src[a].at[c], dst_ref=dst[a].at[own, c], send_sem=ici_s.at[k], recv_sem=ici_r.at[k],
                    device_id=_peer(x, y, c, r), device_id_type=MESH)
                fwd[a, r] = pltpu.make_async_remote_copy(
                    src_ref=dst[a].at[own ^ r, c], dst_ref=dst[a].at[own ^ r, c], send_sem=fwd_s.at[k],
                    recv_sem=fwd_r.at[k], device_id=sib, device_id_type=MESH)
        for a in range(n):
            for r in (1, 2, 3):
                ici[a, r].start()
        for r in (1, 2, 3):
            for a in range(n):
                pltpu.make_async_remote_copy(
                    src_ref=src[a].at[c], dst_ref=dst[a].at[own ^ r, c], send_sem=ici_s.at[3 * a + r - 1],
                    recv_sem=ici_r.at[3 * a + r - 1], device_id=_peer(x, y, c, r), device_id_type=MESH).wait_recv()
                fwd[a, r].start()
        for a in range(n):
            for r in (1, 2, 3):
                pltpu.make_async_remote_copy(
                    src_ref=dst[a].at[own ^ r, 1 - c], dst_ref=dst[a].at[own ^ r, 1 - c], send_sem=fwd_s.at[3 * a + r - 1],
                    recv_sem=fwd_r.at[3 * a + r - 1], device_id=sib, device_id_type=MESH).wait_recv()
        for a in range(n):
            for r in (1, 2, 3):
                ici[a, r].wait_send()
                fwd[a, r].wait_send()
        for cp in local:
            cp.wait()

    out_shape = [jax.ShapeDtypeStruct((N_CHIPS,) + s.shape, s.dtype) for s in shards]
    return pl.pallas_call(
        body, name="gather_weights", out_shape=out_shape, in_specs=[ANY] * n, out_specs=[ANY] * n,
        scratch_shapes=[pltpu.SemaphoreType.DMA((n,))] + [pltpu.SemaphoreType.DMA((3 * n,))] * 4,
        compiler_params=pltpu.CompilerParams(has_side_effects=True),
    )(*shards)


def _mixer_fwd(proj_ref, lg, lb, ws_ref, bsb_ref, cw_ref, mix_ref, xcbuf, halo_xc):
    u = proj_ref[:, 0 * WIDTH : 1 * WIDTH]
    v = proj_ref[:, 1 * WIDTH : 2 * WIDTH]
    za = proj_ref[:, 2 * WIDTH : 3 * WIDTH]
    h = proj_ref[:, 3 * WIDTH : 4 * WIDTH]
    gb = proj_ref[:, 4 * WIDTH : 5 * WIDTH]
    gc = proj_ref[:, 5 * WIDTH : 6 * WIDTH]
    zb = proj_ref[:, 6 * WIDTH : 7 * WIDTH]
    rs, vhat = _ln_stats(v)
    vl = vhat * lg + lb
    vlb = vl.astype(BF16)
    for j in range(TILE // CHUNK):
        rows = slice(j * CHUNK, (j + 1) * CHUNK)
        for hd in range(HEADS):
            cols = slice(hd * CHUNK, (hd + 1) * CHUNK)
            mix_ref[rows, cols] = _mm(ws_ref[hd], vlb[rows, cols]) + bsb_ref[:, cols]
    mixed = mix_ref[...]
    siga = _sigmoid(za)
    sigb = _sigmoid(zb)
    xc = gc * h
    xcbuf[0:HALO, :] = halo_xc
    xcbuf[HALO : HALO + TILE, :] = xc
    y = cw_ref[0:1, :] * xcbuf[HALO - 2 : HALO - 2 + TILE, :] + cw_ref[1:2, :] * xcbuf[HALO - 1 : HALO - 1 + TILE, :]
    y = y + cw_ref[2:3, :] * xc
    return dict(u=u, za=za, h=h, gb=gb, gc=gc, zb=zb, rs=rs, vhat=vhat, vlb=vlb, mixed=mixed, siga=siga, sigb=sigb,
                xc=xc, y=y)


def fwd_layer(layer, x, p, wts, small):
    win, wout, wg, wpp = wts
    ng, lg, lb, ws, bsb, cw, pg = small

    def body(x_ref, p_ref, win_ref, wout_ref, wg_ref, wpp_ref, ng_ref, lg_ref, lb_ref, ws_ref, bsb_ref, cw_ref, pg_ref,
             proj_ref, x2_ref, gate_ref, pp_ref, x3_ref, hnT_ref, catT_ref, hn2T_ref, pT_ref, mix_ref, xcbuf, carry):
        i = pl.program_id(0)

        @pl.when(i == 0)
        def _():
            carry[...] = jnp.zeros_like(carry)

        xv = x_ref[...]
        r1, xh = _rms_stats(xv)
        hn = xh * ng_ref[...]
        hnb = hn.astype(BF16)
        hnT_ref[...] = hn.T.astype(BF16)
        for k in range(N_CHIPS):
            proj_ref[:, k * COL_BLK : (k + 1) * COL_BLK] = _mm(hnb, win_ref[k])
        m = _mixer_fwd(proj_ref, lg_ref[...], lb_ref[...], ws_ref, bsb_ref, cw_ref, mix_ref, xcbuf, carry[...])
        carry[...] = m["xc"][TILE - HALO : TILE, :]
        out_a = (m["u"] * m["mixed"]) * (m["za"] * m["siga"])
        out_b = (m["gb"] * m["y"]) * (m["zb"] * m["sigb"])
        cat = jnp.concatenate([out_a, out_b], axis=-1)
        catT_ref[...] = cat.T.astype(BF16)
        x2 = xv + _mm_rows(cat.astype(BF16), wout_ref)
        x2_ref[...] = x2
        r2, xh2 = _rms_stats(x2)
        hn2 = xh2 * pg_ref[...]
        hn2T_ref[...] = hn2.T.astype(BF16)
        gate = _sigmoid(_mm_rows(hn2.astype(BF16), wg_ref))
        gate_ref[...] = gate
        pv = p_ref[...]
        pT_ref[...] = pv.T.astype(BF16)
        pb = pv.astype(BF16)
        for k in range(N_CHIPS):
            pp_ref[:, k * PLE : (k + 1) * PLE] = _mm(pb, wpp_ref[k])
        x3_ref[...] = x2 + gate * pp_ref[...]

    tok = lambda w: pl.BlockSpec((TILE, w), lambda i: (i, 0))
    tokT = lambda w: pl.BlockSpec((w, TILE), lambda i: (0, i))
    f32 = lambda w: jax.ShapeDtypeStruct((SEQ, w), F32)
    bfT = lambda w: jax.ShapeDtypeStruct((w, SEQ), BF16)
    outs = pl.pallas_call(
        body, name=f"fwd_layer{layer}", grid=(N_TILES,),
        in_specs=[tok(D_MODEL), pl.BlockSpec((None, None, TILE, PLE), lambda i: (layer, 0, i, 0)),
                  pl.BlockSpec((N_CHIPS, None, D_MODEL, COL_BLK), lambda i: (0, layer, 0, 0)),
                  pl.BlockSpec((N_CHIPS, None, D_MODEL // N_CHIPS, D_MODEL), lambda i: (0, layer, 0, 0)),
                  pl.BlockSpec((N_CHIPS, None, D_MODEL // N_CHIPS, D_MODEL), lambda i: (0, layer, 0, 0)),
                  pl.BlockSpec((N_CHIPS, None, PLE, PLE), lambda i: (0, layer, 0, 0)),
                  _const((1, D_MODEL), (layer,)), _const((1, WIDTH), (layer,)), _const((1, WIDTH), (layer,)),
                  _const((HEADS, CHUNK, CHUNK), (layer,)), _const((CHUNK, WIDTH), (layer,)), _const((3, WIDTH), (layer,)),
                  _const((1, D_MODEL), (layer,))],
        out_specs=[tok(PROJ), tok(D_MODEL), tok(D_MODEL), tok(D_MODEL), tok(D_MODEL),
                   tokT(D_MODEL), tokT(D_MODEL), tokT(D_MODEL), tokT(PLE)],
        out_shape=[f32(PROJ), f32(D_MODEL), f32(D_MODEL), f32(D_MODEL), f32(D_MODEL),
                   bfT(D_MODEL), bfT(D_MODEL), bfT(D_MODEL), bfT(PLE)],
        scratch_shapes=[pltpu.VMEM((TILE, WIDTH), F32), pltpu.VMEM((HALO + TILE, WIDTH), F32), pltpu.VMEM((HALO, WIDTH), F32)],
        compiler_params=pltpu.CompilerParams(dimension_semantics=("arbitrary",), vmem_limit_bytes=VMEM_LIMIT),
    )(x, p, win, wout, wg, wpp, ng, lg, lb, ws, bsb, cw, pg)
    return outs


def loss_head(x, tgt, gf):
    def body(x_ref, t_ref, g_ref, dx_ref, loss_ref, gg_ref):
        i = pl.program_id(0)

        @pl.when(i == 0)
        def _():
            loss_ref[...] = jnp.zeros_like(loss_ref)
            gg_ref[...] = jnp.zeros_like(gg_ref)

        r, xh = _rms_stats(x_ref[...])
        g = g_ref[...]
        err = xh * g - t_ref[...]
        loss_ref[...] += (0.5 / D_MODEL) * jnp.sum(err * err).reshape(1, 1)
        dy = err * (1.0 / D_MODEL)
        gg_ref[...] += jnp.sum(dy * xh, axis=0, keepdims=True)
        dx_ref[...] = _rms_bwd(dy * g, xh, r)

    tile = 512
    tok = pl.BlockSpec((tile, D_MODEL), lambda i: (i, 0))
    return pl.pallas_call(
        body, name="loss_head", grid=(SEQ // tile,), in_specs=[tok, tok, _full((1, D_MODEL))],
        out_specs=[tok, _full((1, 1)), _full((1, D_MODEL))],
        out_shape=[jax.ShapeDtypeStruct((SEQ, D_MODEL), F32), jax.ShapeDtypeStruct((1, 1), F32),
                   jax.ShapeDtypeStruct((1, D_MODEL), F32)],
        compiler_params=pltpu.CompilerParams(dimension_semantics=("arbitrary",)),
    )(x, tgt, gf)


def bwd_ple(layer, dx3, x2, gate, pp, wg, wout, pg):
    def body(dx3_ref, x2_ref, gate_ref, pp_ref, wg_ref, wout_ref, pg_ref,
             dpp_ref, dgl_ref, dx2_ref, dx2b_ref, dcat_ref, gpg_ref):
        i = pl.program_id(0)

        @pl.when(i == 0)
        def _():
            gpg_ref[...] = jnp.zeros_like(gpg_ref)

        dx3v = dx3_ref[...]
        gate_v = gate_ref[...]
        dpp_ref[...] = (dx3v * gate_v).astype(BF16)
        dgl = ((dx3v * pp_ref[...]) * gate_v * (1.0 - gate_v)).astype(BF16)
        dgl_ref[...] = dgl
        dhn2 = _mm_nt_rows(dgl, wg_ref)
        r2, xh2 = _rms_stats(x2_ref[...])
        gpg_ref[...] += jnp.sum(dhn2 * xh2, axis=0, keepdims=True)
        dx2 = dx3v + _rms_bwd(dhn2 * pg_ref[...], xh2, r2)
        dx2_ref[...] = dx2
        dx2b = dx2.astype(BF16)
        dx2b_ref[...] = dx2b
        dcat_ref[...] = _mm_nt_rows(dx2b, wout_ref)

    tok = pl.BlockSpec((TILE, D_MODEL), lambda i: (i, 0))
    wspec = pl.BlockSpec((N_CHIPS, None, D_MODEL // N_CHIPS, D_MODEL), lambda i: (0, layer, 0, 0))
    f32 = jax.ShapeDtypeStruct((SEQ, D_MODEL), F32)
    b16 = jax.ShapeDtypeStruct((SEQ, D_MODEL), BF16)

    return pl.pallas_call(
        body, name=f"bwd_ple{layer}", grid=(N_TILES,),
        in_specs=[tok, tok, tok, tok, wspec, wspec, _const((1, D_MODEL), (layer,))],
        out_specs=[tok, tok, tok, tok, tok, _full((1, D_MODEL))],
        out_shape=[b16, b16, f32, b16, f32, jax.ShapeDtypeStruct((1, D_MODEL), F32)],
        compiler_params=pltpu.CompilerParams(dimension_semantics=("arbitrary",), vmem_limit_bytes=VMEM_LIMIT),
    )(dx3, x2, gate, pp, wg, wout, pg)


def bwd_mix(layer, dcat, dx2, x, proj, win, small, wsT, tril):
    ng, lg, lb, ws, bsb, cw, _ = small

    def body(dcat_ref, dx2_ref, x_ref, proj_ref, halo_ref, win_ref, ng_ref, lg_ref, lb_ref, ws_ref, wsT_ref, bsb_ref,
             cw_ref, tril_ref,
             dx_ref, dproj_ref, gn_ref, glg_ref, glb_ref, gws_ref, gbs_ref, gcw_ref,
             mix_ref, xcbuf, dycbuf, dvl_ref, bs_acc):
        i = pl.program_id(0)

        @pl.when(i == 0)
        def _():
            for ref in (gn_ref, glg_ref, glb_ref, gws_ref, gcw_ref, bs_acc):
                ref[...] = jnp.zeros_like(ref)
            dycbuf[TILE : TILE + HALO, :] = jnp.zeros((HALO, WIDTH), F32)

        lgv = lg_ref[...]
        halo_xc = halo_ref[:, 5 * WIDTH : 6 * WIDTH] * halo_ref[:, 3 * WIDTH : 4 * WIDTH]
        halo_xc = jnp.where(i == N_TILES - 1, 0.0, halo_xc)
        m = _mixer_fwd(proj_ref, lgv, lb_ref[...], ws_ref, bsb_ref, cw_ref, mix_ref, xcbuf, halo_xc)
        u, za, h, gb, gc, zb = m["u"], m["za"], m["h"], m["gb"], m["gc"], m["zb"]
        mixed, siga, sigb, xc, y = m["mixed"], m["siga"], m["sigb"], m["xc"], m["y"]
        doa = dcat_ref[:, 0:WIDTH]
        dob = dcat_ref[:, WIDTH : 2 * WIDTH]
        sa = za * siga
        sb = zb * sigb
        doa_sa = doa * sa
        dproj_ref[:, 0 * WIDTH : 1 * WIDTH] = (doa_sa * mixed).astype(BF16)
        dmixed = doa_sa * u
        dza = (doa * (u * mixed)) * (siga * (1.0 + za * (1.0 - siga)))
        dproj_ref[:, 2 * WIDTH : 3 * WIDTH] = dza.astype(BF16)
        dob_sb = dob * sb
        dproj_ref[:, 4 * WIDTH : 5 * WIDTH] = (dob_sb * y).astype(BF16)
        dyc = dob_sb * gb
        dzb = (dob * (gb * y)) * (sigb * (1.0 + zb * (1.0 - sigb)))
        dproj_ref[:, 6 * WIDTH : 7 * WIDTH] = dzb.astype(BF16)
        dycbuf[0:TILE, :] = dyc
        dyc1 = dycbuf[1 : 1 + TILE, :]
        dyc2 = dycbuf[2 : 2 + TILE, :]
        dxc = cw_ref[2:3, :] * dyc + cw_ref[1:2, :] * dyc1 + cw_ref[0:1, :] * dyc2
        gcw_ref[0:1, :] += jnp.sum(xc * dyc2, axis=0, keepdims=True)
        gcw_ref[1:2, :] += jnp.sum(xc * dyc1, axis=0, keepdims=True)
        gcw_ref[2:3, :] += jnp.sum(xc * dyc, axis=0, keepdims=True)
        dycbuf[TILE : TILE + HALO, :] = dyc[0:HALO, :]
        dproj_ref[:, 5 * WIDTH : 6 * WIDTH] = (dxc * h).astype(BF16)
        dproj_ref[:, 3 * WIDTH : 4 * WIDTH] = (dxc * gc).astype(BF16)
        dmb = dmixed.astype(BF16)
        vlb = m["vlb"]
        bsum = jnp.zeros((CHUNK, WIDTH), F32)
        for j in range(TILE // CHUNK):
            rows = slice(j * CHUNK, (j + 1) * CHUNK)
            bsum = bsum + dmixed[rows, :]
            for hd in range(HEADS):
                cols = slice(hd * CHUNK, (hd + 1) * CHUNK)
                gws_ref[hd] += _mm_nt(dmb[rows, cols], vlb[rows, cols])
                dvl_ref[rows, cols] = _mm(wsT_ref[hd], dmb[rows, cols])
        bs_acc[...] += bsum
        dvl = dvl_ref[...]
        vhat = m["vhat"]
        glb_ref[...] += jnp.sum(dvl, axis=0, keepdims=True)
        glg_ref[...] += jnp.sum(dvl * vhat, axis=0, keepdims=True)
        dvh = dvl * lgv
        dv = m["rs"] * (dvh - jnp.mean(dvh, axis=-1, keepdims=True) - vhat * jnp.mean(dvh * vhat, axis=-1, keepdims=True))
        dproj_ref[:, 1 * WIDTH : 2 * WIDTH] = dv.astype(BF16)
        dhn = _mm_nt(dproj_ref[:, 0:COL_BLK], win_ref[0])
        for k in range(1, N_CHIPS):
            dhn = dhn + _mm_nt(dproj_ref[:, k * COL_BLK : (k + 1) * COL_BLK], win_ref[k])
        r1, xh = _rms_stats(x_ref[...])
        gn_ref[...] += jnp.sum(dhn * xh, axis=0, keepdims=True)
        dx_ref[...] = dx2_ref[...] + _rms_bwd(dhn * ng_ref[...], xh, r1)

        @pl.when(i == N_TILES - 1)
        def _():
            for hd in range(HEADS):
                gws_ref[hd] = gws_ref[hd] * tril_ref[...]
                gbs_ref[hd : hd + 1, :] = jnp.sum(bs_acc[:, hd * CHUNK : (hd + 1) * CHUNK].T, axis=0, keepdims=True)

    rev = lambda w: pl.BlockSpec((TILE, w), lambda i: (N_TILES - 1 - i, 0))
    halo = pl.BlockSpec((HALO, PROJ), lambda i: (jnp.maximum((N_TILES - 1 - i) * (TILE // HALO) - 1, 0), 0))
    return pl.pallas_call(
        body, name=f"bwd_mix{layer}", grid=(N_TILES,),
        in_specs=[rev(D_MODEL), rev(D_MODEL), rev(D_MODEL), rev(PROJ), halo,
                  pl.BlockSpec((N_CHIPS, None, D_MODEL, COL_BLK), lambda i: (0, layer, 0, 0)),
                  _const((1, D_MODEL), (layer,)), _const((1, WIDTH), (layer,)), _const((1, WIDTH), (layer,)),
                  _const((HEADS, CHUNK, CHUNK), (layer,)), _const((HEADS, CHUNK, CHUNK), (layer,)),
                  _const((CHUNK, WIDTH), (layer,)), _const((3, WIDTH), (layer,)), _full((CHUNK, CHUNK))],
        out_specs=[rev(D_MODEL), rev(PROJ), _full((1, D_MODEL)), _full((1, WIDTH)), _full((1, WIDTH)),
                   _full((HEADS, CHUNK, CHUNK)), _full((HEADS, CHUNK)), _full((3, WIDTH))],
        out_shape=[jax.ShapeDtypeStruct((SEQ, D_MODEL), F32), jax.ShapeDtypeStruct((SEQ, PROJ), BF16),
                   jax.ShapeDtypeStruct((1, D_MODEL), F32), jax.ShapeDtypeStruct((1, WIDTH), F32),
                   jax.ShapeDtypeStruct((1, WIDTH), F32), jax.ShapeDtypeStruct((HEADS, CHUNK, CHUNK), F32),
                   jax.ShapeDtypeStruct((HEADS, CHUNK), F32), jax.ShapeDtypeStruct((3, WIDTH), F32)],
        scratch_shapes=[pltpu.VMEM((TILE, WIDTH), F32), pltpu.VMEM((HALO + TILE, WIDTH), F32),
                        pltpu.VMEM((TILE + HALO, WIDTH), F32), pltpu.VMEM((TILE, WIDTH), F32),
                        pltpu.VMEM((CHUNK, WIDTH), F32)],
        compiler_params=pltpu.CompilerParams(dimension_semantics=("arbitrary",), vmem_limit_bytes=VMEM_LIMIT),
    )(dcat, dx2, x, proj, proj, win, ng, lg, lb, ws, wsT, bsb, cw, tril)


def wgrad(layer, aT, b, tn, col_blocked, prev, name):
    m_dim, n_dim = aT.shape[0], b.shape[1]
    tk = 1024
    n_k = SEQ // tk

    def body(*refs):
        a_ref, b_ref, o_ref = refs[0], refs[1], refs[-1]
        k = pl.program_id(1)
        prod = _mm(a_ref[...], b_ref[...])
        if not col_blocked:
            prod = prod.reshape(N_CHIPS, m_dim // N_CHIPS, tn)

        @pl.when(k == 0)
        def _():
            o_ref[...] = prod

        @pl.when(k > 0)
        def _():
            o_ref[...] += prod

    if col_blocked:
        shape = (DEPTH, N_CHIPS, m_dim, tn)
        o_spec = pl.BlockSpec((None, None, m_dim, tn), lambda n, k: (layer, n, 0, 0))
    else:
        shape = (DEPTH, N_CHIPS, m_dim // N_CHIPS, n_dim)
        o_spec = pl.BlockSpec((None, N_CHIPS, m_dim // N_CHIPS, tn), lambda n, k: (layer, 0, 0, n))
    in_specs = [pl.BlockSpec((m_dim, tk), lambda n, k: (0, k)), pl.BlockSpec((tk, tn), lambda n, k: (k, n))]
    args = [aT, b]
    aliases = {}
    if prev is not None:
        in_specs.append(ANY)
        args.append(prev)
        aliases = {2: 0}
    return pl.pallas_call(
        body, name=name, grid=(n_dim // tn, n_k), in_specs=in_specs, out_specs=o_spec,
        out_shape=jax.ShapeDtypeStruct(shape, F32), input_output_aliases=aliases,
        compiler_params=pltpu.CompilerParams(dimension_semantics=("arbitrary", "arbitrary"), vmem_limit_bytes=VMEM_LIMIT),
    )(*args)


def rs_to_sibling(grads):
    n = len(grads)

    def body(*refs):
        src, dst = refs[:n], refs[n : 2 * n]
        send, recv = refs[2 * n :]
        x, y, c, _ = _mesh_pos()
        cps = []
        for a in range(n):
            rh = src[a].shape[2] // 2
            cps.append(pltpu.make_async_remote_copy(
                src_ref=src[a].at[:, :, pl.ds((1 - c) * rh, rh), :], dst_ref=dst[a], send_sem=send.at[a],
                recv_sem=recv.at[a], device_id=(x, y, 1 - c), device_id_type=MESH))
        for cp in cps:
            cp.start()
        for cp in cps:
            cp.wait()

    out_shape = [jax.ShapeDtypeStruct((DEPTH, N_CHIPS, g.shape[2] // 2, g.shape[3]), F32) for g in grads]
    return pl.pallas_call(
        body, name="rs_to_sibling", out_shape=out_shape, in_specs=[ANY] * n, out_specs=[ANY] * n,
        scratch_shapes=[pltpu.SemaphoreType.DMA((n,))] * 2, compiler_params=pltpu.CompilerParams(has_side_effects=True),
    )(*grads)


def rs_add_sibling(ids, grads, recvd):
    n = len(grads)

    def body(ids_ref, *refs):
        g, r = refs[:n], refs[n : 2 * n]
        pb, pf = refs[2 * n : 3 * n], refs[3 * n :]
        k = pl.program_id(1)
        for a in range(n):
            s = g[a][...] + r[a][...]
            pb[a][...] = s.astype(BF16)

            @pl.when(k == ids_ref[1])
            def _():
                pf[a][...] = s

    in_specs, out_specs, out_shape = [], [], []
    for g in grads:
        rh, cc = g.shape[2] // 2, g.shape[3]
        in_specs.append(pl.BlockSpec((None, None, rh, cc), lambda l, k, ids: (l, k, ids[0], 0)))
    for g in grads:
        rh, cc = g.shape[2] // 2, g.shape[3]
        in_specs.append(pl.BlockSpec((None, None, rh, cc), lambda l, k, ids: (l, k, 0, 0)))
        out_specs.append(pl.BlockSpec((None, None, rh, cc), lambda l, k, ids: (l, k, 0, 0)))
        out_shape.append(jax.ShapeDtypeStruct((DEPTH, N_CHIPS, rh, cc), BF16))
    for g in grads:
        rh, cc = g.shape[2] // 2, g.shape[3]
        out_specs.append(pl.BlockSpec((None, rh, cc), lambda l, k, ids: (l, 0, 0)))
        out_shape.append(jax.ShapeDtypeStruct((DEPTH, rh, cc), F32))
    outs = pl.pallas_call(
        body, name="rs_add_sibling", out_shape=out_shape,
        grid_spec=pltpu.PrefetchScalarGridSpec(num_scalar_prefetch=1, grid=(DEPTH, N_CHIPS), in_specs=in_specs,
                                               out_specs=out_specs),
        compiler_params=pltpu.CompilerParams(dimension_semantics=("arbitrary", "arbitrary"), vmem_limit_bytes=VMEM_LIMIT),
    )(ids, *grads, *recvd)
    return outs[:n], outs[n:]


def rs_to_owners(partials):
    n = len(partials)

    def body(*refs):
        src, dst = refs[:n], refs[n : 2 * n]
        send, recv = refs[2 * n :]
        x, y, c, own = _mesh_pos()
        cps = []
        for a in range(n):
            for r in (1, 2, 3):
                cps.append(pltpu.make_async_remote_copy(
                    src_ref=src[a].at[:, own ^ r], dst_ref=dst[a].at[r - 1], send_sem=send.at[3 * a + r - 1],
                    recv_sem=recv.at[3 * a + r - 1], device_id=_peer(x, y, c, r), device_id_type=MESH))
        for cp in cps:
            cp.start()
        for cp in cps:
            cp.wait()

    out_shape = [jax.ShapeDtypeStruct((3, DEPTH) + p.shape[2:], BF16) for p in partials]
    return pl.pallas_call(
        body, name="rs_to_owners", out_shape=out_shape, in_specs=[ANY] * n, out_specs=[ANY] * n,
        scratch_shapes=[pltpu.SemaphoreType.DMA((3 * n,))] * 2, compiler_params=pltpu.CompilerParams(has_side_effects=True),
    )(*partials)


def rs_add_owners(own_f32, recvd):
    n = len(own_f32)
    nb = 2

    def body(*refs):
        o, r, f = refs[:n], refs[n : 2 * n], refs[2 * n :]
        for a in range(n):
            f[a][...] = ((o[a][...] + r[a][0].astype(F32)) + r[a][1].astype(F32)) + r[a][2].astype(F32)

    in_specs, out_specs, out_shape = [], [], []
    for o in own_f32:
        rb, cc = o.shape[1] // nb, o.shape[2]
        in_specs.append(pl.BlockSpec((None, rb, cc), lambda l, j: (l, j, 0)))
    for o in own_f32:
        rb, cc = o.shape[1] // nb, o.shape[2]
        in_specs.append(pl.BlockSpec((3, None, rb, cc), lambda l, j: (0, l, j, 0)))
        out_specs.append(pl.BlockSpec((None, rb, cc), lambda l, j: (l, j, 0)))
        out_shape.append(jax.ShapeDtypeStruct(o.shape, F32))
    return pl.pallas_call(
        body, name="rs_add_owners", grid=(DEPTH, nb), in_specs=in_specs, out_specs=out_specs, out_shape=out_shape,
        compiler_params=pltpu.CompilerParams(dimension_semantics=("arbitrary", "arbitrary"), vmem_limit_bytes=VMEM_LIMIT),
    )(*own_f32, *recvd)


def rs_exchange_halves(halves):
    n = len(halves)

    def body(*refs):
        src, dst = refs[:n], refs[n : 2 * n]
        loc, send, recv = refs[2 * n :]
        x, y, c, _ = _mesh_pos()
        cps = []
        for a in range(n):
            rh = src[a].shape[1]
            rows = dst[a].at[:, pl.ds(c * rh, rh), :]
            cps.append(pltpu.make_async_copy(src[a], rows, loc.at[a]))
            cps.append(pltpu.make_async_remote_copy(
                src_ref=src[a], dst_ref=rows, send_sem=send.at[a], recv_sem=recv.at[a], device_id=(x, y, 1 - c),
                device_id_type=MESH))
        for cp in cps:
            cp.start()
        for cp in cps:
            cp.wait()

    out_shape = [jax.ShapeDtypeStruct((DEPTH, 2 * h.shape[1], h.shape[2]), F32) for h in halves]
    return pl.pallas_call(
        body, name="rs_exchange_halves", out_shape=out_shape, in_specs=[ANY] * n, out_specs=[ANY] * n,
        scratch_shapes=[pltpu.SemaphoreType.DMA((n,))] * 3, compiler_params=pltpu.CompilerParams(has_side_effects=True),
    )(*halves)


def _adamw(w, g, m, v):
    m2 = ADAM_B1 * m + (1.0 - ADAM_B1) * g
    v2 = ADAM_B2 * v + (1.0 - ADAM_B2) * (g * g)
    delta = -ADAM_LR * ((m2 / ADAM_C1) / (jnp.sqrt(v2 / ADAM_C2) + ADAM_EPS) + ADAM_WD * w)
    return delta, m2, v2


def adamw_big(ws, gs, ms, vs):
    n = len(ws)
    nb = 4

    def body(*refs):
        for a in range(n):
            w, g, m, v = (refs[j * n + a][...] for j in range(4))
            d, m2, v2 = _adamw(w, g, m, v)
            refs[4 * n + a][...] = d
            refs[5 * n + a][...] = m2
            refs[6 * n + a][...] = v2

    specs = [pl.BlockSpec((None, w.shape[1] // nb, w.shape[2]), lambda l, j: (l, j, 0)) for w in ws]
    shapes = [jax.ShapeDtypeStruct(w.shape, F32) for w in ws]
    outs = pl.pallas_call(
        body, name="adamw_big", grid=(DEPTH, nb), in_specs=specs * 4, out_specs=specs * 3, out_shape=shapes * 3,
        compiler_params=pltpu.CompilerParams(dimension_semantics=("arbitrary", "arbitrary"), vmem_limit_bytes=VMEM_LIMIT),
    )(*ws, *gs, *ms, *vs)
    return outs[:n], outs[n : 2 * n], outs[2 * n :]


def allreduce_small(g, w, m, v):
    rows = g.shape[0]

    def body(g_ref, w_ref, m_ref, v_ref, gsum_ref, d_ref, m2_ref, v2_ref, sib_buf, slots, s1, r1, s2, r2):
        x, y, c, own = _mesh_pos()
        first = pltpu.make_async_remote_copy(src_ref=g_ref, dst_ref=sib_buf, send_sem=s1, recv_sem=r1,
                                             device_id=(x, y, 1 - c), device_id_type=MESH)
        first.start()
        first.wait()
        slots[0] = g_ref[...] + sib_buf[...]
        cps = [pltpu.make_async_remote_copy(src_ref=slots.at[0], dst_ref=slots.at[r], send_sem=s2.at[r - 1],
                                            recv_sem=r2.at[r - 1], device_id=_peer(x, y, c, r), device_id_type=MESH)
               for r in (1, 2, 3)]
        for cp in cps:
            cp.start()
        for cp in cps:
            cp.wait()
        tot = slots[own] + slots[own ^ 1]
        tot = tot + slots[own ^ 2]
        tot = tot + slots[own ^ 3]
        gsum_ref[...] = tot
        d, m2, v2 = _adamw(w_ref[...], tot, m_ref[...], v_ref[...])
        d_ref[...] = d
        m2_ref[...] = m2
        v2_ref[...] = v2

    vm = pl.BlockSpec(memory_space=pltpu.VMEM)
    shape = jax.ShapeDtypeStruct(g.shape, F32)
    return pl.pallas_call(
        body, name="allreduce_small", out_shape=[shape] * 4, in_specs=[vm] * 4, out_specs=[vm] * 4,
        scratch_shapes=[pltpu.VMEM((rows, 128), F32), pltpu.VMEM((N_CHIPS, rows, 128), F32),
                        pltpu.SemaphoreType.DMA, pltpu.SemaphoreType.DMA,
                        pltpu.SemaphoreType.DMA((3,)), pltpu.SemaphoreType.DMA((3,))],
        compiler_params=pltpu.CompilerParams(has_side_effects=True),
    )(g, w, m, v)


def adamw_rows(w, g, m, v):
    def body(w_ref, g_ref, m_ref, v_ref, d_ref, m2_ref, v2_ref):
        d, m2, v2 = _adamw(w_ref[...], g_ref[...], m_ref[...], v_ref[...])
        d_ref[...] = d
        m2_ref[...] = m2
        v2_ref[...] = v2

    return pl.pallas_call(body, name="adamw_conv", out_shape=[jax.ShapeDtypeStruct(w.shape, F32)] * 3)(w, g, m, v)


def _pack(parts):
    out = []
    for a in parts:
        flat = a.reshape(-1)
        rows = -(-flat.shape[0] // 1024) * 8
        flat = jnp.pad(flat, (0, rows * 128 - flat.shape[0]))
        out.append(flat.reshape(rows, 128))
    return jnp.concatenate(out, axis=0)


def _unpack(packed, like):
    out, row = [], 0
    for a in like:
        size = 1
        for s in a.shape:
            size *= s
        rows = -(-size // 1024) * 8
        out.append(packed[row : row + rows].reshape(-1)[:size].reshape(a.shape))
        row += rows
    return out


def kernel(x, p, norm_g, w_in, ln_v_g, ln_v_b, w_s, b_s, conv_w, w_out, ple_norm_g, w_ple_gate, w_ple_proj, final_g, loss_target, m_norm_g, m_w_in, m_ln_v_g, m_ln_v_b, m_w_s, m_b_s, m_conv_w, m_w_out, m_ple_norm_g, m_w_ple_gate, m_w_ple_proj, m_final_g, v_norm_g, v_w_in, v_ln_v_g, v_ln_v_b, v_w_s, v_b_s, v_conv_w, v_w_out, v_ple_norm_g, v_w_ple_gate, v_w_ple_proj, v_final_g):
    cx, cy, cc = lax.axis_index("x"), lax.axis_index("y"), lax.axis_index("c")
    own = 2 * cx + cy
    ids = jnp.stack([cc, own]).astype(jnp.int32)

    cw_rows = jnp.transpose(conv_w, (0, 2, 1))
    g_in, g_out, g_gate, g_pp, g_cw = gather_weights(
        [w_in.astype(BF16), w_out.astype(BF16), w_ple_gate.astype(BF16), w_ple_proj.astype(BF16), cw_rows])
    cw_full = jnp.transpose(g_cw, (1, 2, 0, 3)).reshape(DEPTH, 3, WIDTH)
    tril = jnp.tril(jnp.ones((CHUNK, CHUNK), F32))
    ws_masked = w_s * tril[None, None]
    ws_b = ws_masked.astype(BF16)
    wsT_b = jnp.swapaxes(ws_masked, 2, 3).astype(BF16)
    bsb = jnp.repeat(jnp.swapaxes(b_s, 1, 2), CHUNK, axis=2)
    small = (norm_g[:, None, :], ln_v_g[:, None, :], ln_v_b[:, None, :], ws_b, bsb, cw_full, ple_norm_g[:, None, :])
    wts = (g_in, g_out, g_gate, g_pp)

    xs = [x[0]]
    saved = []
    for layer in range(DEPTH):
        outs = fwd_layer(layer, xs[-1], p, wts, small)
        saved.append(outs)
        xs.append(outs[4])
    dx, loss_part, g_final = loss_head(xs[-1], loss_target[0], final_g[None, :])
    loss = lax.psum(loss_part[0, 0], ("x", "y", "c"))

    big = [None] * 4
    small_g = [None] * DEPTH
    for layer in reversed(range(DEPTH)):
        proj, x2, gate, pp, _, hnT, catT, hn2T, pT = saved[layer]
        dpp_b, dgl_b, dx2, dx2_b, dcat, g_pg = bwd_ple(layer, dx, x2, gate, pp, g_gate, g_out, small[6])
        dx, dproj_b, g_ng, g_lg, g_lb, g_ws, g_bs, g_cwl = bwd_mix(layer, dcat, dx2, xs[layer], proj, g_in, small, wsT_b, tril)
        big[0] = wgrad(layer, hnT, dproj_b, COL_BLK, True, big[0], f"wgrad_in{layer}")
        big[1] = wgrad(layer, catT, dx2_b, 512, False, big[1], f"wgrad_out{layer}")
        big[2] = wgrad(layer, hn2T, dgl_b, 512, False, big[2], f"wgrad_gate{layer}")
        big[3] = wgrad(layer, pT, dpp_b, PLE, True, big[3], f"wgrad_pp{layer}")
        small_g[layer] = [g_ng, g_lg, g_lb, g_ws, g_bs, g_pg, g_cwl]

    recvd = rs_to_sibling(big)
    part_b, part_f = rs_add_sibling(ids, big, recvd)
    from_chips = rs_to_owners(part_b)
    halves = rs_add_owners(part_f, from_chips)
    gw_in, gw_out, gw_gate, gw_pp = rs_exchange_halves(halves)
    (d_in, d_out, d_gate, d_pp), (m_in, m_out, m_gate, m_pp), (v_in, v_out, v_gate, v_pp) = adamw_big(
        [w_in, w_out, w_ple_gate, w_ple_proj], [gw_in, gw_out, gw_gate, gw_pp],
        [m_w_in, m_w_out, m_w_ple_gate, m_w_ple_proj], [v_w_in, v_w_out, v_w_ple_gate, v_w_ple_proj])

    def stack(j):
        return jnp.stack([small_g[layer][j].reshape(small_like[j].shape[1:]) for layer in range(DEPTH)])

    small_like = [norm_g, ln_v_g, ln_v_b, w_s, b_s, ple_norm_g]
    grads_small = [stack(j) for j in range(6)] + [g_final.reshape(D_MODEL), jnp.stack([small_g[l][6] for l in range(DEPTH)])]
    zeros_cw = jnp.zeros((DEPTH, 3, WIDTH), F32)
    params = [norm_g, ln_v_g, ln_v_b, w_s, b_s, ple_norm_g, final_g, zeros_cw]
    ms = [m_norm_g, m_ln_v_g, m_ln_v_b, m_w_s, m_b_s, m_ple_norm_g, m_final_g, zeros_cw]
    vs = [v_norm_g, v_ln_v_g, v_ln_v_b, v_w_s, v_b_s, v_ple_norm_g, v_final_g, zeros_cw + 1.0]
    gsum, dsm, msm, vsm = allreduce_small(_pack(grads_small), _pack(params), _pack(ms), _pack(vs))
    gs = _unpack(gsum, params)
    ds = _unpack(dsm, params)
    m2s = _unpack(msm, params)
    v2s = _unpack(vsm, params)
    g_cw_own = lax.dynamic_slice_in_dim(gs[7], own * (WIDTH // N_CHIPS), WIDTH // N_CHIPS, axis=2)
    rows2 = lambda a: a.reshape(DEPTH * 3, WIDTH // N_CHIPS)
    d_cw, m_cw, v_cw = adamw_rows(rows2(cw_rows), rows2(g_cw_own), rows2(jnp.transpose(m_conv_w, (0, 2, 1))),
                                  rows2(jnp.transpose(v_conv_w, (0, 2, 1))))
    back = lambda a: jnp.transpose(a.reshape(DEPTH, 3, WIDTH // N_CHIPS), (0, 2, 1))
    g_conv = jnp.transpose(g_cw_own, (0, 2, 1))

    def ordered(sm, cw_v, w_in_v, w_out_v, gate_v, pp_v):
        return [sm[0], w_in_v, sm[1], sm[2], sm[3], sm[4], cw_v, w_out_v, sm[5], gate_v, pp_v, sm[6]]

    grads = ordered(gs, g_conv, gw_in, gw_out, gw_gate, gw_pp)
    deltas = ordered(ds, back(d_cw), d_in, d_out, d_gate, d_pp)
    new_m = ordered(m2s, back(m_cw), m_in, m_out, m_gate, m_pp)
    new_v = ordered(v2s, back(v_cw), v_in, v_out, v_gate, v_pp)
    return (loss, dx[None], *grads, *deltas, *new_m, *new_v)
```

```python
import jax
import jax.numpy as jnp
from jax import lax
from jax.experimental import pallas as pl
from jax.experimental.pallas import tpu as pltpu

F32 = jnp.float32
BF16 = jnp.bfloat16

SEQ = 8192
D_MODEL = 1024
WIDTH = 512
PROJ = 7 * WIDTH
N_CHIPS = 4
COL_BLK = PROJ // N_CHIPS
PLE = 256
HEADS = 4
CHUNK = 128
DEPTH = 2
EPS = 1e-6
TILE = 256
N_TILES = SEQ // TILE
HALO = 8
VMEM_LIMIT = 60 * 1024 * 1024

ADAM_LR, ADAM_B1, ADAM_B2, ADAM_EPS, ADAM_WD, ADAM_STEP = 0.001, 0.9, 0.999, 1e-08, 0.01, 10
ADAM_C1 = 1.0 - ADAM_B1**ADAM_STEP
ADAM_C2 = 1.0 - ADAM_B2**ADAM_STEP

MESH = pl.DeviceIdType.MESH
ANY = pl.BlockSpec(memory_space=pl.ANY)


def _mm(a, b):
    return lax.dot_general(a, b, (((1,), (0,)), ((), ())), preferred_element_type=F32)


def _mm_nt(a, b):
    return lax.dot_general(a, b, (((1,), (1,)), ((), ())), preferred_element_type=F32)


def _mm_rows(a, w_ref):
    blk = w_ref.shape[1]
    acc = _mm(a[:, 0:blk], w_ref[0])
    for k in range(1, N_CHIPS):
        acc = acc + _mm(a[:, k * blk : (k + 1) * blk], w_ref[k])
    return acc


def _mm_nt_rows(a, w_ref):
    return jnp.concatenate([_mm_nt(a, w_ref[k]) for k in range(N_CHIPS)], axis=-1)


def _sigmoid(z):
    return 1.0 / (1.0 + jnp.exp(-z))


def _rms_stats(x):
    r = lax.rsqrt(jnp.mean(x * x, axis=-1, keepdims=True) + EPS)
    return r, x * r


def _rms_bwd(dyg, xh, r):
    return r * (dyg - xh * jnp.mean(dyg * xh, axis=-1, keepdims=True))


def _ln_stats(v):
    mu = jnp.mean(v, axis=-1, keepdims=True)
    vc = v - mu
    rs = lax.rsqrt(jnp.mean(vc * vc, axis=-1, keepdims=True) + EPS)
    return rs, vc * rs


def _mesh_pos():
    x, y, c = lax.axis_index("x"), lax.axis_index("y"), lax.axis_index("c")
    return x, y, c, 2 * x + y


def _peer(x, y, c, r):
    return ((1 - x) if (r >> 1) else x, (1 - y) if (r & 1) else y, c)


def _full(shape):
    return pl.BlockSpec(shape, lambda *_: (0,) * len(shape))


def _const(shape, pos):
    return pl.BlockSpec((None,) * len(pos) + tuple(shape), lambda *_: tuple(pos) + (0,) * len(shape))


class Job:
    def __init__(self, ins, out_shapes, sems, start, middle, finish, aliases=None):
        self.ins, self.out_shapes, self.sems = list(ins), list(out_shapes), list(sems)
        self.start, self.middle, self.finish = start, middle, finish
        self.aliases = aliases or {}


def run_job(job, name):
    ni, no = len(job.ins), len(job.out_shapes)

    def body(*refs):
        parts = (refs[:ni], refs[ni : ni + no], refs[ni + no :])
        job.start(*parts)
        if job.middle is not None:
            job.middle(*parts)
        job.finish(*parts)

    return pl.pallas_call(
        body, name=name, out_shape=job.out_shapes, in_specs=[ANY] * ni, out_specs=[ANY] * no, scratch_shapes=job.sems,
        input_output_aliases=job.aliases, compiler_params=pltpu.CompilerParams(has_side_effects=True),
    )(*job.ins)


def _call(body, *, name, grid, in_specs, out_specs, out_shape, args, scratch=(), job=None, mid=None):
    params = pltpu.CompilerParams(dimension_semantics=("arbitrary",) * len(grid), vmem_limit_bytes=VMEM_LIMIT,
                                  has_side_effects=job is not None)
    n_in, n_out, n_sc = len(in_specs), len(out_specs), len(scratch)
    if job is None:
        outs = pl.pallas_call(body, name=name, grid=grid, in_specs=in_specs, out_specs=out_specs, out_shape=out_shape,
                              scratch_shapes=list(scratch), compiler_params=params)(*args)
        return list(outs), []
    ji, jo = len(job.ins), len(job.out_shapes)
    assert len(grid) == 1 and not job.aliases

    def wrapped(*refs):
        ins, jin = refs[:n_in], refs[n_in : n_in + ji]
        o0 = n_in + ji
        outs, jout = refs[o0 : o0 + n_out], refs[o0 + n_out : o0 + n_out + jo]
        s0 = o0 + n_out + jo
        sc, jsem = refs[s0 : s0 + n_sc], refs[s0 + n_sc :]
        i = pl.program_id(0)

        @pl.when(i == 0)
        def _():
            job.start(jin, jout, jsem)

        if job.middle is not None:
            @pl.when(i == mid)
            def _():
                job.middle(jin, jout, jsem)

        body(*ins, *outs, *sc)

        @pl.when(i == grid[0] - 1)
        def _():
            job.finish(jin, jout, jsem)

    outs = pl.pallas_call(
        wrapped, name=name, grid=grid, in_specs=list(in_specs) + [ANY] * ji, out_specs=list(out_specs) + [ANY] * jo,
        out_shape=list(out_shape) + job.out_shapes, scratch_shapes=list(scratch) + job.sems, compiler_params=params,
    )(*args, *job.ins)
    return list(outs[:n_out]), list(outs[n_out:])


def gather_job(items):
    n = len(items)

    def src_half(a, ref, h):
        arr, layer = items[a]
        if layer is None:
            return ref.at[h]
        rh = arr.shape[1] // 2
        return ref.at[layer, pl.ds(h * rh, rh)]

    def dst_half(a, ref, chip, h):
        arr, layer = items[a]
        if layer is None:
            return ref.at[chip, h]
        rh = arr.shape[1] // 2
        return ref.at[chip, pl.ds(h * rh, rh)]

    def copies(src, dst, sems):
        loc_sem, ici_s, ici_r, fwd_s, fwd_r = sems
        x, y, c, own = _mesh_pos()
        local, ici, fwd, got = [], {}, {}, {}
        for a in range(n):
            layer = items[a][1]
            local.append(pltpu.make_async_copy(src[a] if layer is None else src[a].at[layer], dst[a].at[own], loc_sem.at[a]))
            for r in (1, 2, 3):
                k = 3 * a + r - 1
                ici[a, r] = pltpu.make_async_remote_copy(
                    src_ref=src_half(a, src[a], c), dst_ref=dst_half(a, dst[a], own, c), send_sem=ici_s.at[k],
                    recv_sem=ici_r.at[k], device_id=_peer(x, y, c, r), device_id_type=MESH)
                fwd[a, r] = pltpu.make_async_remote_copy(
                    src_ref=dst_half(a, dst[a], own ^ r, c), dst_ref=dst_half(a, dst[a], own ^ r, c), send_sem=fwd_s.at[k],
                    recv_sem=fwd_r.at[k], device_id=(x, y, 1 - c), device_id_type=MESH)
                got[a, r] = pltpu.make_async_remote_copy(
                    src_ref=dst_half(a, dst[a], own ^ r, 1 - c), dst_ref=dst_half(a, dst[a], own ^ r, 1 - c),
                    send_sem=fwd_s.at[k], recv_sem=fwd_r.at[k], device_id=(x, y, 1 - c), device_id_type=MESH)
        return local, ici, fwd, got

    def start(src, dst, sems):
        local, ici, _, _ = copies(src, dst, sems)
        for cp in local:
            cp.start()
        for a in range(n):
            for r in (1, 2, 3):
                ici[a, r].start()

    def middle(src, dst, sems):
        _, ici, fwd, _ = copies(src, dst, sems)
        for r in (1, 2, 3):
            for a in range(n):
                ici[a, r].wait_recv()
                fwd[a, r].start()

    def finish(src, dst, sems):
        local, ici, fwd, got = copies(src, dst, sems)
        for a in range(n):
            for r in (1, 2, 3):
                got[a, r].wait_recv()
        for a in range(n):
            for r in (1, 2, 3):
                ici[a, r].wait_send()
                fwd[a, r].wait_send()
        for cp in local:
            cp.wait()

    out_shapes = [jax.ShapeDtypeStruct((N_CHIPS,) + (arr.shape if layer is None else arr.shape[1:]), arr.dtype)
                  for arr, layer in items]
    sems = [pltpu.SemaphoreType.DMA((n,))] + [pltpu.SemaphoreType.DMA((3 * n,))] * 4
    return Job([arr for arr, _ in items], out_shapes, sems, start, middle, finish)


def _mixer_fwd(proj_ref, lg, lb, ws_ref, bsb_ref, cw_ref, mix_ref, xcbuf, halo_xc):
    u = proj_ref[:, 0 * WIDTH : 1 * WIDTH]
    v = proj_ref[:, 1 * WIDTH : 2 * WIDTH]
    za = proj_ref[:, 2 * WIDTH : 3 * WIDTH]
    h = proj_ref[:, 3 * WIDTH : 4 * WIDTH]
    gb = proj_ref[:, 4 * WIDTH : 5 * WIDTH]
    gc = proj_ref[:, 5 * WIDTH : 6 * WIDTH]
    zb = proj_ref[:, 6 * WIDTH : 7 * WIDTH]
    rs, vhat = _ln_stats(v)
    vl = vhat * lg + lb
    vlb = vl.astype(BF16)
    for j in range(TILE // CHUNK):
        rows = slice(j * CHUNK, (j + 1) * CHUNK)
        for hd in range(HEADS):
            cols = slice(hd * CHUNK, (hd + 1) * CHUNK)
            mix_ref[rows, cols] = _mm(ws_ref[hd], vlb[rows, cols]) + bsb_ref[:, cols]
    mixed = mix_ref[...]
    siga = _sigmoid(za)
    sigb = _sigmoid(zb)
    xc = gc * h
    xcbuf[0:HALO, :] = halo_xc
    xcbuf[HALO : HALO + TILE, :] = xc
    y = cw_ref[0:1, :] * xcbuf[HALO - 2 : HALO - 2 + TILE, :] + cw_ref[1:2, :] * xcbuf[HALO - 1 : HALO - 1 + TILE, :]
    y = y + cw_ref[2:3, :] * xc
    return dict(u=u, za=za, h=h, gb=gb, gc=gc, zb=zb, rs=rs, vhat=vhat, vlb=vlb, mixed=mixed, siga=siga, sigb=sigb,
                xc=xc, y=y)


def fwd_layer(layer, x, p, wts, small, job=None):
    win, wout, wg, wpp = wts
    ng, lg, lb, ws, bsb, cw, pg = small

    def body(x_ref, p_ref, win_ref, wout_ref, wg_ref, wpp_ref, ng_ref, lg_ref, lb_ref, ws_ref, bsb_ref, cw_ref, pg_ref,
             proj_ref, x2_ref, gate_ref, pp_ref, x3_ref, hnT_ref, catT_ref, hn2T_ref, pT_ref, mix_ref, xcbuf, carry):
        i = pl.program_id(0)

        @pl.when(i == 0)
        def _():
            carry[...] = jnp.zeros_like(carry)

        xv = x_ref[...]
        r1, xh = _rms_stats(xv)
        hn = xh * ng_ref[...]
        hnb = hn.astype(BF16)
        hnT_ref[...] = hn.T.astype(BF16)
        for k in range(N_CHIPS):
            proj_ref[:, k * COL_BLK : (k + 1) * COL_BLK] = _mm(hnb, win_ref[k])
        m = _mixer_fwd(proj_ref, lg_ref[...], lb_ref[...], ws_ref, bsb_ref, cw_ref, mix_ref, xcbuf, carry[...])
        carry[...] = m["xc"][TILE - HALO : TILE, :]
        out_a = (m["u"] * m["mixed"]) * (m["za"] * m["siga"])
        out_b = (m["gb"] * m["y"]) * (m["zb"] * m["sigb"])
        cat = jnp.concatenate([out_a, out_b], axis=-1)
        catT_ref[...] = cat.T.astype(BF16)
        x2 = xv + _mm_rows(cat.astype(BF16), wout_ref)
        x2_ref[...] = x2
        r2, xh2 = _rms_stats(x2)
        hn2 = xh2 * pg_ref[...]
        hn2T_ref[...] = hn2.T.astype(BF16)
        gate = _sigmoid(_mm_rows(hn2.astype(BF16), wg_ref))
        gate_ref[...] = gate
        pv = p_ref[...]
        pT_ref[...] = pv.T.astype(BF16)
        pb = pv.astype(BF16)
        for k in range(N_CHIPS):
            pp_ref[:, k * PLE : (k + 1) * PLE] = _mm(pb, wpp_ref[k])
        x3_ref[...] = x2 + gate * pp_ref[...]

    tok = lambda w: pl.BlockSpec((TILE, w), lambda i: (i, 0))
    tokT = lambda w: pl.BlockSpec((w, TILE), lambda i: (0, i))
    f32 = lambda w: jax.ShapeDtypeStruct((SEQ, w), F32)
    bfT = lambda w: jax.ShapeDtypeStruct((w, SEQ), BF16)
    return _call(
        body, name=f"fwd_layer{layer}", grid=(N_TILES,),
        in_specs=[tok(D_MODEL), pl.BlockSpec((None, None, TILE, PLE), lambda i: (layer, 0, i, 0)),
                  _full(win.shape), _full(wout.shape), _full(wg.shape), _full(wpp.shape),
                  _const((1, D_MODEL), (layer,)), _const((1, WIDTH), (layer,)), _const((1, WIDTH), (layer,)),
                  _const((HEADS, CHUNK, CHUNK), (layer,)), _const((CHUNK, WIDTH), (layer,)), _const((3, WIDTH), (layer,)),
                  _const((1, D_MODEL), (layer,))],
        out_specs=[tok(PROJ), tok(D_MODEL), tok(D_MODEL), tok(D_MODEL), tok(D_MODEL),
                   tokT(D_MODEL), tokT(D_MODEL), tokT(D_MODEL), tokT(PLE)],
        out_shape=[f32(PROJ), f32(D_MODEL), f32(D_MODEL), f32(D_MODEL), f32(D_MODEL),
                   bfT(D_MODEL), bfT(D_MODEL), bfT(D_MODEL), bfT(PLE)],
        scratch=[pltpu.VMEM((TILE, WIDTH), F32), pltpu.VMEM((HALO + TILE, WIDTH), F32), pltpu.VMEM((HALO, WIDTH), F32)],
        args=(x, p, win, wout, wg, wpp, ng, lg, lb, ws, bsb, cw, pg), job=job, mid=N_TILES // 2)


def loss_head(x, tgt, gf):
    def body(x_ref, t_ref, g_ref, dx_ref, loss_ref, gg_ref):
        i = pl.program_id(0)

        @pl.when(i == 0)
        def _():
            loss_ref[...] = jnp.zeros_like(loss_ref)
            gg_ref[...] = jnp.zeros_like(gg_ref)

        r, xh = _rms_stats(x_ref[...])
        g = g_ref[...]
        err = xh * g - t_ref[...]
        loss_ref[...] += (0.5 / D_MODEL) * jnp.sum(err * err).reshape(1, 1)
        dy = err * (1.0 / D_MODEL)
        gg_ref[...] += jnp.sum(dy * xh, axis=0, keepdims=True)
        dx_ref[...] = _rms_bwd(dy * g, xh, r)

    tile = 512
    tok = pl.BlockSpec((tile, D_MODEL), lambda i: (i, 0))
    outs, _ = _call(
        body, name="loss_head", grid=(SEQ // tile,), in_specs=[tok, tok, _full((1, D_MODEL))],
        out_specs=[tok, _full((1, 1)), _full((1, D_MODEL))],
        out_shape=[jax.ShapeDtypeStruct((SEQ, D_MODEL), F32), jax.ShapeDtypeStruct((1, 1), F32),
                   jax.ShapeDtypeStruct((1, D_MODEL), F32)], args=(x, tgt, gf))
    return outs


def bwd_ple(layer, dx3, x2, gate, pp, wg, wout, pg, job=None):
    def body(dx3_ref, x2_ref, gate_ref, pp_ref, wg_ref, wout_ref, pg_ref,
             dpp_ref, dgl_ref, dx2_ref, dx2b_ref, dcat_ref, gpg_ref):
        i = pl.program_id(0)

        @pl.when(i == 0)
        def _():
            gpg_ref[...] = jnp.zeros_like(gpg_ref)

        dx3v = dx3_ref[...]
        gate_v = gate_ref[...]
        dpp_ref[...] = (dx3v * gate_v).astype(BF16)
        dgl = ((dx3v * pp_ref[...]) * gate_v * (1.0 - gate_v)).astype(BF16)
        dgl_ref[...] = dgl
        dhn2 = _mm_nt_rows(dgl, wg_ref)
        r2, xh2 = _rms_stats(x2_ref[...])
        gpg_ref[...] += jnp.sum(dhn2 * xh2, axis=0, keepdims=True)
        dx2 = dx3v + _rms_bwd(dhn2 * pg_ref[...], xh2, r2)
        dx2_ref[...] = dx2
        dx2b = dx2.astype(BF16)
        dx2b_ref[...] = dx2b
        dcat_ref[...] = _mm_nt_rows(dx2b, wout_ref)

    tok = pl.BlockSpec((TILE, D_MODEL), lambda i: (i, 0))
    f32 = jax.ShapeDtypeStruct((SEQ, D_MODEL), F32)
    b16 = jax.ShapeDtypeStruct((SEQ, D_MODEL), BF16)
    return _call(
        body, name=f"bwd_ple{layer}", grid=(N_TILES,),
        in_specs=[tok, tok, tok, tok, _full(wg.shape), _full(wout.shape), _const((1, D_MODEL), (layer,))],
        out_specs=[tok, tok, tok, tok, tok, _full((1, D_MODEL))],
        out_shape=[b16, b16, f32, b16, f32, jax.ShapeDtypeStruct((1, D_MODEL), F32)],
        args=(dx3, x2, gate, pp, wg, wout, pg), job=job)


def bwd_mix(layer, dcat, dx2, x, proj, win, small, wsT, tril):
    ng, lg, lb, ws, bsb, cw, _ = small

    def body(dcat_ref, dx2_ref, x_ref, proj_ref, halo_ref, win_ref, ng_ref, lg_ref, lb_ref, ws_ref, wsT_ref, bsb_ref,
             cw_ref, tril_ref,
             dx_ref, dproj_ref, gn_ref, glg_ref, glb_ref, gws_ref, gbs_ref, gcw_ref,
             mix_ref, xcbuf, dycbuf, dvl_ref, bs_acc):
        i = pl.program_id(0)

        @pl.when(i == 0)
        def _():
            for ref in (gn_ref, glg_ref, glb_ref, gws_ref, gcw_ref, bs_acc):
                ref[...] = jnp.zeros_like(ref)
            dycbuf[TILE : TILE + HALO, :] = jnp.zeros((HALO, WIDTH), F32)

        lgv = lg_ref[...]
        halo_xc = halo_ref[:, 5 * WIDTH : 6 * WIDTH] * halo_ref[:, 3 * WIDTH : 4 * WIDTH]
        halo_xc = jnp.where(i == N_TILES - 1, 0.0, halo_xc)
        m = _mixer_fwd(proj_ref, lgv, lb_ref[...], ws_ref, bsb_ref, cw_ref, mix_ref, xcbuf, halo_xc)
        u, za, h, gb, gc, zb = m["u"], m["za"], m["h"], m["gb"], m["gc"], m["zb"]
        mixed, siga, sigb, xc, y = m["mixed"], m["siga"], m["sigb"], m["xc"], m["y"]
        doa = dcat_ref[:, 0:WIDTH]
        dob = dcat_ref[:, WIDTH : 2 * WIDTH]
        sa = za * siga
        sb = zb * sigb
        doa_sa = doa * sa
        dproj_ref[:, 0 * WIDTH : 1 * WIDTH] = (doa_sa * mixed).astype(BF16)
        dmixed = doa_sa * u
        dza = (doa * (u * mixed)) * (siga * (1.0 + za * (1.0 - siga)))
        dproj_ref[:, 2 * WIDTH : 3 * WIDTH] = dza.astype(BF16)
        dob_sb = dob * sb
        dproj_ref[:, 4 * WIDTH : 5 * WIDTH] = (dob_sb * y).astype(BF16)
        dyc = dob_sb * gb
        dzb = (dob * (gb * y)) * (sigb * (1.0 + zb * (1.0 - sigb)))
        dproj_ref[:, 6 * WIDTH : 7 * WIDTH] = dzb.astype(BF16)
        dycbuf[0:TILE, :] = dyc
        dyc1 = dycbuf[1 : 1 + TILE, :]
        dyc2 = dycbuf[2 : 2 + TILE, :]
        dxc = cw_ref[2:3, :] * dyc + cw_ref[1:2, :] * dyc1 + cw_ref[0:1, :] * dyc2
        gcw_ref[0:1, :] += jnp.sum(xc * dyc2, axis=0, keepdims=True)
        gcw_ref[1:2, :] += jnp.sum(xc * dyc1, axis=0, keepdims=True)
        gcw_ref[2:3, :] += jnp.sum(xc * dyc, axis=0, keepdims=True)
        dycbuf[TILE : TILE + HALO, :] = dyc[0:HALO, :]
        dproj_ref[:, 5 * WIDTH : 6 * WIDTH] = (dxc * h).astype(BF16)
        dproj_ref[:, 3 * WIDTH : 4 * WIDTH] = (dxc * gc).astype(BF16)
        dmb = dmixed.astype(BF16)
        vlb = m["vlb"]
        bsum = jnp.zeros((CHUNK, WIDTH), F32)
        for j in range(TILE // CHUNK):
            rows = slice(j * CHUNK, (j + 1) * CHUNK)
            bsum = bsum + dmixed[rows, :]
            for hd in range(HEADS):
                cols = slice(hd * CHUNK, (hd + 1) * CHUNK)
                gws_ref[hd] += _mm_nt(dmb[rows, cols], vlb[rows, cols])
                dvl_ref[rows, cols] = _mm(wsT_ref[hd], dmb[rows, cols])
        bs_acc[...] += bsum
        dvl = dvl_ref[...]
        vhat = m["vhat"]
        glb_ref[...] += jnp.sum(dvl, axis=0, keepdims=True)
        glg_ref[...] += jnp.sum(dvl * vhat, axis=0, keepdims=True)
        dvh = dvl * lgv
        dv = m["rs"] * (dvh - jnp.mean(dvh, axis=-1, keepdims=True) - vhat * jnp.mean(dvh * vhat, axis=-1, keepdims=True))
        dproj_ref[:, 1 * WIDTH : 2 * WIDTH] = dv.astype(BF16)
        dhn = _mm_nt(dproj_ref[:, 0:COL_BLK], win_ref[0])
        for k in range(1, N_CHIPS):
            dhn = dhn + _mm_nt(dproj_ref[:, k * COL_BLK : (k + 1) * COL_BLK], win_ref[k])
        r1, xh = _rms_stats(x_ref[...])
        gn_ref[...] += jnp.sum(dhn * xh, axis=0, keepdims=True)
        dx_ref[...] = dx2_ref[...] + _rms_bwd(dhn * ng_ref[...], xh, r1)

        @pl.when(i == N_TILES - 1)
        def _():
            for hd in range(HEADS):
                gws_ref[hd] = gws_ref[hd] * tril_ref[...]
                gbs_ref[hd : hd + 1, :] = jnp.sum(bs_acc[:, hd * CHUNK : (hd + 1) * CHUNK].T, axis=0, keepdims=True)

    rev = lambda w: pl.BlockSpec((TILE, w), lambda i: (N_TILES - 1 - i, 0))
    halo = pl.BlockSpec((HALO, PROJ), lambda i: (jnp.maximum((N_TILES - 1 - i) * (TILE // HALO) - 1, 0), 0))
    outs, _ = _call(
        body, name=f"bwd_mix{layer}", grid=(N_TILES,),
        in_specs=[rev(D_MODEL), rev(D_MODEL), rev(D_MODEL), rev(PROJ), halo, _full(win.shape),
                  _const((1, D_MODEL), (layer,)), _const((1, WIDTH), (layer,)), _const((1, WIDTH), (layer,)),
                  _const((HEADS, CHUNK, CHUNK), (layer,)), _const((HEADS, CHUNK, CHUNK), (layer,)),
                  _const((CHUNK, WIDTH), (layer,)), _const((3, WIDTH), (layer,)), _full((CHUNK, CHUNK))],
        out_specs=[rev(D_MODEL), rev(PROJ), _full((1, D_MODEL)), _full((1, WIDTH)), _full((1, WIDTH)),
                   _full((HEADS, CHUNK, CHUNK)), _full((HEADS, CHUNK)), _full((3, WIDTH))],
        out_shape=[jax.ShapeDtypeStruct((SEQ, D_MODEL), F32), jax.ShapeDtypeStruct((SEQ, PROJ), BF16),
                   jax.ShapeDtypeStruct((1, D_MODEL), F32), jax.ShapeDtypeStruct((1, WIDTH), F32),
                   jax.ShapeDtypeStruct((1, WIDTH), F32), jax.ShapeDtypeStruct((HEADS, CHUNK, CHUNK), F32),
                   jax.ShapeDtypeStruct((HEADS, CHUNK), F32), jax.ShapeDtypeStruct((3, WIDTH), F32)],
        scratch=[pltpu.VMEM((TILE, WIDTH), F32), pltpu.VMEM((HALO + TILE, WIDTH), F32),
                 pltpu.VMEM((TILE + HALO, WIDTH), F32), pltpu.VMEM((TILE, WIDTH), F32), pltpu.VMEM((CHUNK, WIDTH), F32)],
        args=(dcat, dx2, x, proj, proj, win, ng, lg, lb, ws, wsT, bsb, cw, tril))
    return outs


def wgrad(aT, b, n_tiles, col_blocked, tk, name):
    m_dim, n_dim = aT.shape[0], b.shape[1]
    tn = n_dim // n_tiles
    cb, mb = n_dim // N_CHIPS, m_dim // N_CHIPS
    per_tile = N_CHIPS // n_tiles
    assert col_blocked or n_tiles == 1

    def body(a_ref, b_ref, o_ref):
        @pl.when(pl.program_id(1) == 0)
        def _():
            o_ref[...] = jnp.zeros_like(o_ref)

        prod = _mm(a_ref[...], b_ref[...])
        if col_blocked:
            for q in range(per_tile):
                o_ref[q] += prod[:, q * cb : (q + 1) * cb]
        else:
            for q in range(N_CHIPS):
                o_ref[q] += prod[q * mb : (q + 1) * mb, :]

    if col_blocked:
        shape = (N_CHIPS, m_dim, cb)
        o_spec = pl.BlockSpec((per_tile, m_dim, cb), lambda n, k: (n, 0, 0))
    else:
        shape = (N_CHIPS, mb, n_dim)
        o_spec = pl.BlockSpec((N_CHIPS, mb, n_dim), lambda n, k: (0, 0, 0))
    return pl.pallas_call(
        body, name=name, grid=(n_tiles, SEQ // tk),
        in_specs=[pl.BlockSpec((m_dim, tk), lambda n, k: (0, k)), pl.BlockSpec((tk, tn), lambda n, k: (k, n))],
        out_specs=o_spec, out_shape=jax.ShapeDtypeStruct(shape, F32),
        compiler_params=pltpu.CompilerParams(dimension_semantics=("arbitrary", "arbitrary"), vmem_limit_bytes=VMEM_LIMIT),
    )(aT, b)


def to_sibling_job(grads):
    n = len(grads)

    def copies(src, dst, sems):
        x, y, c, _ = _mesh_pos()
        out = []
        for a in range(n):
            rh = grads[a].shape[1] // 2
            out.append(pltpu.make_async_remote_copy(
                src_ref=src[a].at[:, pl.ds((1 - c) * rh, rh), :], dst_ref=dst[a], send_sem=sems[0].at[a],
                recv_sem=sems[1].at[a], device_id=(x, y, 1 - c), device_id_type=MESH))
        return out

    def start(src, dst, sems):
        for cp in copies(src, dst, sems):
            cp.start()

    def finish(src, dst, sems):
        for cp in copies(src, dst, sems):
            cp.wait()

    out_shapes = [jax.ShapeDtypeStruct((N_CHIPS, g.shape[1] // 2, g.shape[2]), F32) for g in grads]
    return Job(grads, out_shapes, [pltpu.SemaphoreType.DMA((n,))] * 2, start, None, finish)


def rs_add_sibling(ids, grads, recvd, name):
    n = len(grads)

    def body(ids_ref, *refs):
        g, r = refs[:n], refs[n : 2 * n]
        pb, pf = refs[2 * n : 3 * n], refs[3 * n :]
        k = pl.program_id(0)
        for a in range(n):
            s = g[a][...] + r[a][...]
            pb[a][...] = s.astype(BF16)

            @pl.when(k == ids_ref[1])
            def _():
                pf[a][...] = s

    in_specs, out_specs, out_shape = [], [], []
    for g in grads:
        rh, cc = g.shape[1] // 2, g.shape[2]
        in_specs.append(pl.BlockSpec((None, rh, cc), lambda k, ids: (k, ids[0], 0)))
    for g in grads:
        rh, cc = g.shape[1] // 2, g.shape[2]
        in_specs.append(pl.BlockSpec((None, rh, cc), lambda k, ids: (k, 0, 0)))
        out_specs.append(pl.BlockSpec((None, rh, cc), lambda k, ids: (k, 0, 0)))
        out_shape.append(jax.ShapeDtypeStruct((N_CHIPS, rh, cc), BF16))
    for g in grads:
        rh, cc = g.shape[1] // 2, g.shape[2]
        out_specs.append(pl.BlockSpec((rh, cc), lambda k, ids: (0, 0)))
        out_shape.append(jax.ShapeDtypeStruct((rh, cc), F32))
    outs = pl.pallas_call(
        body, name=name, out_shape=out_shape,
        grid_spec=pltpu.PrefetchScalarGridSpec(num_scalar_prefetch=1, grid=(N_CHIPS,), in_specs=in_specs,
                                               out_specs=out_specs),
        compiler_params=pltpu.CompilerParams(dimension_semantics=("arbitrary",), vmem_limit_bytes=VMEM_LIMIT),
    )(ids, *grads, *recvd)
    return list(outs[:n]), list(outs[n:])


def to_owners_job(partials):
    n = len(partials)

    def copies(src, dst, sems):
        x, y, c, own = _mesh_pos()
        out = []
        for a in range(n):
            for r in (1, 2, 3):
                out.append(pltpu.make_async_remote_copy(
                    src_ref=src[a].at[own ^ r], dst_ref=dst[a].at[r - 1], send_sem=sems[0].at[3 * a + r - 1],
                    recv_sem=sems[1].at[3 * a + r - 1], device_id=_peer(x, y, c, r), device_id_type=MESH))
        return out

    def start(src, dst, sems):
        for cp in copies(src, dst, sems):
            cp.start()

    def finish(src, dst, sems):
        for cp in copies(src, dst, sems):
            cp.wait()

    out_shapes = [jax.ShapeDtypeStruct((3,) + p.shape[1:], BF16) for p in partials]
    return Job(partials, out_shapes, [pltpu.SemaphoreType.DMA((3 * n,))] * 2, start, None, finish)


def rs_add_owners(own_f32, recvd, name):
    n = len(own_f32)
    nb = 2

    def body(*refs):
        o, r, f = refs[:n], refs[n : 2 * n], refs[2 * n :]
        for a in range(n):
            f[a][...] = ((o[a][...] + r[a][0].astype(F32)) + r[a][1].astype(F32)) + r[a][2].astype(F32)

    in_specs, out_specs, out_shape = [], [], []
    for o in own_f32:
        in_specs.append(pl.BlockSpec((o.shape[0] // nb, o.shape[1]), lambda j: (j, 0)))
    for o in own_f32:
        in_specs.append(pl.BlockSpec((3, o.shape[0] // nb, o.shape[1]), lambda j: (0, j, 0)))
        out_specs.append(pl.BlockSpec((o.shape[0] // nb, o.shape[1]), lambda j: (j, 0)))
        out_shape.append(jax.ShapeDtypeStruct(o.shape, F32))
    outs, _ = _call(body, name=name, grid=(nb,), in_specs=in_specs, out_specs=out_specs, out_shape=out_shape,
                    args=(*own_f32, *recvd))
    return outs


def exchange_halves_job(layer, halves, prev):
    n = len(halves)

    def copies(src, dst, sems):
        x, y, c, _ = _mesh_pos()
        out = []
        for a in range(n):
            rh = halves[a].shape[0]
            rows = dst[a].at[layer, pl.ds(c * rh, rh), :]
            out.append(pltpu.make_async_copy(src[a], rows, sems[0].at[a]))
            out.append(pltpu.make_async_remote_copy(
                src_ref=src[a], dst_ref=rows, send_sem=sems[1].at[a], recv_sem=sems[2].at[a], device_id=(x, y, 1 - c),
                device_id_type=MESH))
        return out

    def start(src, dst, sems):
        for cp in copies(src, dst, sems):
            cp.start()

    def finish(src, dst, sems):
        for cp in copies(src, dst, sems):
            cp.wait()

    out_shapes = [jax.ShapeDtypeStruct((DEPTH, 2 * h.shape[0], h.shape[1]), F32) for h in halves]
    ins, aliases = list(halves), {}
    if prev is not None:
        ins += list(prev)
        aliases = {n + a: a for a in range(n)}
    return Job(ins, out_shapes, [pltpu.SemaphoreType.DMA((n,))] * 3, start, None, finish, aliases)


def _adamw(w, g, m, v):
    m2 = ADAM_B1 * m + (1.0 - ADAM_B1) * g
    v2 = ADAM_B2 * v + (1.0 - ADAM_B2) * (g * g)
    delta = -ADAM_LR * ((m2 / ADAM_C1) / (jnp.sqrt(v2 / ADAM_C2) + ADAM_EPS) + ADAM_WD * w)
    return delta, m2, v2


def adamw_big(ws, gs, ms, vs):
    n = len(ws)
    nb = 4

    def body(*refs):
        for a in range(n):
            w, g, m, v = (refs[j * n + a][...] for j in range(4))
            d, m2, v2 = _adamw(w, g, m, v)
            refs[4 * n + a][...] = d
            refs[5 * n + a][...] = m2
            refs[6 * n + a][...] = v2

    specs = [pl.BlockSpec((None, w.shape[1] // nb, w.shape[2]), lambda l, j: (l, j, 0)) for w in ws]
    shapes = [jax.ShapeDtypeStruct(w.shape, F32) for w in ws]
    outs = pl.pallas_call(
        body, name="adamw_big", grid=(DEPTH, nb), in_specs=specs * 4, out_specs=specs * 3, out_shape=shapes * 3,
        compiler_params=pltpu.CompilerParams(dimension_semantics=("arbitrary", "arbitrary"), vmem_limit_bytes=VMEM_LIMIT),
    )(*ws, *gs, *ms, *vs)
    return outs[:n], outs[n : 2 * n], outs[2 * n :]


def allreduce_small(g, w, m, v):
    rows = g.shape[0]

    def body(g_ref, w_ref, m_ref, v_ref, gsum_ref, d_ref, m2_ref, v2_ref, sib_buf, slots, s1, r1, s2, r2):
        x, y, c, own = _mesh_pos()
        first = pltpu.make_async_remote_copy(src_ref=g_ref, dst_ref=sib_buf, send_sem=s1, recv_sem=r1,
                                             device_id=(x, y, 1 - c), device_id_type=MESH)
        first.start()
        first.wait()
        slots[0] = g_ref[...] + sib_buf[...]
        cps = [pltpu.make_async_remote_copy(src_ref=slots.at[0], dst_ref=slots.at[r], send_sem=s2.at[r - 1],
                                            recv_sem=r2.at[r - 1], device_id=_peer(x, y, c, r), device_id_type=MESH)
               for r in (1, 2, 3)]
        for cp in cps:
            cp.start()
        for cp in cps:
            cp.wait()
        tot = slots[own] + slots[own ^ 1]
        tot = tot + slots[own ^ 2]
        tot = tot + slots[own ^ 3]
        gsum_ref[...] = tot
        d, m2, v2 = _adamw(w_ref[...], tot, m_ref[...], v_ref[...])
        d_ref[...] = d
        m2_ref[...] = m2
        v2_ref[...] = v2

    vm = pl.BlockSpec(memory_space=pltpu.VMEM)
    shape = jax.ShapeDtypeStruct(g.shape, F32)
    return pl.pallas_call(
        body, name="allreduce_small", out_shape=[shape] * 4, in_specs=[vm] * 4, out_specs=[vm] * 4,
        scratch_shapes=[pltpu.VMEM((rows, 128), F32), pltpu.VMEM((N_CHIPS, rows, 128), F32),
                        pltpu.SemaphoreType.DMA, pltpu.SemaphoreType.DMA,
                        pltpu.SemaphoreType.DMA((3,)), pltpu.SemaphoreType.DMA((3,))],
        compiler_params=pltpu.CompilerParams(has_side_effects=True),
    )(g, w, m, v)


def adamw_rows(w, g, m, v):
    def body(w_ref, g_ref, m_ref, v_ref, d_ref, m2_ref, v2_ref):
        d, m2, v2 = _adamw(w_ref[...], g_ref[...], m_ref[...], v_ref[...])
        d_ref[...] = d
        m2_ref[...] = m2
        v2_ref[...] = v2

    return pl.pallas_call(body, name="adamw_conv", out_shape=[jax.ShapeDtypeStruct(w.shape, F32)] * 3)(w, g, m, v)


def _pack(parts):
    out = []
    for a in parts:
        flat = a.reshape(-1)
        rows = -(-flat.shape[0] // 1024) * 8
        flat = jnp.pad(flat, (0, rows * 128 - flat.shape[0]))
        out.append(flat.reshape(rows, 128))
    return jnp.concatenate(out, axis=0)


def _unpack(packed, like):
    out, row = [], 0
    for a in like:
        size = 1
        for s in a.shape:
            size *= s
        rows = -(-size // 1024) * 8
        out.append(packed[row : row + rows].reshape(-1)[:size].reshape(a.shape))
        row += rows
    return out


def kernel(x, p, norm_g, w_in, ln_v_g, ln_v_b, w_s, b_s, conv_w, w_out, ple_norm_g, w_ple_gate, w_ple_proj, final_g, loss_target, m_norm_g, m_w_in, m_ln_v_g, m_ln_v_b, m_w_s, m_b_s, m_conv_w, m_w_out, m_ple_norm_g, m_w_ple_gate, m_w_ple_proj, m_final_g, v_norm_g, v_w_in, v_ln_v_g, v_ln_v_b, v_w_s, v_b_s, v_conv_w, v_w_out, v_ple_norm_g, v_w_ple_gate, v_w_ple_proj, v_final_g):
    cx, cy, cc = lax.axis_index("x"), lax.axis_index("y"), lax.axis_index("c")
    own = 2 * cx + cy
    ids = jnp.stack([cc, own]).astype(jnp.int32)

    cw_rows = jnp.transpose(conv_w, (0, 2, 1))
    shards = [w_in.astype(BF16), w_out.astype(BF16), w_ple_gate.astype(BF16), w_ple_proj.astype(BF16)]
    *wts0, g_cw = run_job(gather_job([(s, 0) for s in shards] + [(cw_rows, None)]), "gather_weights0")
    cw_full = jnp.transpose(g_cw, (1, 2, 0, 3)).reshape(DEPTH, 3, WIDTH)
    tril = jnp.tril(jnp.ones((CHUNK, CHUNK), F32))
    ws_masked = w_s * tril[None, None]
    ws_b = ws_masked.astype(BF16)
    wsT_b = jnp.swapaxes(ws_masked, 2, 3).astype(BF16)
    bsb = jnp.repeat(jnp.swapaxes(b_s, 1, 2), CHUNK, axis=2)
    small = (norm_g[:, None, :], ln_v_g[:, None, :], ln_v_b[:, None, :], ws_b, bsb, cw_full, ple_norm_g[:, None, :])

    saved0, wts1 = fwd_layer(0, x[0], p, wts0, small, job=gather_job([(s, 1) for s in shards]))
    saved1, _ = fwd_layer(1, saved0[4], p, wts1, small)
    wts, saved, xs = [wts0, wts1], [saved0, saved1], [x[0], saved0[4]]
    dx, loss_part, g_final = loss_head(saved1[4], loss_target[0], final_g[None, :])
    loss = lax.psum(loss_part[0, 0], ("x", "y", "c"))

    small_g = [None] * DEPTH
    full = None
    pending = None
    for layer in reversed(range(DEPTH)):
        proj, x2, gate, pp, _, hnT, catT, hn2T, pT = saved[layer]
        g_in, g_out, g_gate, g_pp = wts[layer]
        (dpp_b, dgl_b, dx2, dx2_b, dcat, g_pg), from_chips = bwd_ple(
            layer, dx, x2, gate, pp, g_gate, g_out, small[6], job=None if pending is None else to_owners_job(pending[0]))
        if pending is not None:
            halves = rs_add_owners(pending[1], from_chips, f"rs_add_owners{layer + 1}")
            full = run_job(exchange_halves_job(layer + 1, halves, full), f"rs_exchange_halves{layer + 1}")
        dx, dproj_b, g_ng, g_lg, g_lb, g_ws, g_bs, g_cwl = bwd_mix(layer, dcat, dx2, xs[layer], proj, g_in, small, wsT_b, tril)
        small_g[layer] = [g_ng, g_lg, g_lb, g_ws, g_bs, g_pg, g_cwl]
        big = [wgrad(hnT, dproj_b, 2, True, 1024, f"wgrad_in{layer}"),
               wgrad(catT, dx2_b, 1, False, 1024, f"wgrad_out{layer}"),
               wgrad(hn2T, dgl_b, 1, False, 1024, f"wgrad_gate{layer}"),
               wgrad(pT, dpp_b, 1, True, 2048, f"wgrad_pp{layer}")]
        recvd = run_job(to_sibling_job(big), f"rs_to_sibling{layer}")
        pending = rs_add_sibling(ids, big, recvd, f"rs_add_sibling{layer}")
    from_chips = run_job(to_owners_job(pending[0]), "rs_to_owners0")
    halves = rs_add_owners(pending[1], from_chips, "rs_add_owners0")
    gw_in, gw_out, gw_gate, gw_pp = run_job(exchange_halves_job(0, halves, full), "rs_exchange_halves0")
    (d_in, d_out, d_gate, d_pp), (m_in, m_out, m_gate, m_pp), (v_in, v_out, v_gate, v_pp) = adamw_big(
        [w_in, w_out, w_ple_gate, w_ple_proj], [gw_in, gw_out, gw_gate, gw_pp],
        [m_w_in, m_w_out, m_w_ple_gate, m_w_ple_proj], [v_w_in, v_w_out, v_w_ple_gate, v_w_ple_proj])

    small_like = [norm_g, ln_v_g, ln_v_b, w_s, b_s, ple_norm_g]

    def stack(j):
        return jnp.stack([small_g[layer][j].reshape(small_like[j].shape[1:]) for layer in range(DEPTH)])

    grads_small = [stack(j) for j in range(6)] + [g_final.reshape(D_MODEL), jnp.stack([small_g[l][6] for l in range(DEPTH)])]
    zeros_cw = jnp.zeros((DEPTH, 3, WIDTH), F32)
    params = [norm_g, ln_v_g, ln_v_b, w_s, b_s, ple_norm_g, final_g, zeros_cw]
    ms = [m_norm_g, m_ln_v_g, m_ln_v_b, m_w_s, m_b_s, m_ple_norm_g, m_final_g, zeros_cw]
    vs = [v_norm_g, v_ln_v_g, v_ln_v_b, v_w_s, v_b_s, v_ple_norm_g, v_final_g, zeros_cw + 1.0]
    gsum, dsm, msm, vsm = allreduce_small(_pack(grads_small), _pack(params), _pack(ms), _pack(vs))
    gs = _unpack(gsum, params)
    ds = _unpack(dsm, params)
    m2s = _unpack(msm, params)
    v2s = _unpack(vsm, params)
    g_cw_own = lax.dynamic_slice_in_dim(gs[7], own * (WIDTH // N_CHIPS), WIDTH // N_CHIPS, axis=2)
    rows2 = lambda a: a.reshape(DEPTH * 3, WIDTH // N_CHIPS)
    d_cw, m_cw, v_cw = adamw_rows(rows2(cw_rows), rows2(g_cw_own), rows2(jnp.transpose(m_conv_w, (0, 2, 1))),
                                  rows2(jnp.transpose(v_conv_w, (0, 2, 1))))
    back = lambda a: jnp.transpose(a.reshape(DEPTH, 3, WIDTH // N_CHIPS), (0, 2, 1))
    g_conv = jnp.transpose(g_cw_own, (0, 2, 1))

    def ordered(sm, cw_v, w_in_v, w_out_v, gate_v, pp_v):
        return [sm[0], w_in_v, sm[1], sm[2], sm[3], sm[4], cw_v, w_out_v, sm[5], gate_v, pp_v, sm[6]]

    grads = ordered(gs, g_conv, gw_in, gw_out, gw_gate, gw_pp)
    deltas = ordered(ds, back(d_cw), d_in, d_out, d_gate, d_pp)
    new_m = ordered(m2s, back(m_cw), m_in, m_out, m_gate, m_pp)
    new_v = ordered(v2s, back(v_cw), v_in, v_out, v_gate, v_pp)
    return (loss, dx[None], *grads, *deltas, *new_m, *new_v)
```

```python
import jax
import jax.numpy as jnp
from jax import lax
from jax.experimental import pallas as pl
from jax.experimental.pallas import tpu as pltpu

F32 = jnp.float32
BF16 = jnp.bfloat16

SEQ = 8192
D_MODEL = 1024
WIDTH = 512
PROJ = 7 * WIDTH
N_CHIPS = 4
COL_BLK = PROJ // N_CHIPS
PLE = 256
HEADS = 4
CHUNK = 128
DEPTH = 2
EPS = 1e-6
TILE = 256
N_TILES = SEQ // TILE
HALO = 8
VMEM_LIMIT = 60 * 1024 * 1024

ADAM_LR, ADAM_B1, ADAM_B2, ADAM_EPS, ADAM_WD, ADAM_STEP = 0.001, 0.9, 0.999, 1e-08, 0.01, 10
ADAM_C1 = 1.0 - ADAM_B1**ADAM_STEP
ADAM_C2 = 1.0 - ADAM_B2**ADAM_STEP

MESH = pl.DeviceIdType.MESH
ANY = pl.BlockSpec(memory_space=pl.ANY)


def _mm(a, b):
    return lax.dot_general(a, b, (((1,), (0,)), ((), ())), preferred_element_type=F32)


def _mm_nt(a, b):
    return lax.dot_general(a, b, (((1,), (1,)), ((), ())), preferred_element_type=F32)


def _mm_rows(a, w_ref):
    blk = w_ref.shape[1]
    acc = _mm(a[:, 0:blk], w_ref[0])
    for k in range(1, N_CHIPS):
        acc = acc + _mm(a[:, k * blk : (k + 1) * blk], w_ref[k])
    return acc


def _mm_nt_rows(a, w_ref):
    return jnp.concatenate([_mm_nt(a, w_ref[k]) for k in range(N_CHIPS)], axis=-1)


def _sigmoid(z):
    return 1.0 / (1.0 + jnp.exp(-z))


def _rms_stats(x):
    r = lax.rsqrt(jnp.mean(x * x, axis=-1, keepdims=True) + EPS)
    return r, x * r


def _rms_bwd(dyg, xh, r):
    return r * (dyg - xh * jnp.mean(dyg * xh, axis=-1, keepdims=True))


def _ln_stats(v):
    mu = jnp.mean(v, axis=-1, keepdims=True)
    vc = v - mu
    rs = lax.rsqrt(jnp.mean(vc * vc, axis=-1, keepdims=True) + EPS)
    return rs, vc * rs


def _mesh_pos():
    x, y, c = lax.axis_index("x"), lax.axis_index("y"), lax.axis_index("c")
    return x, y, c, 2 * x + y


def _peer(x, y, c, r):
    return ((1 - x) if (r >> 1) else x, (1 - y) if (r & 1) else y, c)


def _full(shape):
    return pl.BlockSpec(shape, lambda *_: (0,) * len(shape))


def _const(shape, pos):
    return pl.BlockSpec((None,) * len(pos) + tuple(shape), lambda *_: tuple(pos) + (0,) * len(shape))


class Job:
    def __init__(self, ins, out_shapes, sems, start, middle, finish, aliases=None):
        self.ins, self.out_shapes, self.sems = list(ins), list(out_shapes), list(sems)
        self.start, self.middle, self.finish = start, middle, finish
        self.aliases = aliases or {}


def run_job(job, name):
    ni, no = len(job.ins), len(job.out_shapes)

    def body(*refs):
        parts = (refs[:ni], refs[ni : ni + no], refs[ni + no :])
        job.start(*parts)
        if job.middle is not None:
            job.middle(*parts)
        job.finish(*parts)

    return pl.pallas_call(
        body, name=name, out_shape=job.out_shapes, in_specs=[ANY] * ni, out_specs=[ANY] * no, scratch_shapes=job.sems,
        input_output_aliases=job.aliases, compiler_params=pltpu.CompilerParams(has_side_effects=True),
    )(*job.ins)


def _call(body, *, name, grid, in_specs, out_specs, out_shape, args, scratch=(), job=None, mid=None):
    params = pltpu.CompilerParams(dimension_semantics=("arbitrary",) * len(grid), vmem_limit_bytes=VMEM_LIMIT,
                                  has_side_effects=job is not None)
    n_in, n_out, n_sc = len(in_specs), len(out_specs), len(scratch)
    if job is None:
        outs = pl.pallas_call(body, name=name, grid=grid, in_specs=in_specs, out_specs=out_specs, out_shape=out_shape,
                              scratch_shapes=list(scratch), compiler_params=params)(*args)
        return list(outs), []
    ji, jo = len(job.ins), len(job.out_shapes)
    assert len(grid) == 1 and not job.aliases

    def wrapped(*refs):
        ins, jin = refs[:n_in], refs[n_in : n_in + ji]
        o0 = n_in + ji
        outs, jout = refs[o0 : o0 + n_out], refs[o0 + n_out : o0 + n_out + jo]
        s0 = o0 + n_out + jo
        sc, jsem = refs[s0 : s0 + n_sc], refs[s0 + n_sc :]
        i = pl.program_id(0)

        @pl.when(i == 0)
        def _():
            job.start(jin, jout, jsem)

        if job.middle is not None:
            @pl.when(i == mid)
            def _():
                job.middle(jin, jout, jsem)

        body(*ins, *outs, *sc)

        @pl.when(i == grid[0] - 1)
        def _():
            job.finish(jin, jout, jsem)

    outs = pl.pallas_call(
        wrapped, name=name, grid=grid, in_specs=list(in_specs) + [ANY] * ji, out_specs=list(out_specs) + [ANY] * jo,
        out_shape=list(out_shape) + job.out_shapes, scratch_shapes=list(scratch) + job.sems, compiler_params=params,
    )(*args, *job.ins)
    return list(outs[:n_out]), list(outs[n_out:])


def gather_job(items):
    n = len(items)

    def src_half(a, ref, h):
        arr, layer = items[a]
        if layer is None:
            return ref.at[h]
        rh = arr.shape[1] // 2
        return ref.at[layer, pl.ds(h * rh, rh)]

    def dst_half(a, ref, chip, h):
        arr, layer = items[a]
        if layer is None:
            return ref.at[chip, h]
        rh = arr.shape[1] // 2
        return ref.at[chip, pl.ds(h * rh, rh)]

    def copies(src, dst, scratch):
        stage, (in_sem, out_sem, ici_s, ici_r, fwd_s, fwd_r) = scratch[:n], scratch[n:]
        x, y, c, own = _mesh_pos()
        local, ici, fwd, got = [], {}, {}, {}
        for a in range(n):
            layer = items[a][1]
            local.append((pltpu.make_async_copy(src[a] if layer is None else src[a].at[layer], stage[a], in_sem.at[a]),
                          pltpu.make_async_copy(stage[a], dst[a].at[own], out_sem.at[a])))
            for r in (1, 2, 3):
                k = 3 * a + r - 1
                ici[a, r] = pltpu.make_async_remote_copy(
                    src_ref=src_half(a, src[a], c), dst_ref=dst_half(a, dst[a], own, c), send_sem=ici_s.at[k],
                    recv_sem=ici_r.at[k], device_id=_peer(x, y, c, r), device_id_type=MESH)
                fwd[a, r] = pltpu.make_async_remote_copy(
                    src_ref=dst_half(a, dst[a], own ^ r, c), dst_ref=dst_half(a, dst[a], own ^ r, c), send_sem=fwd_s.at[k],
                    recv_sem=fwd_r.at[k], device_id=(x, y, 1 - c), device_id_type=MESH)
                got[a, r] = pltpu.make_async_remote_copy(
                    src_ref=dst_half(a, dst[a], own ^ r, 1 - c), dst_ref=dst_half(a, dst[a], own ^ r, 1 - c),
                    send_sem=fwd_s.at[k], recv_sem=fwd_r.at[k], device_id=(x, y, 1 - c), device_id_type=MESH)
        return local, ici, fwd, got

    def start(src, dst, sems):
        local, ici, _, _ = copies(src, dst, sems)
        for a in range(n):
            for r in (1, 2, 3):
                ici[a, r].start()
        for cp_in, _ in local:
            cp_in.start()
        for cp_in, cp_out in local:
            cp_in.wait()
            cp_out.start()

    def middle(src, dst, sems):
        _, ici, fwd, _ = copies(src, dst, sems)
        for r in (1, 2, 3):
            for a in range(n):
                ici[a, r].wait_recv()
                fwd[a, r].start()

    def finish(src, dst, sems):
        local, ici, fwd, got = copies(src, dst, sems)
        for a in range(n):
            for r in (1, 2, 3):
                got[a, r].wait_recv()
        for a in range(n):
            for r in (1, 2, 3):
                ici[a, r].wait_send()
                fwd[a, r].wait_send()
        for _, cp_out in local:
            cp_out.wait()

    out_shapes = [jax.ShapeDtypeStruct((N_CHIPS,) + (arr.shape if layer is None else arr.shape[1:]), arr.dtype)
                  for arr, layer in items]
    stage = [pltpu.VMEM(arr.shape if layer is None else arr.shape[1:], arr.dtype) for arr, layer in items]
    sems = [pltpu.SemaphoreType.DMA((n,))] * 2 + [pltpu.SemaphoreType.DMA((3 * n,))] * 4
    return Job([arr for arr, _ in items], out_shapes, stage + sems, start, middle, finish)


def _mixer_fwd(proj_ref, lg, lb, ws_ref, bsb_ref, cw_ref, mix_ref, xcbuf, halo_xc):
    u = proj_ref[:, 0 * WIDTH : 1 * WIDTH]
    v = proj_ref[:, 1 * WIDTH : 2 * WIDTH]
    za = proj_ref[:, 2 * WIDTH : 3 * WIDTH]
    h = proj_ref[:, 3 * WIDTH : 4 * WIDTH]
    gb = proj_ref[:, 4 * WIDTH : 5 * WIDTH]
    gc = proj_ref[:, 5 * WIDTH : 6 * WIDTH]
    zb = proj_ref[:, 6 * WIDTH : 7 * WIDTH]
    rs, vhat = _ln_stats(v)
    vl = vhat * lg + lb
    vlb = vl.astype(BF16)
    for j in range(TILE // CHUNK):
        rows = slice(j * CHUNK, (j + 1) * CHUNK)
        for hd in range(HEADS):
            cols = slice(hd * CHUNK, (hd + 1) * CHUNK)
            mix_ref[rows, cols] = _mm(ws_ref[hd], vlb[rows, cols]) + bsb_ref[:, cols]
    mixed = mix_ref[...]
    siga = _sigmoid(za)
    sigb = _sigmoid(zb)
    xc = gc * h
    xcbuf[0:HALO, :] = halo_xc
    xcbuf[HALO : HALO + TILE, :] = xc
    y = cw_ref[0:1, :] * xcbuf[HALO - 2 : HALO - 2 + TILE, :] + cw_ref[1:2, :] * xcbuf[HALO - 1 : HALO - 1 + TILE, :]
    y = y + cw_ref[2:3, :] * xc
    return dict(u=u, za=za, h=h, gb=gb, gc=gc, zb=zb, rs=rs, vhat=vhat, vlb=vlb, mixed=mixed, siga=siga, sigb=sigb,
                xc=xc, y=y)


def fwd_layer(layer, x, p, wts, small, job=None):
    win, wout, wg, wpp = wts
    ng, lg, lb, ws, bsb, cw, pg = small

    def body(x_ref, p_ref, win_ref, wout_ref, wg_ref, wpp_ref, ng_ref, lg_ref, lb_ref, ws_ref, bsb_ref, cw_ref, pg_ref,
             proj_ref, x2_ref, gate_ref, pp_ref, x3_ref, hnT_ref, catT_ref, hn2T_ref, pT_ref, mix_ref, xcbuf, carry):
        i = pl.program_id(0)

        @pl.when(i == 0)
        def _():
            carry[...] = jnp.zeros_like(carry)

        xv = x_ref[...]
        r1, xh = _rms_stats(xv)
        hn = xh * ng_ref[...]
        hnb = hn.astype(BF16)
        hnT_ref[...] = hn.T.astype(BF16)
        for k in range(N_CHIPS):
            proj_ref[:, k * COL_BLK : (k + 1) * COL_BLK] = _mm(hnb, win_ref[k])
        m = _mixer_fwd(proj_ref, lg_ref[...], lb_ref[...], ws_ref, bsb_ref, cw_ref, mix_ref, xcbuf, carry[...])
        carry[...] = m["xc"][TILE - HALO : TILE, :]
        out_a = (m["u"] * m["mixed"]) * (m["za"] * m["siga"])
        out_b = (m["gb"] * m["y"]) * (m["zb"] * m["sigb"])
        cat = jnp.concatenate([out_a, out_b], axis=-1)
        catT_ref[...] = cat.T.astype(BF16)
        x2 = xv + _mm_rows(cat.astype(BF16), wout_ref)
        x2_ref[...] = x2
        r2, xh2 = _rms_stats(x2)
        hn2 = xh2 * pg_ref[...]
        hn2T_ref[...] = hn2.T.astype(BF16)
        gate = _sigmoid(_mm_rows(hn2.astype(BF16), wg_ref))
        gate_ref[...] = gate
        pv = p_ref[...]
        pT_ref[...] = pv.T.astype(BF16)
        pb = pv.astype(BF16)
        for k in range(N_CHIPS):
            pp_ref[:, k * PLE : (k + 1) * PLE] = _mm(pb, wpp_ref[k])
        x3_ref[...] = x2 + gate * pp_ref[...]

    tok = lambda w: pl.BlockSpec((TILE, w), lambda i: (i, 0))
    tokT = lambda w: pl.BlockSpec((w, TILE), lambda i: (0, i))
    f32 = lambda w: jax.ShapeDtypeStruct((SEQ, w), F32)
    bfT = lambda w: jax.ShapeDtypeStruct((w, SEQ), BF16)
    return _call(
        body, name=f"fwd_layer{layer}", grid=(N_TILES,),
        in_specs=[tok(D_MODEL), pl.BlockSpec((None, None, TILE, PLE), lambda i: (layer, 0, i, 0)),
                  _full(win.shape), _full(wout.shape), _full(wg.shape), _full(wpp.shape),
                  _const((1, D_MODEL), (layer,)), _const((1, WIDTH), (layer,)), _const((1, WIDTH), (layer,)),
                  _const((HEADS, CHUNK, CHUNK), (layer,)), _const((CHUNK, WIDTH), (layer,)), _const((3, WIDTH), (layer,)),
                  _const((1, D_MODEL), (layer,))],
        out_specs=[tok(PROJ), tok(D_MODEL), tok(D_MODEL), tok(D_MODEL), tok(D_MODEL),
                   tokT(D_MODEL), tokT(D_MODEL), tokT(D_MODEL), tokT(PLE)],
        out_shape=[f32(PROJ), f32(D_MODEL), f32(D_MODEL), f32(D_MODEL), f32(D_MODEL),
                   bfT(D_MODEL), bfT(D_MODEL), bfT(D_MODEL), bfT(PLE)],
        scratch=[pltpu.VMEM((TILE, WIDTH), F32), pltpu.VMEM((HALO + TILE, WIDTH), F32), pltpu.VMEM((HALO, WIDTH), F32)],
        args=(x, p, win, wout, wg, wpp, ng, lg, lb, ws, bsb, cw, pg), job=job, mid=N_TILES // 2)


def loss_head(x, tgt, gf):
    def body(x_ref, t_ref, g_ref, dx_ref, loss_ref, gg_ref):
        i = pl.program_id(0)

        @pl.when(i == 0)
        def _():
            loss_ref[...] = jnp.zeros_like(loss_ref)
            gg_ref[...] = jnp.zeros_like(gg_ref)

        r, xh = _rms_stats(x_ref[...])
        g = g_ref[...]
        err = xh * g - t_ref[...]
        loss_ref[...] += (0.5 / D_MODEL) * jnp.sum(err * err).reshape(1, 1)
        dy = err * (1.0 / D_MODEL)
        gg_ref[...] += jnp.sum(dy * xh, axis=0, keepdims=True)
        dx_ref[...] = _rms_bwd(dy * g, xh, r)

    tile = 512
    tok = pl.BlockSpec((tile, D_MODEL), lambda i: (i, 0))
    outs, _ = _call(
        body, name="loss_head", grid=(SEQ // tile,), in_specs=[tok, tok, _full((1, D_MODEL))],
        out_specs=[tok, _full((1, 1)), _full((1, D_MODEL))],
        out_shape=[jax.ShapeDtypeStruct((SEQ, D_MODEL), F32), jax.ShapeDtypeStruct((1, 1), F32),
                   jax.ShapeDtypeStruct((1, D_MODEL), F32)], args=(x, tgt, gf))
    return outs


def bwd_ple(layer, dx3, x2, gate, pp, wg, wout, pg, job=None):
    def body(dx3_ref, x2_ref, gate_ref, pp_ref, wg_ref, wout_ref, pg_ref,
             dpp_ref, dgl_ref, dx2_ref, dx2b_ref, dcat_ref, gpg_ref):
        i = pl.program_id(0)

        @pl.when(i == 0)
        def _():
            gpg_ref[...] = jnp.zeros_like(gpg_ref)

        dx3v = dx3_ref[...]
        gate_v = gate_ref[...]
        dpp_ref[...] = (dx3v * gate_v).astype(BF16)
        dgl = ((dx3v * pp_ref[...]) * gate_v * (1.0 - gate_v)).astype(BF16)
        dgl_ref[...] = dgl
        dhn2 = _mm_nt_rows(dgl, wg_ref)
        r2, xh2 = _rms_stats(x2_ref[...])
        gpg_ref[...] += jnp.sum(dhn2 * xh2, axis=0, keepdims=True)
        dx2 = dx3v + _rms_bwd(dhn2 * pg_ref[...], xh2, r2)
        dx2_ref[...] = dx2
        dx2b = dx2.astype(BF16)
        dx2b_ref[...] = dx2b
        dcat_ref[...] = _mm_nt_rows(dx2b, wout_ref)

    tok = pl.BlockSpec((TILE, D_MODEL), lambda i: (i, 0))
    f32 = jax.ShapeDtypeStruct((SEQ, D_MODEL), F32)
    b16 = jax.ShapeDtypeStruct((SEQ, D_MODEL), BF16)
    return _call(
        body, name=f"bwd_ple{layer}", grid=(N_TILES,),
        in_specs=[tok, tok, tok, tok, _full(wg.shape), _full(wout.shape), _const((1, D_MODEL), (layer,))],
        out_specs=[tok, tok, tok, tok, tok, _full((1, D_MODEL))],
        out_shape=[b16, b16, f32, b16, f32, jax.ShapeDtypeStruct((1, D_MODEL), F32)],
        args=(dx3, x2, gate, pp, wg, wout, pg), job=job)


def bwd_mix(layer, dcat, dx2, x, proj, win, small, wsT, tril, job=None):
    ng, lg, lb, ws, bsb, cw, _ = small

    def body(dcat_ref, dx2_ref, x_ref, proj_ref, halo_ref, win_ref, ng_ref, lg_ref, lb_ref, ws_ref, wsT_ref, bsb_ref,
             cw_ref, tril_ref,
             dx_ref, dproj_ref, gn_ref, glg_ref, glb_ref, gws_ref, gbs_ref, gcw_ref,
             mix_ref, xcbuf, dycbuf, dvl_ref, bs_acc):
        i = pl.program_id(0)

        @pl.when(i == 0)
        def _():
            for ref in (gn_ref, glg_ref, glb_ref, gws_ref, gcw_ref, bs_acc):
                ref[...] = jnp.zeros_like(ref)
            dycbuf[TILE : TILE + HALO, :] = jnp.zeros((HALO, WIDTH), F32)

        lgv = lg_ref[...]
        halo_xc = halo_ref[:, 5 * WIDTH : 6 * WIDTH] * halo_ref[:, 3 * WIDTH : 4 * WIDTH]
        halo_xc = jnp.where(i == N_TILES - 1, 0.0, halo_xc)
        m = _mixer_fwd(proj_ref, lgv, lb_ref[...], ws_ref, bsb_ref, cw_ref, mix_ref, xcbuf, halo_xc)
        u, za, h, gb, gc, zb = m["u"], m["za"], m["h"], m["gb"], m["gc"], m["zb"]
        mixed, siga, sigb, xc, y = m["mixed"], m["siga"], m["sigb"], m["xc"], m["y"]
        doa = dcat_ref[:, 0:WIDTH]
        dob = dcat_ref[:, WIDTH : 2 * WIDTH]
        sa = za * siga
        sb = zb * sigb
        doa_sa = doa * sa
        dproj_ref[:, 0 * WIDTH : 1 * WIDTH] = (doa_sa * mixed).astype(BF16)
        dmixed = doa_sa * u
        dza = (doa * (u * mixed)) * (siga * (1.0 + za * (1.0 - siga)))
        dproj_ref[:, 2 * WIDTH : 3 * WIDTH] = dza.astype(BF16)
        dob_sb = dob * sb
        dproj_ref[:, 4 * WIDTH : 5 * WIDTH] = (dob_sb * y).astype(BF16)
        dyc = dob_sb * gb
        dzb = (dob * (gb * y)) * (sigb * (1.0 + zb * (1.0 - sigb)))
        dproj_ref[:, 6 * WIDTH : 7 * WIDTH] = dzb.astype(BF16)
        dycbuf[0:TILE, :] = dyc
        dyc1 = dycbuf[1 : 1 + TILE, :]
        dyc2 = dycbuf[2 : 2 + TILE, :]
        dxc = cw_ref[2:3, :] * dyc + cw_ref[1:2, :] * dyc1 + cw_ref[0:1, :] * dyc2
        gcw_ref[0:1, :] += jnp.sum(xc * dyc2, axis=0, keepdims=True)
        gcw_ref[1:2, :] += jnp.sum(xc * dyc1, axis=0, keepdims=True)
        gcw_ref[2:3, :] += jnp.sum(xc * dyc, axis=0, keepdims=True)
        dycbuf[TILE : TILE + HALO, :] = dyc[0:HALO, :]
        dproj_ref[:, 5 * WIDTH : 6 * WIDTH] = (dxc * h).astype(BF16)
        dproj_ref[:, 3 * WIDTH : 4 * WIDTH] = (dxc * gc).astype(BF16)
        dmb = dmixed.astype(BF16)
        vlb = m["vlb"]
        bsum = jnp.zeros((CHUNK, WIDTH), F32)
        for j in range(TILE // CHUNK):
            rows = slice(j * CHUNK, (j + 1) * CHUNK)
            bsum = bsum + dmixed[rows, :]
            for hd in range(HEADS):
                cols = slice(hd * CHUNK, (hd + 1) * CHUNK)
                gws_ref[hd] += _mm_nt(dmb[rows, cols], vlb[rows, cols])
                dvl_ref[rows, cols] = _mm(wsT_ref[hd], dmb[rows, cols])
        bs_acc[...] += bsum
        dvl = dvl_ref[...]
        vhat = m["vhat"]
        glb_ref[...] += jnp.sum(dvl, axis=0, keepdims=True)
        glg_ref[...] += jnp.sum(dvl * vhat, axis=0, keepdims=True)
        dvh = dvl * lgv
        dv = m["rs"] * (dvh - jnp.mean(dvh, axis=-1, keepdims=True) - vhat * jnp.mean(dvh * vhat, axis=-1, keepdims=True))
        dproj_ref[:, 1 * WIDTH : 2 * WIDTH] = dv.astype(BF16)
        dhn = _mm_nt(dproj_ref[:, 0:COL_BLK], win_ref[0])
        for k in range(1, N_CHIPS):
            dhn = dhn + _mm_nt(dproj_ref[:, k * COL_BLK : (k + 1) * COL_BLK], win_ref[k])
        r1, xh = _rms_stats(x_ref[...])
        gn_ref[...] += jnp.sum(dhn * xh, axis=0, keepdims=True)
        dx_ref[...] = dx2_ref[...] + _rms_bwd(dhn * ng_ref[...], xh, r1)

        @pl.when(i == N_TILES - 1)
        def _():
            for hd in range(HEADS):
                gws_ref[hd] = gws_ref[hd] * tril_ref[...]
                gbs_ref[hd : hd + 1, :] = jnp.sum(bs_acc[:, hd * CHUNK : (hd + 1) * CHUNK].T, axis=0, keepdims=True)

    rev = lambda w: pl.BlockSpec((TILE, w), lambda i: (N_TILES - 1 - i, 0))
    halo = pl.BlockSpec((HALO, PROJ), lambda i: (jnp.maximum((N_TILES - 1 - i) * (TILE // HALO) - 1, 0), 0))
    return _call(
        body, name=f"bwd_mix{layer}", grid=(N_TILES,),
        in_specs=[rev(D_MODEL), rev(D_MODEL), rev(D_MODEL), rev(PROJ), halo, _full(win.shape),
                  _const((1, D_MODEL), (layer,)), _const((1, WIDTH), (layer,)), _const((1, WIDTH), (layer,)),
                  _const((HEADS, CHUNK, CHUNK), (layer,)), _const((HEADS, CHUNK, CHUNK), (layer,)),
                  _const((CHUNK, WIDTH), (layer,)), _const((3, WIDTH), (layer,)), _full((CHUNK, CHUNK))],
        out_specs=[rev(D_MODEL), rev(PROJ), _full((1, D_MODEL)), _full((1, WIDTH)), _full((1, WIDTH)),
                   _full((HEADS, CHUNK, CHUNK)), _full((HEADS, CHUNK)), _full((3, WIDTH))],
        out_shape=[jax.ShapeDtypeStruct((SEQ, D_MODEL), F32), jax.ShapeDtypeStruct((SEQ, PROJ), BF16),
                   jax.ShapeDtypeStruct((1, D_MODEL), F32), jax.ShapeDtypeStruct((1, WIDTH), F32),
                   jax.ShapeDtypeStruct((1, WIDTH), F32), jax.ShapeDtypeStruct((HEADS, CHUNK, CHUNK), F32),
                   jax.ShapeDtypeStruct((HEADS, CHUNK), F32), jax.ShapeDtypeStruct((3, WIDTH), F32)],
        scratch=[pltpu.VMEM((TILE, WIDTH), F32), pltpu.VMEM((HALO + TILE, WIDTH), F32),
                 pltpu.VMEM((TILE + HALO, WIDTH), F32), pltpu.VMEM((TILE, WIDTH), F32), pltpu.VMEM((CHUNK, WIDTH), F32)],
        args=(dcat, dx2, x, proj, proj, win, ng, lg, lb, ws, wsT, bsb, cw, tril), job=job)


def wgrad(aT, b, n_tiles, col_blocked, tk, name):
    m_dim, n_dim = aT.shape[0], b.shape[1]
    tn = n_dim // n_tiles
    cb, mb = n_dim // N_CHIPS, m_dim // N_CHIPS
    per_tile = N_CHIPS // n_tiles
    assert col_blocked or n_tiles == 1

    def body(a_ref, b_ref, o_ref):
        @pl.when(pl.program_id(1) == 0)
        def _():
            o_ref[...] = jnp.zeros_like(o_ref)

        prod = _mm(a_ref[...], b_ref[...])
        if col_blocked:
            for q in range(per_tile):
                o_ref[q] += prod[:, q * cb : (q + 1) * cb]
        else:
            for q in range(N_CHIPS):
                o_ref[q] += prod[q * mb : (q + 1) * mb, :]

    if col_blocked:
        shape = (N_CHIPS, m_dim, cb)
        o_spec = pl.BlockSpec((per_tile, m_dim, cb), lambda n, k: (n, 0, 0))
    else:
        shape = (N_CHIPS, mb, n_dim)
        o_spec = pl.BlockSpec((N_CHIPS, mb, n_dim), lambda n, k: (0, 0, 0))
    return pl.pallas_call(
        body, name=name, grid=(n_tiles, SEQ // tk),
        in_specs=[pl.BlockSpec((m_dim, tk), lambda n, k: (0, k)), pl.BlockSpec((tk, tn), lambda n, k: (k, n))],
        out_specs=o_spec, out_shape=jax.ShapeDtypeStruct(shape, F32),
        compiler_params=pltpu.CompilerParams(dimension_semantics=("arbitrary", "arbitrary"), vmem_limit_bytes=VMEM_LIMIT),
    )(aT, b)


def to_sibling_job(grads):
    n = len(grads)

    def copies(src, dst, sems):
        x, y, c, _ = _mesh_pos()
        out = []
        for a in range(n):
            rh = grads[a].shape[1] // 2
            out.append(pltpu.make_async_remote_copy(
                src_ref=src[a].at[:, pl.ds((1 - c) * rh, rh), :], dst_ref=dst[a], send_sem=sems[0].at[a],
                recv_sem=sems[1].at[a], device_id=(x, y, 1 - c), device_id_type=MESH))
        return out

    def start(src, dst, sems):
        for cp in copies(src, dst, sems):
            cp.start()

    def finish(src, dst, sems):
        for cp in copies(src, dst, sems):
            cp.wait()

    out_shapes = [jax.ShapeDtypeStruct((N_CHIPS, g.shape[1] // 2, g.shape[2]), F32) for g in grads]
    return Job(grads, out_shapes, [pltpu.SemaphoreType.DMA((n,))] * 2, start, None, finish)


def rs_add_sibling(ids, grads, recvd, name):
    n = len(grads)

    def body(ids_ref, *refs):
        g, r = refs[:n], refs[n : 2 * n]
        pb, pf = refs[2 * n : 3 * n], refs[3 * n :]
        k = pl.program_id(0)
        for a in range(n):
            s = g[a][...] + r[a][...]
            pb[a][...] = s.astype(BF16)

            @pl.when(k == ids_ref[1])
            def _():
                pf[a][...] = s

    in_specs, out_specs, out_shape = [], [], []
    for g in grads:
        rh, cc = g.shape[1] // 2, g.shape[2]
        in_specs.append(pl.BlockSpec((None, rh, cc), lambda k, ids: (k, ids[0], 0)))
    for g in grads:
        rh, cc = g.shape[1] // 2, g.shape[2]
        in_specs.append(pl.BlockSpec((None, rh, cc), lambda k, ids: (k, 0, 0)))
        out_specs.append(pl.BlockSpec((None, rh, cc), lambda k, ids: (k, 0, 0)))
        out_shape.append(jax.ShapeDtypeStruct((N_CHIPS, rh, cc), BF16))
    for g in grads:
        rh, cc = g.shape[1] // 2, g.shape[2]
        out_specs.append(pl.BlockSpec((rh, cc), lambda k, ids: (0, 0)))
        out_shape.append(jax.ShapeDtypeStruct((rh, cc), F32))
    outs = pl.pallas_call(
        body, name=name, out_shape=out_shape,
        grid_spec=pltpu.PrefetchScalarGridSpec(num_scalar_prefetch=1, grid=(N_CHIPS,), in_specs=in_specs,
                                               out_specs=out_specs),
        compiler_params=pltpu.CompilerParams(dimension_semantics=("arbitrary",), vmem_limit_bytes=VMEM_LIMIT),
    )(ids, *grads, *recvd)
    return list(outs[:n]), list(outs[n:])


def to_owners_job(partials):
    n = len(partials)

    def copies(src, dst, sems):
        x, y, c, own = _mesh_pos()
        out = []
        for a in range(n):
            for r in (1, 2, 3):
                out.append(pltpu.make_async_remote_copy(
                    src_ref=src[a].at[own ^ r], dst_ref=dst[a].at[r - 1], send_sem=sems[0].at[3 * a + r - 1],
                    recv_sem=sems[1].at[3 * a + r - 1], device_id=_peer(x, y, c, r), device_id_type=MESH))
        return out

    def start(src, dst, sems):
        for cp in copies(src, dst, sems):
            cp.start()

    def finish(src, dst, sems):
        for cp in copies(src, dst, sems):
            cp.wait()

    out_shapes = [jax.ShapeDtypeStruct((3,) + p.shape[1:], BF16) for p in partials]
    return Job(partials, out_shapes, [pltpu.SemaphoreType.DMA((3 * n,))] * 2, start, None, finish)


def rs_add_owners(layer, ids, own_f32, recvd, prev, name):
    n = len(own_f32)
    nb = 2

    def body(ids_ref, *refs):
        o, r, f = refs[:n], refs[n : 2 * n], refs[-n:]
        for a in range(n):
            f[a][...] = ((o[a][...] + r[a][0].astype(F32)) + r[a][1].astype(F32)) + r[a][2].astype(F32)

    in_specs, out_specs, out_shape = [], [], []
    for o in own_f32:
        in_specs.append(pl.BlockSpec((o.shape[0] // nb, o.shape[1]), lambda j, ids: (j, 0)))
    for o in own_f32:
        in_specs.append(pl.BlockSpec((3, o.shape[0] // nb, o.shape[1]), lambda j, ids: (0, j, 0)))
        out_specs.append(pl.BlockSpec((None, o.shape[0] // nb, o.shape[1]), lambda j, ids: (layer, ids[0] * nb + j, 0)))
        out_shape.append(jax.ShapeDtypeStruct((DEPTH, 2 * o.shape[0], o.shape[1]), F32))
    args, aliases = [ids, *own_f32, *recvd], {}
    if prev is not None:
        in_specs += [ANY] * n
        args += list(prev)
        aliases = {1 + 2 * n + a: a for a in range(n)}
    return pl.pallas_call(
        body, name=name, out_shape=out_shape, input_output_aliases=aliases,
        grid_spec=pltpu.PrefetchScalarGridSpec(num_scalar_prefetch=1, grid=(nb,), in_specs=in_specs, out_specs=out_specs),
        compiler_params=pltpu.CompilerParams(dimension_semantics=("arbitrary",), vmem_limit_bytes=VMEM_LIMIT),
    )(*args)


def exchange_halves_job(layer, full):
    n = len(full)

    def copies(src, dst, sems):
        x, y, c, _ = _mesh_pos()
        out = []
        for a in range(n):
            rh = full[a].shape[1] // 2
            out.append(pltpu.make_async_remote_copy(
                src_ref=src[a].at[layer, pl.ds(c * rh, rh), :], dst_ref=dst[a].at[layer, pl.ds(c * rh, rh), :],
                send_sem=sems[0].at[a], recv_sem=sems[1].at[a], device_id=(x, y, 1 - c), device_id_type=MESH))
        return out

    def start(src, dst, sems):
        for cp in copies(src, dst, sems):
            cp.start()

    def finish(src, dst, sems):
        for cp in copies(src, dst, sems):
            cp.wait()

    out_shapes = [jax.ShapeDtypeStruct(f.shape, F32) for f in full]
    return Job(full, out_shapes, [pltpu.SemaphoreType.DMA((n,))] * 2, start, None, finish, {a: a for a in range(n)})


def _adamw(w, g, m, v):
    m2 = ADAM_B1 * m + (1.0 - ADAM_B1) * g
    v2 = ADAM_B2 * v + (1.0 - ADAM_B2) * (g * g)
    delta = -ADAM_LR * ((m2 / ADAM_C1) / (jnp.sqrt(v2 / ADAM_C2) + ADAM_EPS) + ADAM_WD * w)
    return delta, m2, v2


def adamw_big(ws, gs, ms, vs):
    n = len(ws)
    nb = 4

    def body(*refs):
        for a in range(n):
            w, g, m, v = (refs[j * n + a][...] for j in range(4))
            d, m2, v2 = _adamw(w, g, m, v)
            refs[4 * n + a][...] = d
            refs[5 * n + a][...] = m2
            refs[6 * n + a][...] = v2

    specs = [pl.BlockSpec((None, w.shape[1] // nb, w.shape[2]), lambda l, j: (l, j, 0)) for w in ws]
    shapes = [jax.ShapeDtypeStruct(w.shape, F32) for w in ws]
    outs = pl.pallas_call(
        body, name="adamw_big", grid=(DEPTH, nb), in_specs=specs * 4, out_specs=specs * 3, out_shape=shapes * 3,
        compiler_params=pltpu.CompilerParams(dimension_semantics=("arbitrary", "arbitrary"), vmem_limit_bytes=VMEM_LIMIT),
    )(*ws, *gs, *ms, *vs)
    return outs[:n], outs[n : 2 * n], outs[2 * n :]


def allreduce_small(g, w, m, v):
    rows = g.shape[0]

    def body(g_ref, w_ref, m_ref, v_ref, gsum_ref, d_ref, m2_ref, v2_ref, sib_buf, slots, s1, r1, s2, r2):
        x, y, c, own = _mesh_pos()
        first = pltpu.make_async_remote_copy(src_ref=g_ref, dst_ref=sib_buf, send_sem=s1, recv_sem=r1,
                                             device_id=(x, y, 1 - c), device_id_type=MESH)
        first.start()
        first.wait()
        slots[0] = g_ref[...] + sib_buf[...]
        cps = [pltpu.make_async_remote_copy(src_ref=slots.at[0], dst_ref=slots.at[r], send_sem=s2.at[r - 1],
                                            recv_sem=r2.at[r - 1], device_id=_peer(x, y, c, r), device_id_type=MESH)
               for r in (1, 2, 3)]
        for cp in cps:
            cp.start()
        for cp in cps:
            cp.wait()
        tot = slots[own] + slots[own ^ 1]
        tot = tot + slots[own ^ 2]
        tot = tot + slots[own ^ 3]
        gsum_ref[...] = tot
        d, m2, v2 = _adamw(w_ref[...], tot, m_ref[...], v_ref[...])
        d_ref[...] = d
        m2_ref[...] = m2
        v2_ref[...] = v2

    vm = pl.BlockSpec(memory_space=pltpu.VMEM)
    shape = jax.ShapeDtypeStruct(g.shape, F32)
    return pl.pallas_call(
        body, name="allreduce_small", out_shape=[shape] * 4, in_specs=[vm] * 4, out_specs=[vm] * 4,
        scratch_shapes=[pltpu.VMEM((rows, 128), F32), pltpu.VMEM((N_CHIPS, rows, 128), F32),
                        pltpu.SemaphoreType.DMA, pltpu.SemaphoreType.DMA,
                        pltpu.SemaphoreType.DMA((3,)), pltpu.SemaphoreType.DMA((3,))],
        compiler_params=pltpu.CompilerParams(has_side_effects=True),
    )(g, w, m, v)


def adamw_rows(w, g, m, v):
    def body(w_ref, g_ref, m_ref, v_ref, d_ref, m2_ref, v2_ref):
        d, m2, v2 = _adamw(w_ref[...], g_ref[...], m_ref[...], v_ref[...])
        d_ref[...] = d
        m2_ref[...] = m2
        v2_ref[...] = v2

    return pl.pallas_call(body, name="adamw_conv", out_shape=[jax.ShapeDtypeStruct(w.shape, F32)] * 3)(w, g, m, v)


def _pack(parts):
    out = []
    for a in parts:
        flat = a.reshape(-1)
        rows = -(-flat.shape[0] // 1024) * 8
        flat = jnp.pad(flat, (0, rows * 128 - flat.shape[0]))
        out.append(flat.reshape(rows, 128))
    return jnp.concatenate(out, axis=0)


def _unpack(packed, like):
    out, row = [], 0
    for a in like:
        size = 1
        for s in a.shape:
            size *= s
        rows = -(-size // 1024) * 8
        out.append(packed[row : row + rows].reshape(-1)[:size].reshape(a.shape))
        row += rows
    return out


def kernel(x, p, norm_g, w_in, ln_v_g, ln_v_b, w_s, b_s, conv_w, w_out, ple_norm_g, w_ple_gate, w_ple_proj, final_g, loss_target, m_norm_g, m_w_in, m_ln_v_g, m_ln_v_b, m_w_s, m_b_s, m_conv_w, m_w_out, m_ple_norm_g, m_w_ple_gate, m_w_ple_proj, m_final_g, v_norm_g, v_w_in, v_ln_v_g, v_ln_v_b, v_w_s, v_b_s, v_conv_w, v_w_out, v_ple_norm_g, v_w_ple_gate, v_w_ple_proj, v_final_g):
    cx, cy, cc = lax.axis_index("x"), lax.axis_index("y"), lax.axis_index("c")
    own = 2 * cx + cy
    ids = jnp.stack([cc, own]).astype(jnp.int32)

    cw_rows = jnp.transpose(conv_w, (0, 2, 1))
    shards = [w_in.astype(BF16), w_out.astype(BF16), w_ple_gate.astype(BF16), w_ple_proj.astype(BF16)]
    *wts0, g_cw = run_job(gather_job([(s, 0) for s in shards] + [(cw_rows, None)]), "gather_weights0")
    cw_full = jnp.transpose(g_cw, (1, 2, 0, 3)).reshape(DEPTH, 3, WIDTH)
    tril = jnp.tril(jnp.ones((CHUNK, CHUNK), F32))
    ws_masked = w_s * tril[None, None]
    ws_b = ws_masked.astype(BF16)
    wsT_b = jnp.swapaxes(ws_masked, 2, 3).astype(BF16)
    bsb = jnp.repeat(jnp.swapaxes(b_s, 1, 2), CHUNK, axis=2)
    small = (norm_g[:, None, :], ln_v_g[:, None, :], ln_v_b[:, None, :], ws_b, bsb, cw_full, ple_norm_g[:, None, :])

    saved0, wts1 = fwd_layer(0, x[0], p, wts0, small, job=gather_job([(s, 1) for s in shards]))
    saved1, _ = fwd_layer(1, saved0[4], p, wts1, small)
    wts, saved, xs = [wts0, wts1], [saved0, saved1], [x[0], saved0[4]]
    dx, loss_part, g_final = loss_head(saved1[4], loss_target[0], final_g[None, :])
    loss = lax.psum(loss_part[0, 0], ("x", "y", "c"))

    def chip_sums(arrs, tag):
        recvd = run_job(to_sibling_job(arrs), f"rs_to_sibling{tag}")
        return rs_add_sibling(ids, arrs, recvd, f"rs_add_sibling{tag}")

    small_g = [None] * DEPTH
    proj, x2, gate, pp, _, hnT, catT, hn2T, pT = saved[1]
    (dpp_b, dgl_b, dx2, dx2_b, dcat, g_pg), _ = bwd_ple(1, dx, x2, gate, pp, wts1[2], wts1[1], small[6])
    (dx, dproj_b, *g_mix), _ = bwd_mix(1, dcat, dx2, xs[1], proj, wts1[0], small, wsT_b, tril)
    small_g[1] = g_mix[:5] + [g_pg, g_mix[5]]
    part1 = chip_sums([wgrad(hnT, dproj_b, 2, True, 1024, "wgrad_in1"), wgrad(catT, dx2_b, 1, False, 1024, "wgrad_out1"),
                       wgrad(hn2T, dgl_b, 1, False, 1024, "wgrad_gate1"), wgrad(pT, dpp_b, 1, True, 2048, "wgrad_pp1")], "1")
    proj, x2, gate, pp, _, hnT, catT, hn2T, pT = saved[0]
    (dpp_b, dgl_b, dx2, dx2_b, dcat, g_pg), from1 = bwd_ple(0, dx, x2, gate, pp, wts0[2], wts0[1], small[6],
                                                            job=to_owners_job(part1[0]))
    full = rs_add_owners(1, ids, part1[1], from1, None, "rs_add_owners1")
    full = run_job(exchange_halves_job(1, full), "rs_exchange_halves1")
    part0a = chip_sums([wgrad(catT, dx2_b, 1, False, 1024, "wgrad_out0"), wgrad(hn2T, dgl_b, 1, False, 1024, "wgrad_gate0"),
                        wgrad(pT, dpp_b, 1, True, 2048, "wgrad_pp0")], "0a")
    (dx, dproj_b, *g_mix), from0a = bwd_mix(0, dcat, dx2, xs[0], proj, wts0[0], small, wsT_b, tril,
                                            job=to_owners_job(part0a[0]))
    small_g[0] = g_mix[:5] + [g_pg, g_mix[5]]
    full_a = rs_add_owners(0, ids, part0a[1], from0a, full[1:], "rs_add_owners0a")
    part0b = chip_sums([wgrad(hnT, dproj_b, 2, True, 1024, "wgrad_in0")], "0b")
    from0b = run_job(to_owners_job(part0b[0]), "rs_to_owners0b")
    full_b = rs_add_owners(0, ids, part0b[1], from0b, full[:1], "rs_add_owners0b")
    gw_in, gw_out, gw_gate, gw_pp = run_job(exchange_halves_job(0, list(full_b) + list(full_a)), "rs_exchange_halves0")
    (d_in, d_out, d_gate, d_pp), (m_in, m_out, m_gate, m_pp), (v_in, v_out, v_gate, v_pp) = adamw_big(
        [w_in, w_out, w_ple_gate, w_ple_proj], [gw_in, gw_out, gw_gate, gw_pp],
        [m_w_in, m_w_out, m_w_ple_gate, m_w_ple_proj], [v_w_in, v_w_out, v_w_ple_gate, v_w_ple_proj])

    small_like = [norm_g, ln_v_g, ln_v_b, w_s, b_s, ple_norm_g]

    def stack(j):
        return jnp.stack([small_g[layer][j].reshape(small_like[j].shape[1:]) for layer in range(DEPTH)])

    grads_small = [stack(j) for j in range(6)] + [g_final.reshape(D_MODEL), jnp.stack([small_g[l][6] for l in range(DEPTH)])]
    zeros_cw = jnp.zeros((DEPTH, 3, WIDTH), F32)
    params = [norm_g, ln_v_g, ln_v_b, w_s, b_s, ple_norm_g, final_g, zeros_cw]
    ms = [m_norm_g, m_ln_v_g, m_ln_v_b, m_w_s, m_b_s, m_ple_norm_g, m_final_g, zeros_cw]
    vs = [v_norm_g, v_ln_v_g, v_ln_v_b, v_w_s, v_b_s, v_ple_norm_g, v_final_g, zeros_cw + 1.0]
    gsum, dsm, msm, vsm = allreduce_small(_pack(grads_small), _pack(params), _pack(ms), _pack(vs))
    gs = _unpack(gsum, params)
    ds = _unpack(dsm, params)
    m2s = _unpack(msm, params)
    v2s = _unpack(vsm, params)
    g_cw_own = lax.dynamic_slice_in_dim(gs[7], own * (WIDTH // N_CHIPS), WIDTH // N_CHIPS, axis=2)
    rows2 = lambda a: a.reshape(DEPTH * 3, WIDTH // N_CHIPS)
    d_cw, m_cw, v_cw = adamw_rows(rows2(cw_rows), rows2(g_cw_own), rows2(jnp.transpose(m_conv_w, (0, 2, 1))),
                                  rows2(jnp.transpose(v_conv_w, (0, 2, 1))))
    back = lambda a: jnp.transpose(a.reshape(DEPTH, 3, WIDTH // N_CHIPS), (0, 2, 1))
    g_conv = jnp.transpose(g_cw_own, (0, 2, 1))

    def ordered(sm, cw_v, w_in_v, w_out_v, gate_v, pp_v):
        return [sm[0], w_in_v, sm[1], sm[2], sm[3], sm[4], cw_v, w_out_v, sm[5], gate_v, pp_v, sm[6]]

    grads = ordered(gs, g_conv, gw_in, gw_out, gw_gate, gw_pp)
    deltas = ordered(ds, back(d_cw), d_in, d_out, d_gate, d_pp)
    new_m = ordered(m2s, back(m_cw), m_in, m_out, m_gate, m_pp)
    new_v = ordered(v2s, back(v_cw), v_in, v_out, v_gate, v_pp)
    return (loss, dx[None], *grads, *deltas, *new_m, *new_v)
```

```python
import jax
import jax.numpy as jnp
from jax import lax
from jax.experimental import pallas as pl
from jax.experimental.pallas import tpu as pltpu

F32 = jnp.float32
BF16 = jnp.bfloat16

SEQ = 8192
D_MODEL = 1024
WIDTH = 512
PROJ = 7 * WIDTH
N_CHIPS = 4
COL_BLK = PROJ // N_CHIPS
PLE = 256
HEADS = 4
CHUNK = 128
DEPTH = 2
EPS = 1e-6
TILE = 256
N_TILES = SEQ // TILE
HALO = 8
VMEM_LIMIT = 60 * 1024 * 1024

ADAM_LR, ADAM_B1, ADAM_B2, ADAM_EPS, ADAM_WD, ADAM_STEP = 0.001, 0.9, 0.999, 1e-08, 0.01, 10
ADAM_C1 = 1.0 - ADAM_B1**ADAM_STEP
ADAM_C2 = 1.0 - ADAM_B2**ADAM_STEP

MESH = pl.DeviceIdType.MESH
ANY = pl.BlockSpec(memory_space=pl.ANY)


def _mm(a, b):
    return lax.dot_general(a, b, (((1,), (0,)), ((), ())), preferred_element_type=F32)


def _mm_nt(a, b):
    return lax.dot_general(a, b, (((1,), (1,)), ((), ())), preferred_element_type=F32)


def _mm_rows(a, w_ref):
    blk = w_ref.shape[1]
    acc = _mm(a[:, 0:blk], w_ref[0])
    for k in range(1, N_CHIPS):
        acc = acc + _mm(a[:, k * blk : (k + 1) * blk], w_ref[k])
    return acc


def _mm_nt_rows(a, w_ref):
    return jnp.concatenate([_mm_nt(a, w_ref[k]) for k in range(N_CHIPS)], axis=-1)


def _load_side_by_side(w_hbm, w_vmem, sems):
    copies = [pltpu.make_async_copy(w_hbm.at[k], w_vmem.at[:, pl.ds(k * COL_BLK, COL_BLK)], sems.at[k])
              for k in range(N_CHIPS)]
    for cp in copies:
        cp.start()
    for cp in copies:
        cp.wait()


def _sigmoid(z):
    return 1.0 / (1.0 + jnp.exp(-z))


def _rms_stats(x):
    r = lax.rsqrt(jnp.mean(x * x, axis=-1, keepdims=True) + EPS)
    return r, x * r


def _rms_bwd(dyg, xh, r):
    return r * (dyg - xh * jnp.mean(dyg * xh, axis=-1, keepdims=True))


def _ln_stats(v):
    mu = jnp.mean(v, axis=-1, keepdims=True)
    vc = v - mu
    rs = lax.rsqrt(jnp.mean(vc * vc, axis=-1, keepdims=True) + EPS)
    return rs, vc * rs


def _mesh_pos():
    x, y, c = lax.axis_index("x"), lax.axis_index("y"), lax.axis_index("c")
    return x, y, c, 2 * x + y


def _peer(x, y, c, r):
    return ((1 - x) if (r >> 1) else x, (1 - y) if (r & 1) else y, c)


def _full(shape):
    return pl.BlockSpec(shape, lambda *_: (0,) * len(shape))


def _const(shape, pos):
    return pl.BlockSpec((None,) * len(pos) + tuple(shape), lambda *_: tuple(pos) + (0,) * len(shape))


class Job:
    def __init__(self, ins, out_shapes, sems, start, middle, finish, aliases=None):
        self.ins, self.out_shapes, self.sems = list(ins), list(out_shapes), list(sems)
        self.start, self.middle, self.finish = start, middle, finish
        self.aliases = aliases or {}


def run_job(job, name):
    ni, no = len(job.ins), len(job.out_shapes)

    def body(*refs):
        parts = (refs[:ni], refs[ni : ni + no], refs[ni + no :])
        job.start(*parts)
        if job.middle is not None:
            job.middle(*parts)
        job.finish(*parts)

    return pl.pallas_call(
        body, name=name, out_shape=job.out_shapes, in_specs=[ANY] * ni, out_specs=[ANY] * no, scratch_shapes=job.sems,
        input_output_aliases=job.aliases, compiler_params=pltpu.CompilerParams(has_side_effects=True),
    )(*job.ins)


def _call(body, *, name, grid, in_specs, out_specs, out_shape, args, scratch=(), job=None, mid=None):
    params = pltpu.CompilerParams(dimension_semantics=("arbitrary",) * len(grid), vmem_limit_bytes=VMEM_LIMIT,
                                  has_side_effects=job is not None)
    n_in, n_out, n_sc = len(in_specs), len(out_specs), len(scratch)
    if job is None:
        outs = pl.pallas_call(body, name=name, grid=grid, in_specs=in_specs, out_specs=out_specs, out_shape=out_shape,
                              scratch_shapes=list(scratch), compiler_params=params)(*args)
        return list(outs), []
    ji, jo = len(job.ins), len(job.out_shapes)
    assert not job.aliases and (job.middle is None or len(grid) == 1)

    def wrapped(*refs):
        ins, jin = refs[:n_in], refs[n_in : n_in + ji]
        o0 = n_in + ji
        outs, jout = refs[o0 : o0 + n_out], refs[o0 + n_out : o0 + n_out + jo]
        s0 = o0 + n_out + jo
        sc, jsem = refs[s0 : s0 + n_sc], refs[s0 + n_sc :]
        step = pl.program_id(0)
        for d in range(1, len(grid)):
            step = step * grid[d] + pl.program_id(d)
        n_steps = 1
        for g in grid:
            n_steps *= g

        @pl.when(step == 0)
        def _():
            job.start(jin, jout, jsem)

        if job.middle is not None:
            @pl.when(step == mid)
            def _():
                job.middle(jin, jout, jsem)

        body(*ins, *outs, *sc)

        @pl.when(step == n_steps - 1)
        def _():
            job.finish(jin, jout, jsem)

    outs = pl.pallas_call(
        wrapped, name=name, grid=grid, in_specs=list(in_specs) + [ANY] * ji, out_specs=list(out_specs) + [ANY] * jo,
        out_shape=list(out_shape) + job.out_shapes, scratch_shapes=list(scratch) + job.sems, compiler_params=params,
    )(*args, *job.ins)
    return list(outs[:n_out]), list(outs[n_out:])


def gather_job(items):
    n = len(items)

    def src_half(a, ref, h):
        arr, layer = items[a]
        if layer is None:
            return ref.at[h]
        rh = arr.shape[1] // 2
        return ref.at[layer, pl.ds(h * rh, rh)]

    def dst_half(a, ref, chip, h):
        arr, layer = items[a]
        if layer is None:
            return ref.at[chip, h]
        rh = arr.shape[1] // 2
        return ref.at[chip, pl.ds(h * rh, rh)]

    def copies(src, dst, scratch):
        stage, (in_sem, out_sem, ici_s, ici_r, fwd_s, fwd_r) = scratch[:n], scratch[n:]
        x, y, c, own = _mesh_pos()
        local, ici, fwd, got = [], {}, {}, {}
        for a in range(n):
            layer = items[a][1]
            local.append((pltpu.make_async_copy(src[a] if layer is None else src[a].at[layer], stage[a], in_sem.at[a]),
                          pltpu.make_async_copy(stage[a], dst[a].at[own], out_sem.at[a])))
            for r in (1, 2, 3):
                k = 3 * a + r - 1
                ici[a, r] = pltpu.make_async_remote_copy(
                    src_ref=src_half(a, src[a], c), dst_ref=dst_half(a, dst[a], own, c), send_sem=ici_s.at[k],
                    recv_sem=ici_r.at[k], device_id=_peer(x, y, c, r), device_id_type=MESH)
                fwd[a, r] = pltpu.make_async_remote_copy(
                    src_ref=dst_half(a, dst[a], own ^ r, c), dst_ref=dst_half(a, dst[a], own ^ r, c), send_sem=fwd_s.at[k],
                    recv_sem=fwd_r.at[k], device_id=(x, y, 1 - c), device_id_type=MESH)
                got[a, r] = pltpu.make_async_remote_copy(
                    src_ref=dst_half(a, dst[a], own ^ r, 1 - c), dst_ref=dst_half(a, dst[a], own ^ r, 1 - c),
                    send_sem=fwd_s.at[k], recv_sem=fwd_r.at[k], device_id=(x, y, 1 - c), device_id_type=MESH)
        return local, ici, fwd, got

    def start(src, dst, sems):
        local, ici, _, _ = copies(src, dst, sems)
        for a in range(n):
            for r in (1, 2, 3):
                ici[a, r].start()
        for cp_in, _ in local:
            cp_in.start()
        for cp_in, cp_out in local:
            cp_in.wait()
            cp_out.start()

    def middle(src, dst, sems):
        _, ici, fwd, _ = copies(src, dst, sems)
        for r in (1, 2, 3):
            for a in range(n):
                ici[a, r].wait_recv()
                fwd[a, r].start()

    def finish(src, dst, sems):
        local, ici, fwd, got = copies(src, dst, sems)
        for a in range(n):
            for r in (1, 2, 3):
                got[a, r].wait_recv()
        for a in range(n):
            for r in (1, 2, 3):
                ici[a, r].wait_send()
                fwd[a, r].wait_send()
        for _, cp_out in local:
            cp_out.wait()

    out_shapes = [jax.ShapeDtypeStruct((N_CHIPS,) + (arr.shape if layer is None else arr.shape[1:]), arr.dtype)
                  for arr, layer in items]
    stage = [pltpu.VMEM(arr.shape if layer is None else arr.shape[1:], arr.dtype) for arr, layer in items]
    sems = [pltpu.SemaphoreType.DMA((n,))] * 2 + [pltpu.SemaphoreType.DMA((3 * n,))] * 4
    return Job([arr for arr, _ in items], out_shapes, stage + sems, start, middle, finish)


def _mixer_fwd(proj_ref, lg, lb, ws_ref, bsb_ref, cw_ref, mix_ref, xcbuf, halo_xc):
    u = proj_ref[:, 0 * WIDTH : 1 * WIDTH]
    v = proj_ref[:, 1 * WIDTH : 2 * WIDTH]
    za = proj_ref[:, 2 * WIDTH : 3 * WIDTH]
    h = proj_ref[:, 3 * WIDTH : 4 * WIDTH]
    gb = proj_ref[:, 4 * WIDTH : 5 * WIDTH]
    gc = proj_ref[:, 5 * WIDTH : 6 * WIDTH]
    zb = proj_ref[:, 6 * WIDTH : 7 * WIDTH]
    rs, vhat = _ln_stats(v)
    vl = vhat * lg + lb
    vlb = vl.astype(BF16)
    for j in range(TILE // CHUNK):
        rows = slice(j * CHUNK, (j + 1) * CHUNK)
        for hd in range(HEADS):
            cols = slice(hd * CHUNK, (hd + 1) * CHUNK)
            mix_ref[rows, cols] = _mm(ws_ref[hd], vlb[rows, cols]) + bsb_ref[:, cols]
    mixed = mix_ref[...]
    siga = _sigmoid(za)
    sigb = _sigmoid(zb)
    xc = gc * h
    xcbuf[0:HALO, :] = halo_xc
    xcbuf[HALO : HALO + TILE, :] = xc
    y = cw_ref[0:1, :] * xcbuf[HALO - 2 : HALO - 2 + TILE, :] + cw_ref[1:2, :] * xcbuf[HALO - 1 : HALO - 1 + TILE, :]
    y = y + cw_ref[2:3, :] * xc
    return dict(u=u, za=za, h=h, gb=gb, gc=gc, zb=zb, rs=rs, vhat=vhat, vlb=vlb, mixed=mixed, siga=siga, sigb=sigb,
                xc=xc, y=y)


def fwd_layer(layer, x, p, wts, small, job=None, head=None):
    win, wout, wg, wpp = wts
    ng, lg, lb, ws, bsb, cw, pg = small
    n_head = 0 if head is None else 2

    def body(*refs):
        (x_ref, p_ref, win_ref, wout_ref, wg_ref, wpp_ref, ng_ref, lg_ref, lb_ref, ws_ref, bsb_ref, cw_ref,
         pg_ref) = refs[:13]
        head_in = refs[13 : 13 + n_head]
        (proj_ref, x2_ref, gate_ref, pp_ref, x3_ref, hnT_ref, catT_ref, hn2T_ref,
         pT_ref) = refs[13 + n_head : 22 + n_head]
        head_out = refs[22 + n_head : 22 + 2 * n_head]
        mix_ref, xcbuf, carry, wcat, wsem = refs[22 + 2 * n_head :]
        i = pl.program_id(0)

        @pl.when(i == 0)
        def _():
            _load_side_by_side(win_ref, wcat, wsem)
            carry[...] = jnp.zeros_like(carry)
            for ref in head_out:
                ref[...] = jnp.zeros_like(ref)

        xv = x_ref[...]
        _, xh = _rms_stats(xv)
        hn = xh * ng_ref[...]
        hnT_ref[...] = hn.T.astype(BF16)
        proj_ref[...] = _mm(hn.astype(BF16), wcat[...])
        m = _mixer_fwd(proj_ref, lg_ref[...], lb_ref[...], ws_ref, bsb_ref, cw_ref, mix_ref, xcbuf, carry[...])
        carry[...] = m["xc"][TILE - HALO : TILE, :]
        out_a = (m["u"] * m["mixed"]) * (m["za"] * m["siga"])
        out_b = (m["gb"] * m["y"]) * (m["zb"] * m["sigb"])
        cat = jnp.concatenate([out_a, out_b], axis=-1)
        catT_ref[...] = cat.T.astype(BF16)
        x2 = xv + _mm_rows(cat.astype(BF16), wout_ref)
        x2_ref[...] = x2
        _, xh2 = _rms_stats(x2)
        hn2 = xh2 * pg_ref[...]
        hn2T_ref[...] = hn2.T.astype(BF16)
        gate = _sigmoid(_mm_rows(hn2.astype(BF16), wg_ref))
        gate_ref[...] = gate
        pv = p_ref[...]
        pT_ref[...] = pv.T.astype(BF16)
        pb = pv.astype(BF16)
        for k in range(N_CHIPS):
            pp_ref[:, k * PLE : (k + 1) * PLE] = _mm(pb, wpp_ref[k])
        x3 = x2 + gate * pp_ref[...]
        if head is None:
            x3_ref[...] = x3
        else:
            t_ref, gf_ref = head_in
            loss_ref, gg_ref = head_out
            r3, xh3 = _rms_stats(x3)
            gf = gf_ref[...]
            err = xh3 * gf - t_ref[...]
            loss_ref[...] += (0.5 / D_MODEL) * jnp.sum(err * err).reshape(1, 1)
            dy = err * (1.0 / D_MODEL)
            gg_ref[...] += jnp.sum(dy * xh3, axis=0, keepdims=True)
            x3_ref[...] = _rms_bwd(dy * gf, xh3, r3)

    tok = lambda w: pl.BlockSpec((TILE, w), lambda i: (i, 0))
    tokT = lambda w: pl.BlockSpec((w, TILE), lambda i: (0, i))
    once = lambda a: pl.BlockSpec(a.shape, lambda i: (0,) * a.ndim, pipeline_mode=pl.Buffered(1))
    f32 = lambda w: jax.ShapeDtypeStruct((SEQ, w), F32)
    bfT = lambda w: jax.ShapeDtypeStruct((w, SEQ), BF16)
    head_specs = [] if head is None else [tok(D_MODEL), _full((1, D_MODEL))]
    head_outs = [] if head is None else [_full((1, 1)), _full((1, D_MODEL))]
    head_shapes = [] if head is None else [jax.ShapeDtypeStruct((1, 1), F32), jax.ShapeDtypeStruct((1, D_MODEL), F32)]
    return _call(
        body, name=f"fwd_layer{layer}", grid=(N_TILES,),
        in_specs=[tok(D_MODEL), pl.BlockSpec((None, None, TILE, PLE), lambda i: (layer, 0, i, 0)),
                  ANY, once(wout), once(wg), once(wpp),
                  _const((1, D_MODEL), (layer,)), _const((1, WIDTH), (layer,)), _const((1, WIDTH), (layer,)),
                  _const((HEADS, CHUNK, CHUNK), (layer,)), _const((CHUNK, WIDTH), (layer,)), _const((3, WIDTH), (layer,)),
                  _const((1, D_MODEL), (layer,))] + head_specs,
        out_specs=[tok(PROJ), tok(D_MODEL), tok(D_MODEL), tok(D_MODEL), tok(D_MODEL),
                   tokT(D_MODEL), tokT(D_MODEL), tokT(D_MODEL), tokT(PLE)] + head_outs,
        out_shape=[f32(PROJ), f32(D_MODEL), f32(D_MODEL), f32(D_MODEL), f32(D_MODEL),
                   bfT(D_MODEL), bfT(D_MODEL), bfT(D_MODEL), bfT(PLE)] + head_shapes,
        scratch=[pltpu.VMEM((TILE, WIDTH), F32), pltpu.VMEM((HALO + TILE, WIDTH), F32), pltpu.VMEM((HALO, WIDTH), F32),
                 pltpu.VMEM((D_MODEL, PROJ), BF16), pltpu.SemaphoreType.DMA((N_CHIPS,))],
        args=(x, p, win, wout, wg, wpp, ng, lg, lb, ws, bsb, cw, pg) + (() if head is None else tuple(head)),
        job=job, mid=N_TILES // 2)


def bwd_ple(layer, dx3, x2, gate, pp, wg, wout, pg, job=None):
    def body(dx3_ref, x2_ref, gate_ref, pp_ref, wg_ref, wout_ref, pg_ref,
             dpp_ref, dgl_ref, dx2_ref, dx2b_ref, dcat_ref, gpg_ref):
        i = pl.program_id(0)

        @pl.when(i == 0)
        def _():
            gpg_ref[...] = jnp.zeros_like(gpg_ref)

        dx3v = dx3_ref[...]
        gate_v = gate_ref[...]
        dpp_ref[...] = (dx3v * gate_v).astype(BF16)
        dgl = ((dx3v * pp_ref[...]) * gate_v * (1.0 - gate_v)).astype(BF16)
        dgl_ref[...] = dgl
        dhn2 = _mm_nt_rows(dgl, wg_ref)
        r2, xh2 = _rms_stats(x2_ref[...])
        gpg_ref[...] += jnp.sum(dhn2 * xh2, axis=0, keepdims=True)
        dx2 = dx3v + _rms_bwd(dhn2 * pg_ref[...], xh2, r2)
        dx2_ref[...] = dx2
        dx2b = dx2.astype(BF16)
        dx2b_ref[...] = dx2b
        dcat_ref[...] = _mm_nt_rows(dx2b, wout_ref)

    tok = pl.BlockSpec((TILE, D_MODEL), lambda i: (i, 0))
    f32 = jax.ShapeDtypeStruct((SEQ, D_MODEL), F32)
    b16 = jax.ShapeDtypeStruct((SEQ, D_MODEL), BF16)
    return _call(
        body, name=f"bwd_ple{layer}", grid=(N_TILES,),
        in_specs=[tok, tok, tok, tok, _full(wg.shape), _full(wout.shape), _const((1, D_MODEL), (layer,))],
        out_specs=[tok, tok, tok, tok, tok, _full((1, D_MODEL))],
        out_shape=[b16, b16, f32, b16, f32, jax.ShapeDtypeStruct((1, D_MODEL), F32)],
        args=(dx3, x2, gate, pp, wg, wout, pg), job=job)


def bwd_mix(layer, dcat, dx2, x, proj, win, small, wsT, tril, job=None):
    ng, lg, lb, ws, bsb, cw, _ = small

    def body(dcat_ref, dx2_ref, x_ref, proj_ref, halo_ref, win_ref, ng_ref, lg_ref, lb_ref, ws_ref, wsT_ref, bsb_ref,
             cw_ref, tril_ref,
             dx_ref, dproj_ref, gn_ref, glg_ref, glb_ref, gws_ref, gbs_ref, gcw_ref,
             mix_ref, xcbuf, dycbuf, dvl_ref, bs_acc, wcat, wsem):
        i = pl.program_id(0)

        @pl.when(i == 0)
        def _():
            _load_side_by_side(win_ref, wcat, wsem)
            for ref in (gn_ref, glg_ref, glb_ref, gws_ref, gcw_ref, bs_acc):
                ref[...] = jnp.zeros_like(ref)
            dycbuf[TILE : TILE + HALO, :] = jnp.zeros((HALO, WIDTH), F32)

        lgv = lg_ref[...]
        halo_xc = halo_ref[:, 5 * WIDTH : 6 * WIDTH] * halo_ref[:, 3 * WIDTH : 4 * WIDTH]
        halo_xc = jnp.where(i == N_TILES - 1, 0.0, halo_xc)
        m = _mixer_fwd(proj_ref, lgv, lb_ref[...], ws_ref, bsb_ref, cw_ref, mix_ref, xcbuf, halo_xc)
        u, za, h, gb, gc, zb = m["u"], m["za"], m["h"], m["gb"], m["gc"], m["zb"]
        mixed, siga, sigb, xc, y = m["mixed"], m["siga"], m["sigb"], m["xc"], m["y"]
        doa = dcat_ref[:, 0:WIDTH]
        dob = dcat_ref[:, WIDTH : 2 * WIDTH]
        sa = za * siga
        sb = zb * sigb
        doa_sa = doa * sa
        dproj_ref[:, 0 * WIDTH : 1 * WIDTH] = (doa_sa * mixed).astype(BF16)
        dmixed = doa_sa * u
        dza = (doa * (u * mixed)) * (siga * (1.0 + za * (1.0 - siga)))
        dproj_ref[:, 2 * WIDTH : 3 * WIDTH] = dza.astype(BF16)
        dob_sb = dob * sb
        dproj_ref[:, 4 * WIDTH : 5 * WIDTH] = (dob_sb * y).astype(BF16)
        dyc = dob_sb * gb
        dzb = (dob * (gb * y)) * (sigb * (1.0 + zb * (1.0 - sigb)))
        dproj_ref[:, 6 * WIDTH : 7 * WIDTH] = dzb.astype(BF16)
        dycbuf[0:TILE, :] = dyc
        dyc1 = dycbuf[1 : 1 + TILE, :]
        dyc2 = dycbuf[2 : 2 + TILE, :]
        dxc = cw_ref[2:3, :] * dyc + cw_ref[1:2, :] * dyc1 + cw_ref[0:1, :] * dyc2
        gcw_ref[0:1, :] += jnp.sum(xc * dyc2, axis=0, keepdims=True)
        gcw_ref[1:2, :] += jnp.sum(xc * dyc1, axis=0, keepdims=True)
        gcw_ref[2:3, :] += jnp.sum(xc * dyc, axis=0, keepdims=True)
        dycbuf[TILE : TILE + HALO, :] = dyc[0:HALO, :]
        dproj_ref[:, 5 * WIDTH : 6 * WIDTH] = (dxc * h).astype(BF16)
        dproj_ref[:, 3 * WIDTH : 4 * WIDTH] = (dxc * gc).astype(BF16)
        dmb = dmixed.astype(BF16)
        vlb = m["vlb"]
        bsum = jnp.zeros((CHUNK, WIDTH), F32)
        for j in range(TILE // CHUNK):
            rows = slice(j * CHUNK, (j + 1) * CHUNK)
            bsum = bsum + dmixed[rows, :]
            for hd in range(HEADS):
                cols = slice(hd * CHUNK, (hd + 1) * CHUNK)
                gws_ref[hd] += _mm_nt(dmb[rows, cols], vlb[rows, cols])
                dvl_ref[rows, cols] = _mm(wsT_ref[hd], dmb[rows, cols])
        bs_acc[...] += bsum
        dvl = dvl_ref[...]
        vhat = m["vhat"]
        glb_ref[...] += jnp.sum(dvl, axis=0, keepdims=True)
        glg_ref[...] += jnp.sum(dvl * vhat, axis=0, keepdims=True)
        dvh = dvl * lgv
        dv = m["rs"] * (dvh - jnp.mean(dvh, axis=-1, keepdims=True) - vhat * jnp.mean(dvh * vhat, axis=-1, keepdims=True))
        dproj_ref[:, 1 * WIDTH : 2 * WIDTH] = dv.astype(BF16)
        dhn = _mm_nt(dproj_ref[...], wcat[...])
        r1, xh = _rms_stats(x_ref[...])
        gn_ref[...] += jnp.sum(dhn * xh, axis=0, keepdims=True)
        dx_ref[...] = dx2_ref[...] + _rms_bwd(dhn * ng_ref[...], xh, r1)

        @pl.when(i == N_TILES - 1)
        def _():
            for hd in range(HEADS):
                gws_ref[hd] = gws_ref[hd] * tril_ref[...]
                gbs_ref[hd : hd + 1, :] = jnp.sum(bs_acc[:, hd * CHUNK : (hd + 1) * CHUNK].T, axis=0, keepdims=True)

    rev = lambda w: pl.BlockSpec((TILE, w), lambda i: (N_TILES - 1 - i, 0))
    halo = pl.BlockSpec((HALO, PROJ), lambda i: (jnp.maximum((N_TILES - 1 - i) * (TILE // HALO) - 1, 0), 0))
    return _call(
        body, name=f"bwd_mix{layer}", grid=(N_TILES,),
        in_specs=[rev(D_MODEL), rev(D_MODEL), rev(D_MODEL), rev(PROJ), halo, ANY,
                  _const((1, D_MODEL), (layer,)), _const((1, WIDTH), (layer,)), _const((1, WIDTH), (layer,)),
                  _const((HEADS, CHUNK, CHUNK), (layer,)), _const((HEADS, CHUNK, CHUNK), (layer,)),
                  _const((CHUNK, WIDTH), (layer,)), _const((3, WIDTH), (layer,)), _full((CHUNK, CHUNK))],
        out_specs=[rev(D_MODEL), rev(PROJ), _full((1, D_MODEL)), _full((1, WIDTH)), _full((1, WIDTH)),
                   _full((HEADS, CHUNK, CHUNK)), _full((HEADS, CHUNK)), _full((3, WIDTH))],
        out_shape=[jax.ShapeDtypeStruct((SEQ, D_MODEL), F32), jax.ShapeDtypeStruct((SEQ, PROJ), BF16),
                   jax.ShapeDtypeStruct((1, D_MODEL), F32), jax.ShapeDtypeStruct((1, WIDTH), F32),
                   jax.ShapeDtypeStruct((1, WIDTH), F32), jax.ShapeDtypeStruct((HEADS, CHUNK, CHUNK), F32),
                   jax.ShapeDtypeStruct((HEADS, CHUNK), F32), jax.ShapeDtypeStruct((3, WIDTH), F32)],
        scratch=[pltpu.VMEM((TILE, WIDTH), F32), pltpu.VMEM((HALO + TILE, WIDTH), F32),
                 pltpu.VMEM((TILE + HALO, WIDTH), F32), pltpu.VMEM((TILE, WIDTH), F32), pltpu.VMEM((CHUNK, WIDTH), F32),
                 pltpu.VMEM((D_MODEL, PROJ), BF16), pltpu.SemaphoreType.DMA((N_CHIPS,))],
        args=(dcat, dx2, x, proj, proj, win, ng, lg, lb, ws, wsT, bsb, cw, tril), job=job)


def wgrad(pairs, n_tiles, col_blocked, tk, name, job=None):
    n = len(pairs)
    m_dim, n_dim = pairs[0][0].shape[0], pairs[0][1].shape[1]
    tn = n_dim // n_tiles
    cb, mb = n_dim // N_CHIPS, m_dim // N_CHIPS
    per_tile = N_CHIPS // n_tiles
    assert col_blocked or n_tiles == 1

    def body(*refs):
        for a in range(n):
            a_ref, b_ref, o_ref = refs[2 * a], refs[2 * a + 1], refs[2 * n + a]

            @pl.when(pl.program_id(1) == 0)
            def _():
                o_ref[...] = jnp.zeros_like(o_ref)

            prod = _mm(a_ref[...], b_ref[...])
            if col_blocked:
                for q in range(per_tile):
                    o_ref[q] += prod[:, q * cb : (q + 1) * cb]
            else:
                for q in range(N_CHIPS):
                    o_ref[q] += prod[q * mb : (q + 1) * mb, :]

    if col_blocked:
        shape = (N_CHIPS, m_dim, cb)
        o_spec = pl.BlockSpec((per_tile, m_dim, cb), lambda j, k: (j, 0, 0))
    else:
        shape = (N_CHIPS, mb, n_dim)
        o_spec = pl.BlockSpec((N_CHIPS, mb, n_dim), lambda j, k: (0, 0, 0))
    return _call(
        body, name=name, grid=(n_tiles, SEQ // tk),
        in_specs=[pl.BlockSpec((m_dim, tk), lambda j, k: (0, k)), pl.BlockSpec((tk, tn), lambda j, k: (k, j))] * n,
        out_specs=[o_spec] * n, out_shape=[jax.ShapeDtypeStruct(shape, F32)] * n,
        args=[t for pair in pairs for t in pair], job=job)


def to_sibling_job(grads):
    n = len(grads)

    def copies(src, dst, sems):
        x, y, c, _ = _mesh_pos()
        out = []
        for a in range(n):
            rh = grads[a].shape[1] // 2
            out.append(pltpu.make_async_remote_copy(
                src_ref=src[a].at[:, pl.ds((1 - c) * rh, rh), :], dst_ref=dst[a], send_sem=sems[0].at[a],
                recv_sem=sems[1].at[a], device_id=(x, y, 1 - c), device_id_type=MESH))
        return out

    def start(src, dst, sems):
        for cp in copies(src, dst, sems):
            cp.start()

    def finish(src, dst, sems):
        for cp in copies(src, dst, sems):
            cp.wait()

    out_shapes = [jax.ShapeDtypeStruct((N_CHIPS, g.shape[1] // 2, g.shape[2]), F32) for g in grads]
    return Job(grads, out_shapes, [pltpu.SemaphoreType.DMA((n,))] * 2, start, None, finish)


def rs_add_sibling(ids, grads, recvd, name):
    n = len(grads)

    def body(ids_ref, *refs):
        g, r = refs[:n], refs[n : 2 * n]
        pb, pf = refs[2 * n : 3 * n], refs[3 * n :]
        k = pl.program_id(0)
        for a in range(n):
            s = g[a][...] + r[a][...]
            pb[a][...] = s.astype(BF16)

            @pl.when(k == ids_ref[1])
            def _():
                pf[a][...] = s

    in_specs, out_specs, out_shape = [], [], []
    for g in grads:
        rh, cc = g.shape[1] // 2, g.shape[2]
        in_specs.append(pl.BlockSpec((None, rh, cc), lambda k, ids: (k, ids[0], 0)))
    for g in grads:
        rh, cc = g.shape[1] // 2, g.shape[2]
        in_specs.append(pl.BlockSpec((None, rh, cc), lambda k, ids: (k, 0, 0)))
        out_specs.append(pl.BlockSpec((None, rh, cc), lambda k, ids: (k, 0, 0)))
        out_shape.append(jax.ShapeDtypeStruct((N_CHIPS, rh, cc), BF16))
    for g in grads:
        rh, cc = g.shape[1] // 2, g.shape[2]
        out_specs.append(pl.BlockSpec((rh, cc), lambda k, ids: (0, 0)))
        out_shape.append(jax.ShapeDtypeStruct((rh, cc), F32))
    outs = pl.pallas_call(
        body, name=name, out_shape=out_shape,
        grid_spec=pltpu.PrefetchScalarGridSpec(num_scalar_prefetch=1, grid=(N_CHIPS,), in_specs=in_specs,
                                               out_specs=out_specs),
        compiler_params=pltpu.CompilerParams(dimension_semantics=("arbitrary",), vmem_limit_bytes=VMEM_LIMIT),
    )(ids, *grads, *recvd)
    return list(outs[:n]), list(outs[n:])


def to_owners_job(partials):
    n = len(partials)

    def copies(src, dst, sems):
        x, y, c, own = _mesh_pos()
        out = []
        for a in range(n):
            for r in (1, 2, 3):
                out.append(pltpu.make_async_remote_copy(
                    src_ref=src[a].at[own ^ r], dst_ref=dst[a].at[r - 1], send_sem=sems[0].at[3 * a + r - 1],
                    recv_sem=sems[1].at[3 * a + r - 1], device_id=_peer(x, y, c, r), device_id_type=MESH))
        return out

    def start(src, dst, sems):
        for cp in copies(src, dst, sems):
            cp.start()

    def finish(src, dst, sems):
        for cp in copies(src, dst, sems):
            cp.wait()

    out_shapes = [jax.ShapeDtypeStruct((3,) + p.shape[1:], BF16) for p in partials]
    return Job(partials, out_shapes, [pltpu.SemaphoreType.DMA((3 * n,))] * 2, start, None, finish)


def rs_add_owners(layer, ids, own_f32, recvd, prev, name):
    n = len(own_f32)
    nb = 2

    def body(ids_ref, *refs):
        o, r, f = refs[:n], refs[n : 2 * n], refs[-n:]
        for a in range(n):
            f[a][...] = ((o[a][...] + r[a][0].astype(F32)) + r[a][1].astype(F32)) + r[a][2].astype(F32)

    in_specs, out_specs, out_shape = [], [], []
    for o in own_f32:
        in_specs.append(pl.BlockSpec((o.shape[0] // nb, o.shape[1]), lambda j, ids: (j, 0)))
    for o in own_f32:
        in_specs.append(pl.BlockSpec((3, o.shape[0] // nb, o.shape[1]), lambda j, ids: (0, j, 0)))
        out_specs.append(pl.BlockSpec((None, o.shape[0] // nb, o.shape[1]), lambda j, ids: (layer, ids[0] * nb + j, 0)))
        out_shape.append(jax.ShapeDtypeStruct((DEPTH, 2 * o.shape[0], o.shape[1]), F32))
    args, aliases = [ids, *own_f32, *recvd], {}
    if prev is not None:
        in_specs += [ANY] * n
        args += list(prev)
        aliases = {1 + 2 * n + a: a for a in range(n)}
    return pl.pallas_call(
        body, name=name, out_shape=out_shape, input_output_aliases=aliases,
        grid_spec=pltpu.PrefetchScalarGridSpec(num_scalar_prefetch=1, grid=(nb,), in_specs=in_specs, out_specs=out_specs),
        compiler_params=pltpu.CompilerParams(dimension_semantics=("arbitrary",), vmem_limit_bytes=VMEM_LIMIT),
    )(*args)


def exchange_halves_job(layer, full):
    n = len(full)

    def copies(src, dst, sems):
        x, y, c, _ = _mesh_pos()
        out = []
        for a in range(n):
            rh = full[a].shape[1] // 2
            out.append(pltpu.make_async_remote_copy(
                src_ref=src[a].at[layer, pl.ds(c * rh, rh), :], dst_ref=dst[a].at[layer, pl.ds(c * rh, rh), :],
                send_sem=sems[0].at[a], recv_sem=sems[1].at[a], device_id=(x, y, 1 - c), device_id_type=MESH))
        return out

    def start(src, dst, sems):
        for cp in copies(src, dst, sems):
            cp.start()

    def finish(src, dst, sems):
        for cp in copies(src, dst, sems):
            cp.wait()

    out_shapes = [jax.ShapeDtypeStruct(f.shape, F32) for f in full]
    return Job(full, out_shapes, [pltpu.SemaphoreType.DMA((n,))] * 2, start, None, finish, {a: a for a in range(n)})


def _adamw(w, g, m, v):
    m2 = ADAM_B1 * m + (1.0 - ADAM_B1) * g
    v2 = ADAM_B2 * v + (1.0 - ADAM_B2) * (g * g)
    delta = -ADAM_LR * ((m2 / ADAM_C1) / (jnp.sqrt(v2 / ADAM_C2) + ADAM_EPS) + ADAM_WD * w)
    return delta, m2, v2


def adamw_big(ws, gs, ms, vs, name, job=None):
    n = len(ws)
    nb = 4

    def body(*refs):
        for a in range(n):
            w, g, m, v = (refs[j * n + a][...] for j in range(4))
            d, m2, v2 = _adamw(w, g, m, v)
            refs[4 * n + a][...] = d
            refs[5 * n + a][...] = m2
            refs[6 * n + a][...] = v2

    specs = [pl.BlockSpec((None, w.shape[1] // nb, w.shape[2]), lambda l, j: (l, j, 0)) for w in ws]
    shapes = [jax.ShapeDtypeStruct(w.shape, F32) for w in ws]
    outs, job_outs = _call(body, name=name, grid=(DEPTH, nb), in_specs=specs * 4, out_specs=specs * 3,
                           out_shape=shapes * 3, args=(*ws, *gs, *ms, *vs), job=job)
    return (outs[:n], outs[n : 2 * n], outs[2 * n :]), job_outs


def allreduce_small(g, w, m, v):
    rows = g.shape[0]

    def body(g_ref, w_ref, m_ref, v_ref, gsum_ref, d_ref, m2_ref, v2_ref, sib_buf, slots, s1, r1, s2, r2):
        x, y, c, own = _mesh_pos()
        first = pltpu.make_async_remote_copy(src_ref=g_ref, dst_ref=sib_buf, send_sem=s1, recv_sem=r1,
                                             device_id=(x, y, 1 - c), device_id_type=MESH)
        first.start()
        first.wait()
        slots[0] = g_ref[...] + sib_buf[...]
        cps = [pltpu.make_async_remote_copy(src_ref=slots.at[0], dst_ref=slots.at[r], send_sem=s2.at[r - 1],
                                            recv_sem=r2.at[r - 1], device_id=_peer(x, y, c, r), device_id_type=MESH)
               for r in (1, 2, 3)]
        for cp in cps:
            cp.start()
        for cp in cps:
            cp.wait()
        tot = slots[own] + slots[own ^ 1]
        tot = tot + slots[own ^ 2]
        tot = tot + slots[own ^ 3]
        gsum_ref[...] = tot
        d, m2, v2 = _adamw(w_ref[...], tot, m_ref[...], v_ref[...])
        d_ref[...] = d
        m2_ref[...] = m2
        v2_ref[...] = v2

    vm = pl.BlockSpec(memory_space=pltpu.VMEM)
    shape = jax.ShapeDtypeStruct(g.shape, F32)
    return pl.pallas_call(
        body, name="allreduce_small", out_shape=[shape] * 4, in_specs=[vm] * 4, out_specs=[vm] * 4,
        scratch_shapes=[pltpu.VMEM((rows, 128), F32), pltpu.VMEM((N_CHIPS, rows, 128), F32),
                        pltpu.SemaphoreType.DMA, pltpu.SemaphoreType.DMA,
                        pltpu.SemaphoreType.DMA((3,)), pltpu.SemaphoreType.DMA((3,))],
        compiler_params=pltpu.CompilerParams(has_side_effects=True),
    )(g, w, m, v)


def adamw_rows(w, g, m, v):
    def body(w_ref, g_ref, m_ref, v_ref, d_ref, m2_ref, v2_ref):
        d, m2, v2 = _adamw(w_ref[...], g_ref[...], m_ref[...], v_ref[...])
        d_ref[...] = d
        m2_ref[...] = m2
        v2_ref[...] = v2

    return pl.pallas_call(body, name="adamw_conv", out_shape=[jax.ShapeDtypeStruct(w.shape, F32)] * 3)(w, g, m, v)


def _pack(parts):
    out = []
    for a in parts:
        flat = a.reshape(-1)
        rows = -(-flat.shape[0] // 1024) * 8
        flat = jnp.pad(flat, (0, rows * 128 - flat.shape[0]))
        out.append(flat.reshape(rows, 128))
    return jnp.concatenate(out, axis=0)


def _unpack(packed, like):
    out, row = [], 0
    for a in like:
        size = 1
        for s in a.shape:
            size *= s
        rows = -(-size // 1024) * 8
        out.append(packed[row : row + rows].reshape(-1)[:size].reshape(a.shape))
        row += rows
    return out


def kernel(x, p, norm_g, w_in, ln_v_g, ln_v_b, w_s, b_s, conv_w, w_out, ple_norm_g, w_ple_gate, w_ple_proj, final_g, loss_target, m_norm_g, m_w_in, m_ln_v_g, m_ln_v_b, m_w_s, m_b_s, m_conv_w, m_w_out, m_ple_norm_g, m_w_ple_gate, m_w_ple_proj, m_final_g, v_norm_g, v_w_in, v_ln_v_g, v_ln_v_b, v_w_s, v_b_s, v_conv_w, v_w_out, v_ple_norm_g, v_w_ple_gate, v_w_ple_proj, v_final_g):
    cx, cy, cc = lax.axis_index("x"), lax.axis_index("y"), lax.axis_index("c")
    own = 2 * cx + cy
    ids = jnp.stack([cc, own]).astype(jnp.int32)

    cw_rows = jnp.transpose(conv_w, (0, 2, 1))
    shards = [w_in.astype(BF16), w_out.astype(BF16), w_ple_gate.astype(BF16), w_ple_proj.astype(BF16)]
    *wts0, g_cw = run_job(gather_job([(s, 0) for s in shards] + [(cw_rows, None)]), "gather_weights0")
    cw_full = jnp.transpose(g_cw, (1, 2, 0, 3)).reshape(DEPTH, 3, WIDTH)
    tril = jnp.tril(jnp.ones((CHUNK, CHUNK), F32))
    ws_masked = w_s * tril[None, None]
    ws_b = ws_masked.astype(BF16)
    wsT_b = jnp.swapaxes(ws_masked, 2, 3).astype(BF16)
    bsb = jnp.repeat(jnp.swapaxes(b_s, 1, 2), CHUNK, axis=2)
    small = (norm_g[:, None, :], ln_v_g[:, None, :], ln_v_b[:, None, :], ws_b, bsb, cw_full, ple_norm_g[:, None, :])

    saved0, wts1 = fwd_layer(0, x[0], p, wts0, small, job=gather_job([(s, 1) for s in shards]))
    saved1, _ = fwd_layer(1, saved0[4], p, wts1, small, head=(loss_target[0], final_g[None, :]))
    saved, xs = [saved0[:9], saved1[:9]], [x[0], saved0[4]]
    dx, loss_part, g_final = saved1[4], saved1[9], saved1[10]
    loss = lax.psum(loss_part[0, 0], ("x", "y", "c"))

    def chip_sums(arrs, tag):
        recvd = run_job(to_sibling_job(arrs), f"rs_to_sibling{tag}")
        return rs_add_sibling(ids, arrs, recvd, f"rs_add_sibling{tag}")

    small_g = [None] * DEPTH
    proj, x2, gate, pp, _, hnT, catT, hn2T, pT = saved[1]
    (dpp_b, dgl_b, dx2, dx2_b, dcat, g_pg), _ = bwd_ple(1, dx, x2, gate, pp, wts1[2], wts1[1], small[6])
    (dx, dproj_b, *g_mix), _ = bwd_mix(1, dcat, dx2, xs[1], proj, wts1[0], small, wsT_b, tril)
    small_g[1] = g_mix[:5] + [g_pg, g_mix[5]]
    (g1_in,), _ = wgrad([(hnT, dproj_b)], 2, True, 1024, "wgrad_in1")
    (g1_out, g1_gate), _ = wgrad([(catT, dx2_b), (hn2T, dgl_b)], 1, False, 1024, "wgrad_outgate1")
    (g1_pp,), _ = wgrad([(pT, dpp_b)], 1, True, 2048, "wgrad_pp1")
    part1 = chip_sums([g1_in, g1_out, g1_gate, g1_pp], "1")
    proj, x2, gate, pp, _, hnT, catT, hn2T, pT = saved[0]
    (dpp_b, dgl_b, dx2, dx2_b, dcat, g_pg), from1 = bwd_ple(0, dx, x2, gate, pp, wts0[2], wts0[1], small[6],
                                                            job=to_owners_job(part1[0]))
    full = rs_add_owners(1, ids, part1[1], from1, None, "rs_add_owners1")
    full = run_job(exchange_halves_job(1, full), "rs_exchange_halves1")
    (dx, dproj_b, *g_mix), _ = bwd_mix(0, dcat, dx2, xs[0], proj, wts0[0], small, wsT_b, tril)
    small_g[0] = g_mix[:5] + [g_pg, g_mix[5]]
    (g0_in,), _ = wgrad([(hnT, dproj_b)], 2, True, 1024, "wgrad_in0")
    part0b = chip_sums([g0_in], "0b")
    (g0_out, g0_gate), from0b = wgrad([(catT, dx2_b), (hn2T, dgl_b)], 1, False, 1024, "wgrad_outgate0",
                                      job=to_owners_job(part0b[0]))
    full_b = rs_add_owners(0, ids, part0b[1], from0b, full[:1], "rs_add_owners0b")
    (g0_pp,), _ = wgrad([(pT, dpp_b)], 1, True, 2048, "wgrad_pp0")
    part0a = chip_sums([g0_out, g0_gate, g0_pp], "0a")
    from0a = run_job(to_owners_job(part0a[0]), "rs_to_owners0a")
    full_a = rs_add_owners(0, ids, part0a[1], from0a, full[1:], "rs_add_owners0a")
    gw_in, gw_out, gw_gate, gw_pp = run_job(exchange_halves_job(0, list(full_b) + list(full_a)), "rs_exchange_halves0")
    ((d_in, d_out, d_gate, d_pp), (m_in, m_out, m_gate, m_pp), (v_in, v_out, v_gate, v_pp)), _ = adamw_big(
        [w_in, w_out, w_ple_gate, w_ple_proj], [gw_in, gw_out, gw_gate, gw_pp],
        [m_w_in, m_w_out, m_w_ple_gate, m_w_ple_proj], [v_w_in, v_w_out, v_w_ple_gate, v_w_ple_proj], "adamw_big")

    small_like = [norm_g, ln_v_g, ln_v_b, w_s, b_s, ple_norm_g]

    def stack(j):
        return jnp.stack([small_g[layer][j].reshape(small_like[j].shape[1:]) for layer in range(DEPTH)])

    grads_small = [stack(j) for j in range(6)] + [g_final.reshape(D_MODEL), jnp.stack([small_g[l][6] for l in range(DEPTH)])]
    zeros_cw = jnp.zeros((DEPTH, 3, WIDTH), F32)
    params = [norm_g, ln_v_g, ln_v_b, w_s, b_s, ple_norm_g, final_g, zeros_cw]
    ms = [m_norm_g, m_ln_v_g, m_ln_v_b, m_w_s, m_b_s, m_ple_norm_g, m_final_g, zeros_cw]
    vs = [v_norm_g, v_ln_v_g, v_ln_v_b, v_w_s, v_b_s, v_ple_norm_g, v_final_g, zeros_cw + 1.0]
    gsum, dsm, msm, vsm = allreduce_small(_pack(grads_small), _pack(params), _pack(ms), _pack(vs))
    gs = _unpack(gsum, params)
    ds = _unpack(dsm, params)
    m2s = _unpack(msm, params)
    v2s = _unpack(vsm, params)
    g_cw_own = lax.dynamic_slice_in_dim(gs[7], own * (WIDTH // N_CHIPS), WIDTH // N_CHIPS, axis=2)
    rows2 = lambda a: a.reshape(DEPTH * 3, WIDTH // N_CHIPS)
    d_cw, m_cw, v_cw = adamw_rows(rows2(cw_rows), rows2(g_cw_own), rows2(jnp.transpose(m_conv_w, (0, 2, 1))),
                                  rows2(jnp.transpose(v_conv_w, (0, 2, 1))))
    back = lambda a: jnp.transpose(a.reshape(DEPTH, 3, WIDTH // N_CHIPS), (0, 2, 1))
    g_conv = jnp.transpose(g_cw_own, (0, 2, 1))

    def ordered(sm, cw_v, w_in_v, w_out_v, gate_v, pp_v):
        return [sm[0], w_in_v, sm[1], sm[2], sm[3], sm[4], cw_v, w_out_v, sm[5], gate_v, pp_v, sm[6]]

    grads = ordered(gs, g_conv, gw_in, gw_out, gw_gate, gw_pp)
    deltas = ordered(ds, back(d_cw), d_in, d_out, d_gate, d_pp)
    new_m = ordered(m2s, back(m_cw), m_in, m_out, m_gate, m_pp)
    new_v = ordered(v2s, back(v_cw), v_in, v_out, v_gate, v_pp)
    return (loss, dx[None], *grads, *deltas, *new_m, *new_v)
```

```python
import jax
import jax.numpy as jnp
from jax import lax
from jax.experimental import pallas as pl
from jax.experimental.pallas import tpu as pltpu

F32 = jnp.float32
BF16 = jnp.bfloat16

SEQ = 8192
D_MODEL = 1024
WIDTH = 512
PROJ = 7 * WIDTH
N_CHIPS = 4
COL_BLK = PROJ // N_CHIPS
PLE = 256
HEADS = 4
CHUNK = 128
DEPTH = 2
EPS = 1e-6
TILE = 256
N_TILES = SEQ // TILE
HALO = 8
VMEM_LIMIT = 60 * 1024 * 1024

ADAM_LR, ADAM_B1, ADAM_B2, ADAM_EPS, ADAM_WD, ADAM_STEP = 0.001, 0.9, 0.999, 1e-08, 0.01, 10
ADAM_C1 = 1.0 - ADAM_B1**ADAM_STEP
ADAM_C2 = 1.0 - ADAM_B2**ADAM_STEP

MESH = pl.DeviceIdType.MESH
ANY = pl.BlockSpec(memory_space=pl.ANY)


def _mm(a, b):
    return lax.dot_general(a, b, (((1,), (0,)), ((), ())), preferred_element_type=F32)


def _mm_nt(a, b):
    return lax.dot_general(a, b, (((1,), (1,)), ((), ())), preferred_element_type=F32)


def _mm_rows(a, w_ref):
    blk = w_ref.shape[1]
    acc = _mm(a[:, 0:blk], w_ref[0])
    for k in range(1, N_CHIPS):
        acc = acc + _mm(a[:, k * blk : (k + 1) * blk], w_ref[k])
    return acc


def _mm_nt_rows(a, w_ref):
    return jnp.concatenate([_mm_nt(a, w_ref[k]) for k in range(N_CHIPS)], axis=-1)


def _load_side_by_side(w_hbm, w_vmem, sems):
    copies = [pltpu.make_async_copy(w_hbm.at[k], w_vmem.at[:, pl.ds(k * COL_BLK, COL_BLK)], sems.at[k])
              for k in range(N_CHIPS)]
    for cp in copies:
        cp.start()
    for cp in copies:
        cp.wait()


def _sigmoid(z):
    return 1.0 / (1.0 + jnp.exp(-z))


def _rms_stats(x):
    r = lax.rsqrt(jnp.mean(x * x, axis=-1, keepdims=True) + EPS)
    return r, x * r


def _rms_bwd(dyg, xh, r):
    return r * (dyg - xh * jnp.mean(dyg * xh, axis=-1, keepdims=True))


def _ln_stats(v):
    mu = jnp.mean(v, axis=-1, keepdims=True)
    vc = v - mu
    rs = lax.rsqrt(jnp.mean(vc * vc, axis=-1, keepdims=True) + EPS)
    return rs, vc * rs


def _mesh_pos():
    x, y, c = lax.axis_index("x"), lax.axis_index("y"), lax.axis_index("c")
    return x, y, c, 2 * x + y


def _peer(x, y, c, r):
    return ((1 - x) if (r >> 1) else x, (1 - y) if (r & 1) else y, c)


def _full(shape):
    return pl.BlockSpec(shape, lambda *_: (0,) * len(shape))


def _const(shape, pos):
    return pl.BlockSpec((None,) * len(pos) + tuple(shape), lambda *_: tuple(pos) + (0,) * len(shape))


class Job:
    def __init__(self, ins, out_shapes, sems, start, middle, finish, aliases=None):
        self.ins, self.out_shapes, self.sems = list(ins), list(out_shapes), list(sems)
        self.start, self.middle, self.finish = start, middle, finish
        self.aliases = aliases or {}


def run_job(job, name):
    ni, no = len(job.ins), len(job.out_shapes)

    def body(*refs):
        parts = (refs[:ni], refs[ni : ni + no], refs[ni + no :])
        job.start(*parts)
        if job.middle is not None:
            job.middle(*parts)
        job.finish(*parts)

    return pl.pallas_call(
        body, name=name, out_shape=job.out_shapes, in_specs=[ANY] * ni, out_specs=[ANY] * no, scratch_shapes=job.sems,
        input_output_aliases=job.aliases, compiler_params=pltpu.CompilerParams(has_side_effects=True),
    )(*job.ins)


def _call(body, *, name, grid, in_specs, out_specs, out_shape, args, scratch=(), job=None, mid=None):
    params = pltpu.CompilerParams(dimension_semantics=("arbitrary",) * len(grid), vmem_limit_bytes=VMEM_LIMIT,
                                  has_side_effects=job is not None)
    n_in, n_out, n_sc = len(in_specs), len(out_specs), len(scratch)
    if job is None:
        outs = pl.pallas_call(body, name=name, grid=grid, in_specs=in_specs, out_specs=out_specs, out_shape=out_shape,
                              scratch_shapes=list(scratch), compiler_params=params)(*args)
        return list(outs), []
    ji, jo = len(job.ins), len(job.out_shapes)
    assert not job.aliases and (job.middle is None or len(grid) == 1)

    def wrapped(*refs):
        ins, jin = refs[:n_in], refs[n_in : n_in + ji]
        o0 = n_in + ji
        outs, jout = refs[o0 : o0 + n_out], refs[o0 + n_out : o0 + n_out + jo]
        s0 = o0 + n_out + jo
        sc, jsem = refs[s0 : s0 + n_sc], refs[s0 + n_sc :]
        step = pl.program_id(0)
        for d in range(1, len(grid)):
            step = step * grid[d] + pl.program_id(d)
        n_steps = 1
        for g in grid:
            n_steps *= g

        @pl.when(step == 0)
        def _():
            job.start(jin, jout, jsem)

        if job.middle is not None:
            @pl.when(step == mid)
            def _():
                job.middle(jin, jout, jsem)

        body(*ins, *outs, *sc)

        @pl.when(step == n_steps - 1)
        def _():
            job.finish(jin, jout, jsem)

    outs = pl.pallas_call(
        wrapped, name=name, grid=grid, in_specs=list(in_specs) + [ANY] * ji, out_specs=list(out_specs) + [ANY] * jo,
        out_shape=list(out_shape) + job.out_shapes, scratch_shapes=list(scratch) + job.sems, compiler_params=params,
    )(*args, *job.ins)
    return list(outs[:n_out]), list(outs[n_out:])


def gather_job(items):
    n = len(items)

    def src_half(a, ref, h):
        arr, layer = items[a]
        if layer is None:
            return ref.at[h]
        rh = arr.shape[1] // 2
        return ref.at[layer, pl.ds(h * rh, rh)]

    def dst_half(a, ref, chip, h):
        arr, layer = items[a]
        if layer is None:
            return ref.at[chip, h]
        rh = arr.shape[1] // 2
        return ref.at[chip, pl.ds(h * rh, rh)]

    def copies(src, dst, scratch):
        stage, (in_sem, out_sem, ici_s, ici_r, fwd_s, fwd_r) = scratch[:n], scratch[n:]
        x, y, c, own = _mesh_pos()
        local, ici, fwd, got = [], {}, {}, {}
        for a in range(n):
            layer = items[a][1]
            local.append((pltpu.make_async_copy(src[a] if layer is None else src[a].at[layer], stage[a], in_sem.at[a]),
                          pltpu.make_async_copy(stage[a], dst[a].at[own], out_sem.at[a])))
            for r in (1, 2, 3):
                k = 3 * a + r - 1
                ici[a, r] = pltpu.make_async_remote_copy(
                    src_ref=src_half(a, src[a], c), dst_ref=dst_half(a, dst[a], own, c), send_sem=ici_s.at[k],
                    recv_sem=ici_r.at[k], device_id=_peer(x, y, c, r), device_id_type=MESH)
                fwd[a, r] = pltpu.make_async_remote_copy(
                    src_ref=dst_half(a, dst[a], own ^ r, c), dst_ref=dst_half(a, dst[a], own ^ r, c), send_sem=fwd_s.at[k],
                    recv_sem=fwd_r.at[k], device_id=(x, y, 1 - c), device_id_type=MESH)
                got[a, r] = pltpu.make_async_remote_copy(
                    src_ref=dst_half(a, dst[a], own ^ r, 1 - c), dst_ref=dst_half(a, dst[a], own ^ r, 1 - c),
                    send_sem=fwd_s.at[k], recv_sem=fwd_r.at[k], device_id=(x, y, 1 - c), device_id_type=MESH)
        return local, ici, fwd, got

    def start(src, dst, sems):
        local, ici, _, _ = copies(src, dst, sems)
        for a in range(n):
            for r in (1, 2, 3):
                ici[a, r].start()
        for cp_in, _ in local:
            cp_in.start()
        for cp_in, cp_out in local:
            cp_in.wait()
            cp_out.start()

    def middle(src, dst, sems):
        _, ici, fwd, _ = copies(src, dst, sems)
        for r in (1, 2, 3):
            for a in range(n):
                ici[a, r].wait_recv()
                fwd[a, r].start()

    def finish(src, dst, sems):
        local, ici, fwd, got = copies(src, dst, sems)
        for a in range(n):
            for r in (1, 2, 3):
                got[a, r].wait_recv()
        for a in range(n):
            for r in (1, 2, 3):
                ici[a, r].wait_send()
                fwd[a, r].wait_send()
        for _, cp_out in local:
            cp_out.wait()

    out_shapes = [jax.ShapeDtypeStruct((N_CHIPS,) + (arr.shape if layer is None else arr.shape[1:]), arr.dtype)
                  for arr, layer in items]
    stage = [pltpu.VMEM(arr.shape if layer is None else arr.shape[1:], arr.dtype) for arr, layer in items]
    sems = [pltpu.SemaphoreType.DMA((n,))] * 2 + [pltpu.SemaphoreType.DMA((3 * n,))] * 4
    return Job([arr for arr, _ in items], out_shapes, stage + sems, start, middle, finish)


def _mixer_fwd(proj_ref, lg, lb, ws_ref, bsb_ref, cw_ref, mix_ref, xcbuf, halo_xc):
    u = proj_ref[:, 0 * WIDTH : 1 * WIDTH]
    v = proj_ref[:, 1 * WIDTH : 2 * WIDTH]
    za = proj_ref[:, 2 * WIDTH : 3 * WIDTH]
    h = proj_ref[:, 3 * WIDTH : 4 * WIDTH]
    gb = proj_ref[:, 4 * WIDTH : 5 * WIDTH]
    gc = proj_ref[:, 5 * WIDTH : 6 * WIDTH]
    zb = proj_ref[:, 6 * WIDTH : 7 * WIDTH]
    rs, vhat = _ln_stats(v)
    vl = vhat * lg + lb
    vlb = vl.astype(BF16)
    for j in range(TILE // CHUNK):
        rows = slice(j * CHUNK, (j + 1) * CHUNK)
        for hd in range(HEADS):
            cols = slice(hd * CHUNK, (hd + 1) * CHUNK)
            mix_ref[rows, cols] = _mm(ws_ref[hd], vlb[rows, cols]) + bsb_ref[:, cols]
    mixed = mix_ref[...]
    siga = _sigmoid(za)
    sigb = _sigmoid(zb)
    xc = gc * h
    xcbuf[0:HALO, :] = halo_xc
    xcbuf[HALO : HALO + TILE, :] = xc
    y = cw_ref[0:1, :] * xcbuf[HALO - 2 : HALO - 2 + TILE, :] + cw_ref[1:2, :] * xcbuf[HALO - 1 : HALO - 1 + TILE, :]
    y = y + cw_ref[2:3, :] * xc
    return dict(u=u, za=za, h=h, gb=gb, gc=gc, zb=zb, rs=rs, vhat=vhat, vlb=vlb, mixed=mixed, siga=siga, sigb=sigb,
                xc=xc, y=y)


def fwd_layer(layer, x, p, wts, small, job=None, head=None):
    win, wout, wg, wpp = wts
    ng, lg, lb, ws, bsb, cw, pg = small
    n_head = 0 if head is None else 2

    def body(*refs):
        (x_ref, p_ref, win_ref, wout_ref, wg_ref, wpp_ref, ng_ref, lg_ref, lb_ref, ws_ref, bsb_ref, cw_ref,
         pg_ref) = refs[:13]
        head_in = refs[13 : 13 + n_head]
        (proj_ref, x2_ref, gate_ref, pp_ref, x3_ref, hnT_ref, catT_ref, hn2T_ref,
         pT_ref) = refs[13 + n_head : 22 + n_head]
        head_out = refs[22 + n_head : 22 + 2 * n_head]
        mix_ref, xcbuf, carry, wcat, wsem = refs[22 + 2 * n_head :]
        i = pl.program_id(0)

        @pl.when(i == 0)
        def _():
            _load_side_by_side(win_ref, wcat, wsem)
            carry[...] = jnp.zeros_like(carry)
            for ref in head_out:
                ref[...] = jnp.zeros_like(ref)

        xv = x_ref[...]
        _, xh = _rms_stats(xv)
        hn = xh * ng_ref[...]
        hnT_ref[...] = hn.T.astype(BF16)
        proj_ref[...] = _mm(hn.astype(BF16), wcat[...])
        m = _mixer_fwd(proj_ref, lg_ref[...], lb_ref[...], ws_ref, bsb_ref, cw_ref, mix_ref, xcbuf, carry[...])
        carry[...] = m["xc"][TILE - HALO : TILE, :]
        out_a = (m["u"] * m["mixed"]) * (m["za"] * m["siga"])
        out_b = (m["gb"] * m["y"]) * (m["zb"] * m["sigb"])
        cat = jnp.concatenate([out_a, out_b], axis=-1)
        catT_ref[...] = cat.T.astype(BF16)
        x2 = xv + _mm_rows(cat.astype(BF16), wout_ref)
        x2_ref[...] = x2
        _, xh2 = _rms_stats(x2)
        hn2 = xh2 * pg_ref[...]
        hn2T_ref[...] = hn2.T.astype(BF16)
        gate = _sigmoid(_mm_rows(hn2.astype(BF16), wg_ref))
        gate_ref[...] = gate
        pv = p_ref[...]
        pT_ref[...] = pv.T.astype(BF16)
        pb = pv.astype(BF16)
        for k in range(N_CHIPS):
            pp_ref[:, k * PLE : (k + 1) * PLE] = _mm(pb, wpp_ref[k])
        x3 = x2 + gate * pp_ref[...]
        if head is None:
            x3_ref[...] = x3
        else:
            t_ref, gf_ref = head_in
            loss_ref, gg_ref = head_out
            r3, xh3 = _rms_stats(x3)
            gf = gf_ref[...]
            err = xh3 * gf - t_ref[...]
            loss_ref[...] += (0.5 / D_MODEL) * jnp.sum(err * err).reshape(1, 1)
            dy = err * (1.0 / D_MODEL)
            gg_ref[...] += jnp.sum(dy * xh3, axis=0, keepdims=True)
            x3_ref[...] = _rms_bwd(dy * gf, xh3, r3)

    tok = lambda w: pl.BlockSpec((TILE, w), lambda i: (i, 0))
    tokT = lambda w: pl.BlockSpec((w, TILE), lambda i: (0, i))
    once = lambda a: pl.BlockSpec(a.shape, lambda i: (0,) * a.ndim, pipeline_mode=pl.Buffered(1))
    f32 = lambda w: jax.ShapeDtypeStruct((SEQ, w), F32)
    bfT = lambda w: jax.ShapeDtypeStruct((w, SEQ), BF16)
    head_specs = [] if head is None else [tok(D_MODEL), _full((1, D_MODEL))]
    head_outs = [] if head is None else [_full((1, 1)), _full((1, D_MODEL))]
    head_shapes = [] if head is None else [jax.ShapeDtypeStruct((1, 1), F32), jax.ShapeDtypeStruct((1, D_MODEL), F32)]
    return _call(
        body, name=f"fwd_layer{layer}", grid=(N_TILES,),
        in_specs=[tok(D_MODEL), pl.BlockSpec((None, None, TILE, PLE), lambda i: (layer, 0, i, 0)),
                  ANY, once(wout), once(wg), once(wpp),
                  _const((1, D_MODEL), (layer,)), _const((1, WIDTH), (layer,)), _const((1, WIDTH), (layer,)),
                  _const((HEADS, CHUNK, CHUNK), (layer,)), _const((CHUNK, WIDTH), (layer,)), _const((3, WIDTH), (layer,)),
                  _const((1, D_MODEL), (layer,))] + head_specs,
        out_specs=[tok(PROJ), tok(D_MODEL), tok(D_MODEL), tok(D_MODEL), tok(D_MODEL),
                   tokT(D_MODEL), tokT(D_MODEL), tokT(D_MODEL), tokT(PLE)] + head_outs,
        out_shape=[f32(PROJ), f32(D_MODEL), f32(D_MODEL), f32(D_MODEL), f32(D_MODEL),
                   bfT(D_MODEL), bfT(D_MODEL), bfT(D_MODEL), bfT(PLE)] + head_shapes,
        scratch=[pltpu.VMEM((TILE, WIDTH), F32), pltpu.VMEM((HALO + TILE, WIDTH), F32), pltpu.VMEM((HALO, WIDTH), F32),
                 pltpu.VMEM((D_MODEL, PROJ), BF16), pltpu.SemaphoreType.DMA((N_CHIPS,))],
        args=(x, p, win, wout, wg, wpp, ng, lg, lb, ws, bsb, cw, pg) + (() if head is None else tuple(head)),
        job=job, mid=N_TILES // 2)


def bwd_layer(layer, dx3, x2, gate, pp, x, proj, wts, small, wsT, tril):
    win, wout, wg, _ = wts
    ng, lg, lb, ws, bsb, cw, pg = small

    def body(dx3_ref, x2_ref, gate_ref, pp_ref, x_ref, proj_ref, halo_ref, win_ref, wout_ref, wg_ref, ng_ref, lg_ref,
             lb_ref, ws_ref, wsT_ref, bsb_ref, cw_ref, pg_ref, tril_ref,
             dx_ref, dproj_ref, dpp_ref, dgl_ref, dx2b_ref, gn_ref, glg_ref, glb_ref, gws_ref, gbs_ref, gcw_ref, gpg_ref,
             mix_ref, xcbuf, dycbuf, dvl_ref, bs_acc, wcat, wsem):
        i = pl.program_id(0)

        @pl.when(i == 0)
        def _():
            _load_side_by_side(win_ref, wcat, wsem)
            for ref in (gn_ref, glg_ref, glb_ref, gws_ref, gcw_ref, gpg_ref, bs_acc):
                ref[...] = jnp.zeros_like(ref)
            dycbuf[TILE : TILE + HALO, :] = jnp.zeros((HALO, WIDTH), F32)

        dx3v = dx3_ref[...]
        gate_v = gate_ref[...]
        dpp_ref[...] = (dx3v * gate_v).astype(BF16)
        dgl = ((dx3v * pp_ref[...]) * gate_v * (1.0 - gate_v)).astype(BF16)
        dgl_ref[...] = dgl
        dhn2 = _mm_nt_rows(dgl, wg_ref)
        r2, xh2 = _rms_stats(x2_ref[...])
        gpg_ref[...] += jnp.sum(dhn2 * xh2, axis=0, keepdims=True)
        dx2 = dx3v + _rms_bwd(dhn2 * pg_ref[...], xh2, r2)
        dx2b = dx2.astype(BF16)
        dx2b_ref[...] = dx2b
        dcat = _mm_nt_rows(dx2b, wout_ref)
        lgv = lg_ref[...]
        halo_xc = halo_ref[:, 5 * WIDTH : 6 * WIDTH] * halo_ref[:, 3 * WIDTH : 4 * WIDTH]
        halo_xc = jnp.where(i == N_TILES - 1, 0.0, halo_xc)
        m = _mixer_fwd(proj_ref, lgv, lb_ref[...], ws_ref, bsb_ref, cw_ref, mix_ref, xcbuf, halo_xc)
        u, za, h, gb, gc, zb = m["u"], m["za"], m["h"], m["gb"], m["gc"], m["zb"]
        mixed, siga, sigb, xc, y = m["mixed"], m["siga"], m["sigb"], m["xc"], m["y"]
        doa = dcat[:, 0:WIDTH]
        dob = dcat[:, WIDTH : 2 * WIDTH]
        sa = za * siga
        sb = zb * sigb
        doa_sa = doa * sa
        dproj_ref[:, 0 * WIDTH : 1 * WIDTH] = (doa_sa * mixed).astype(BF16)
        dmixed = doa_sa * u
        dza = (doa * (u * mixed)) * (siga * (1.0 + za * (1.0 - siga)))
        dproj_ref[:, 2 * WIDTH : 3 * WIDTH] = dza.astype(BF16)
        dob_sb = dob * sb
        dproj_ref[:, 4 * WIDTH : 5 * WIDTH] = (dob_sb * y).astype(BF16)
        dyc = dob_sb * gb
        dzb = (dob * (gb * y)) * (sigb * (1.0 + zb * (1.0 - sigb)))
        dproj_ref[:, 6 * WIDTH : 7 * WIDTH] = dzb.astype(BF16)
        dycbuf[0:TILE, :] = dyc
        dyc1 = dycbuf[1 : 1 + TILE, :]
        dyc2 = dycbuf[2 : 2 + TILE, :]
        dxc = cw_ref[2:3, :] * dyc + cw_ref[1:2, :] * dyc1 + cw_ref[0:1, :] * dyc2
        gcw_ref[0:1, :] += jnp.sum(xc * dyc2, axis=0, keepdims=True)
        gcw_ref[1:2, :] += jnp.sum(xc * dyc1, axis=0, keepdims=True)
        gcw_ref[2:3, :] += jnp.sum(xc * dyc, axis=0, keepdims=True)
        dycbuf[TILE : TILE + HALO, :] = dyc[0:HALO, :]
        dproj_ref[:, 5 * WIDTH : 6 * WIDTH] = (dxc * h).astype(BF16)
        dproj_ref[:, 3 * WIDTH : 4 * WIDTH] = (dxc * gc).astype(BF16)
        dmb = dmixed.astype(BF16)
        vlb = m["vlb"]
        bsum = jnp.zeros((CHUNK, WIDTH), F32)
        for j in range(TILE // CHUNK):
            rows = slice(j * CHUNK, (j + 1) * CHUNK)
            bsum = bsum + dmixed[rows, :]
            for hd in range(HEADS):
                cols = slice(hd * CHUNK, (hd + 1) * CHUNK)
                gws_ref[hd] += _mm_nt(dmb[rows, cols], vlb[rows, cols])
                dvl_ref[rows, cols] = _mm(wsT_ref[hd], dmb[rows, cols])
        bs_acc[...] += bsum
        dvl = dvl_ref[...]
        vhat = m["vhat"]
        glb_ref[...] += jnp.sum(dvl, axis=0, keepdims=True)
        glg_ref[...] += jnp.sum(dvl * vhat, axis=0, keepdims=True)
        dvh = dvl * lgv
        dv = m["rs"] * (dvh - jnp.mean(dvh, axis=-1, keepdims=True) - vhat * jnp.mean(dvh * vhat, axis=-1, keepdims=True))
        dproj_ref[:, 1 * WIDTH : 2 * WIDTH] = dv.astype(BF16)
        dhn = _mm_nt(dproj_ref[...], wcat[...])
        r1, xh = _rms_stats(x_ref[...])
        gn_ref[...] += jnp.sum(dhn * xh, axis=0, keepdims=True)
        dx_ref[...] = dx2 + _rms_bwd(dhn * ng_ref[...], xh, r1)

        @pl.when(i == N_TILES - 1)
        def _():
            for hd in range(HEADS):
                gws_ref[hd] = gws_ref[hd] * tril_ref[...]
                gbs_ref[hd : hd + 1, :] = jnp.sum(bs_acc[:, hd * CHUNK : (hd + 1) * CHUNK].T, axis=0, keepdims=True)

    rev = lambda w: pl.BlockSpec((TILE, w), lambda i: (N_TILES - 1 - i, 0))
    halo = pl.BlockSpec((HALO, PROJ), lambda i: (jnp.maximum((N_TILES - 1 - i) * (TILE // HALO) - 1, 0), 0))
    once = lambda a: pl.BlockSpec(a.shape, lambda i: (0,) * a.ndim, pipeline_mode=pl.Buffered(1))
    vec = lambda w: jax.ShapeDtypeStruct((1, w), F32)
    b16 = lambda w: jax.ShapeDtypeStruct((SEQ, w), BF16)
    outs, _ = _call(
        body, name=f"bwd_layer{layer}", grid=(N_TILES,),
        in_specs=[rev(D_MODEL), rev(D_MODEL), rev(D_MODEL), rev(D_MODEL), rev(D_MODEL), rev(PROJ), halo, ANY,
                  once(wout), once(wg),
                  _const((1, D_MODEL), (layer,)), _const((1, WIDTH), (layer,)), _const((1, WIDTH), (layer,)),
                  _const((HEADS, CHUNK, CHUNK), (layer,)), _const((HEADS, CHUNK, CHUNK), (layer,)),
                  _const((CHUNK, WIDTH), (layer,)), _const((3, WIDTH), (layer,)), _const((1, D_MODEL), (layer,)),
                  _full((CHUNK, CHUNK))],
        out_specs=[rev(D_MODEL), rev(PROJ), rev(D_MODEL), rev(D_MODEL), rev(D_MODEL),
                   _full((1, D_MODEL)), _full((1, WIDTH)), _full((1, WIDTH)), _full((HEADS, CHUNK, CHUNK)),
                   _full((HEADS, CHUNK)), _full((3, WIDTH)), _full((1, D_MODEL))],
        out_shape=[jax.ShapeDtypeStruct((SEQ, D_MODEL), F32), b16(PROJ), b16(D_MODEL), b16(D_MODEL), b16(D_MODEL),
                   vec(D_MODEL), vec(WIDTH), vec(WIDTH), jax.ShapeDtypeStruct((HEADS, CHUNK, CHUNK), F32),
                   jax.ShapeDtypeStruct((HEADS, CHUNK), F32), jax.ShapeDtypeStruct((3, WIDTH), F32), vec(D_MODEL)],
        scratch=[pltpu.VMEM((TILE, WIDTH), F32), pltpu.VMEM((HALO + TILE, WIDTH), F32),
                 pltpu.VMEM((TILE + HALO, WIDTH), F32), pltpu.VMEM((TILE, WIDTH), F32), pltpu.VMEM((CHUNK, WIDTH), F32),
                 pltpu.VMEM((D_MODEL, PROJ), BF16), pltpu.SemaphoreType.DMA((N_CHIPS,))],
        args=(dx3, x2, gate, pp, x, proj, proj, win, wout, wg, ng, lg, lb, ws, wsT, bsb, cw, pg, tril))
    return outs


def wgrad(pairs, n_tiles, col_blocked, tk, name, job=None):
    n = len(pairs)
    m_dim, n_dim = pairs[0][0].shape[0], pairs[0][1].shape[1]
    tn = n_dim // n_tiles
    cb, mb = n_dim // N_CHIPS, m_dim // N_CHIPS
    per_tile = N_CHIPS // n_tiles
    assert col_blocked or n_tiles == 1

    def body(*refs):
        for a in range(n):
            a_ref, b_ref, o_ref = refs[2 * a], refs[2 * a + 1], refs[2 * n + a]

            @pl.when(pl.program_id(1) == 0)
            def _():
                o_ref[...] = jnp.zeros_like(o_ref)

            prod = _mm(a_ref[...], b_ref[...])
            if col_blocked:
                for q in range(per_tile):
                    o_ref[q] += prod[:, q * cb : (q + 1) * cb]
            else:
                for q in range(N_CHIPS):
                    o_ref[q] += prod[q * mb : (q + 1) * mb, :]

    if col_blocked:
        shape = (N_CHIPS, m_dim, cb)
        o_spec = pl.BlockSpec((per_tile, m_dim, cb), lambda j, k: (j, 0, 0))
    else:
        shape = (N_CHIPS, mb, n_dim)
        o_spec = pl.BlockSpec((N_CHIPS, mb, n_dim), lambda j, k: (0, 0, 0))
    return _call(
        body, name=name, grid=(n_tiles, SEQ // tk),
        in_specs=[pl.BlockSpec((m_dim, tk), lambda j, k: (0, k)), pl.BlockSpec((tk, tn), lambda j, k: (k, j))] * n,
        out_specs=[o_spec] * n, out_shape=[jax.ShapeDtypeStruct(shape, F32)] * n,
        args=[t for pair in pairs for t in pair], job=job)


def to_sibling_job(grads):
    n = len(grads)

    def copies(src, dst, sems):
        x, y, c, _ = _mesh_pos()
        out = []
        for a in range(n):
            rh = grads[a].shape[1] // 2
            out.append(pltpu.make_async_remote_copy(
                src_ref=src[a].at[:, pl.ds((1 - c) * rh, rh), :], dst_ref=dst[a], send_sem=sems[0].at[a],
                recv_sem=sems[1].at[a], device_id=(x, y, 1 - c), device_id_type=MESH))
        return out

    def start(src, dst, sems):
        for cp in copies(src, dst, sems):
            cp.start()

    def finish(src, dst, sems):
        for cp in copies(src, dst, sems):
            cp.wait()

    out_shapes = [jax.ShapeDtypeStruct((N_CHIPS, g.shape[1] // 2, g.shape[2]), F32) for g in grads]
    return Job(grads, out_shapes, [pltpu.SemaphoreType.DMA((n,))] * 2, start, None, finish)


def rs_add_sibling(ids, grads, recvd, name):
    n = len(grads)

    def body(ids_ref, *refs):
        g, r = refs[:n], refs[n : 2 * n]
        pb, pf = refs[2 * n : 3 * n], refs[3 * n :]
        k = pl.program_id(0)
        for a in range(n):
            s = g[a][...] + r[a][...]
            pb[a][...] = s.astype(BF16)

            @pl.when(k == ids_ref[1])
            def _():
                pf[a][...] = s

    in_specs, out_specs, out_shape = [], [], []
    for g in grads:
        rh, cc = g.shape[1] // 2, g.shape[2]
        in_specs.append(pl.BlockSpec((None, rh, cc), lambda k, ids: (k, ids[0], 0)))
    for g in grads:
        rh, cc = g.shape[1] // 2, g.shape[2]
        in_specs.append(pl.BlockSpec((None, rh, cc), lambda k, ids: (k, 0, 0)))
        out_specs.append(pl.BlockSpec((None, rh, cc), lambda k, ids: (k, 0, 0)))
        out_shape.append(jax.ShapeDtypeStruct((N_CHIPS, rh, cc), BF16))
    for g in grads:
        rh, cc = g.shape[1] // 2, g.shape[2]
        out_specs.append(pl.BlockSpec((rh, cc), lambda k, ids: (0, 0)))
        out_shape.append(jax.ShapeDtypeStruct((rh, cc), F32))
    outs = pl.pallas_call(
        body, name=name, out_shape=out_shape,
        grid_spec=pltpu.PrefetchScalarGridSpec(num_scalar_prefetch=1, grid=(N_CHIPS,), in_specs=in_specs,
                                               out_specs=out_specs),
        compiler_params=pltpu.CompilerParams(dimension_semantics=("arbitrary",), vmem_limit_bytes=VMEM_LIMIT),
    )(ids, *grads, *recvd)
    return list(outs[:n]), list(outs[n:])


def to_owners_job(partials):
    n = len(partials)

    def copies(src, dst, sems):
        x, y, c, own = _mesh_pos()
        out = []
        for a in range(n):
            for r in (1, 2, 3):
                out.append(pltpu.make_async_remote_copy(
                    src_ref=src[a].at[own ^ r], dst_ref=dst[a].at[r - 1], send_sem=sems[0].at[3 * a + r - 1],
                    recv_sem=sems[1].at[3 * a + r - 1], device_id=_peer(x, y, c, r), device_id_type=MESH))
        return out

    def start(src, dst, sems):
        for cp in copies(src, dst, sems):
            cp.start()

    def finish(src, dst, sems):
        for cp in copies(src, dst, sems):
            cp.wait()

    out_shapes = [jax.ShapeDtypeStruct((3,) + p.shape[1:], BF16) for p in partials]
    return Job(partials, out_shapes, [pltpu.SemaphoreType.DMA((3 * n,))] * 2, start, None, finish)


def rs_add_owners(layer, ids, own_f32, recvd, prev, name):
    n = len(own_f32)
    nb = 2

    def body(ids_ref, *refs):
        o, r, f = refs[:n], refs[n : 2 * n], refs[-n:]
        for a in range(n):
            f[a][...] = ((o[a][...] + r[a][0].astype(F32)) + r[a][1].astype(F32)) + r[a][2].astype(F32)

    in_specs, out_specs, out_shape = [], [], []
    for o in own_f32:
        in_specs.append(pl.BlockSpec((o.shape[0] // nb, o.shape[1]), lambda j, ids: (j, 0)))
    for o in own_f32:
        in_specs.append(pl.BlockSpec((3, o.shape[0] // nb, o.shape[1]), lambda j, ids: (0, j, 0)))
        out_specs.append(pl.BlockSpec((None, o.shape[0] // nb, o.shape[1]), lambda j, ids: (layer, ids[0] * nb + j, 0)))
        out_shape.append(jax.ShapeDtypeStruct((DEPTH, 2 * o.shape[0], o.shape[1]), F32))
    args, aliases = [ids, *own_f32, *recvd], {}
    if prev is not None:
        in_specs += [ANY] * n
        args += list(prev)
        aliases = {1 + 2 * n + a: a for a in range(n)}
    return pl.pallas_call(
        body, name=name, out_shape=out_shape, input_output_aliases=aliases,
        grid_spec=pltpu.PrefetchScalarGridSpec(num_scalar_prefetch=1, grid=(nb,), in_specs=in_specs, out_specs=out_specs),
        compiler_params=pltpu.CompilerParams(dimension_semantics=("arbitrary",), vmem_limit_bytes=VMEM_LIMIT),
    )(*args)


def exchange_halves_job(layer, full):
    n = len(full)

    def copies(src, dst, sems):
        x, y, c, _ = _mesh_pos()
        out = []
        for a in range(n):
            rh = full[a].shape[1] // 2
            out.append(pltpu.make_async_remote_copy(
                src_ref=src[a].at[layer, pl.ds(c * rh, rh), :], dst_ref=dst[a].at[layer, pl.ds(c * rh, rh), :],
                send_sem=sems[0].at[a], recv_sem=sems[1].at[a], device_id=(x, y, 1 - c), device_id_type=MESH))
        return out

    def start(src, dst, sems):
        for cp in copies(src, dst, sems):
            cp.start()

    def finish(src, dst, sems):
        for cp in copies(src, dst, sems):
            cp.wait()

    out_shapes = [jax.ShapeDtypeStruct(f.shape, F32) for f in full]
    return Job(full, out_shapes, [pltpu.SemaphoreType.DMA((n,))] * 2, start, None, finish, {a: a for a in range(n)})


def _adamw(w, g, m, v):
    m2 = ADAM_B1 * m + (1.0 - ADAM_B1) * g
    v2 = ADAM_B2 * v + (1.0 - ADAM_B2) * (g * g)
    delta = -ADAM_LR * ((m2 / ADAM_C1) / (jnp.sqrt(v2 / ADAM_C2) + ADAM_EPS) + ADAM_WD * w)
    return delta, m2, v2


def adamw_big(ws, gs, ms, vs, name, job=None):
    n = len(ws)
    nb = 4

    def body(*refs):
        for a in range(n):
            w, g, m, v = (refs[j * n + a][...] for j in range(4))
            d, m2, v2 = _adamw(w, g, m, v)
            refs[4 * n + a][...] = d
            refs[5 * n + a][...] = m2
            refs[6 * n + a][...] = v2

    specs = [pl.BlockSpec((None, w.shape[1] // nb, w.shape[2]), lambda l, j: (l, j, 0)) for w in ws]
    shapes = [jax.ShapeDtypeStruct(w.shape, F32) for w in ws]
    outs, job_outs = _call(body, name=name, grid=(DEPTH, nb), in_specs=specs * 4, out_specs=specs * 3,
                           out_shape=shapes * 3, args=(*ws, *gs, *ms, *vs), job=job)
    return (outs[:n], outs[n : 2 * n], outs[2 * n :]), job_outs


def allreduce_small(g, w, m, v):
    rows = g.shape[0]

    def body(g_ref, w_ref, m_ref, v_ref, gsum_ref, d_ref, m2_ref, v2_ref, sib_buf, slots, s1, r1, s2, r2):
        x, y, c, own = _mesh_pos()
        first = pltpu.make_async_remote_copy(src_ref=g_ref, dst_ref=sib_buf, send_sem=s1, recv_sem=r1,
                                             device_id=(x, y, 1 - c), device_id_type=MESH)
        first.start()
        first.wait()
        slots[0] = g_ref[...] + sib_buf[...]
        cps = [pltpu.make_async_remote_copy(src_ref=slots.at[0], dst_ref=slots.at[r], send_sem=s2.at[r - 1],
                                            recv_sem=r2.at[r - 1], device_id=_peer(x, y, c, r), device_id_type=MESH)
               for r in (1, 2, 3)]
        for cp in cps:
            cp.start()
        for cp in cps:
            cp.wait()
        tot = slots[own] + slots[own ^ 1]
        tot = tot + slots[own ^ 2]
        tot = tot + slots[own ^ 3]
        gsum_ref[...] = tot
        d, m2, v2 = _adamw(w_ref[...], tot, m_ref[...], v_ref[...])
        d_ref[...] = d
        m2_ref[...] = m2
        v2_ref[...] = v2

    vm = pl.BlockSpec(memory_space=pltpu.VMEM)
    shape = jax.ShapeDtypeStruct(g.shape, F32)
    return pl.pallas_call(
        body, name="allreduce_small", out_shape=[shape] * 4, in_specs=[vm] * 4, out_specs=[vm] * 4,
        scratch_shapes=[pltpu.VMEM((rows, 128), F32), pltpu.VMEM((N_CHIPS, rows, 128), F32),
                        pltpu.SemaphoreType.DMA, pltpu.SemaphoreType.DMA,
                        pltpu.SemaphoreType.DMA((3,)), pltpu.SemaphoreType.DMA((3,))],
        compiler_params=pltpu.CompilerParams(has_side_effects=True),
    )(g, w, m, v)


def adamw_rows(w, g, m, v):
    def body(w_ref, g_ref, m_ref, v_ref, d_ref, m2_ref, v2_ref):
        d, m2, v2 = _adamw(w_ref[...], g_ref[...], m_ref[...], v_ref[...])
        d_ref[...] = d
        m2_ref[...] = m2
        v2_ref[...] = v2

    return pl.pallas_call(body, name="adamw_conv", out_shape=[jax.ShapeDtypeStruct(w.shape, F32)] * 3)(w, g, m, v)


def _pack(parts):
    out = []
    for a in parts:
        flat = a.reshape(-1)
        rows = -(-flat.shape[0] // 1024) * 8
        flat = jnp.pad(flat, (0, rows * 128 - flat.shape[0]))
        out.append(flat.reshape(rows, 128))
    return jnp.concatenate(out, axis=0)


def _unpack(packed, like):
    out, row = [], 0
    for a in like:
        size = 1
        for s in a.shape:
            size *= s
        rows = -(-size // 1024) * 8
        out.append(packed[row : row + rows].reshape(-1)[:size].reshape(a.shape))
        row += rows
    return out


def kernel(x, p, norm_g, w_in, ln_v_g, ln_v_b, w_s, b_s, conv_w, w_out, ple_norm_g, w_ple_gate, w_ple_proj, final_g, loss_target, m_norm_g, m_w_in, m_ln_v_g, m_ln_v_b, m_w_s, m_b_s, m_conv_w, m_w_out, m_ple_norm_g, m_w_ple_gate, m_w_ple_proj, m_final_g, v_norm_g, v_w_in, v_ln_v_g, v_ln_v_b, v_w_s, v_b_s, v_conv_w, v_w_out, v_ple_norm_g, v_w_ple_gate, v_w_ple_proj, v_final_g):
    cx, cy, cc = lax.axis_index("x"), lax.axis_index("y"), lax.axis_index("c")
    own = 2 * cx + cy
    ids = jnp.stack([cc, own]).astype(jnp.int32)

    cw_rows = jnp.transpose(conv_w, (0, 2, 1))
    shards = [w_in.astype(BF16), w_out.astype(BF16), w_ple_gate.astype(BF16), w_ple_proj.astype(BF16)]
    *wts0, g_cw = run_job(gather_job([(s, 0) for s in shards] + [(cw_rows, None)]), "gather_weights0")
    cw_full = jnp.transpose(g_cw, (1, 2, 0, 3)).reshape(DEPTH, 3, WIDTH)
    tril = jnp.tril(jnp.ones((CHUNK, CHUNK), F32))
    ws_masked = w_s * tril[None, None]
    ws_b = ws_masked.astype(BF16)
    wsT_b = jnp.swapaxes(ws_masked, 2, 3).astype(BF16)
    bsb = jnp.repeat(jnp.swapaxes(b_s, 1, 2), CHUNK, axis=2)
    small = (norm_g[:, None, :], ln_v_g[:, None, :], ln_v_b[:, None, :], ws_b, bsb, cw_full, ple_norm_g[:, None, :])

    saved0, wts1 = fwd_layer(0, x[0], p, wts0, small, job=gather_job([(s, 1) for s in shards]))
    saved1, _ = fwd_layer(1, saved0[4], p, wts1, small, head=(loss_target[0], final_g[None, :]))
    saved, xs = [saved0[:9], saved1[:9]], [x[0], saved0[4]]
    dx, loss_part, g_final = saved1[4], saved1[9], saved1[10]
    loss = lax.psum(loss_part[0, 0], ("x", "y", "c"))

    def chip_sums(arrs, tag):
        recvd = run_job(to_sibling_job(arrs), f"rs_to_sibling{tag}")
        return rs_add_sibling(ids, arrs, recvd, f"rs_add_sibling{tag}")

    small_g = [None] * DEPTH
    proj, x2, gate, pp, _, hnT, catT, hn2T, pT = saved[1]
    dx, dproj_b, dpp_b, dgl_b, dx2_b, *g_sm = bwd_layer(1, dx, x2, gate, pp, xs[1], proj, wts1, small, wsT_b, tril)
    small_g[1] = g_sm[:5] + [g_sm[6], g_sm[5]]
    (g1_in,), _ = wgrad([(hnT, dproj_b)], 2, True, 1024, "wgrad_in1")
    (g1_out, g1_gate), _ = wgrad([(catT, dx2_b), (hn2T, dgl_b)], 1, False, 1024, "wgrad_outgate1")
    (g1_pp,), _ = wgrad([(pT, dpp_b)], 1, True, 2048, "wgrad_pp1")
    part1 = chip_sums([g1_in, g1_out, g1_gate, g1_pp], "1")
    proj, x2, gate, pp, _, hnT, catT, hn2T, pT = saved[0]
    dx, dproj_b, dpp_b, dgl_b, dx2_b, *g_sm = bwd_layer(0, dx, x2, gate, pp, xs[0], proj, wts0, small, wsT_b, tril)
    small_g[0] = g_sm[:5] + [g_sm[6], g_sm[5]]
    (g0_in,), from1 = wgrad([(hnT, dproj_b)], 2, True, 1024, "wgrad_in0", job=to_owners_job(part1[0]))
    full = rs_add_owners(1, ids, part1[1], from1, None, "rs_add_owners1")
    full = run_job(exchange_halves_job(1, full), "rs_exchange_halves1")
    part0b = chip_sums([g0_in], "0b")
    (g0_out, g0_gate), from0b = wgrad([(catT, dx2_b), (hn2T, dgl_b)], 1, False, 1024, "wgrad_outgate0",
                                      job=to_owners_job(part0b[0]))
    full_b = rs_add_owners(0, ids, part0b[1], from0b, full[:1], "rs_add_owners0b")
    (g0_pp,), _ = wgrad([(pT, dpp_b)], 1, True, 2048, "wgrad_pp0")
    part0a = chip_sums([g0_out, g0_gate, g0_pp], "0a")
    from0a = run_job(to_owners_job(part0a[0]), "rs_to_owners0a")
    full_a = rs_add_owners(0, ids, part0a[1], from0a, full[1:], "rs_add_owners0a")
    gw_in, gw_out, gw_gate, gw_pp = run_job(exchange_halves_job(0, list(full_b) + list(full_a)), "rs_exchange_halves0")
    ((d_in, d_out, d_gate, d_pp), (m_in, m_out, m_gate, m_pp), (v_in, v_out, v_gate, v_pp)), _ = adamw_big(
        [w_in, w_out, w_ple_gate, w_ple_proj], [gw_in, gw_out, gw_gate, gw_pp],
        [m_w_in, m_w_out, m_w_ple_gate, m_w_ple_proj], [v_w_in, v_w_out, v_w_ple_gate, v_w_ple_proj], "adamw_big")

    small_like = [norm_g, ln_v_g, ln_v_b, w_s, b_s, ple_norm_g]

    def stack(j):
        return jnp.stack([small_g[layer][j].reshape(small_like[j].shape[1:]) for layer in range(DEPTH)])

    grads_small = [stack(j) for j in range(6)] + [g_final.reshape(D_MODEL), jnp.stack([small_g[l][6] for l in range(DEPTH)])]
    zeros_cw = jnp.zeros((DEPTH, 3, WIDTH), F32)
    params = [norm_g, ln_v_g, ln_v_b, w_s, b_s, ple_norm_g, final_g, zeros_cw]
    ms = [m_norm_g, m_ln_v_g, m_ln_v_b, m_w_s, m_b_s, m_ple_norm_g, m_final_g, zeros_cw]
    vs = [v_norm_g, v_ln_v_g, v_ln_v_b, v_w_s, v_b_s, v_ple_norm_g, v_final_g, zeros_cw + 1.0]
    gsum, dsm, msm, vsm = allreduce_small(_pack(grads_small), _pack(params), _pack(ms), _pack(vs))
    gs = _unpack(gsum, params)
    ds = _unpack(dsm, params)
    m2s = _unpack(msm, params)
    v2s = _unpack(vsm, params)
    g_cw_own = lax.dynamic_slice_in_dim(gs[7], own * (WIDTH // N_CHIPS), WIDTH // N_CHIPS, axis=2)
    rows2 = lambda a: a.reshape(DEPTH * 3, WIDTH // N_CHIPS)
    d_cw, m_cw, v_cw = adamw_rows(rows2(cw_rows), rows2(g_cw_own), rows2(jnp.transpose(m_conv_w, (0, 2, 1))),
                                  rows2(jnp.transpose(v_conv_w, (0, 2, 1))))
    back = lambda a: jnp.transpose(a.reshape(DEPTH, 3, WIDTH // N_CHIPS), (0, 2, 1))
    g_conv = jnp.transpose(g_cw_own, (0, 2, 1))

    def ordered(sm, cw_v, w_in_v, w_out_v, gate_v, pp_v):
        return [sm[0], w_in_v, sm[1], sm[2], sm[3], sm[4], cw_v, w_out_v, sm[5], gate_v, pp_v, sm[6]]

    grads = ordered(gs, g_conv, gw_in, gw_out, gw_gate, gw_pp)
    deltas = ordered(ds, back(d_cw), d_in, d_out, d_gate, d_pp)
    new_m = ordered(m2s, back(m_cw), m_in, m_out, m_gate, m_pp)
    new_v = ordered(v2s, back(v_cw), v_in, v_out, v_gate, v_pp)
    return (loss, dx[None], *grads, *deltas, *new_m, *new_v)
```

```python
import jax
import jax.numpy as jnp
from jax import lax
from jax.experimental import pallas as pl
from jax.experimental.pallas import tpu as pltpu

F32 = jnp.float32
BF16 = jnp.bfloat16

SEQ = 8192
D_MODEL = 1024
WIDTH = 512
PROJ = 7 * WIDTH
N_CHIPS = 4
COL_BLK = PROJ // N_CHIPS
PLE = 256
HEADS = 4
CHUNK = 128
DEPTH = 2
EPS = 1e-6
TILE = 256
N_TILES = SEQ // TILE
HALO = 8
VMEM_LIMIT = 60 * 1024 * 1024

ADAM_LR, ADAM_B1, ADAM_B2, ADAM_EPS, ADAM_WD, ADAM_STEP = 0.001, 0.9, 0.999, 1e-08, 0.01, 10
ADAM_C1 = 1.0 - ADAM_B1**ADAM_STEP
ADAM_C2 = 1.0 - ADAM_B2**ADAM_STEP

MESH = pl.DeviceIdType.MESH
ANY = pl.BlockSpec(memory_space=pl.ANY)


def _mm(a, b):
    return lax.dot_general(a, b, (((1,), (0,)), ((), ())), preferred_element_type=F32)


def _mm_nt(a, b):
    return lax.dot_general(a, b, (((1,), (1,)), ((), ())), preferred_element_type=F32)


def _mm_rows(a, w_ref):
    blk = w_ref.shape[1]
    acc = _mm(a[:, 0:blk], w_ref[0])
    for k in range(1, N_CHIPS):
        acc = acc + _mm(a[:, k * blk : (k + 1) * blk], w_ref[k])
    return acc


def _mm_nt_rows(a, w_ref):
    return jnp.concatenate([_mm_nt(a, w_ref[k]) for k in range(N_CHIPS)], axis=-1)


def _load_side_by_side(w_hbm, w_vmem, sems):
    copies = [pltpu.make_async_copy(w_hbm.at[k], w_vmem.at[:, pl.ds(k * COL_BLK, COL_BLK)], sems.at[k])
              for k in range(N_CHIPS)]
    for cp in copies:
        cp.start()
    for cp in copies:
        cp.wait()


def _sigmoid(z):
    return 1.0 / (1.0 + jnp.exp(-z))


def _rms_stats(x):
    r = lax.rsqrt(jnp.mean(x * x, axis=-1, keepdims=True) + EPS)
    return r, x * r


def _rms_bwd(dyg, xh, r):
    return r * (dyg - xh * jnp.mean(dyg * xh, axis=-1, keepdims=True))


def _ln_stats(v):
    mu = jnp.mean(v, axis=-1, keepdims=True)
    vc = v - mu
    rs = lax.rsqrt(jnp.mean(vc * vc, axis=-1, keepdims=True) + EPS)
    return rs, vc * rs


def _mesh_pos():
    x, y, c = lax.axis_index("x"), lax.axis_index("y"), lax.axis_index("c")
    return x, y, c, 2 * x + y


def _peer(x, y, c, r):
    return ((1 - x) if (r >> 1) else x, (1 - y) if (r & 1) else y, c)


def _full(shape):
    return pl.BlockSpec(shape, lambda *_: (0,) * len(shape))


def _const(shape, pos):
    return pl.BlockSpec((None,) * len(pos) + tuple(shape), lambda *_: tuple(pos) + (0,) * len(shape))


class Job:
    def __init__(self, ins, out_shapes, sems, start, middle, finish, aliases=None):
        self.ins, self.out_shapes, self.sems = list(ins), list(out_shapes), list(sems)
        self.start, self.middle, self.finish = start, middle, finish
        self.aliases = aliases or {}


def run_job(job, name):
    ni, no = len(job.ins), len(job.out_shapes)

    def body(*refs):
        parts = (refs[:ni], refs[ni : ni + no], refs[ni + no :])
        job.start(*parts)
        if job.middle is not None:
            job.middle(*parts)
        job.finish(*parts)

    return pl.pallas_call(
        body, name=name, out_shape=job.out_shapes, in_specs=[ANY] * ni, out_specs=[ANY] * no, scratch_shapes=job.sems,
        input_output_aliases=job.aliases, compiler_params=pltpu.CompilerParams(has_side_effects=True),
    )(*job.ins)


def _call(body, *, name, grid, in_specs, out_specs, out_shape, args, scratch=(), job=None, mid=None):
    params = pltpu.CompilerParams(dimension_semantics=("arbitrary",) * len(grid), vmem_limit_bytes=VMEM_LIMIT,
                                  has_side_effects=job is not None)
    n_in, n_out, n_sc = len(in_specs), len(out_specs), len(scratch)
    if job is None:
        outs = pl.pallas_call(body, name=name, grid=grid, in_specs=in_specs, out_specs=out_specs, out_shape=out_shape,
                              scratch_shapes=list(scratch), compiler_params=params)(*args)
        return list(outs), []
    ji, jo = len(job.ins), len(job.out_shapes)
    assert not job.aliases and (job.middle is None or len(grid) == 1)

    def wrapped(*refs):
        ins, jin = refs[:n_in], refs[n_in : n_in + ji]
        o0 = n_in + ji
        outs, jout = refs[o0 : o0 + n_out], refs[o0 + n_out : o0 + n_out + jo]
        s0 = o0 + n_out + jo
        sc, jsem = refs[s0 : s0 + n_sc], refs[s0 + n_sc :]
        step = pl.program_id(0)
        for d in range(1, len(grid)):
            step = step * grid[d] + pl.program_id(d)
        n_steps = 1
        for g in grid:
            n_steps *= g

        @pl.when(step == 0)
        def _():
            job.start(jin, jout, jsem)

        if job.middle is not None:
            @pl.when(step == mid)
            def _():
                job.middle(jin, jout, jsem)

        body(*ins, *outs, *sc)

        @pl.when(step == n_steps - 1)
        def _():
            job.finish(jin, jout, jsem)

    outs = pl.pallas_call(
        wrapped, name=name, grid=grid, in_specs=list(in_specs) + [ANY] * ji, out_specs=list(out_specs) + [ANY] * jo,
        out_shape=list(out_shape) + job.out_shapes, scratch_shapes=list(scratch) + job.sems, compiler_params=params,
    )(*args, *job.ins)
    return list(outs[:n_out]), list(outs[n_out:])


def gather_job(items):
    n = len(items)

    def src_half(a, ref, h):
        arr, layer = items[a]
        if layer is None:
            return ref.at[h]
        rh = arr.shape[1] // 2
        return ref.at[layer, pl.ds(h * rh, rh)]

    def dst_half(a, ref, chip, h):
        arr, layer = items[a]
        if layer is None:
            return ref.at[chip, h]
        rh = arr.shape[1] // 2
        return ref.at[chip, pl.ds(h * rh, rh)]

    def copies(src, dst, scratch):
        stage, (in_sem, out_sem, ici_s, ici_r, fwd_s, fwd_r) = scratch[:n], scratch[n:]
        x, y, c, own = _mesh_pos()
        local, ici, fwd, got = [], {}, {}, {}
        for a in range(n):
            layer = items[a][1]
            local.append((pltpu.make_async_copy(src[a] if layer is None else src[a].at[layer], stage[a], in_sem.at[a]),
                          pltpu.make_async_copy(stage[a], dst[a].at[own], out_sem.at[a])))
            for r in (1, 2, 3):
                k = 3 * a + r - 1
                ici[a, r] = pltpu.make_async_remote_copy(
                    src_ref=src_half(a, src[a], c), dst_ref=dst_half(a, dst[a], own, c), send_sem=ici_s.at[k],
                    recv_sem=ici_r.at[k], device_id=_peer(x, y, c, r), device_id_type=MESH)
                fwd[a, r] = pltpu.make_async_remote_copy(
                    src_ref=dst_half(a, dst[a], own ^ r, c), dst_ref=dst_half(a, dst[a], own ^ r, c), send_sem=fwd_s.at[k],
                    recv_sem=fwd_r.at[k], device_id=(x, y, 1 - c), device_id_type=MESH)
                got[a, r] = pltpu.make_async_remote_copy(
                    src_ref=dst_half(a, dst[a], own ^ r, 1 - c), dst_ref=dst_half(a, dst[a], own ^ r, 1 - c),
                    send_sem=fwd_s.at[k], recv_sem=fwd_r.at[k], device_id=(x, y, 1 - c), device_id_type=MESH)
        return local, ici, fwd, got

    def start(src, dst, sems):
        local, ici, _, _ = copies(src, dst, sems)
        for a in range(n):
            for r in (1, 2, 3):
                ici[a, r].start()
        for cp_in, _ in local:
            cp_in.start()
        for cp_in, cp_out in local:
            cp_in.wait()
            cp_out.start()

    def middle(src, dst, sems):
        _, ici, fwd, _ = copies(src, dst, sems)
        for r in (1, 2, 3):
            for a in range(n):
                ici[a, r].wait_recv()
                fwd[a, r].start()

    def finish(src, dst, sems):
        local, ici, fwd, got = copies(src, dst, sems)
        for a in range(n):
            for r in (1, 2, 3):
                got[a, r].wait_recv()
        for a in range(n):
            for r in (1, 2, 3):
                ici[a, r].wait_send()
                fwd[a, r].wait_send()
        for _, cp_out in local:
            cp_out.wait()

    out_shapes = [jax.ShapeDtypeStruct((N_CHIPS,) + (arr.shape if layer is None else arr.shape[1:]), arr.dtype)
                  for arr, layer in items]
    stage = [pltpu.VMEM(arr.shape if layer is None else arr.shape[1:], arr.dtype) for arr, layer in items]
    sems = [pltpu.SemaphoreType.DMA((n,))] * 2 + [pltpu.SemaphoreType.DMA((3 * n,))] * 4
    return Job([arr for arr, _ in items], out_shapes, stage + sems, start, middle, finish)


def _mixer_fwd(proj_ref, lg, lb, ws_ref, bsb_ref, cw_ref, mix_ref, xcbuf, halo_xc):
    u = proj_ref[:, 0 * WIDTH : 1 * WIDTH]
    v = proj_ref[:, 1 * WIDTH : 2 * WIDTH]
    za = proj_ref[:, 2 * WIDTH : 3 * WIDTH]
    h = proj_ref[:, 3 * WIDTH : 4 * WIDTH]
    gb = proj_ref[:, 4 * WIDTH : 5 * WIDTH]
    gc = proj_ref[:, 5 * WIDTH : 6 * WIDTH]
    zb = proj_ref[:, 6 * WIDTH : 7 * WIDTH]
    rs, vhat = _ln_stats(v)
    vl = vhat * lg + lb
    vlb = vl.astype(BF16)
    for j in range(TILE // CHUNK):
        rows = slice(j * CHUNK, (j + 1) * CHUNK)
        for hd in range(HEADS):
            cols = slice(hd * CHUNK, (hd + 1) * CHUNK)
            mix_ref[rows, cols] = _mm(ws_ref[hd], vlb[rows, cols]) + bsb_ref[:, cols]
    mixed = mix_ref[...]
    siga = _sigmoid(za)
    sigb = _sigmoid(zb)
    xc = gc * h
    xcbuf[0:HALO, :] = halo_xc
    xcbuf[HALO : HALO + TILE, :] = xc
    y = cw_ref[0:1, :] * xcbuf[HALO - 2 : HALO - 2 + TILE, :] + cw_ref[1:2, :] * xcbuf[HALO - 1 : HALO - 1 + TILE, :]
    y = y + cw_ref[2:3, :] * xc
    return dict(u=u, za=za, h=h, gb=gb, gc=gc, zb=zb, rs=rs, vhat=vhat, vlb=vlb, mixed=mixed, siga=siga, sigb=sigb,
                xc=xc, y=y)


def fwd_layer(layer, x, p, wts, small, job=None, head=None):
    win, wout, wg, wpp = wts
    ng, lg, lb, ws, bsb, cw, pg = small
    n_head = 0 if head is None else 2

    def body(*refs):
        (x_ref, p_ref, win_ref, wout_ref, wg_ref, wpp_ref, ng_ref, lg_ref, lb_ref, ws_ref, bsb_ref, cw_ref,
         pg_ref) = refs[:13]
        head_in = refs[13 : 13 + n_head]
        (proj_ref, x2_ref, gate_ref, pp_ref, x3_ref, hnT_ref, catT_ref, hn2T_ref,
         pT_ref) = refs[13 + n_head : 22 + n_head]
        head_out = refs[22 + n_head : 22 + 2 * n_head]
        mix_ref, xcbuf, carry, wcat, wsem = refs[22 + 2 * n_head :]
        i = pl.program_id(0)

        @pl.when(i == 0)
        def _():
            _load_side_by_side(win_ref, wcat, wsem)
            carry[...] = jnp.zeros_like(carry)
            for ref in head_out:
                ref[...] = jnp.zeros_like(ref)

        xv = x_ref[...]
        _, xh = _rms_stats(xv)
        hn = xh * ng_ref[...]
        hnT_ref[...] = hn.T.astype(BF16)
        proj_ref[...] = _mm(hn.astype(BF16), wcat[...])
        m = _mixer_fwd(proj_ref, lg_ref[...], lb_ref[...], ws_ref, bsb_ref, cw_ref, mix_ref, xcbuf, carry[...])
        carry[...] = m["xc"][TILE - HALO : TILE, :]
        out_a = (m["u"] * m["mixed"]) * (m["za"] * m["siga"])
        out_b = (m["gb"] * m["y"]) * (m["zb"] * m["sigb"])
        cat = jnp.concatenate([out_a, out_b], axis=-1)
        catT_ref[...] = cat.T.astype(BF16)
        x2 = xv + _mm_rows(cat.astype(BF16), wout_ref)
        x2_ref[...] = x2
        _, xh2 = _rms_stats(x2)
        hn2 = xh2 * pg_ref[...]
        hn2T_ref[...] = hn2.T.astype(BF16)
        gate = _sigmoid(_mm_rows(hn2.astype(BF16), wg_ref))
        gate_ref[...] = gate
        pv = p_ref[...]
        pT_ref[...] = pv.T.astype(BF16)
        pb = pv.astype(BF16)
        for k in range(N_CHIPS):
            pp_ref[:, k * PLE : (k + 1) * PLE] = _mm(pb, wpp_ref[k])
        x3 = x2 + gate * pp_ref[...]
        if head is None:
            x3_ref[...] = x3
        else:
            t_ref, gf_ref = head_in
            loss_ref, gg_ref = head_out
            r3, xh3 = _rms_stats(x3)
            gf = gf_ref[...]
            err = xh3 * gf - t_ref[...]
            loss_ref[...] += (0.5 / D_MODEL) * jnp.sum(err * err).reshape(1, 1)
            dy = err * (1.0 / D_MODEL)
            gg_ref[...] += jnp.sum(dy * xh3, axis=0, keepdims=True)
            x3_ref[...] = _rms_bwd(dy * gf, xh3, r3)

    tok = lambda w: pl.BlockSpec((TILE, w), lambda i: (i, 0))
    tokT = lambda w: pl.BlockSpec((None, w, TILE), lambda i: (i, 0, 0))
    once = lambda a: pl.BlockSpec(a.shape, lambda i: (0,) * a.ndim, pipeline_mode=pl.Buffered(1))
    f32 = lambda w: jax.ShapeDtypeStruct((SEQ, w), F32)
    bfT = lambda w: jax.ShapeDtypeStruct((N_TILES, w, TILE), BF16)
    head_specs = [] if head is None else [tok(D_MODEL), _full((1, D_MODEL))]
    head_outs = [] if head is None else [_full((1, 1)), _full((1, D_MODEL))]
    head_shapes = [] if head is None else [jax.ShapeDtypeStruct((1, 1), F32), jax.ShapeDtypeStruct((1, D_MODEL), F32)]
    return _call(
        body, name=f"fwd_layer{layer}", grid=(N_TILES,),
        in_specs=[tok(D_MODEL), pl.BlockSpec((None, None, TILE, PLE), lambda i: (layer, 0, i, 0)),
                  ANY, once(wout), once(wg), once(wpp),
                  _const((1, D_MODEL), (layer,)), _const((1, WIDTH), (layer,)), _const((1, WIDTH), (layer,)),
                  _const((HEADS, CHUNK, CHUNK), (layer,)), _const((CHUNK, WIDTH), (layer,)), _const((3, WIDTH), (layer,)),
                  _const((1, D_MODEL), (layer,))] + head_specs,
        out_specs=[tok(PROJ), tok(D_MODEL), tok(D_MODEL), tok(D_MODEL), tok(D_MODEL),
                   tokT(D_MODEL), tokT(D_MODEL), tokT(D_MODEL), tokT(PLE)] + head_outs,
        out_shape=[f32(PROJ), f32(D_MODEL), f32(D_MODEL), f32(D_MODEL), f32(D_MODEL),
                   bfT(D_MODEL), bfT(D_MODEL), bfT(D_MODEL), bfT(PLE)] + head_shapes,
        scratch=[pltpu.VMEM((TILE, WIDTH), F32), pltpu.VMEM((HALO + TILE, WIDTH), F32), pltpu.VMEM((HALO, WIDTH), F32),
                 pltpu.VMEM((D_MODEL, PROJ), BF16), pltpu.SemaphoreType.DMA((N_CHIPS,))],
        args=(x, p, win, wout, wg, wpp, ng, lg, lb, ws, bsb, cw, pg) + (() if head is None else tuple(head)),
        job=job, mid=N_TILES // 2)


def bwd_layer(layer, dx3, x2, gate, pp, x, proj, wts, small, wsT, tril):
    win, wout, wg, _ = wts
    ng, lg, lb, ws, bsb, cw, pg = small

    def body(dx3_ref, x2_ref, gate_ref, pp_ref, x_ref, proj_ref, halo_ref, win_ref, wout_ref, wg_ref, ng_ref, lg_ref,
             lb_ref, ws_ref, wsT_ref, bsb_ref, cw_ref, pg_ref, tril_ref,
             dx_ref, dproj_ref, dpp_ref, dgl_ref, dx2b_ref, gn_ref, glg_ref, glb_ref, gws_ref, gbs_ref, gcw_ref, gpg_ref,
             mix_ref, xcbuf, dycbuf, dvl_ref, bs_acc, wcat, wsem):
        i = pl.program_id(0)

        @pl.when(i == 0)
        def _():
            _load_side_by_side(win_ref, wcat, wsem)
            for ref in (gn_ref, glg_ref, glb_ref, gws_ref, gcw_ref, gpg_ref, bs_acc):
                ref[...] = jnp.zeros_like(ref)
            dycbuf[TILE : TILE + HALO, :] = jnp.zeros((HALO, WIDTH), F32)

        dx3v = dx3_ref[...]
        gate_v = gate_ref[...]
        dpp_ref[...] = (dx3v * gate_v).astype(BF16)
        dgl = ((dx3v * pp_ref[...]) * gate_v * (1.0 - gate_v)).astype(BF16)
        dgl_ref[...] = dgl
        dhn2 = _mm_nt_rows(dgl, wg_ref)
        r2, xh2 = _rms_stats(x2_ref[...])
        gpg_ref[...] += jnp.sum(dhn2 * xh2, axis=0, keepdims=True)
        dx2 = dx3v + _rms_bwd(dhn2 * pg_ref[...], xh2, r2)
        dx2b = dx2.astype(BF16)
        dx2b_ref[...] = dx2b
        dcat = _mm_nt_rows(dx2b, wout_ref)
        lgv = lg_ref[...]
        halo_xc = halo_ref[:, 5 * WIDTH : 6 * WIDTH] * halo_ref[:, 3 * WIDTH : 4 * WIDTH]
        halo_xc = jnp.where(i == N_TILES - 1, 0.0, halo_xc)
        m = _mixer_fwd(proj_ref, lgv, lb_ref[...], ws_ref, bsb_ref, cw_ref, mix_ref, xcbuf, halo_xc)
        u, za, h, gb, gc, zb = m["u"], m["za"], m["h"], m["gb"], m["gc"], m["zb"]
        mixed, siga, sigb, xc, y = m["mixed"], m["siga"], m["sigb"], m["xc"], m["y"]
        doa = dcat[:, 0:WIDTH]
        dob = dcat[:, WIDTH : 2 * WIDTH]
        sa = za * siga
        sb = zb * sigb
        doa_sa = doa * sa
        dproj_ref[:, 0 * WIDTH : 1 * WIDTH] = (doa_sa * mixed).astype(BF16)
        dmixed = doa_sa * u
        dza = (doa * (u * mixed)) * (siga * (1.0 + za * (1.0 - siga)))
        dproj_ref[:, 2 * WIDTH : 3 * WIDTH] = dza.astype(BF16)
        dob_sb = dob * sb
        dproj_ref[:, 4 * WIDTH : 5 * WIDTH] = (dob_sb * y).astype(BF16)
        dyc = dob_sb * gb
        dzb = (dob * (gb * y)) * (sigb * (1.0 + zb * (1.0 - sigb)))
        dproj_ref[:, 6 * WIDTH : 7 * WIDTH] = dzb.astype(BF16)
        dycbuf[0:TILE, :] = dyc
        dyc1 = dycbuf[1 : 1 + TILE, :]
        dyc2 = dycbuf[2 : 2 + TILE, :]
        dxc = cw_ref[2:3, :] * dyc + cw_ref[1:2, :] * dyc1 + cw_ref[0:1, :] * dyc2
        gcw_ref[0:1, :] += jnp.sum(xc * dyc2, axis=0, keepdims=True)
        gcw_ref[1:2, :] += jnp.sum(xc * dyc1, axis=0, keepdims=True)
        gcw_ref[2:3, :] += jnp.sum(xc * dyc, axis=0, keepdims=True)
        dycbuf[TILE : TILE + HALO, :] = dyc[0:HALO, :]
        dproj_ref[:, 5 * WIDTH : 6 * WIDTH] = (dxc * h).astype(BF16)
        dproj_ref[:, 3 * WIDTH : 4 * WIDTH] = (dxc * gc).astype(BF16)
        dmb = dmixed.astype(BF16)
        vlb = m["vlb"]
        bsum = jnp.zeros((CHUNK, WIDTH), F32)
        for j in range(TILE // CHUNK):
            rows = slice(j * CHUNK, (j + 1) * CHUNK)
            bsum = bsum + dmixed[rows, :]
            for hd in range(HEADS):
                cols = slice(hd * CHUNK, (hd + 1) * CHUNK)
                gws_ref[hd] += _mm_nt(dmb[rows, cols], vlb[rows, cols])
                dvl_ref[rows, cols] = _mm(wsT_ref[hd], dmb[rows, cols])
        bs_acc[...] += bsum
        dvl = dvl_ref[...]
        vhat = m["vhat"]
        glb_ref[...] += jnp.sum(dvl, axis=0, keepdims=True)
        glg_ref[...] += jnp.sum(dvl * vhat, axis=0, keepdims=True)
        dvh = dvl * lgv
        dv = m["rs"] * (dvh - jnp.mean(dvh, axis=-1, keepdims=True) - vhat * jnp.mean(dvh * vhat, axis=-1, keepdims=True))
        dproj_ref[:, 1 * WIDTH : 2 * WIDTH] = dv.astype(BF16)
        dhn = _mm_nt(dproj_ref[...], wcat[...])
        r1, xh = _rms_stats(x_ref[...])
        gn_ref[...] += jnp.sum(dhn * xh, axis=0, keepdims=True)
        dx_ref[...] = dx2 + _rms_bwd(dhn * ng_ref[...], xh, r1)

        @pl.when(i == N_TILES - 1)
        def _():
            for hd in range(HEADS):
                gws_ref[hd] = gws_ref[hd] * tril_ref[...]
                gbs_ref[hd : hd + 1, :] = jnp.sum(bs_acc[:, hd * CHUNK : (hd + 1) * CHUNK].T, axis=0, keepdims=True)

    rev = lambda w: pl.BlockSpec((TILE, w), lambda i: (N_TILES - 1 - i, 0))
    halo = pl.BlockSpec((HALO, PROJ), lambda i: (jnp.maximum((N_TILES - 1 - i) * (TILE // HALO) - 1, 0), 0))
    once = lambda a: pl.BlockSpec(a.shape, lambda i: (0,) * a.ndim, pipeline_mode=pl.Buffered(1))
    vec = lambda w: jax.ShapeDtypeStruct((1, w), F32)
    b16 = lambda w: jax.ShapeDtypeStruct((SEQ, w), BF16)
    outs, _ = _call(
        body, name=f"bwd_layer{layer}", grid=(N_TILES,),
        in_specs=[rev(D_MODEL), rev(D_MODEL), rev(D_MODEL), rev(D_MODEL), rev(D_MODEL), rev(PROJ), halo, ANY,
                  once(wout), once(wg),
                  _const((1, D_MODEL), (layer,)), _const((1, WIDTH), (layer,)), _const((1, WIDTH), (layer,)),
                  _const((HEADS, CHUNK, CHUNK), (layer,)), _const((HEADS, CHUNK, CHUNK), (layer,)),
                  _const((CHUNK, WIDTH), (layer,)), _const((3, WIDTH), (layer,)), _const((1, D_MODEL), (layer,)),
                  _full((CHUNK, CHUNK))],
        out_specs=[rev(D_MODEL), rev(PROJ), rev(D_MODEL), rev(D_MODEL), rev(D_MODEL),
                   _full((1, D_MODEL)), _full((1, WIDTH)), _full((1, WIDTH)), _full((HEADS, CHUNK, CHUNK)),
                   _full((HEADS, CHUNK)), _full((3, WIDTH)), _full((1, D_MODEL))],
        out_shape=[jax.ShapeDtypeStruct((SEQ, D_MODEL), F32), b16(PROJ), b16(D_MODEL), b16(D_MODEL), b16(D_MODEL),
                   vec(D_MODEL), vec(WIDTH), vec(WIDTH), jax.ShapeDtypeStruct((HEADS, CHUNK, CHUNK), F32),
                   jax.ShapeDtypeStruct((HEADS, CHUNK), F32), jax.ShapeDtypeStruct((3, WIDTH), F32), vec(D_MODEL)],
        scratch=[pltpu.VMEM((TILE, WIDTH), F32), pltpu.VMEM((HALO + TILE, WIDTH), F32),
                 pltpu.VMEM((TILE + HALO, WIDTH), F32), pltpu.VMEM((TILE, WIDTH), F32), pltpu.VMEM((CHUNK, WIDTH), F32),
                 pltpu.VMEM((D_MODEL, PROJ), BF16), pltpu.SemaphoreType.DMA((N_CHIPS,))],
        args=(dx3, x2, gate, pp, x, proj, proj, win, wout, wg, ng, lg, lb, ws, wsT, bsb, cw, pg, tril))
    return outs


def wgrad(pairs, n_tiles, col_blocked, tk, name, job=None):
    n = len(pairs)
    m_dim, n_dim = pairs[0][0].shape[1], pairs[0][1].shape[1]
    tn = n_dim // n_tiles
    cb, mb = n_dim // N_CHIPS, m_dim // N_CHIPS
    per_tile = N_CHIPS // n_tiles
    assert col_blocked or n_tiles == 1

    def body(*refs):
        for a in range(n):
            a_ref, b_ref, o_ref = refs[2 * a], refs[2 * a + 1], refs[2 * n + a]

            @pl.when(pl.program_id(1) == 0)
            def _():
                o_ref[...] = jnp.zeros_like(o_ref)

            prod = _mm(a_ref[0], b_ref[0:TILE, :])
            for t in range(1, tk // TILE):
                prod = prod + _mm(a_ref[t], b_ref[t * TILE : (t + 1) * TILE, :])
            if col_blocked:
                for q in range(per_tile):
                    o_ref[q] += prod[:, q * cb : (q + 1) * cb]
            else:
                for q in range(N_CHIPS):
                    o_ref[q] += prod[q * mb : (q + 1) * mb, :]

    if col_blocked:
        shape = (N_CHIPS, m_dim, cb)
        o_spec = pl.BlockSpec((per_tile, m_dim, cb), lambda j, k: (j, 0, 0))
    else:
        shape = (N_CHIPS, mb, n_dim)
        o_spec = pl.BlockSpec((N_CHIPS, mb, n_dim), lambda j, k: (0, 0, 0))
    return _call(
        body, name=name, grid=(n_tiles, SEQ // tk),
        in_specs=[pl.BlockSpec((tk // TILE, m_dim, TILE), lambda j, k: (k, 0, 0)),
                  pl.BlockSpec((tk, tn), lambda j, k: (k, j))] * n,
        out_specs=[o_spec] * n, out_shape=[jax.ShapeDtypeStruct(shape, F32)] * n,
        args=[t for pair in pairs for t in pair], job=job)


def to_sibling_job(grads):
    n = len(grads)

    def copies(src, dst, sems):
        x, y, c, _ = _mesh_pos()
        out = []
        for a in range(n):
            rh = grads[a].shape[1] // 2
            out.append(pltpu.make_async_remote_copy(
                src_ref=src[a].at[:, pl.ds((1 - c) * rh, rh), :], dst_ref=dst[a], send_sem=sems[0].at[a],
                recv_sem=sems[1].at[a], device_id=(x, y, 1 - c), device_id_type=MESH))
        return out

    def start(src, dst, sems):
        for cp in copies(src, dst, sems):
            cp.start()

    def finish(src, dst, sems):
        for cp in copies(src, dst, sems):
            cp.wait()

    out_shapes = [jax.ShapeDtypeStruct((N_CHIPS, g.shape[1] // 2, g.shape[2]), F32) for g in grads]
    return Job(grads, out_shapes, [pltpu.SemaphoreType.DMA((n,))] * 2, start, None, finish)


def rs_add_sibling(ids, grads, recvd, name):
    n = len(grads)

    def body(ids_ref, *refs):
        g, r = refs[:n], refs[n : 2 * n]
        pb, pf = refs[2 * n : 3 * n], refs[3 * n :]
        k = pl.program_id(0)
        for a in range(n):
            s = g[a][...] + r[a][...]
            pb[a][...] = s.astype(BF16)

            @pl.when(k == ids_ref[1])
            def _():
                pf[a][...] = s

    in_specs, out_specs, out_shape = [], [], []
    for g in grads:
        rh, cc = g.shape[1] // 2, g.shape[2]
        in_specs.append(pl.BlockSpec((None, rh, cc), lambda k, ids: (k, ids[0], 0)))
    for g in grads:
        rh, cc = g.shape[1] // 2, g.shape[2]
        in_specs.append(pl.BlockSpec((None, rh, cc), lambda k, ids: (k, 0, 0)))
        out_specs.append(pl.BlockSpec((None, rh, cc), lambda k, ids: (k, 0, 0)))
        out_shape.append(jax.ShapeDtypeStruct((N_CHIPS, rh, cc), BF16))
    for g in grads:
        rh, cc = g.shape[1] // 2, g.shape[2]
        out_specs.append(pl.BlockSpec((rh, cc), lambda k, ids: (0, 0)))
        out_shape.append(jax.ShapeDtypeStruct((rh, cc), F32))
    outs = pl.pallas_call(
        body, name=name, out_shape=out_shape,
        grid_spec=pltpu.PrefetchScalarGridSpec(num_scalar_prefetch=1, grid=(N_CHIPS,), in_specs=in_specs,
                                               out_specs=out_specs),
        compiler_params=pltpu.CompilerParams(dimension_semantics=("arbitrary",), vmem_limit_bytes=VMEM_LIMIT),
    )(ids, *grads, *recvd)
    return list(outs[:n]), list(outs[n:])


def to_owners_job(partials):
    n = len(partials)

    def copies(src, dst, sems):
        x, y, c, own = _mesh_pos()
        out = []
        for a in range(n):
            for r in (1, 2, 3):
                out.append(pltpu.make_async_remote_copy(
                    src_ref=src[a].at[own ^ r], dst_ref=dst[a].at[r - 1], send_sem=sems[0].at[3 * a + r - 1],
                    recv_sem=sems[1].at[3 * a + r - 1], device_id=_peer(x, y, c, r), device_id_type=MESH))
        return out

    def start(src, dst, sems):
        for cp in copies(src, dst, sems):
            cp.start()

    def finish(src, dst, sems):
        for cp in copies(src, dst, sems):
            cp.wait()

    out_shapes = [jax.ShapeDtypeStruct((3,) + p.shape[1:], BF16) for p in partials]
    return Job(partials, out_shapes, [pltpu.SemaphoreType.DMA((3 * n,))] * 2, start, None, finish)


def rs_add_owners(layer, ids, own_f32, recvd, prev, name):
    n = len(own_f32)
    nb = 2

    def body(ids_ref, *refs):
        o, r, f = refs[:n], refs[n : 2 * n], refs[-n:]
        for a in range(n):
            f[a][...] = ((o[a][...] + r[a][0].astype(F32)) + r[a][1].astype(F32)) + r[a][2].astype(F32)

    in_specs, out_specs, out_shape = [], [], []
    for o in own_f32:
        in_specs.append(pl.BlockSpec((o.shape[0] // nb, o.shape[1]), lambda j, ids: (j, 0)))
    for o in own_f32:
        in_specs.append(pl.BlockSpec((3, o.shape[0] // nb, o.shape[1]), lambda j, ids: (0, j, 0)))
        out_specs.append(pl.BlockSpec((None, o.shape[0] // nb, o.shape[1]), lambda j, ids: (layer, ids[0] * nb + j, 0)))
        out_shape.append(jax.ShapeDtypeStruct((DEPTH, 2 * o.shape[0], o.shape[1]), F32))
    args, aliases = [ids, *own_f32, *recvd], {}
    if prev is not None:
        in_specs += [ANY] * n
        args += list(prev)
        aliases = {1 + 2 * n + a: a for a in range(n)}
    return pl.pallas_call(
        body, name=name, out_shape=out_shape, input_output_aliases=aliases,
        grid_spec=pltpu.PrefetchScalarGridSpec(num_scalar_prefetch=1, grid=(nb,), in_specs=in_specs, out_specs=out_specs),
        compiler_params=pltpu.CompilerParams(dimension_semantics=("arbitrary",), vmem_limit_bytes=VMEM_LIMIT),
    )(*args)


def exchange_halves_job(layer, full):
    n = len(full)

    def copies(src, dst, sems):
        x, y, c, _ = _mesh_pos()
        out = []
        for a in range(n):
            rh = full[a].shape[1] // 2
            out.append(pltpu.make_async_remote_copy(
                src_ref=src[a].at[layer, pl.ds(c * rh, rh), :], dst_ref=dst[a].at[layer, pl.ds(c * rh, rh), :],
                send_sem=sems[0].at[a], recv_sem=sems[1].at[a], device_id=(x, y, 1 - c), device_id_type=MESH))
        return out

    def start(src, dst, sems):
        for cp in copies(src, dst, sems):
            cp.start()

    def finish(src, dst, sems):
        for cp in copies(src, dst, sems):
            cp.wait()

    out_shapes = [jax.ShapeDtypeStruct(f.shape, F32) for f in full]
    return Job(full, out_shapes, [pltpu.SemaphoreType.DMA((n,))] * 2, start, None, finish, {a: a for a in range(n)})


def _adamw(w, g, m, v):
    m2 = ADAM_B1 * m + (1.0 - ADAM_B1) * g
    v2 = ADAM_B2 * v + (1.0 - ADAM_B2) * (g * g)
    delta = -ADAM_LR * ((m2 / ADAM_C1) / (jnp.sqrt(v2 / ADAM_C2) + ADAM_EPS) + ADAM_WD * w)
    return delta, m2, v2


def adamw_big(ws, gs, ms, vs, name, job=None):
    n = len(ws)
    nb = 4

    def body(*refs):
        for a in range(n):
            w, g, m, v = (refs[j * n + a][...] for j in range(4))
            d, m2, v2 = _adamw(w, g, m, v)
            refs[4 * n + a][...] = d
            refs[5 * n + a][...] = m2
            refs[6 * n + a][...] = v2

    specs = [pl.BlockSpec((None, w.shape[1] // nb, w.shape[2]), lambda l, j: (l, j, 0)) for w in ws]
    shapes = [jax.ShapeDtypeStruct(w.shape, F32) for w in ws]
    outs, job_outs = _call(body, name=name, grid=(DEPTH, nb), in_specs=specs * 4, out_specs=specs * 3,
                           out_shape=shapes * 3, args=(*ws, *gs, *ms, *vs), job=job)
    return (outs[:n], outs[n : 2 * n], outs[2 * n :]), job_outs


def small_to_sibling_job(g):
    def copy(src, dst, sems):
        x, y, c, _ = _mesh_pos()
        return pltpu.make_async_remote_copy(src_ref=src[0], dst_ref=dst[0], send_sem=sems[0].at[0], recv_sem=sems[1].at[0],
                                            device_id=(x, y, 1 - c), device_id_type=MESH)

    return Job([g], [jax.ShapeDtypeStruct(g.shape, F32)], [pltpu.SemaphoreType.DMA((1,))] * 2,
               lambda *r: copy(*r).start(), None, lambda *r: copy(*r).wait())


def small_chip_sum(g, recvd):
    def body(g_ref, r_ref, o_ref):
        o_ref[...] = g_ref[...] + r_ref[...]

    return pl.pallas_call(body, name="small_chip_sum", out_shape=jax.ShapeDtypeStruct(g.shape, F32))(g, recvd)


def small_to_chips_job(chip_sum):
    def copies(src, dst, sems):
        x, y, c, _ = _mesh_pos()
        return [pltpu.make_async_remote_copy(src_ref=src[0], dst_ref=dst[0].at[r - 1], send_sem=sems[0].at[r - 1],
                                             recv_sem=sems[1].at[r - 1], device_id=_peer(x, y, c, r), device_id_type=MESH)
                for r in (1, 2, 3)]

    def start(*refs):
        for cp in copies(*refs):
            cp.start()

    def finish(*refs):
        for cp in copies(*refs):
            cp.wait()

    return Job([chip_sum], [jax.ShapeDtypeStruct((3,) + chip_sum.shape, F32)], [pltpu.SemaphoreType.DMA((3,))] * 2,
               start, None, finish)


def small_finish(chip_sum, from_chips, w, m, v):
    rows = chip_sum.shape[0]

    def body(cs_ref, fc_ref, w_ref, m_ref, v_ref, gsum_ref, d_ref, m2_ref, v2_ref, sums):
        own = 2 * lax.axis_index("x") + lax.axis_index("y")
        sums[0] = cs_ref[...]
        sums[1:N_CHIPS] = fc_ref[...]
        tot = sums[own] + sums[own ^ 1]
        tot = tot + sums[own ^ 2]
        tot = tot + sums[own ^ 3]
        gsum_ref[...] = tot
        d, m2, v2 = _adamw(w_ref[...], tot, m_ref[...], v_ref[...])
        d_ref[...] = d
        m2_ref[...] = m2
        v2_ref[...] = v2

    return pl.pallas_call(
        body, name="small_finish", out_shape=[jax.ShapeDtypeStruct(chip_sum.shape, F32)] * 4,
        scratch_shapes=[pltpu.VMEM((N_CHIPS, rows, 128), F32)],
    )(chip_sum, from_chips, w, m, v)


def merge_jobs(*jobs):
    def parts(refs, counts):
        out, pos = [], 0
        for cnt in counts:
            out.append(refs[pos : pos + cnt])
            pos += cnt
        return out

    def run(phase):
        def go(ins, outs, sems):
            for job, i, o, s in zip(jobs, parts(ins, [len(j.ins) for j in jobs]),
                                    parts(outs, [len(j.out_shapes) for j in jobs]), parts(sems, [len(j.sems) for j in jobs])):
                getattr(job, phase)(i, o, s)
        return go

    assert all(j.middle is None and not j.aliases for j in jobs)
    return Job([a for j in jobs for a in j.ins], [a for j in jobs for a in j.out_shapes], [a for j in jobs for a in j.sems],
               run("start"), None, run("finish"))


def adamw_rows(w, g, m, v):
    def body(w_ref, g_ref, m_ref, v_ref, d_ref, m2_ref, v2_ref):
        d, m2, v2 = _adamw(w_ref[...], g_ref[...], m_ref[...], v_ref[...])
        d_ref[...] = d
        m2_ref[...] = m2
        v2_ref[...] = v2

    return pl.pallas_call(body, name="adamw_conv", out_shape=[jax.ShapeDtypeStruct(w.shape, F32)] * 3)(w, g, m, v)


def _pack(parts):
    out = []
    for a in parts:
        flat = a.reshape(-1)
        rows = -(-flat.shape[0] // 1024) * 8
        flat = jnp.pad(flat, (0, rows * 128 - flat.shape[0]))
        out.append(flat.reshape(rows, 128))
    return jnp.concatenate(out, axis=0)


def _unpack(packed, like):
    out, row = [], 0
    for a in like:
        size = 1
        for s in a.shape:
            size *= s
        rows = -(-size // 1024) * 8
        out.append(packed[row : row + rows].reshape(-1)[:size].reshape(a.shape))
        row += rows
    return out


def kernel(x, p, norm_g, w_in, ln_v_g, ln_v_b, w_s, b_s, conv_w, w_out, ple_norm_g, w_ple_gate, w_ple_proj, final_g, loss_target, m_norm_g, m_w_in, m_ln_v_g, m_ln_v_b, m_w_s, m_b_s, m_conv_w, m_w_out, m_ple_norm_g, m_w_ple_gate, m_w_ple_proj, m_final_g, v_norm_g, v_w_in, v_ln_v_g, v_ln_v_b, v_w_s, v_b_s, v_conv_w, v_w_out, v_ple_norm_g, v_w_ple_gate, v_w_ple_proj, v_final_g):
    cx, cy, cc = lax.axis_index("x"), lax.axis_index("y"), lax.axis_index("c")
    own = 2 * cx + cy
    ids = jnp.stack([cc, own]).astype(jnp.int32)

    cw_rows = jnp.transpose(conv_w, (0, 2, 1))
    shards = [w_in.astype(BF16), w_out.astype(BF16), w_ple_gate.astype(BF16), w_ple_proj.astype(BF16)]
    *wts0, g_cw = run_job(gather_job([(s, 0) for s in shards] + [(cw_rows, None)]), "gather_weights0")
    cw_full = jnp.transpose(g_cw, (1, 2, 0, 3)).reshape(DEPTH, 3, WIDTH)
    tril = jnp.tril(jnp.ones((CHUNK, CHUNK), F32))
    ws_masked = w_s * tril[None, None]
    ws_b = ws_masked.astype(BF16)
    wsT_b = jnp.swapaxes(ws_masked, 2, 3).astype(BF16)
    bsb = jnp.repeat(jnp.swapaxes(b_s, 1, 2), CHUNK, axis=2)
    small = (norm_g[:, None, :], ln_v_g[:, None, :], ln_v_b[:, None, :], ws_b, bsb, cw_full, ple_norm_g[:, None, :])

    saved0, wts1 = fwd_layer(0, x[0], p, wts0, small, job=gather_job([(s, 1) for s in shards]))
    saved1, _ = fwd_layer(1, saved0[4], p, wts1, small, head=(loss_target[0], final_g[None, :]))
    saved, xs = [saved0[:9], saved1[:9]], [x[0], saved0[4]]
    dx, loss_part, g_final = saved1[4], saved1[9], saved1[10]
    loss = lax.psum(loss_part[0, 0], ("x", "y", "c"))

    def chip_sums(arrs, tag):
        recvd = run_job(to_sibling_job(arrs), f"rs_to_sibling{tag}")
        return rs_add_sibling(ids, arrs, recvd, f"rs_add_sibling{tag}")

    small_g = [None] * DEPTH
    proj, x2, gate, pp, _, hnT, catT, hn2T, pT = saved[1]
    dx, dproj_b, dpp_b, dgl_b, dx2_b, *g_sm = bwd_layer(1, dx, x2, gate, pp, xs[1], proj, wts1, small, wsT_b, tril)
    small_g[1] = g_sm[:5] + [g_sm[6], g_sm[5]]
    (g1_in,), _ = wgrad([(hnT, dproj_b)], 2, True, 1024, "wgrad_in1")
    (g1_out, g1_gate), _ = wgrad([(catT, dx2_b), (hn2T, dgl_b)], 1, False, 1024, "wgrad_outgate1")
    (g1_pp,), _ = wgrad([(pT, dpp_b)], 1, True, 2048, "wgrad_pp1")
    part1 = chip_sums([g1_in, g1_out, g1_gate, g1_pp], "1")
    proj, x2, gate, pp, _, hnT, catT, hn2T, pT = saved[0]
    dx, dproj_b, dpp_b, dgl_b, dx2_b, *g_sm = bwd_layer(0, dx, x2, gate, pp, xs[0], proj, wts0, small, wsT_b, tril)
    small_g[0] = g_sm[:5] + [g_sm[6], g_sm[5]]
    small_like = [norm_g, ln_v_g, ln_v_b, w_s, b_s, ple_norm_g]

    def stack(j):
        return jnp.stack([small_g[layer][j].reshape(small_like[j].shape[1:]) for layer in range(DEPTH)])

    g_small = _pack([stack(j) for j in range(6)] + [g_final.reshape(D_MODEL), jnp.stack([small_g[l][6] for l in range(DEPTH)])])
    (g0_in,), landed = wgrad([(hnT, dproj_b)], 2, True, 1024, "wgrad_in0",
                             job=merge_jobs(to_owners_job(part1[0]), small_to_sibling_job(g_small)))
    from1, small_sib = landed[:4], landed[4]
    full = rs_add_owners(1, ids, part1[1], from1, None, "rs_add_owners1")
    full = run_job(exchange_halves_job(1, full), "rs_exchange_halves1")
    small_chip = small_chip_sum(g_small, small_sib)
    part0b = chip_sums([g0_in], "0b")
    (g0_out, g0_gate), landed = wgrad([(catT, dx2_b), (hn2T, dgl_b)], 1, False, 1024, "wgrad_outgate0",
                                      job=merge_jobs(to_owners_job(part0b[0]), small_to_chips_job(small_chip)))
    from0b, small_chips = landed[:1], landed[1]
    full_b = rs_add_owners(0, ids, part0b[1], from0b, full[:1], "rs_add_owners0b")
    (g0_pp,), _ = wgrad([(pT, dpp_b)], 1, True, 2048, "wgrad_pp0")
    part0a = chip_sums([g0_out, g0_gate, g0_pp], "0a")
    from0a = run_job(to_owners_job(part0a[0]), "rs_to_owners0a")
    full_a = rs_add_owners(0, ids, part0a[1], from0a, full[1:], "rs_add_owners0a")
    gw_in, gw_out, gw_gate, gw_pp = run_job(exchange_halves_job(0, list(full_b) + list(full_a)), "rs_exchange_halves0")
    ((d_in, d_out, d_gate, d_pp), (m_in, m_out, m_gate, m_pp), (v_in, v_out, v_gate, v_pp)), _ = adamw_big(
        [w_in, w_out, w_ple_gate, w_ple_proj], [gw_in, gw_out, gw_gate, gw_pp],
        [m_w_in, m_w_out, m_w_ple_gate, m_w_ple_proj], [v_w_in, v_w_out, v_w_ple_gate, v_w_ple_proj], "adamw_big")

    zeros_cw = jnp.zeros((DEPTH, 3, WIDTH), F32)
    params = [norm_g, ln_v_g, ln_v_b, w_s, b_s, ple_norm_g, final_g, zeros_cw]
    ms = [m_norm_g, m_ln_v_g, m_ln_v_b, m_w_s, m_b_s, m_ple_norm_g, m_final_g, zeros_cw]
    vs = [v_norm_g, v_ln_v_g, v_ln_v_b, v_w_s, v_b_s, v_ple_norm_g, v_final_g, zeros_cw + 1.0]
    gsum, dsm, msm, vsm = small_finish(small_chip, small_chips, _pack(params), _pack(ms), _pack(vs))
    gs = _unpack(gsum, params)
    ds = _unpack(dsm, params)
    m2s = _unpack(msm, params)
    v2s = _unpack(vsm, params)
    g_cw_own = lax.dynamic_slice_in_dim(gs[7], own * (WIDTH // N_CHIPS), WIDTH // N_CHIPS, axis=2)
    rows2 = lambda a: a.reshape(DEPTH * 3, WIDTH // N_CHIPS)
    d_cw, m_cw, v_cw = adamw_rows(rows2(cw_rows), rows2(g_cw_own), rows2(jnp.transpose(m_conv_w, (0, 2, 1))),
                                  rows2(jnp.transpose(v_conv_w, (0, 2, 1))))
    back = lambda a: jnp.transpose(a.reshape(DEPTH, 3, WIDTH // N_CHIPS), (0, 2, 1))
    g_conv = jnp.transpose(g_cw_own, (0, 2, 1))

    def ordered(sm, cw_v, w_in_v, w_out_v, gate_v, pp_v):
        return [sm[0], w_in_v, sm[1], sm[2], sm[3], sm[4], cw_v, w_out_v, sm[5], gate_v, pp_v, sm[6]]

    grads = ordered(gs, g_conv, gw_in, gw_out, gw_gate, gw_pp)
    deltas = ordered(ds, back(d_cw), d_in, d_out, d_gate, d_pp)
    new_m = ordered(m2s, back(m_cw), m_in, m_out, m_gate, m_pp)
    new_v = ordered(v2s, back(v_cw), v_in, v_out, v_gate, v_pp)
    return (loss, dx[None], *grads, *deltas, *new_m, *new_v)
```

```python
import jax
import jax.numpy as jnp
from jax import lax
from jax.experimental import pallas as pl
from jax.experimental.pallas import tpu as pltpu

F32 = jnp.float32
BF16 = jnp.bfloat16

SEQ = 8192
D_MODEL = 1024
WIDTH = 512
PROJ = 7 * WIDTH
N_CHIPS = 4
COL_BLK = PROJ // N_CHIPS
PLE = 256
HEADS = 4
CHUNK = 128
DEPTH = 2
EPS = 1e-6
TILE = 256
N_TILES = SEQ // TILE
HALO = 8
VMEM_LIMIT = 60 * 1024 * 1024

ADAM_LR, ADAM_B1, ADAM_B2, ADAM_EPS, ADAM_WD, ADAM_STEP = 0.001, 0.9, 0.999, 1e-08, 0.01, 10
ADAM_C1 = 1.0 - ADAM_B1**ADAM_STEP
ADAM_C2 = 1.0 - ADAM_B2**ADAM_STEP

MESH = pl.DeviceIdType.MESH
ANY = pl.BlockSpec(memory_space=pl.ANY)


def _mm(a, b):
    return lax.dot_general(a, b, (((1,), (0,)), ((), ())), preferred_element_type=F32)


def _mm_nt(a, b):
    return lax.dot_general(a, b, (((1,), (1,)), ((), ())), preferred_element_type=F32)


def _mm_rows(a, w_ref):
    blk = w_ref.shape[1]
    acc = _mm(a[:, 0:blk], w_ref[0])
    for k in range(1, N_CHIPS):
        acc = acc + _mm(a[:, k * blk : (k + 1) * blk], w_ref[k])
    return acc


def _ple_proj(p_tile, wpp_ref):
    pb = p_tile.astype(BF16)
    return jnp.concatenate([_mm(pb, wpp_ref[k]) for k in range(N_CHIPS)], axis=-1)


def _mm_nt_rows(a, w_ref):
    return jnp.concatenate([_mm_nt(a, w_ref[k]) for k in range(N_CHIPS)], axis=-1)


def _load_side_by_side(w_hbm, w_vmem, sems):
    copies = [pltpu.make_async_copy(w_hbm.at[k], w_vmem.at[:, pl.ds(k * COL_BLK, COL_BLK)], sems.at[k])
              for k in range(N_CHIPS)]
    for cp in copies:
        cp.start()
    for cp in copies:
        cp.wait()


def _sigmoid(z):
    return 1.0 / (1.0 + jnp.exp(-z))


def _rms_stats(x):
    r = lax.rsqrt(jnp.mean(x * x, axis=-1, keepdims=True) + EPS)
    return r, x * r


def _rms_bwd(dyg, xh, r):
    return r * (dyg - xh * jnp.mean(dyg * xh, axis=-1, keepdims=True))


def _ln_stats(v):
    mu = jnp.mean(v, axis=-1, keepdims=True)
    vc = v - mu
    rs = lax.rsqrt(jnp.mean(vc * vc, axis=-1, keepdims=True) + EPS)
    return rs, vc * rs


def _mesh_pos():
    x, y, c = lax.axis_index("x"), lax.axis_index("y"), lax.axis_index("c")
    return x, y, c, 2 * x + y


def _peer(x, y, c, r):
    return ((1 - x) if (r >> 1) else x, (1 - y) if (r & 1) else y, c)


def _full(shape):
    return pl.BlockSpec(shape, lambda *_: (0,) * len(shape))


def _const(shape, pos):
    return pl.BlockSpec((None,) * len(pos) + tuple(shape), lambda *_: tuple(pos) + (0,) * len(shape))


class Job:
    def __init__(self, ins, out_shapes, sems, start, middle, finish, aliases=None):
        self.ins, self.out_shapes, self.sems = list(ins), list(out_shapes), list(sems)
        self.start, self.middle, self.finish = start, middle, finish
        self.aliases = aliases or {}


def run_job(job, name):
    ni, no = len(job.ins), len(job.out_shapes)

    def body(*refs):
        parts = (refs[:ni], refs[ni : ni + no], refs[ni + no :])
        job.start(*parts)
        if job.middle is not None:
            job.middle(*parts)
        job.finish(*parts)

    return pl.pallas_call(
        body, name=name, out_shape=job.out_shapes, in_specs=[ANY] * ni, out_specs=[ANY] * no, scratch_shapes=job.sems,
        input_output_aliases=job.aliases, compiler_params=pltpu.CompilerParams(has_side_effects=True),
    )(*job.ins)


def _call(body, *, name, grid, in_specs, out_specs, out_shape, args, scratch=(), job=None, mid=None):
    params = pltpu.CompilerParams(dimension_semantics=("arbitrary",) * len(grid), vmem_limit_bytes=VMEM_LIMIT,
                                  has_side_effects=job is not None)
    n_in, n_out, n_sc = len(in_specs), len(out_specs), len(scratch)
    if job is None:
        outs = pl.pallas_call(body, name=name, grid=grid, in_specs=in_specs, out_specs=out_specs, out_shape=out_shape,
                              scratch_shapes=list(scratch), compiler_params=params)(*args)
        return list(outs), []
    ji, jo = len(job.ins), len(job.out_shapes)
    assert not job.aliases and (job.middle is None or len(grid) == 1)

    def wrapped(*refs):
        ins, jin = refs[:n_in], refs[n_in : n_in + ji]
        o0 = n_in + ji
        outs, jout = refs[o0 : o0 + n_out], refs[o0 + n_out : o0 + n_out + jo]
        s0 = o0 + n_out + jo
        sc, jsem = refs[s0 : s0 + n_sc], refs[s0 + n_sc :]
        step = pl.program_id(0)
        for d in range(1, len(grid)):
            step = step * grid[d] + pl.program_id(d)
        n_steps = 1
        for g in grid:
            n_steps *= g

        @pl.when(step == 0)
        def _():
            job.start(jin, jout, jsem)

        if job.middle is not None:
            @pl.when(step == mid)
            def _():
                job.middle(jin, jout, jsem)

        body(*ins, *outs, *sc)

        @pl.when(step == n_steps - 1)
        def _():
            job.finish(jin, jout, jsem)

    outs = pl.pallas_call(
        wrapped, name=name, grid=grid, in_specs=list(in_specs) + [ANY] * ji, out_specs=list(out_specs) + [ANY] * jo,
        out_shape=list(out_shape) + job.out_shapes, scratch_shapes=list(scratch) + job.sems, compiler_params=params,
    )(*args, *job.ins)
    return list(outs[:n_out]), list(outs[n_out:])


def gather_job(items):
    n = len(items)

    def src_half(a, ref, h):
        arr, layer = items[a]
        if layer is None:
            return ref.at[h]
        rh = arr.shape[1] // 2
        return ref.at[layer, pl.ds(h * rh, rh)]

    def dst_half(a, ref, chip, h):
        arr, layer = items[a]
        if layer is None:
            return ref.at[chip, h]
        rh = arr.shape[1] // 2
        return ref.at[chip, pl.ds(h * rh, rh)]

    def copies(src, dst, scratch):
        stage, (in_sem, out_sem, ici_s, ici_r, fwd_s, fwd_r) = scratch[:n], scratch[n:]
        x, y, c, own = _mesh_pos()
        local, ici, fwd, got = [], {}, {}, {}
        for a in range(n):
            layer = items[a][1]
            local.append((pltpu.make_async_copy(src[a] if layer is None else src[a].at[layer], stage[a], in_sem.at[a]),
                          pltpu.make_async_copy(stage[a], dst[a].at[own], out_sem.at[a])))
            for r in (1, 2, 3):
                k = 3 * a + r - 1
                ici[a, r] = pltpu.make_async_remote_copy(
                    src_ref=src_half(a, src[a], c), dst_ref=dst_half(a, dst[a], own, c), send_sem=ici_s.at[k],
                    recv_sem=ici_r.at[k], device_id=_peer(x, y, c, r), device_id_type=MESH)
                fwd[a, r] = pltpu.make_async_remote_copy(
                    src_ref=dst_half(a, dst[a], own ^ r, c), dst_ref=dst_half(a, dst[a], own ^ r, c), send_sem=fwd_s.at[k],
                    recv_sem=fwd_r.at[k], device_id=(x, y, 1 - c), device_id_type=MESH)
                got[a, r] = pltpu.make_async_remote_copy(
                    src_ref=dst_half(a, dst[a], own ^ r, 1 - c), dst_ref=dst_half(a, dst[a], own ^ r, 1 - c),
                    send_sem=fwd_s.at[k], recv_sem=fwd_r.at[k], device_id=(x, y, 1 - c), device_id_type=MESH)
        return local, ici, fwd, got

    def start(src, dst, sems):
        local, ici, _, _ = copies(src, dst, sems)
        for a in range(n):
            for r in (1, 2, 3):
                ici[a, r].start()
        for cp_in, _ in local:
            cp_in.start()
        for cp_in, cp_out in local:
            cp_in.wait()
            cp_out.start()

    def middle(src, dst, sems):
        _, ici, fwd, _ = copies(src, dst, sems)
        for r in (1, 2, 3):
            for a in range(n):
                ici[a, r].wait_recv()
                fwd[a, r].start()

    def finish(src, dst, sems):
        local, ici, fwd, got = copies(src, dst, sems)
        for a in range(n):
            for r in (1, 2, 3):
                got[a, r].wait_recv()
        for a in range(n):
            for r in (1, 2, 3):
                ici[a, r].wait_send()
                fwd[a, r].wait_send()
        for _, cp_out in local:
            cp_out.wait()

    out_shapes = [jax.ShapeDtypeStruct((N_CHIPS,) + (arr.shape if layer is None else arr.shape[1:]), arr.dtype)
                  for arr, layer in items]
    stage = [pltpu.VMEM(arr.shape if layer is None else arr.shape[1:], arr.dtype) for arr, layer in items]
    sems = [pltpu.SemaphoreType.DMA((n,))] * 2 + [pltpu.SemaphoreType.DMA((3 * n,))] * 4
    return Job([arr for arr, _ in items], out_shapes, stage + sems, start, middle, finish)


def _mixer_fwd(proj_ref, lg, lb, ws_ref, bsb_ref, cw_ref, mix_ref, xcbuf, halo_xc):
    u = proj_ref[:, 0 * WIDTH : 1 * WIDTH]
    v = proj_ref[:, 1 * WIDTH : 2 * WIDTH]
    za = proj_ref[:, 2 * WIDTH : 3 * WIDTH]
    h = proj_ref[:, 3 * WIDTH : 4 * WIDTH]
    gb = proj_ref[:, 4 * WIDTH : 5 * WIDTH]
    gc = proj_ref[:, 5 * WIDTH : 6 * WIDTH]
    zb = proj_ref[:, 6 * WIDTH : 7 * WIDTH]
    rs, vhat = _ln_stats(v)
    vl = vhat * lg + lb
    vlb = vl.astype(BF16)
    for j in range(TILE // CHUNK):
        rows = slice(j * CHUNK, (j + 1) * CHUNK)
        for hd in range(HEADS):
            cols = slice(hd * CHUNK, (hd + 1) * CHUNK)
            mix_ref[rows, cols] = _mm(ws_ref[hd], vlb[rows, cols]) + bsb_ref[:, cols]
    mixed = mix_ref[...]
    siga = _sigmoid(za)
    sigb = _sigmoid(zb)
    xc = gc * h
    xcbuf[0:HALO, :] = halo_xc
    xcbuf[HALO : HALO + TILE, :] = xc
    y = cw_ref[0:1, :] * xcbuf[HALO - 2 : HALO - 2 + TILE, :] + cw_ref[1:2, :] * xcbuf[HALO - 1 : HALO - 1 + TILE, :]
    y = y + cw_ref[2:3, :] * xc
    return dict(u=u, za=za, h=h, gb=gb, gc=gc, zb=zb, rs=rs, vhat=vhat, vlb=vlb, mixed=mixed, siga=siga, sigb=sigb,
                xc=xc, y=y)


def fwd_layer(layer, x, p, wts, small, job=None, head=None):
    win, wout, wg, wpp = wts
    ng, lg, lb, ws, bsb, cw, pg = small
    n_head = 0 if head is None else 2

    def body(*refs):
        (x_ref, p_ref, win_ref, wout_ref, wg_ref, wpp_ref, ng_ref, lg_ref, lb_ref, ws_ref, bsb_ref, cw_ref,
         pg_ref) = refs[:13]
        head_in = refs[13 : 13 + n_head]
        proj_ref, x2_ref, gate_ref, x3_ref, hnT_ref, catT_ref, hn2T_ref, pT_ref = refs[13 + n_head : 21 + n_head]
        head_out = refs[21 + n_head : 21 + 2 * n_head]
        mix_ref, xcbuf, carry, wcat, wsem = refs[21 + 2 * n_head :]
        i = pl.program_id(0)

        @pl.when(i == 0)
        def _():
            _load_side_by_side(win_ref, wcat, wsem)
            carry[...] = jnp.zeros_like(carry)
            for ref in head_out:
                ref[...] = jnp.zeros_like(ref)

        xv = x_ref[...]
        _, xh = _rms_stats(xv)
        hn = xh * ng_ref[...]
        hnT_ref[...] = hn.T.astype(BF16)
        proj_ref[...] = _mm(hn.astype(BF16), wcat[...])
        m = _mixer_fwd(proj_ref, lg_ref[...], lb_ref[...], ws_ref, bsb_ref, cw_ref, mix_ref, xcbuf, carry[...])
        carry[...] = m["xc"][TILE - HALO : TILE, :]
        out_a = (m["u"] * m["mixed"]) * (m["za"] * m["siga"])
        out_b = (m["gb"] * m["y"]) * (m["zb"] * m["sigb"])
        cat = jnp.concatenate([out_a, out_b], axis=-1)
        catT_ref[...] = cat.T.astype(BF16)
        x2 = xv + _mm_rows(cat.astype(BF16), wout_ref)
        x2_ref[...] = x2
        _, xh2 = _rms_stats(x2)
        hn2 = xh2 * pg_ref[...]
        hn2T_ref[...] = hn2.T.astype(BF16)
        gate = _sigmoid(_mm_rows(hn2.astype(BF16), wg_ref))
        gate_ref[...] = gate
        pv = p_ref[...]
        pT_ref[...] = pv.T.astype(BF16)
        x3 = x2 + gate * _ple_proj(pv, wpp_ref)
        if head is None:
            x3_ref[...] = x3
        else:
            t_ref, gf_ref = head_in
            loss_ref, gg_ref = head_out
            r3, xh3 = _rms_stats(x3)
            gf = gf_ref[...]
            err = xh3 * gf - t_ref[...]
            loss_ref[...] += (0.5 / D_MODEL) * jnp.sum(err * err).reshape(1, 1)
            dy = err * (1.0 / D_MODEL)
            gg_ref[...] += jnp.sum(dy * xh3, axis=0, keepdims=True)
            x3_ref[...] = _rms_bwd(dy * gf, xh3, r3)

    tok = lambda w: pl.BlockSpec((TILE, w), lambda i: (i, 0))
    tokT = lambda w: pl.BlockSpec((None, w, TILE), lambda i: (i, 0, 0))
    once = lambda a: pl.BlockSpec(a.shape, lambda i: (0,) * a.ndim, pipeline_mode=pl.Buffered(1))
    f32 = lambda w: jax.ShapeDtypeStruct((SEQ, w), F32)
    bfT = lambda w: jax.ShapeDtypeStruct((N_TILES, w, TILE), BF16)
    head_specs = [] if head is None else [tok(D_MODEL), _full((1, D_MODEL))]
    head_outs = [] if head is None else [_full((1, 1)), _full((1, D_MODEL))]
    head_shapes = [] if head is None else [jax.ShapeDtypeStruct((1, 1), F32), jax.ShapeDtypeStruct((1, D_MODEL), F32)]
    return _call(
        body, name=f"fwd_layer{layer}", grid=(N_TILES,),
        in_specs=[tok(D_MODEL), pl.BlockSpec((None, None, TILE, PLE), lambda i: (layer, 0, i, 0)),
                  ANY, once(wout), once(wg), once(wpp),
                  _const((1, D_MODEL), (layer,)), _const((1, WIDTH), (layer,)), _const((1, WIDTH), (layer,)),
                  _const((HEADS, CHUNK, CHUNK), (layer,)), _const((CHUNK, WIDTH), (layer,)), _const((3, WIDTH), (layer,)),
                  _const((1, D_MODEL), (layer,))] + head_specs,
        out_specs=[tok(PROJ), tok(D_MODEL), tok(D_MODEL), tok(D_MODEL),
                   tokT(D_MODEL), tokT(D_MODEL), tokT(D_MODEL), tokT(PLE)] + head_outs,
        out_shape=[f32(PROJ), f32(D_MODEL), f32(D_MODEL), f32(D_MODEL),
                   bfT(D_MODEL), bfT(D_MODEL), bfT(D_MODEL), bfT(PLE)] + head_shapes,
        scratch=[pltpu.VMEM((TILE, WIDTH), F32), pltpu.VMEM((HALO + TILE, WIDTH), F32), pltpu.VMEM((HALO, WIDTH), F32),
                 pltpu.VMEM((D_MODEL, PROJ), BF16), pltpu.SemaphoreType.DMA((N_CHIPS,))],
        args=(x, p, win, wout, wg, wpp, ng, lg, lb, ws, bsb, cw, pg) + (() if head is None else tuple(head)),
        job=job, mid=N_TILES // 2)


def bwd_layer(layer, dx3, x2, gate, p, x, proj, wts, small, wsT, tril):
    win, wout, wg, wpp = wts
    ng, lg, lb, ws, bsb, cw, pg = small

    def body(dx3_ref, x2_ref, gate_ref, p_ref, x_ref, proj_ref, halo_ref, win_ref, wout_ref, wg_ref, wpp_ref, ng_ref, lg_ref,
             lb_ref, ws_ref, wsT_ref, bsb_ref, cw_ref, pg_ref, tril_ref,
             dx_ref, dproj_ref, dpp_ref, dgl_ref, dx2b_ref, gn_ref, glg_ref, glb_ref, gws_ref, gbs_ref, gcw_ref, gpg_ref,
             mix_ref, xcbuf, dycbuf, dvl_ref, bs_acc, wcat, wsem):
        i = pl.program_id(0)

        @pl.when(i == 0)
        def _():
            _load_side_by_side(win_ref, wcat, wsem)
            for ref in (gn_ref, glg_ref, glb_ref, gws_ref, gcw_ref, gpg_ref, bs_acc):
                ref[...] = jnp.zeros_like(ref)
            dycbuf[TILE : TILE + HALO, :] = jnp.zeros((HALO, WIDTH), F32)

        dx3v = dx3_ref[...]
        gate_v = gate_ref[...]
        dpp_ref[...] = (dx3v * gate_v).astype(BF16)
        dgl = ((dx3v * _ple_proj(p_ref[...], wpp_ref)) * gate_v * (1.0 - gate_v)).astype(BF16)
        dgl_ref[...] = dgl
        dhn2 = _mm_nt_rows(dgl, wg_ref)
        r2, xh2 = _rms_stats(x2_ref[...])
        gpg_ref[...] += jnp.sum(dhn2 * xh2, axis=0, keepdims=True)
        dx2 = dx3v + _rms_bwd(dhn2 * pg_ref[...], xh2, r2)
        dx2b = dx2.astype(BF16)
        dx2b_ref[...] = dx2b
        dcat = _mm_nt_rows(dx2b, wout_ref)
        lgv = lg_ref[...]
        halo_xc = halo_ref[:, 5 * WIDTH : 6 * WIDTH] * halo_ref[:, 3 * WIDTH : 4 * WIDTH]
        halo_xc = jnp.where(i == N_TILES - 1, 0.0, halo_xc)
        m = _mixer_fwd(proj_ref, lgv, lb_ref[...], ws_ref, bsb_ref, cw_ref, mix_ref, xcbuf, halo_xc)
        u, za, h, gb, gc, zb = m["u"], m["za"], m["h"], m["gb"], m["gc"], m["zb"]
        mixed, siga, sigb, xc, y = m["mixed"], m["siga"], m["sigb"], m["xc"], m["y"]
        doa = dcat[:, 0:WIDTH]
        dob = dcat[:, WIDTH : 2 * WIDTH]
        sa = za * siga
        sb = zb * sigb
        doa_sa = doa * sa
        dproj_ref[:, 0 * WIDTH : 1 * WIDTH] = (doa_sa * mixed).astype(BF16)
        dmixed = doa_sa * u
        dza = (doa * (u * mixed)) * (siga * (1.0 + za * (1.0 - siga)))
        dproj_ref[:, 2 * WIDTH : 3 * WIDTH] = dza.astype(BF16)
        dob_sb = dob * sb
        dproj_ref[:, 4 * WIDTH : 5 * WIDTH] = (dob_sb * y).astype(BF16)
        dyc = dob_sb * gb
        dzb = (dob * (gb * y)) * (sigb * (1.0 + zb * (1.0 - sigb)))
        dproj_ref[:, 6 * WIDTH : 7 * WIDTH] = dzb.astype(BF16)
        dycbuf[0:TILE, :] = dyc
        dyc1 = dycbuf[1 : 1 + TILE, :]
        dyc2 = dycbuf[2 : 2 + TILE, :]
        dxc = cw_ref[2:3, :] * dyc + cw_ref[1:2, :] * dyc1 + cw_ref[0:1, :] * dyc2
        gcw_ref[0:1, :] += jnp.sum(xc * dyc2, axis=0, keepdims=True)
        gcw_ref[1:2, :] += jnp.sum(xc * dyc1, axis=0, keepdims=True)
        gcw_ref[2:3, :] += jnp.sum(xc * dyc, axis=0, keepdims=True)
        dycbuf[TILE : TILE + HALO, :] = dyc[0:HALO, :]
        dproj_ref[:, 5 * WIDTH : 6 * WIDTH] = (dxc * h).astype(BF16)
        dproj_ref[:, 3 * WIDTH : 4 * WIDTH] = (dxc * gc).astype(BF16)
        dmb = dmixed.astype(BF16)
        vlb = m["vlb"]
        bsum = jnp.zeros((CHUNK, WIDTH), F32)
        for j in range(TILE // CHUNK):
            rows = slice(j * CHUNK, (j + 1) * CHUNK)
            bsum = bsum + dmixed[rows, :]
            for hd in range(HEADS):
                cols = slice(hd * CHUNK, (hd + 1) * CHUNK)
                gws_ref[hd] += _mm_nt(dmb[rows, cols], vlb[rows, cols])
                dvl_ref[rows, cols] = _mm(wsT_ref[hd], dmb[rows, cols])
        bs_acc[...] += bsum
        dvl = dvl_ref[...]
        vhat = m["vhat"]
        glb_ref[...] += jnp.sum(dvl, axis=0, keepdims=True)
        glg_ref[...] += jnp.sum(dvl * vhat, axis=0, keepdims=True)
        dvh = dvl * lgv
        dv = m["rs"] * (dvh - jnp.mean(dvh, axis=-1, keepdims=True) - vhat * jnp.mean(dvh * vhat, axis=-1, keepdims=True))
        dproj_ref[:, 1 * WIDTH : 2 * WIDTH] = dv.astype(BF16)
        dhn = _mm_nt(dproj_ref[...], wcat[...])
        r1, xh = _rms_stats(x_ref[...])
        gn_ref[...] += jnp.sum(dhn * xh, axis=0, keepdims=True)
        dx_ref[...] = dx2 + _rms_bwd(dhn * ng_ref[...], xh, r1)

        @pl.when(i == N_TILES - 1)
        def _():
            for hd in range(HEADS):
                gws_ref[hd] = gws_ref[hd] * tril_ref[...]
                gbs_ref[hd : hd + 1, :] = jnp.sum(bs_acc[:, hd * CHUNK : (hd + 1) * CHUNK].T, axis=0, keepdims=True)

    rev = lambda w: pl.BlockSpec((TILE, w), lambda i: (N_TILES - 1 - i, 0))
    halo = pl.BlockSpec((HALO, PROJ), lambda i: (jnp.maximum((N_TILES - 1 - i) * (TILE // HALO) - 1, 0), 0))
    once = lambda a: pl.BlockSpec(a.shape, lambda i: (0,) * a.ndim, pipeline_mode=pl.Buffered(1))
    vec = lambda w: jax.ShapeDtypeStruct((1, w), F32)
    b16 = lambda w: jax.ShapeDtypeStruct((SEQ, w), BF16)
    outs, _ = _call(
        body, name=f"bwd_layer{layer}", grid=(N_TILES,),
        in_specs=[rev(D_MODEL), rev(D_MODEL), rev(D_MODEL),
                  pl.BlockSpec((None, None, TILE, PLE), lambda i: (layer, 0, N_TILES - 1 - i, 0)),
                  rev(D_MODEL), rev(PROJ), halo, ANY, once(wout), once(wg), once(wpp),
                  _const((1, D_MODEL), (layer,)), _const((1, WIDTH), (layer,)), _const((1, WIDTH), (layer,)),
                  _const((HEADS, CHUNK, CHUNK), (layer,)), _const((HEADS, CHUNK, CHUNK), (layer,)),
                  _const((CHUNK, WIDTH), (layer,)), _const((3, WIDTH), (layer,)), _const((1, D_MODEL), (layer,)),
                  _full((CHUNK, CHUNK))],
        out_specs=[rev(D_MODEL), rev(PROJ), rev(D_MODEL), rev(D_MODEL), rev(D_MODEL),
                   _full((1, D_MODEL)), _full((1, WIDTH)), _full((1, WIDTH)), _full((HEADS, CHUNK, CHUNK)),
                   _full((HEADS, CHUNK)), _full((3, WIDTH)), _full((1, D_MODEL))],
        out_shape=[jax.ShapeDtypeStruct((SEQ, D_MODEL), F32), b16(PROJ), b16(D_MODEL), b16(D_MODEL), b16(D_MODEL),
                   vec(D_MODEL), vec(WIDTH), vec(WIDTH), jax.ShapeDtypeStruct((HEADS, CHUNK, CHUNK), F32),
                   jax.ShapeDtypeStruct((HEADS, CHUNK), F32), jax.ShapeDtypeStruct((3, WIDTH), F32), vec(D_MODEL)],
        scratch=[pltpu.VMEM((TILE, WIDTH), F32), pltpu.VMEM((HALO + TILE, WIDTH), F32),
                 pltpu.VMEM((TILE + HALO, WIDTH), F32), pltpu.VMEM((TILE, WIDTH), F32), pltpu.VMEM((CHUNK, WIDTH), F32),
                 pltpu.VMEM((D_MODEL, PROJ), BF16), pltpu.SemaphoreType.DMA((N_CHIPS,))],
        args=(dx3, x2, gate, p, x, proj, proj, win, wout, wg, wpp, ng, lg, lb, ws, wsT, bsb, cw, pg, tril))
    return outs


def wgrad(pairs, n_tiles, col_blocked, tk, name, job=None):
    n = len(pairs)
    m_dim, n_dim = pairs[0][0].shape[1], pairs[0][1].shape[1]
    tn = n_dim // n_tiles
    cb, mb = n_dim // N_CHIPS, m_dim // N_CHIPS
    per_tile = N_CHIPS // n_tiles
    assert col_blocked or n_tiles == 1

    def body(*refs):
        for a in range(n):
            a_ref, b_ref, o_ref = refs[2 * a], refs[2 * a + 1], refs[2 * n + a]

            @pl.when(pl.program_id(1) == 0)
            def _():
                o_ref[...] = jnp.zeros_like(o_ref)

            prod = _mm(a_ref[0], b_ref[0:TILE, :])
            for t in range(1, tk // TILE):
                prod = prod + _mm(a_ref[t], b_ref[t * TILE : (t + 1) * TILE, :])
            if col_blocked:
                for q in range(per_tile):
                    o_ref[q] += prod[:, q * cb : (q + 1) * cb]
            else:
                for q in range(N_CHIPS):
                    o_ref[q] += prod[q * mb : (q + 1) * mb, :]

    if col_blocked:
        shape = (N_CHIPS, m_dim, cb)
        o_spec = pl.BlockSpec((per_tile, m_dim, cb), lambda j, k: (j, 0, 0))
    else:
        shape = (N_CHIPS, mb, n_dim)
        o_spec = pl.BlockSpec((N_CHIPS, mb, n_dim), lambda j, k: (0, 0, 0))
    return _call(
        body, name=name, grid=(n_tiles, SEQ // tk),
        in_specs=[pl.BlockSpec((tk // TILE, m_dim, TILE), lambda j, k: (k, 0, 0)),
                  pl.BlockSpec((tk, tn), lambda j, k: (k, j))] * n,
        out_specs=[o_spec] * n, out_shape=[jax.ShapeDtypeStruct(shape, F32)] * n,
        args=[t for pair in pairs for t in pair], job=job)


def to_sibling_job(grads):
    n = len(grads)

    def copies(src, dst, sems):
        x, y, c, _ = _mesh_pos()
        out = []
        for a in range(n):
            rh = grads[a].shape[1] // 2
            out.append(pltpu.make_async_remote_copy(
                src_ref=src[a].at[:, pl.ds((1 - c) * rh, rh), :], dst_ref=dst[a], send_sem=sems[0].at[a],
                recv_sem=sems[1].at[a], device_id=(x, y, 1 - c), device_id_type=MESH))
        return out

    def start(src, dst, sems):
        for cp in copies(src, dst, sems):
            cp.start()

    def finish(src, dst, sems):
        for cp in copies(src, dst, sems):
            cp.wait()

    out_shapes = [jax.ShapeDtypeStruct((N_CHIPS, g.shape[1] // 2, g.shape[2]), F32) for g in grads]
    return Job(grads, out_shapes, [pltpu.SemaphoreType.DMA((n,))] * 2, start, None, finish)


def rs_add_sibling(ids, grads, recvd, name):
    n = len(grads)

    def body(ids_ref, *refs):
        g, r = refs[:n], refs[n : 2 * n]
        pb, pf = refs[2 * n : 3 * n], refs[3 * n :]
        k = pl.program_id(0)
        for a in range(n):
            s = g[a][...] + r[a][...]
            pb[a][...] = s.astype(BF16)

            @pl.when(k == ids_ref[1])
            def _():
                pf[a][...] = s

    in_specs, out_specs, out_shape = [], [], []
    for g in grads:
        rh, cc = g.shape[1] // 2, g.shape[2]
        in_specs.append(pl.BlockSpec((None, rh, cc), lambda k, ids: (k, ids[0], 0)))
    for g in grads:
        rh, cc = g.shape[1] // 2, g.shape[2]
        in_specs.append(pl.BlockSpec((None, rh, cc), lambda k, ids: (k, 0, 0)))
        out_specs.append(pl.BlockSpec((None, rh, cc), lambda k, ids: (k, 0, 0)))
        out_shape.append(jax.ShapeDtypeStruct((N_CHIPS, rh, cc), BF16))
    for g in grads:
        rh, cc = g.shape[1] // 2, g.shape[2]
        out_specs.append(pl.BlockSpec((rh, cc), lambda k, ids: (0, 0)))
        out_shape.append(jax.ShapeDtypeStruct((rh, cc), F32))
    outs = pl.pallas_call(
        body, name=name, out_shape=out_shape,
        grid_spec=pltpu.PrefetchScalarGridSpec(num_scalar_prefetch=1, grid=(N_CHIPS,), in_specs=in_specs,
                                               out_specs=out_specs),
        compiler_params=pltpu.CompilerParams(dimension_semantics=("arbitrary",), vmem_limit_bytes=VMEM_LIMIT),
    )(ids, *grads, *recvd)
    return list(outs[:n]), list(outs[n:])


def to_owners_job(partials):
    n = len(partials)

    def copies(src, dst, sems):
        x, y, c, own = _mesh_pos()
        out = []
        for a in range(n):
            for r in (1, 2, 3):
                out.append(pltpu.make_async_remote_copy(
                    src_ref=src[a].at[own ^ r], dst_ref=dst[a].at[r - 1], send_sem=sems[0].at[3 * a + r - 1],
                    recv_sem=sems[1].at[3 * a + r - 1], device_id=_peer(x, y, c, r), device_id_type=MESH))
        return out

    def start(src, dst, sems):
        for cp in copies(src, dst, sems):
            cp.start()

    def finish(src, dst, sems):
        for cp in copies(src, dst, sems):
            cp.wait()

    out_shapes = [jax.ShapeDtypeStruct((3,) + p.shape[1:], BF16) for p in partials]
    return Job(partials, out_shapes, [pltpu.SemaphoreType.DMA((3 * n,))] * 2, start, None, finish)


def rs_add_owners(layer, ids, own_f32, recvd, prev, name):
    n = len(own_f32)
    nb = 2

    def body(ids_ref, *refs):
        o, r, f = refs[:n], refs[n : 2 * n], refs[-n:]
        for a in range(n):
            f[a][...] = ((o[a][...] + r[a][0].astype(F32)) + r[a][1].astype(F32)) + r[a][2].astype(F32)

    in_specs, out_specs, out_shape = [], [], []
    for o in own_f32:
        in_specs.append(pl.BlockSpec((o.shape[0] // nb, o.shape[1]), lambda j, ids: (j, 0)))
    for o in own_f32:
        in_specs.append(pl.BlockSpec((3, o.shape[0] // nb, o.shape[1]), lambda j, ids: (0, j, 0)))
        out_specs.append(pl.BlockSpec((None, o.shape[0] // nb, o.shape[1]), lambda j, ids: (layer, ids[0] * nb + j, 0)))
        out_shape.append(jax.ShapeDtypeStruct((DEPTH, 2 * o.shape[0], o.shape[1]), F32))
    args, aliases = [ids, *own_f32, *recvd], {}
    if prev is not None:
        in_specs += [ANY] * n
        args += list(prev)
        aliases = {1 + 2 * n + a: a for a in range(n)}
    return pl.pallas_call(
        body, name=name, out_shape=out_shape, input_output_aliases=aliases,
        grid_spec=pltpu.PrefetchScalarGridSpec(num_scalar_prefetch=1, grid=(nb,), in_specs=in_specs, out_specs=out_specs),
        compiler_params=pltpu.CompilerParams(dimension_semantics=("arbitrary",), vmem_limit_bytes=VMEM_LIMIT),
    )(*args)


def exchange_halves_job(layer, full):
    n = len(full)

    def copies(src, dst, sems):
        x, y, c, _ = _mesh_pos()
        out = []
        for a in range(n):
            rh = full[a].shape[1] // 2
            out.append(pltpu.make_async_remote_copy(
                src_ref=src[a].at[layer, pl.ds(c * rh, rh), :], dst_ref=dst[a].at[layer, pl.ds(c * rh, rh), :],
                send_sem=sems[0].at[a], recv_sem=sems[1].at[a], device_id=(x, y, 1 - c), device_id_type=MESH))
        return out

    def start(src, dst, sems):
        for cp in copies(src, dst, sems):
            cp.start()

    def finish(src, dst, sems):
        for cp in copies(src, dst, sems):
            cp.wait()

    out_shapes = [jax.ShapeDtypeStruct(f.shape, F32) for f in full]
    return Job(full, out_shapes, [pltpu.SemaphoreType.DMA((n,))] * 2, start, None, finish, {a: a for a in range(n)})


def _adamw(w, g, m, v):
    m2 = ADAM_B1 * m + (1.0 - ADAM_B1) * g
    v2 = ADAM_B2 * v + (1.0 - ADAM_B2) * (g * g)
    delta = -ADAM_LR * ((m2 / ADAM_C1) / (jnp.sqrt(v2 / ADAM_C2) + ADAM_EPS) + ADAM_WD * w)
    return delta, m2, v2


def adamw_big(ws, gs, ms, vs, name, job=None):
    n = len(ws)
    nb = 4

    def body(*refs):
        for a in range(n):
            w, g, m, v = (refs[j * n + a][...] for j in range(4))
            d, m2, v2 = _adamw(w, g, m, v)
            refs[4 * n + a][...] = d
            refs[5 * n + a][...] = m2
            refs[6 * n + a][...] = v2

    specs = [pl.BlockSpec((None, w.shape[1] // nb, w.shape[2]), lambda l, j: (l, j, 0)) for w in ws]
    shapes = [jax.ShapeDtypeStruct(w.shape, F32) for w in ws]
    outs, job_outs = _call(body, name=name, grid=(DEPTH, nb), in_specs=specs * 4, out_specs=specs * 3,
                           out_shape=shapes * 3, args=(*ws, *gs, *ms, *vs), job=job)
    return (outs[:n], outs[n : 2 * n], outs[2 * n :]), job_outs


def small_to_sibling_job(gs):
    n = len(gs)

    def copies(src, dst, sems):
        x, y, c, _ = _mesh_pos()
        return [pltpu.make_async_remote_copy(src_ref=src[a], dst_ref=dst[a], send_sem=sems[0].at[a], recv_sem=sems[1].at[a],
                                             device_id=(x, y, 1 - c), device_id_type=MESH) for a in range(n)]

    def start(*refs):
        for cp in copies(*refs):
            cp.start()

    def finish(*refs):
        for cp in copies(*refs):
            cp.wait()

    return Job(gs, [jax.ShapeDtypeStruct(g.shape, F32) for g in gs], [pltpu.SemaphoreType.DMA((n,))] * 2, start, None, finish)


def small_chip_sum(gs, recvd):
    n = len(gs)

    def body(*refs):
        for a in range(n):
            refs[2 * n + a][...] = refs[a][...] + refs[n + a][...]

    return pl.pallas_call(body, name="small_chip_sum", out_shape=[jax.ShapeDtypeStruct(g.shape, F32) for g in gs])(*gs, *recvd)


def small_to_chips_job(chip_sums):
    n = len(chip_sums)

    def copies(src, dst, sems):
        x, y, c, _ = _mesh_pos()
        return [pltpu.make_async_remote_copy(src_ref=src[a], dst_ref=dst[a].at[r - 1], send_sem=sems[0].at[3 * a + r - 1],
                                             recv_sem=sems[1].at[3 * a + r - 1], device_id=_peer(x, y, c, r),
                                             device_id_type=MESH) for a in range(n) for r in (1, 2, 3)]

    def start(*refs):
        for cp in copies(*refs):
            cp.start()

    def finish(*refs):
        for cp in copies(*refs):
            cp.wait()

    return Job(chip_sums, [jax.ShapeDtypeStruct((3,) + s.shape, F32) for s in chip_sums],
               [pltpu.SemaphoreType.DMA((3 * n,))] * 2, start, None, finish)


def small_finish(chip_sums, from_chips, ws, ms, vs):
    n, n_w = len(chip_sums), len(ws)

    def body(*refs):
        cs, fc = refs[:n], refs[n : 2 * n]
        w, m, v = (refs[2 * n + j * n_w : 2 * n + (j + 1) * n_w] for j in range(3))
        outs = refs[2 * n + 3 * n_w :]
        own = 2 * lax.axis_index("x") + lax.axis_index("y")
        for a in range(n):
            mine = cs[a][...]

            def of_chip(k):
                r = own ^ k
                return jnp.where(r == 0, mine, jnp.where(r == 1, fc[a][0], jnp.where(r == 2, fc[a][1], fc[a][2])))

            tot = ((of_chip(0) + of_chip(1)) + of_chip(2)) + of_chip(3)
            outs[a][...] = tot
            if a < n_w:
                d, m2, v2 = _adamw(w[a][...], tot, m[a][...], v[a][...])
                outs[n + a][...] = d
                outs[n + n_w + a][...] = m2
                outs[n + 2 * n_w + a][...] = v2

    shapes = [jax.ShapeDtypeStruct(s.shape, F32) for s in chip_sums]
    outs = pl.pallas_call(body, name="small_finish", out_shape=shapes + shapes[:n_w] * 3)(
        *chip_sums, *from_chips, *ws, *ms, *vs)
    return outs[:n], outs[n : n + n_w], outs[n + n_w : n + 2 * n_w], outs[n + 2 * n_w :]


def merge_jobs(*jobs):
    def parts(refs, counts):
        out, pos = [], 0
        for cnt in counts:
            out.append(refs[pos : pos + cnt])
            pos += cnt
        return out

    def run(phase):
        def go(ins, outs, sems):
            for job, i, o, s in zip(jobs, parts(ins, [len(j.ins) for j in jobs]),
                                    parts(outs, [len(j.out_shapes) for j in jobs]), parts(sems, [len(j.sems) for j in jobs])):
                getattr(job, phase)(i, o, s)
        return go

    assert all(j.middle is None and not j.aliases for j in jobs)
    return Job([a for j in jobs for a in j.ins], [a for j in jobs for a in j.out_shapes], [a for j in jobs for a in j.sems],
               run("start"), None, run("finish"))


def adamw_rows(w, g, m, v):
    def body(w_ref, g_ref, m_ref, v_ref, d_ref, m2_ref, v2_ref):
        d, m2, v2 = _adamw(w_ref[...], g_ref[...], m_ref[...], v_ref[...])
        d_ref[...] = d
        m2_ref[...] = m2
        v2_ref[...] = v2

    return pl.pallas_call(body, name="adamw_conv", out_shape=[jax.ShapeDtypeStruct(w.shape, F32)] * 3)(w, g, m, v)


def kernel(x, p, norm_g, w_in, ln_v_g, ln_v_b, w_s, b_s, conv_w, w_out, ple_norm_g, w_ple_gate, w_ple_proj, final_g, loss_target, m_norm_g, m_w_in, m_ln_v_g, m_ln_v_b, m_w_s, m_b_s, m_conv_w, m_w_out, m_ple_norm_g, m_w_ple_gate, m_w_ple_proj, m_final_g, v_norm_g, v_w_in, v_ln_v_g, v_ln_v_b, v_w_s, v_b_s, v_conv_w, v_w_out, v_ple_norm_g, v_w_ple_gate, v_w_ple_proj, v_final_g):
    cx, cy, cc = lax.axis_index("x"), lax.axis_index("y"), lax.axis_index("c")
    own = 2 * cx + cy
    ids = jnp.stack([cc, own]).astype(jnp.int32)

    cw_rows = jnp.transpose(conv_w, (0, 2, 1))
    shards = [w_in.astype(BF16), w_out.astype(BF16), w_ple_gate.astype(BF16), w_ple_proj.astype(BF16)]
    *wts0, g_cw = run_job(gather_job([(s, 0) for s in shards] + [(cw_rows, None)]), "gather_weights0")
    cw_full = jnp.transpose(g_cw, (1, 2, 0, 3)).reshape(DEPTH, 3, WIDTH)
    tril = jnp.tril(jnp.ones((CHUNK, CHUNK), F32))
    ws_masked = w_s * tril[None, None]
    ws_b = ws_masked.astype(BF16)
    wsT_b = jnp.swapaxes(ws_masked, 2, 3).astype(BF16)
    bsb = jnp.repeat(jnp.swapaxes(b_s, 1, 2), CHUNK, axis=2)
    small = (norm_g[:, None, :], ln_v_g[:, None, :], ln_v_b[:, None, :], ws_b, bsb, cw_full, ple_norm_g[:, None, :])

    saved0, wts1 = fwd_layer(0, x[0], p, wts0, small, job=gather_job([(s, 1) for s in shards]))
    saved1, _ = fwd_layer(1, saved0[3], p, wts1, small, head=(loss_target[0], final_g[None, :]))
    saved, xs = [saved0[:8], saved1[:8]], [x[0], saved0[3]]
    dx, loss_part, g_final = saved1[3], saved1[8], saved1[9]

    def chip_sums(arrs, tag):
        recvd = run_job(to_sibling_job(arrs), f"rs_to_sibling{tag}")
        return rs_add_sibling(ids, arrs, recvd, f"rs_add_sibling{tag}")

    small_g = [None] * DEPTH
    proj, x2, gate, _, hnT, catT, hn2T, pT = saved[1]
    dx, dproj_b, dpp_b, dgl_b, dx2_b, *g_sm = bwd_layer(1, dx, x2, gate, p, xs[1], proj, wts1, small, wsT_b, tril)
    small_g[1] = g_sm[:5] + [g_sm[6], g_sm[5]]
    (g1_in,), _ = wgrad([(hnT, dproj_b)], 2, True, 1024, "wgrad_in1")
    (g1_out, g1_gate), _ = wgrad([(catT, dx2_b), (hn2T, dgl_b)], 1, False, 1024, "wgrad_outgate1")
    (g1_pp,), _ = wgrad([(pT, dpp_b)], 1, True, 2048, "wgrad_pp1")
    part1 = chip_sums([g1_in, g1_out, g1_gate, g1_pp], "1")
    proj, x2, gate, _, hnT, catT, hn2T, pT = saved[0]
    dx, dproj_b, dpp_b, dgl_b, dx2_b, *g_sm = bwd_layer(0, dx, x2, gate, p, xs[0], proj, wts0, small, wsT_b, tril)
    small_g[0] = g_sm[:5] + [g_sm[6], g_sm[5]]
    def both(j):
        return jnp.concatenate([small_g[0][j], small_g[1][j]], axis=0)

    g_small = [both(0), both(1), both(2), both(3).reshape(DEPTH * HEADS * CHUNK, CHUNK), both(4), both(5), g_final, both(6),
               jnp.broadcast_to(loss_part, (1, 128))]
    (g0_in,), landed = wgrad([(hnT, dproj_b)], 2, True, 1024, "wgrad_in0",
                             job=merge_jobs(to_owners_job(part1[0]), small_to_sibling_job(g_small)))
    from1, small_sib = landed[:4], landed[4:]
    full = rs_add_owners(1, ids, part1[1], from1, None, "rs_add_owners1")
    full = run_job(exchange_halves_job(1, full), "rs_exchange_halves1")
    small_chip = small_chip_sum(g_small, small_sib)
    part0b = chip_sums([g0_in], "0b")
    (g0_out, g0_gate), landed = wgrad([(catT, dx2_b), (hn2T, dgl_b)], 1, False, 1024, "wgrad_outgate0",
                                      job=merge_jobs(to_owners_job(part0b[0]), small_to_chips_job(small_chip)))
    from0b, small_chips = landed[:1], landed[1:]
    full_b = rs_add_owners(0, ids, part0b[1], from0b, full[:1], "rs_add_owners0b")
    (g0_pp,), _ = wgrad([(pT, dpp_b)], 1, True, 2048, "wgrad_pp0")
    part0a = chip_sums([g0_out, g0_gate, g0_pp], "0a")
    from0a = run_job(to_owners_job(part0a[0]), "rs_to_owners0a")
    full_a = rs_add_owners(0, ids, part0a[1], from0a, full[1:], "rs_add_owners0a")
    gw_in, gw_out, gw_gate, gw_pp = run_job(exchange_halves_job(0, list(full_b) + list(full_a)), "rs_exchange_halves0")
    ((d_in, d_out, d_gate, d_pp), (m_in, m_out, m_gate, m_pp), (v_in, v_out, v_gate, v_pp)), _ = adamw_big(
        [w_in, w_out, w_ple_gate, w_ple_proj], [gw_in, gw_out, gw_gate, gw_pp],
        [m_w_in, m_w_out, m_w_ple_gate, m_w_ple_proj], [v_w_in, v_w_out, v_w_ple_gate, v_w_ple_proj], "adamw_big")

    flat = lambda a: a.reshape(-1, a.shape[-1])
    gsum, dsm, msm, vsm = small_finish(
        small_chip, small_chips, [flat(a) for a in (norm_g, ln_v_g, ln_v_b, w_s, b_s, ple_norm_g, final_g[None])],
        [flat(a) for a in (m_norm_g, m_ln_v_g, m_ln_v_b, m_w_s, m_b_s, m_ple_norm_g, m_final_g[None])],
        [flat(a) for a in (v_norm_g, v_ln_v_g, v_ln_v_b, v_w_s, v_b_s, v_ple_norm_g, v_final_g[None])])
    like = [norm_g, ln_v_g, ln_v_b, w_s, b_s, ple_norm_g, final_g]
    gs, ds, m2s, v2s = ([a.reshape(b.shape) for a, b in zip(res, like)] for res in (gsum, dsm, msm, vsm))
    loss = gsum[8][0, 0]
    g_cw_own = lax.dynamic_slice_in_dim(gsum[7].reshape(DEPTH, 3, WIDTH), own * (WIDTH // N_CHIPS), WIDTH // N_CHIPS, axis=2)
    rows2 = lambda a: a.reshape(DEPTH * 3, WIDTH // N_CHIPS)
    d_cw, m_cw, v_cw = adamw_rows(rows2(cw_rows), rows2(g_cw_own), rows2(jnp.transpose(m_conv_w, (0, 2, 1))),
                                  rows2(jnp.transpose(v_conv_w, (0, 2, 1))))
    back = lambda a: jnp.transpose(a.reshape(DEPTH, 3, WIDTH // N_CHIPS), (0, 2, 1))
    g_conv = jnp.transpose(g_cw_own, (0, 2, 1))

    def ordered(sm, cw_v, w_in_v, w_out_v, gate_v, pp_v):
        return [sm[0], w_in_v, sm[1], sm[2], sm[3], sm[4], cw_v, w_out_v, sm[5], gate_v, pp_v, sm[6]]

    grads = ordered(gs, g_conv, gw_in, gw_out, gw_gate, gw_pp)
    deltas = ordered(ds, back(d_cw), d_in, d_out, d_gate, d_pp)
    new_m = ordered(m2s, back(m_cw), m_in, m_out, m_gate, m_pp)
    new_v = ordered(v2s, back(v_cw), v_in, v_out, v_gate, v_pp)
    return (loss, dx[None], *grads, *deltas, *new_m, *new_v)
```

```python
import jax
import jax.numpy as jnp
from jax import lax
from jax.experimental import pallas as pl
from jax.experimental.pallas import tpu as pltpu

F32 = jnp.float32
BF16 = jnp.bfloat16

SEQ = 8192
D_MODEL = 1024
WIDTH = 512
PROJ = 7 * WIDTH
N_CHIPS = 4
COL_BLK = PROJ // N_CHIPS
PLE = 256
HEADS = 4
CHUNK = 128
DEPTH = 2
EPS = 1e-6
TILE = 256
N_TILES = SEQ // TILE
HALO = 8
VMEM_LIMIT = 60 * 1024 * 1024

ADAM_LR, ADAM_B1, ADAM_B2, ADAM_EPS, ADAM_WD, ADAM_STEP = 0.001, 0.9, 0.999, 1e-08, 0.01, 10
ADAM_C1 = 1.0 - ADAM_B1**ADAM_STEP
ADAM_C2 = 1.0 - ADAM_B2**ADAM_STEP

MESH = pl.DeviceIdType.MESH
ANY = pl.BlockSpec(memory_space=pl.ANY)


def _mm(a, b):
    return lax.dot_general(a, b, (((1,), (0,)), ((), ())), preferred_element_type=F32)


def _mm_nt(a, b):
    return lax.dot_general(a, b, (((1,), (1,)), ((), ())), preferred_element_type=F32)


def _mm_rows(a, w_ref):
    blk = w_ref.shape[1]
    acc = _mm(a[:, 0:blk], w_ref[0])
    for k in range(1, N_CHIPS):
        acc = acc + _mm(a[:, k * blk : (k + 1) * blk], w_ref[k])
    return acc


def _ple_proj(p_tile, wpp_ref):
    pb = p_tile.astype(BF16)
    return jnp.concatenate([_mm(pb, wpp_ref[k]) for k in range(N_CHIPS)], axis=-1)


def _mm_nt_rows(a, w_ref):
    return jnp.concatenate([_mm_nt(a, w_ref[k]) for k in range(N_CHIPS)], axis=-1)


def _load_side_by_side(w_hbm, w_vmem, sems):
    copies = [pltpu.make_async_copy(w_hbm.at[k], w_vmem.at[:, pl.ds(k * COL_BLK, COL_BLK)], sems.at[k])
              for k in range(N_CHIPS)]
    for cp in copies:
        cp.start()
    for cp in copies:
        cp.wait()


def _sigmoid(z):
    return 1.0 / (1.0 + jnp.exp(-z))


def _rms_stats(x):
    r = lax.rsqrt(jnp.mean(x * x, axis=-1, keepdims=True) + EPS)
    return r, x * r


def _rms_bwd(dyg, xh, r):
    return r * (dyg - xh * jnp.mean(dyg * xh, axis=-1, keepdims=True))


def _ln_stats(v):
    mu = jnp.mean(v, axis=-1, keepdims=True)
    vc = v - mu
    rs = lax.rsqrt(jnp.mean(vc * vc, axis=-1, keepdims=True) + EPS)
    return rs, vc * rs


def _mesh_pos():
    x, y, c = lax.axis_index("x"), lax.axis_index("y"), lax.axis_index("c")
    return x, y, c, 2 * x + y


def _peer(x, y, c, r):
    return ((1 - x) if (r >> 1) else x, (1 - y) if (r & 1) else y, c)


def _full(shape):
    return pl.BlockSpec(shape, lambda *_: (0,) * len(shape))


def _const(shape, pos):
    return pl.BlockSpec((None,) * len(pos) + tuple(shape), lambda *_: tuple(pos) + (0,) * len(shape))


class Job:
    def __init__(self, ins, out_shapes, sems, start, middle, finish, aliases=None):
        self.ins, self.out_shapes, self.sems = list(ins), list(out_shapes), list(sems)
        self.start, self.middle, self.finish = start, middle, finish
        self.aliases = aliases or {}


def run_job(job, name):
    ni, no = len(job.ins), len(job.out_shapes)

    def body(*refs):
        parts = (refs[:ni], refs[ni : ni + no], refs[ni + no :])
        job.start(*parts)
        if job.middle is not None:
            job.middle(*parts)
        job.finish(*parts)

    return pl.pallas_call(
        body, name=name, out_shape=job.out_shapes, in_specs=[ANY] * ni, out_specs=[ANY] * no, scratch_shapes=job.sems,
        input_output_aliases=job.aliases, compiler_params=pltpu.CompilerParams(has_side_effects=True),
    )(*job.ins)


def _call(body, *, name, grid, in_specs, out_specs, out_shape, args, scratch=(), job=None, mid=None):
    params = pltpu.CompilerParams(dimension_semantics=("arbitrary",) * len(grid), vmem_limit_bytes=VMEM_LIMIT,
                                  has_side_effects=job is not None)
    n_in, n_out, n_sc = len(in_specs), len(out_specs), len(scratch)
    if job is None:
        outs = pl.pallas_call(body, name=name, grid=grid, in_specs=in_specs, out_specs=out_specs, out_shape=out_shape,
                              scratch_shapes=list(scratch), compiler_params=params)(*args)
        return list(outs), []
    ji, jo = len(job.ins), len(job.out_shapes)
    assert not job.aliases and (job.middle is None or len(grid) == 1)

    def wrapped(*refs):
        ins, jin = refs[:n_in], refs[n_in : n_in + ji]
        o0 = n_in + ji
        outs, jout = refs[o0 : o0 + n_out], refs[o0 + n_out : o0 + n_out + jo]
        s0 = o0 + n_out + jo
        sc, jsem = refs[s0 : s0 + n_sc], refs[s0 + n_sc :]
        step = pl.program_id(0)
        for d in range(1, len(grid)):
            step = step * grid[d] + pl.program_id(d)
        n_steps = 1
        for g in grid:
            n_steps *= g

        @pl.when(step == 0)
        def _():
            job.start(jin, jout, jsem)

        if job.middle is not None:
            @pl.when(step == mid)
            def _():
                job.middle(jin, jout, jsem)

        body(*ins, *outs, *sc)

        @pl.when(step == n_steps - 1)
        def _():
            job.finish(jin, jout, jsem)

    outs = pl.pallas_call(
        wrapped, name=name, grid=grid, in_specs=list(in_specs) + [ANY] * ji, out_specs=list(out_specs) + [ANY] * jo,
        out_shape=list(out_shape) + job.out_shapes, scratch_shapes=list(scratch) + job.sems, compiler_params=params,
    )(*args, *job.ins)
    return list(outs[:n_out]), list(outs[n_out:])


def gather_job(items):
    n = len(items)

    def src_half(a, ref, h):
        arr, layer = items[a]
        if layer is None:
            return ref.at[h]
        rh = arr.shape[1] // 2
        return ref.at[layer, pl.ds(h * rh, rh)]

    def dst_half(a, ref, chip, h):
        arr, layer = items[a]
        if layer is None:
            return ref.at[chip, h]
        rh = arr.shape[1] // 2
        return ref.at[chip, pl.ds(h * rh, rh)]

    def copies(src, dst, scratch):
        stage, (in_sem, out_sem, ici_s, ici_r, fwd_s, fwd_r) = scratch[:n], scratch[n:]
        x, y, c, own = _mesh_pos()
        local, ici, fwd, got = [], {}, {}, {}
        for a in range(n):
            layer = items[a][1]
            local.append((pltpu.make_async_copy(src[a] if layer is None else src[a].at[layer], stage[a], in_sem.at[a]),
                          pltpu.make_async_copy(stage[a], dst[a].at[own], out_sem.at[a])))
            for r in (1, 2, 3):
                k = 3 * a + r - 1
                ici[a, r] = pltpu.make_async_remote_copy(
                    src_ref=src_half(a, src[a], c), dst_ref=dst_half(a, dst[a], own, c), send_sem=ici_s.at[k],
                    recv_sem=ici_r.at[k], device_id=_peer(x, y, c, r), device_id_type=MESH)
                fwd[a, r] = pltpu.make_async_remote_copy(
                    src_ref=dst_half(a, dst[a], own ^ r, c), dst_ref=dst_half(a, dst[a], own ^ r, c), send_sem=fwd_s.at[k],
                    recv_sem=fwd_r.at[k], device_id=(x, y, 1 - c), device_id_type=MESH)
                got[a, r] = pltpu.make_async_remote_copy(
                    src_ref=dst_half(a, dst[a], own ^ r, 1 - c), dst_ref=dst_half(a, dst[a], own ^ r, 1 - c),
                    send_sem=fwd_s.at[k], recv_sem=fwd_r.at[k], device_id=(x, y, 1 - c), device_id_type=MESH)
        return local, ici, fwd, got

    def start(src, dst, sems):
        local, ici, _, _ = copies(src, dst, sems)
        for a in range(n):
            for r in (1, 2, 3):
                ici[a, r].start()
        for cp_in, _ in local:
            cp_in.start()
        for cp_in, cp_out in local:
            cp_in.wait()
            cp_out.start()

    def middle(src, dst, sems):
        _, ici, fwd, _ = copies(src, dst, sems)
        for r in (1, 2, 3):
            for a in range(n):
                ici[a, r].wait_recv()
                fwd[a, r].start()

    def finish(src, dst, sems):
        local, ici, fwd, got = copies(src, dst, sems)
        for a in range(n):
            for r in (1, 2, 3):
                got[a, r].wait_recv()
        for a in range(n):
            for r in (1, 2, 3):
                ici[a, r].wait_send()
                fwd[a, r].wait_send()
        for _, cp_out in local:
            cp_out.wait()

    out_shapes = [jax.ShapeDtypeStruct((N_CHIPS,) + (arr.shape if layer is None else arr.shape[1:]), arr.dtype)
                  for arr, layer in items]
    stage = [pltpu.VMEM(arr.shape if layer is None else arr.shape[1:], arr.dtype) for arr, layer in items]
    sems = [pltpu.SemaphoreType.DMA((n,))] * 2 + [pltpu.SemaphoreType.DMA((3 * n,))] * 4
    return Job([arr for arr, _ in items], out_shapes, stage + sems, start, middle, finish)


def _mixer_fwd(proj_ref, lg, lb, ws_ref, bsb_ref, cw_ref, mix_ref, xcbuf, halo_xc):
    u = proj_ref[:, 0 * WIDTH : 1 * WIDTH]
    v = proj_ref[:, 1 * WIDTH : 2 * WIDTH]
    za = proj_ref[:, 2 * WIDTH : 3 * WIDTH]
    h = proj_ref[:, 3 * WIDTH : 4 * WIDTH]
    gb = proj_ref[:, 4 * WIDTH : 5 * WIDTH]
    gc = proj_ref[:, 5 * WIDTH : 6 * WIDTH]
    zb = proj_ref[:, 6 * WIDTH : 7 * WIDTH]
    rs, vhat = _ln_stats(v)
    vl = vhat * lg + lb
    vlb = vl.astype(BF16)
    for j in range(TILE // CHUNK):
        rows = slice(j * CHUNK, (j + 1) * CHUNK)
        for hd in range(HEADS):
            cols = slice(hd * CHUNK, (hd + 1) * CHUNK)
            mix_ref[rows, cols] = _mm(ws_ref[hd], vlb[rows, cols]) + bsb_ref[:, cols]
    mixed = mix_ref[...]
    siga = _sigmoid(za)
    sigb = _sigmoid(zb)
    xc = gc * h
    xcbuf[0:HALO, :] = halo_xc
    xcbuf[HALO : HALO + TILE, :] = xc
    y = cw_ref[0:1, :] * xcbuf[HALO - 2 : HALO - 2 + TILE, :] + cw_ref[1:2, :] * xcbuf[HALO - 1 : HALO - 1 + TILE, :]
    y = y + cw_ref[2:3, :] * xc
    return dict(u=u, za=za, h=h, gb=gb, gc=gc, zb=zb, rs=rs, vhat=vhat, vlb=vlb, mixed=mixed, siga=siga, sigb=sigb,
                xc=xc, y=y)


def fwd_layer(layer, x, p, wts, small, job=None, head=None):
    win, wout, wg, wpp = wts
    ng, lg, lb, ws, bsb, cw, pg = small
    n_head = 0 if head is None else 2

    def body(*refs):
        (x_ref, p_ref, win_ref, wout_ref, wg_ref, wpp_ref, ng_ref, lg_ref, lb_ref, ws_ref, bsb_ref, cw_ref,
         pg_ref) = refs[:13]
        head_in = refs[13 : 13 + n_head]
        proj_ref, x2_ref, gate_ref, x3_ref, hnT_ref, catT_ref, hn2T_ref, pT_ref = refs[13 + n_head : 21 + n_head]
        head_out = refs[21 + n_head : 21 + 2 * n_head]
        mix_ref, xcbuf, carry, wcat, wsem = refs[21 + 2 * n_head :]
        i = pl.program_id(0)

        @pl.when(i == 0)
        def _():
            _load_side_by_side(win_ref, wcat, wsem)
            carry[...] = jnp.zeros_like(carry)
            for ref in head_out:
                ref[...] = jnp.zeros_like(ref)

        xv = x_ref[...]
        _, xh = _rms_stats(xv)
        hn = xh * ng_ref[...]
        hnT_ref[...] = hn.T.astype(BF16)
        proj_ref[...] = _mm(hn.astype(BF16), wcat[...])
        m = _mixer_fwd(proj_ref, lg_ref[...], lb_ref[...], ws_ref, bsb_ref, cw_ref, mix_ref, xcbuf, carry[...])
        carry[...] = m["xc"][TILE - HALO : TILE, :]
        out_a = (m["u"] * m["mixed"]) * (m["za"] * m["siga"])
        out_b = (m["gb"] * m["y"]) * (m["zb"] * m["sigb"])
        cat = jnp.concatenate([out_a, out_b], axis=-1)
        catT_ref[...] = cat.T.astype(BF16)
        x2 = xv + _mm_rows(cat.astype(BF16), wout_ref)
        x2_ref[...] = x2
        _, xh2 = _rms_stats(x2)
        hn2 = xh2 * pg_ref[...]
        hn2T_ref[...] = hn2.T.astype(BF16)
        gate = _sigmoid(_mm_rows(hn2.astype(BF16), wg_ref))
        gate_ref[...] = gate
        pv = p_ref[...]
        pT_ref[...] = pv.T.astype(BF16)
        x3 = x2 + gate * _ple_proj(pv, wpp_ref)
        if head is None:
            x3_ref[...] = x3
        else:
            t_ref, gf_ref = head_in
            loss_ref, gg_ref = head_out
            r3, xh3 = _rms_stats(x3)
            gf = gf_ref[...]
            err = xh3 * gf - t_ref[...]
            loss_ref[...] += (0.5 / D_MODEL) * jnp.sum(err * err).reshape(1, 1)
            dy = err * (1.0 / D_MODEL)
            gg_ref[...] += jnp.sum(dy * xh3, axis=0, keepdims=True)
            x3_ref[...] = _rms_bwd(dy * gf, xh3, r3)

    tok = lambda w: pl.BlockSpec((TILE, w), lambda i: (i, 0))
    tokT = lambda w: pl.BlockSpec((None, w, TILE), lambda i: (i, 0, 0))
    once = lambda a: pl.BlockSpec(a.shape, lambda i: (0,) * a.ndim, pipeline_mode=pl.Buffered(1))
    f32 = lambda w: jax.ShapeDtypeStruct((SEQ, w), F32)
    bfT = lambda w: jax.ShapeDtypeStruct((N_TILES, w, TILE), BF16)
    head_specs = [] if head is None else [tok(D_MODEL), _full((1, D_MODEL))]
    head_outs = [] if head is None else [_full((1, 1)), _full((1, D_MODEL))]
    head_shapes = [] if head is None else [jax.ShapeDtypeStruct((1, 1), F32), jax.ShapeDtypeStruct((1, D_MODEL), F32)]
    return _call(
        body, name=f"fwd_layer{layer}", grid=(N_TILES,),
        in_specs=[tok(D_MODEL), pl.BlockSpec((None, None, TILE, PLE), lambda i: (layer, 0, i, 0)),
                  ANY, once(wout), once(wg), once(wpp),
                  _const((1, D_MODEL), (layer,)), _const((1, WIDTH), (layer,)), _const((1, WIDTH), (layer,)),
                  _const((HEADS, CHUNK, CHUNK), (layer,)), _const((CHUNK, WIDTH), (layer,)), _const((3, WIDTH), (layer,)),
                  _const((1, D_MODEL), (layer,))] + head_specs,
        out_specs=[tok(PROJ), tok(D_MODEL), tok(D_MODEL), tok(D_MODEL),
                   tokT(D_MODEL), tokT(D_MODEL), tokT(D_MODEL), tokT(PLE)] + head_outs,
        out_shape=[f32(PROJ), f32(D_MODEL), f32(D_MODEL), f32(D_MODEL),
                   bfT(D_MODEL), bfT(D_MODEL), bfT(D_MODEL), bfT(PLE)] + head_shapes,
        scratch=[pltpu.VMEM((TILE, WIDTH), F32), pltpu.VMEM((HALO + TILE, WIDTH), F32), pltpu.VMEM((HALO, WIDTH), F32),
                 pltpu.VMEM((D_MODEL, PROJ), BF16), pltpu.SemaphoreType.DMA((N_CHIPS,))],
        args=(x, p, win, wout, wg, wpp, ng, lg, lb, ws, bsb, cw, pg) + (() if head is None else tuple(head)),
        job=job, mid=N_TILES // 2)


def bwd_layer(layer, dx3, x2, gate, p, x, proj, wts, small, wsT, tril):
    win, wout, wg, wpp = wts
    ng, lg, lb, ws, bsb, cw, pg = small

    def body(dx3_ref, x2_ref, gate_ref, p_ref, x_ref, proj_ref, halo_ref, win_ref, wout_ref, wg_ref, wpp_ref, ng_ref, lg_ref,
             lb_ref, ws_ref, wsT_ref, bsb_ref, cw_ref, pg_ref, tril_ref,
             dx_ref, dproj_ref, dpp_ref, dgl_ref, dx2b_ref, gn_ref, glg_ref, glb_ref, gws_ref, gbs_ref, gcw_ref, gpg_ref,
             mix_ref, xcbuf, dycbuf, dvl_ref, bs_acc, wcat, wsem):
        i = pl.program_id(0)

        @pl.when(i == 0)
        def _():
            _load_side_by_side(win_ref, wcat, wsem)
            for ref in (gn_ref, glg_ref, glb_ref, gws_ref, gcw_ref, gpg_ref, bs_acc):
                ref[...] = jnp.zeros_like(ref)
            dycbuf[TILE : TILE + HALO, :] = jnp.zeros((HALO, WIDTH), F32)

        dx3v = dx3_ref[...]
        gate_v = gate_ref[...]
        dpp_ref[...] = (dx3v * gate_v).astype(BF16)
        dgl = ((dx3v * _ple_proj(p_ref[...], wpp_ref)) * gate_v * (1.0 - gate_v)).astype(BF16)
        dgl_ref[...] = dgl
        dhn2 = _mm_nt_rows(dgl, wg_ref)
        r2, xh2 = _rms_stats(x2_ref[...])
        gpg_ref[...] += jnp.sum(dhn2 * xh2, axis=0, keepdims=True)
        dx2 = dx3v + _rms_bwd(dhn2 * pg_ref[...], xh2, r2)
        dx2b = dx2.astype(BF16)
        dx2b_ref[...] = dx2b
        dcat = _mm_nt_rows(dx2b, wout_ref)
        lgv = lg_ref[...]
        halo_xc = halo_ref[:, 5 * WIDTH : 6 * WIDTH] * halo_ref[:, 3 * WIDTH : 4 * WIDTH]
        halo_xc = jnp.where(i == N_TILES - 1, 0.0, halo_xc)
        m = _mixer_fwd(proj_ref, lgv, lb_ref[...], ws_ref, bsb_ref, cw_ref, mix_ref, xcbuf, halo_xc)
        u, za, h, gb, gc, zb = m["u"], m["za"], m["h"], m["gb"], m["gc"], m["zb"]
        mixed, siga, sigb, xc, y = m["mixed"], m["siga"], m["sigb"], m["xc"], m["y"]
        doa = dcat[:, 0:WIDTH]
        dob = dcat[:, WIDTH : 2 * WIDTH]
        sa = za * siga
        sb = zb * sigb
        doa_sa = doa * sa
        dproj_ref[:, 0 * WIDTH : 1 * WIDTH] = (doa_sa * mixed).astype(BF16)
        dmixed = doa_sa * u
        dza = (doa * (u * mixed)) * (siga * (1.0 + za * (1.0 - siga)))
        dproj_ref[:, 2 * WIDTH : 3 * WIDTH] = dza.astype(BF16)
        dob_sb = dob * sb
        dproj_ref[:, 4 * WIDTH : 5 * WIDTH] = (dob_sb * y).astype(BF16)
        dyc = dob_sb * gb
        dzb = (dob * (gb * y)) * (sigb * (1.0 + zb * (1.0 - sigb)))
        dproj_ref[:, 6 * WIDTH : 7 * WIDTH] = dzb.astype(BF16)
        dycbuf[0:TILE, :] = dyc
        dyc1 = dycbuf[1 : 1 + TILE, :]
        dyc2 = dycbuf[2 : 2 + TILE, :]
        dxc = cw_ref[2:3, :] * dyc + cw_ref[1:2, :] * dyc1 + cw_ref[0:1, :] * dyc2
        gcw_ref[0:1, :] += jnp.sum(xc * dyc2, axis=0, keepdims=True)
        gcw_ref[1:2, :] += jnp.sum(xc * dyc1, axis=0, keepdims=True)
        gcw_ref[2:3, :] += jnp.sum(xc * dyc, axis=0, keepdims=True)
        dycbuf[TILE : TILE + HALO, :] = dyc[0:HALO, :]
        dproj_ref[:, 5 * WIDTH : 6 * WIDTH] = (dxc * h).astype(BF16)
        dproj_ref[:, 3 * WIDTH : 4 * WIDTH] = (dxc * gc).astype(BF16)
        dmb = dmixed.astype(BF16)
        vlb = m["vlb"]
        bsum = jnp.zeros((CHUNK, WIDTH), F32)
        for j in range(TILE // CHUNK):
            rows = slice(j * CHUNK, (j + 1) * CHUNK)
            bsum = bsum + dmixed[rows, :]
            for hd in range(HEADS):
                cols = slice(hd * CHUNK, (hd + 1) * CHUNK)
                gws_ref[hd] += _mm_nt(dmb[rows, cols], vlb[rows, cols])
                dvl_ref[rows, cols] = _mm(wsT_ref[hd], dmb[rows, cols])
        bs_acc[...] += bsum
        dvl = dvl_ref[...]
        vhat = m["vhat"]
        glb_ref[...] += jnp.sum(dvl, axis=0, keepdims=True)
        glg_ref[...] += jnp.sum(dvl * vhat, axis=0, keepdims=True)
        dvh = dvl * lgv
        dv = m["rs"] * (dvh - jnp.mean(dvh, axis=-1, keepdims=True) - vhat * jnp.mean(dvh * vhat, axis=-1, keepdims=True))
        dproj_ref[:, 1 * WIDTH : 2 * WIDTH] = dv.astype(BF16)
        dhn = _mm_nt(dproj_ref[...], wcat[...])
        r1, xh = _rms_stats(x_ref[...])
        gn_ref[...] += jnp.sum(dhn * xh, axis=0, keepdims=True)
        dx_ref[...] = dx2 + _rms_bwd(dhn * ng_ref[...], xh, r1)

        @pl.when(i == N_TILES - 1)
        def _():
            for hd in range(HEADS):
                gws_ref[hd] = gws_ref[hd] * tril_ref[...]
                gbs_ref[hd : hd + 1, :] = jnp.sum(bs_acc[:, hd * CHUNK : (hd + 1) * CHUNK].T, axis=0, keepdims=True)

    rev = lambda w: pl.BlockSpec((TILE, w), lambda i: (N_TILES - 1 - i, 0))
    halo = pl.BlockSpec((HALO, PROJ), lambda i: (jnp.maximum((N_TILES - 1 - i) * (TILE // HALO) - 1, 0), 0))
    once = lambda a: pl.BlockSpec(a.shape, lambda i: (0,) * a.ndim, pipeline_mode=pl.Buffered(1))
    vec = lambda w: jax.ShapeDtypeStruct((1, w), F32)
    b16 = lambda w: jax.ShapeDtypeStruct((SEQ, w), BF16)
    outs, _ = _call(
        body, name=f"bwd_layer{layer}", grid=(N_TILES,),
        in_specs=[rev(D_MODEL), rev(D_MODEL), rev(D_MODEL),
                  pl.BlockSpec((None, None, TILE, PLE), lambda i: (layer, 0, N_TILES - 1 - i, 0)),
                  rev(D_MODEL), rev(PROJ), halo, ANY, once(wout), once(wg), once(wpp),
                  _const((1, D_MODEL), (layer,)), _const((1, WIDTH), (layer,)), _const((1, WIDTH), (layer,)),
                  _const((HEADS, CHUNK, CHUNK), (layer,)), _const((HEADS, CHUNK, CHUNK), (layer,)),
                  _const((CHUNK, WIDTH), (layer,)), _const((3, WIDTH), (layer,)), _const((1, D_MODEL), (layer,)),
                  _full((CHUNK, CHUNK))],
        out_specs=[rev(D_MODEL), rev(PROJ), rev(D_MODEL), rev(D_MODEL), rev(D_MODEL),
                   _full((1, D_MODEL)), _full((1, WIDTH)), _full((1, WIDTH)), _full((HEADS, CHUNK, CHUNK)),
                   _full((HEADS, CHUNK)), _full((3, WIDTH)), _full((1, D_MODEL))],
        out_shape=[jax.ShapeDtypeStruct((SEQ, D_MODEL), F32), b16(PROJ), b16(D_MODEL), b16(D_MODEL), b16(D_MODEL),
                   vec(D_MODEL), vec(WIDTH), vec(WIDTH), jax.ShapeDtypeStruct((HEADS, CHUNK, CHUNK), F32),
                   jax.ShapeDtypeStruct((HEADS, CHUNK), F32), jax.ShapeDtypeStruct((3, WIDTH), F32), vec(D_MODEL)],
        scratch=[pltpu.VMEM((TILE, WIDTH), F32), pltpu.VMEM((HALO + TILE, WIDTH), F32),
                 pltpu.VMEM((TILE + HALO, WIDTH), F32), pltpu.VMEM((TILE, WIDTH), F32), pltpu.VMEM((CHUNK, WIDTH), F32),
                 pltpu.VMEM((D_MODEL, PROJ), BF16), pltpu.SemaphoreType.DMA((N_CHIPS,))],
        args=(dx3, x2, gate, p, x, proj, proj, win, wout, wg, wpp, ng, lg, lb, ws, wsT, bsb, cw, pg, tril))
    return outs


def wgrad(pairs, n_tiles, col_blocked, tk, name, job=None):
    n = len(pairs)
    m_dim, n_dim = pairs[0][0].shape[1], pairs[0][1].shape[1]
    tn = n_dim // n_tiles
    cb, mb = n_dim // N_CHIPS, m_dim // N_CHIPS
    per_tile = N_CHIPS // n_tiles
    assert col_blocked or n_tiles == 1

    def body(*refs):
        for a in range(n):
            a_ref, b_ref, o_ref = refs[2 * a], refs[2 * a + 1], refs[2 * n + a]

            @pl.when(pl.program_id(1) == 0)
            def _():
                o_ref[...] = jnp.zeros_like(o_ref)

            prod = _mm(a_ref[0], b_ref[0:TILE, :])
            for t in range(1, tk // TILE):
                prod = prod + _mm(a_ref[t], b_ref[t * TILE : (t + 1) * TILE, :])
            if col_blocked:
                for q in range(per_tile):
                    o_ref[q] += prod[:, q * cb : (q + 1) * cb]
            else:
                for q in range(N_CHIPS):
                    o_ref[q] += prod[q * mb : (q + 1) * mb, :]

            @pl.when(pl.program_id(1) == SEQ // tk - 1)
            def _():
                refs[3 * n + a][...] = o_ref[...].astype(BF16)

    if col_blocked:
        shape = (N_CHIPS, m_dim, cb)
        o_spec = pl.BlockSpec((per_tile, m_dim, cb), lambda j, k: (j, 0, 0))
    else:
        shape = (N_CHIPS, mb, n_dim)
        o_spec = pl.BlockSpec((N_CHIPS, mb, n_dim), lambda j, k: (0, 0, 0))
    outs, job_outs = _call(
        body, name=name, grid=(n_tiles, SEQ // tk),
        in_specs=[pl.BlockSpec((tk // TILE, m_dim, TILE), lambda j, k: (k, 0, 0)),
                  pl.BlockSpec((tk, tn), lambda j, k: (k, j))] * n,
        out_specs=[o_spec] * (2 * n), out_shape=[jax.ShapeDtypeStruct(shape, F32)] * n + [jax.ShapeDtypeStruct(shape, BF16)] * n,
        args=[t for pair in pairs for t in pair], job=job)
    return outs[:n], outs[n:], job_outs


def to_sibling_job(grads):
    n = len(grads)

    def copies(src, dst, sems):
        x, y, c, _ = _mesh_pos()
        out = []
        for a in range(n):
            rh = grads[a].shape[1] // 2
            out.append(pltpu.make_async_remote_copy(
                src_ref=src[a].at[:, pl.ds((1 - c) * rh, rh), :], dst_ref=dst[a], send_sem=sems[0].at[a],
                recv_sem=sems[1].at[a], device_id=(x, y, 1 - c), device_id_type=MESH))
        return out

    def start(src, dst, sems):
        for cp in copies(src, dst, sems):
            cp.start()

    def finish(src, dst, sems):
        for cp in copies(src, dst, sems):
            cp.wait()

    out_shapes = [jax.ShapeDtypeStruct((N_CHIPS, g.shape[1] // 2, g.shape[2]), g.dtype) for g in grads]
    return Job(grads, out_shapes, [pltpu.SemaphoreType.DMA((n,))] * 2, start, None, finish)


def rs_add_sibling(ids, grads, recvd, name):
    n = len(grads)

    def body(ids_ref, *refs):
        g, r = refs[:n], refs[n : 2 * n]
        pb, pf = refs[2 * n : 3 * n], refs[3 * n :]
        k = pl.program_id(0)
        for a in range(n):
            s = g[a][...] + r[a][...].astype(F32)
            pb[a][...] = s.astype(BF16)

            @pl.when(k == ids_ref[1])
            def _():
                pf[a][...] = s

    in_specs, out_specs, out_shape = [], [], []
    for g in grads:
        rh, cc = g.shape[1] // 2, g.shape[2]
        in_specs.append(pl.BlockSpec((None, rh, cc), lambda k, ids: (k, ids[0], 0)))
    for g in grads:
        rh, cc = g.shape[1] // 2, g.shape[2]
        in_specs.append(pl.BlockSpec((None, rh, cc), lambda k, ids: (k, 0, 0)))
        out_specs.append(pl.BlockSpec((None, rh, cc), lambda k, ids: (k, 0, 0)))
        out_shape.append(jax.ShapeDtypeStruct((N_CHIPS, rh, cc), BF16))
    for g in grads:
        rh, cc = g.shape[1] // 2, g.shape[2]
        out_specs.append(pl.BlockSpec((rh, cc), lambda k, ids: (0, 0)))
        out_shape.append(jax.ShapeDtypeStruct((rh, cc), F32))
    outs = pl.pallas_call(
        body, name=name, out_shape=out_shape,
        grid_spec=pltpu.PrefetchScalarGridSpec(num_scalar_prefetch=1, grid=(N_CHIPS,), in_specs=in_specs,
                                               out_specs=out_specs),
        compiler_params=pltpu.CompilerParams(dimension_semantics=("arbitrary",), vmem_limit_bytes=VMEM_LIMIT),
    )(ids, *grads, *recvd)
    return list(outs[:n]), list(outs[n:])


def to_owners_job(partials):
    n = len(partials)

    def copies(src, dst, sems):
        x, y, c, own = _mesh_pos()
        out = []
        for a in range(n):
            for r in (1, 2, 3):
                out.append(pltpu.make_async_remote_copy(
                    src_ref=src[a].at[own ^ r], dst_ref=dst[a].at[r - 1], send_sem=sems[0].at[3 * a + r - 1],
                    recv_sem=sems[1].at[3 * a + r - 1], device_id=_peer(x, y, c, r), device_id_type=MESH))
        return out

    def start(src, dst, sems):
        for cp in copies(src, dst, sems):
            cp.start()

    def finish(src, dst, sems):
        for cp in copies(src, dst, sems):
            cp.wait()

    out_shapes = [jax.ShapeDtypeStruct((3,) + p.shape[1:], BF16) for p in partials]
    return Job(partials, out_shapes, [pltpu.SemaphoreType.DMA((3 * n,))] * 2, start, None, finish)


def rs_add_owners(layer, ids, own_f32, recvd, prev, name):
    n = len(own_f32)
    nb = 2

    def body(ids_ref, *refs):
        o, r, f = refs[:n], refs[n : 2 * n], refs[-n:]
        for a in range(n):
            f[a][...] = ((o[a][...] + r[a][0].astype(F32)) + r[a][1].astype(F32)) + r[a][2].astype(F32)

    in_specs, out_specs, out_shape = [], [], []
    for o in own_f32:
        in_specs.append(pl.BlockSpec((o.shape[0] // nb, o.shape[1]), lambda j, ids: (j, 0)))
    for o in own_f32:
        in_specs.append(pl.BlockSpec((3, o.shape[0] // nb, o.shape[1]), lambda j, ids: (0, j, 0)))
        out_specs.append(pl.BlockSpec((None, o.shape[0] // nb, o.shape[1]), lambda j, ids: (layer, ids[0] * nb + j, 0)))
        out_shape.append(jax.ShapeDtypeStruct((DEPTH, 2 * o.shape[0], o.shape[1]), F32))
    args, aliases = [ids, *own_f32, *recvd], {}
    if prev is not None:
        in_specs += [ANY] * n
        args += list(prev)
        aliases = {1 + 2 * n + a: a for a in range(n)}
    return pl.pallas_call(
        body, name=name, out_shape=out_shape, input_output_aliases=aliases,
        grid_spec=pltpu.PrefetchScalarGridSpec(num_scalar_prefetch=1, grid=(nb,), in_specs=in_specs, out_specs=out_specs),
        compiler_params=pltpu.CompilerParams(dimension_semantics=("arbitrary",), vmem_limit_bytes=VMEM_LIMIT),
    )(*args)


def exchange_halves_job(layer, full):
    n = len(full)

    def copies(src, dst, sems):
        x, y, c, _ = _mesh_pos()
        out = []
        for a in range(n):
            rh = full[a].shape[1] // 2
            out.append(pltpu.make_async_remote_copy(
                src_ref=src[a].at[layer, pl.ds(c * rh, rh), :], dst_ref=dst[a].at[layer, pl.ds(c * rh, rh), :],
                send_sem=sems[0].at[a], recv_sem=sems[1].at[a], device_id=(x, y, 1 - c), device_id_type=MESH))
        return out

    def start(src, dst, sems):
        for cp in copies(src, dst, sems):
            cp.start()

    def finish(src, dst, sems):
        for cp in copies(src, dst, sems):
            cp.wait()

    out_shapes = [jax.ShapeDtypeStruct(f.shape, F32) for f in full]
    return Job(full, out_shapes, [pltpu.SemaphoreType.DMA((n,))] * 2, start, None, finish, {a: a for a in range(n)})


def _adamw(w, g, m, v):
    m2 = ADAM_B1 * m + (1.0 - ADAM_B1) * g
    v2 = ADAM_B2 * v + (1.0 - ADAM_B2) * (g * g)
    delta = -ADAM_LR * ((m2 / ADAM_C1) / (jnp.sqrt(v2 / ADAM_C2) + ADAM_EPS) + ADAM_WD * w)
    return delta, m2, v2


def adamw_big(ws, gs, ms, vs, name, job=None):
    n = len(ws)
    nb = 4

    def body(*refs):
        for a in range(n):
            w, g, m, v = (refs[j * n + a][...] for j in range(4))
            d, m2, v2 = _adamw(w, g, m, v)
            refs[4 * n + a][...] = d
            refs[5 * n + a][...] = m2
            refs[6 * n + a][...] = v2

    specs = [pl.BlockSpec((None, w.shape[1] // nb, w.shape[2]), lambda l, j: (l, j, 0)) for w in ws]
    shapes = [jax.ShapeDtypeStruct(w.shape, F32) for w in ws]
    outs, job_outs = _call(body, name=name, grid=(DEPTH, nb), in_specs=specs * 4, out_specs=specs * 3,
                           out_shape=shapes * 3, args=(*ws, *gs, *ms, *vs), job=job)
    return (outs[:n], outs[n : 2 * n], outs[2 * n :]), job_outs


def small_to_sibling_job(gs):
    n = len(gs)

    def copies(src, dst, sems):
        x, y, c, _ = _mesh_pos()
        return [pltpu.make_async_remote_copy(src_ref=src[a], dst_ref=dst[a], send_sem=sems[0].at[a], recv_sem=sems[1].at[a],
                                             device_id=(x, y, 1 - c), device_id_type=MESH) for a in range(n)]

    def start(*refs):
        for cp in copies(*refs):
            cp.start()

    def finish(*refs):
        for cp in copies(*refs):
            cp.wait()

    return Job(gs, [jax.ShapeDtypeStruct(g.shape, F32) for g in gs], [pltpu.SemaphoreType.DMA((n,))] * 2, start, None, finish)


def small_chip_sum(gs, recvd):
    n = len(gs)

    def body(*refs):
        for a in range(n):
            refs[2 * n + a][...] = refs[a][...] + refs[n + a][...]

    return pl.pallas_call(body, name="small_chip_sum", out_shape=[jax.ShapeDtypeStruct(g.shape, F32) for g in gs])(*gs, *recvd)


def small_to_chips_job(chip_sums):
    n = len(chip_sums)

    def copies(src, dst, sems):
        x, y, c, _ = _mesh_pos()
        return [pltpu.make_async_remote_copy(src_ref=src[a], dst_ref=dst[a].at[r - 1], send_sem=sems[0].at[3 * a + r - 1],
                                             recv_sem=sems[1].at[3 * a + r - 1], device_id=_peer(x, y, c, r),
                                             device_id_type=MESH) for a in range(n) for r in (1, 2, 3)]

    def start(*refs):
        for cp in copies(*refs):
            cp.start()

    def finish(*refs):
        for cp in copies(*refs):
            cp.wait()

    return Job(chip_sums, [jax.ShapeDtypeStruct((3,) + s.shape, F32) for s in chip_sums],
               [pltpu.SemaphoreType.DMA((3 * n,))] * 2, start, None, finish)


def small_finish(chip_sums, from_chips, ws, ms, vs):
    n, n_w = len(chip_sums), len(ws)

    def body(*refs):
        cs, fc = refs[:n], refs[n : 2 * n]
        w, m, v = (refs[2 * n + j * n_w : 2 * n + (j + 1) * n_w] for j in range(3))
        outs = refs[2 * n + 3 * n_w :]
        own = 2 * lax.axis_index("x") + lax.axis_index("y")
        for a in range(n):
            mine = cs[a][...]

            def of_chip(k):
                r = own ^ k
                return jnp.where(r == 0, mine, jnp.where(r == 1, fc[a][0], jnp.where(r == 2, fc[a][1], fc[a][2])))

            tot = ((of_chip(0) + of_chip(1)) + of_chip(2)) + of_chip(3)
            outs[a][...] = tot
            if a < n_w:
                d, m2, v2 = _adamw(w[a][...], tot, m[a][...], v[a][...])
                outs[n + a][...] = d
                outs[n + n_w + a][...] = m2
                outs[n + 2 * n_w + a][...] = v2

    shapes = [jax.ShapeDtypeStruct(s.shape, F32) for s in chip_sums]
    outs = pl.pallas_call(body, name="small_finish", out_shape=shapes + shapes[:n_w] * 3)(
        *chip_sums, *from_chips, *ws, *ms, *vs)
    return outs[:n], outs[n : n + n_w], outs[n + n_w : n + 2 * n_w], outs[n + 2 * n_w :]


def merge_jobs(*jobs):
    def parts(refs, counts):
        out, pos = [], 0
        for cnt in counts:
            out.append(refs[pos : pos + cnt])
            pos += cnt
        return out

    def run(phase):
        def go(ins, outs, sems):
            for job, i, o, s in zip(jobs, parts(ins, [len(j.ins) for j in jobs]),
                                    parts(outs, [len(j.out_shapes) for j in jobs]), parts(sems, [len(j.sems) for j in jobs])):
                getattr(job, phase)(i, o, s)
        return go

    assert all(j.middle is None and not j.aliases for j in jobs)
    return Job([a for j in jobs for a in j.ins], [a for j in jobs for a in j.out_shapes], [a for j in jobs for a in j.sems],
               run("start"), None, run("finish"))


def adamw_rows(w, g, m, v):
    def body(w_ref, g_ref, m_ref, v_ref, d_ref, m2_ref, v2_ref):
        d, m2, v2 = _adamw(w_ref[...], g_ref[...], m_ref[...], v_ref[...])
        d_ref[...] = d
        m2_ref[...] = m2
        v2_ref[...] = v2

    return pl.pallas_call(body, name="adamw_conv", out_shape=[jax.ShapeDtypeStruct(w.shape, F32)] * 3)(w, g, m, v)


def kernel(x, p, norm_g, w_in, ln_v_g, ln_v_b, w_s, b_s, conv_w, w_out, ple_norm_g, w_ple_gate, w_ple_proj, final_g, loss_target, m_norm_g, m_w_in, m_ln_v_g, m_ln_v_b, m_w_s, m_b_s, m_conv_w, m_w_out, m_ple_norm_g, m_w_ple_gate, m_w_ple_proj, m_final_g, v_norm_g, v_w_in, v_ln_v_g, v_ln_v_b, v_w_s, v_b_s, v_conv_w, v_w_out, v_ple_norm_g, v_w_ple_gate, v_w_ple_proj, v_final_g):
    cx, cy, cc = lax.axis_index("x"), lax.axis_index("y"), lax.axis_index("c")
    own = 2 * cx + cy
    ids = jnp.stack([cc, own]).astype(jnp.int32)

    cw_rows = jnp.transpose(conv_w, (0, 2, 1))
    shards = [w_in.astype(BF16), w_out.astype(BF16), w_ple_gate.astype(BF16), w_ple_proj.astype(BF16)]
    *wts0, g_cw = run_job(gather_job([(s, 0) for s in shards] + [(cw_rows, None)]), "gather_weights0")
    cw_full = jnp.transpose(g_cw, (1, 2, 0, 3)).reshape(DEPTH, 3, WIDTH)
    tril = jnp.tril(jnp.ones((CHUNK, CHUNK), F32))
    ws_masked = w_s * tril[None, None]
    ws_b = ws_masked.astype(BF16)
    wsT_b = jnp.swapaxes(ws_masked, 2, 3).astype(BF16)
    bsb = jnp.repeat(jnp.swapaxes(b_s, 1, 2), CHUNK, axis=2)
    small = (norm_g[:, None, :], ln_v_g[:, None, :], ln_v_b[:, None, :], ws_b, bsb, cw_full, ple_norm_g[:, None, :])

    saved0, wts1 = fwd_layer(0, x[0], p, wts0, small, job=gather_job([(s, 1) for s in shards]))
    saved1, _ = fwd_layer(1, saved0[3], p, wts1, small, head=(loss_target[0], final_g[None, :]))
    saved, xs = [saved0[:8], saved1[:8]], [x[0], saved0[3]]
    dx, loss_part, g_final = saved1[3], saved1[8], saved1[9]

    def chip_sums(arrs, arrs_bf16, tag):
        recvd = run_job(to_sibling_job(arrs_bf16), f"rs_to_sibling{tag}")
        return rs_add_sibling(ids, arrs, recvd, f"rs_add_sibling{tag}")

    small_g = [None] * DEPTH
    proj, x2, gate, _, hnT, catT, hn2T, pT = saved[1]
    dx, dproj_b, dpp_b, dgl_b, dx2_b, *g_sm = bwd_layer(1, dx, x2, gate, p, xs[1], proj, wts1, small, wsT_b, tril)
    small_g[1] = g_sm[:5] + [g_sm[6], g_sm[5]]
    g1_in, b1_in, _ = wgrad([(hnT, dproj_b)], 2, True, 1024, "wgrad_in1")
    g1_og, b1_og, _ = wgrad([(catT, dx2_b), (hn2T, dgl_b)], 1, False, 1024, "wgrad_outgate1")
    g1_pp, b1_pp, _ = wgrad([(pT, dpp_b)], 1, True, 2048, "wgrad_pp1")
    part1 = chip_sums(g1_in + g1_og + g1_pp, b1_in + b1_og + b1_pp, "1")
    proj, x2, gate, _, hnT, catT, hn2T, pT = saved[0]
    dx, dproj_b, dpp_b, dgl_b, dx2_b, *g_sm = bwd_layer(0, dx, x2, gate, p, xs[0], proj, wts0, small, wsT_b, tril)
    small_g[0] = g_sm[:5] + [g_sm[6], g_sm[5]]
    def both(j):
        return jnp.concatenate([small_g[0][j], small_g[1][j]], axis=0)

    g_small = [both(0), both(1), both(2), both(3).reshape(DEPTH * HEADS * CHUNK, CHUNK), both(4), both(5), g_final, both(6),
               jnp.broadcast_to(loss_part, (1, 128))]
    g0_in, b0_in, landed = wgrad([(hnT, dproj_b)], 2, True, 1024, "wgrad_in0",
                                 job=merge_jobs(to_owners_job(part1[0]), small_to_sibling_job(g_small)))
    from1, small_sib = landed[:4], landed[4:]
    full = rs_add_owners(1, ids, part1[1], from1, None, "rs_add_owners1")
    full = run_job(exchange_halves_job(1, full), "rs_exchange_halves1")
    small_chip = small_chip_sum(g_small, small_sib)
    part0b = chip_sums(g0_in, b0_in, "0b")
    g0_og, b0_og, landed = wgrad([(catT, dx2_b), (hn2T, dgl_b)], 1, False, 1024, "wgrad_outgate0",
                                 job=merge_jobs(to_owners_job(part0b[0]), small_to_chips_job(small_chip)))
    from0b, small_chips = landed[:1], landed[1:]
    full_b = rs_add_owners(0, ids, part0b[1], from0b, full[:1], "rs_add_owners0b")
    g0_pp, b0_pp, _ = wgrad([(pT, dpp_b)], 1, True, 2048, "wgrad_pp0")
    part0a = chip_sums(g0_og + g0_pp, b0_og + b0_pp, "0a")
    from0a = run_job(to_owners_job(part0a[0]), "rs_to_owners0a")
    full_a = rs_add_owners(0, ids, part0a[1], from0a, full[1:], "rs_add_owners0a")
    gw_in, gw_out, gw_gate, gw_pp = run_job(exchange_halves_job(0, list(full_b) + list(full_a)), "rs_exchange_halves0")
    ((d_in, d_out, d_gate, d_pp), (m_in, m_out, m_gate, m_pp), (v_in, v_out, v_gate, v_pp)), _ = adamw_big(
        [w_in, w_out, w_ple_gate, w_ple_proj], [gw_in, gw_out, gw_gate, gw_pp],
        [m_w_in, m_w_out, m_w_ple_gate, m_w_ple_proj], [v_w_in, v_w_out, v_w_ple_gate, v_w_ple_proj], "adamw_big")

    flat = lambda a: a.reshape(-1, a.shape[-1])
    gsum, dsm, msm, vsm = small_finish(
        small_chip, small_chips, [flat(a) for a in (norm_g, ln_v_g, ln_v_b, w_s, b_s, ple_norm_g, final_g[None])],
        [flat(a) for a in (m_norm_g, m_ln_v_g, m_ln_v_b, m_w_s, m_b_s, m_ple_norm_g, m_final_g[None])],
        [flat(a) for a in (v_norm_g, v_ln_v_g, v_ln_v_b, v_w_s, v_b_s, v_ple_norm_g, v_final_g[None])])
    like = [norm_g, ln_v_g, ln_v_b, w_s, b_s, ple_norm_g, final_g]
    gs, ds, m2s, v2s = ([a.reshape(b.shape) for a, b in zip(res, like)] for res in (gsum, dsm, msm, vsm))
    loss = gsum[8][0, 0]
    g_cw_own = lax.dynamic_slice_in_dim(gsum[7].reshape(DEPTH, 3, WIDTH), own * (WIDTH // N_CHIPS), WIDTH // N_CHIPS, axis=2)
    rows2 = lambda a: a.reshape(DEPTH * 3, WIDTH // N_CHIPS)
    d_cw, m_cw, v_cw = adamw_rows(rows2(cw_rows), rows2(g_cw_own), rows2(jnp.transpose(m_conv_w, (0, 2, 1))),
                                  rows2(jnp.transpose(v_conv_w, (0, 2, 1))))
    back = lambda a: jnp.transpose(a.reshape(DEPTH, 3, WIDTH // N_CHIPS), (0, 2, 1))
    g_conv = jnp.transpose(g_cw_own, (0, 2, 1))

    def ordered(sm, cw_v, w_in_v, w_out_v, gate_v, pp_v):
        return [sm[0], w_in_v, sm[1], sm[2], sm[3], sm[4], cw_v, w_out_v, sm[5], gate_v, pp_v, sm[6]]

    grads = ordered(gs, g_conv, gw_in, gw_out, gw_gate, gw_pp)
    deltas = ordered(ds, back(d_cw), d_in, d_out, d_gate, d_pp)
    new_m = ordered(m2s, back(m_cw), m_in, m_out, m_gate, m_pp)
    new_v = ordered(v2s, back(v_cw), v_in, v_out, v_gate, v_pp)
    return (loss, dx[None], *grads, *deltas, *new_m, *new_v)
```

```python
import jax
import jax.numpy as jnp
from jax import lax
from jax.experimental import pallas as pl
from jax.experimental.pallas import tpu as pltpu

F32 = jnp.float32
BF16 = jnp.bfloat16

SEQ = 8192
D_MODEL = 1024
WIDTH = 512
PROJ = 7 * WIDTH
N_CHIPS = 4
COL_BLK = PROJ // N_CHIPS
PLE = 256
HEADS = 4
CHUNK = 128
DEPTH = 2
EPS = 1e-6
TILE = 256
N_TILES = SEQ // TILE
FWD_TILE = 512
HALO = 8
VMEM_LIMIT = 60 * 1024 * 1024

ADAM_LR, ADAM_B1, ADAM_B2, ADAM_EPS, ADAM_WD, ADAM_STEP = 0.001, 0.9, 0.999, 1e-08, 0.01, 10
ADAM_C1 = 1.0 - ADAM_B1**ADAM_STEP
ADAM_C2 = 1.0 - ADAM_B2**ADAM_STEP

MESH = pl.DeviceIdType.MESH
ANY = pl.BlockSpec(memory_space=pl.ANY)


def _mm(a, b):
    return lax.dot_general(a, b, (((1,), (0,)), ((), ())), preferred_element_type=F32)


def _mm_nt(a, b):
    return lax.dot_general(a, b, (((1,), (1,)), ((), ())), preferred_element_type=F32)


def _mm_rows(a, w_ref):
    blk = w_ref.shape[1]
    acc = _mm(a[:, 0:blk], w_ref[0])
    for k in range(1, N_CHIPS):
        acc = acc + _mm(a[:, k * blk : (k + 1) * blk], w_ref[k])
    return acc


def _ple_proj(p_tile, wpp_ref):
    pb = p_tile.astype(BF16)
    return jnp.concatenate([_mm(pb, wpp_ref[k]) for k in range(N_CHIPS)], axis=-1)


def _mm_nt_rows(a, w_ref):
    return jnp.concatenate([_mm_nt(a, w_ref[k]) for k in range(N_CHIPS)], axis=-1)


def _load_side_by_side(w_hbm, w_vmem, sems):
    copies = [pltpu.make_async_copy(w_hbm.at[k], w_vmem.at[:, pl.ds(k * COL_BLK, COL_BLK)], sems.at[k])
              for k in range(N_CHIPS)]
    for cp in copies:
        cp.start()
    for cp in copies:
        cp.wait()


def _sigmoid(z):
    return 1.0 / (1.0 + jnp.exp(-z))


def _rms_stats(x):
    r = lax.rsqrt(jnp.mean(x * x, axis=-1, keepdims=True) + EPS)
    return r, x * r


def _rms_bwd(dyg, xh, r):
    return r * (dyg - xh * jnp.mean(dyg * xh, axis=-1, keepdims=True))


def _ln_stats(v):
    mu = jnp.mean(v, axis=-1, keepdims=True)
    vc = v - mu
    rs = lax.rsqrt(jnp.mean(vc * vc, axis=-1, keepdims=True) + EPS)
    return rs, vc * rs


def _mesh_pos():
    x, y, c = lax.axis_index("x"), lax.axis_index("y"), lax.axis_index("c")
    return x, y, c, 2 * x + y


def _peer(x, y, c, r):
    return ((1 - x) if (r >> 1) else x, (1 - y) if (r & 1) else y, c)


def _full(shape):
    return pl.BlockSpec(shape, lambda *_: (0,) * len(shape))


def _const(shape, pos):
    return pl.BlockSpec((None,) * len(pos) + tuple(shape), lambda *_: tuple(pos) + (0,) * len(shape))


class Job:
    def __init__(self, ins, out_shapes, sems, start, middle, finish, aliases=None):
        self.ins, self.out_shapes, self.sems = list(ins), list(out_shapes), list(sems)
        self.start, self.middle, self.finish = start, middle, finish
        self.aliases = aliases or {}


def run_job(job, name):
    ni, no = len(job.ins), len(job.out_shapes)

    def body(*refs):
        parts = (refs[:ni], refs[ni : ni + no], refs[ni + no :])
        job.start(*parts)
        if job.middle is not None:
            job.middle(*parts)
        job.finish(*parts)

    return pl.pallas_call(
        body, name=name, out_shape=job.out_shapes, in_specs=[ANY] * ni, out_specs=[ANY] * no, scratch_shapes=job.sems,
        input_output_aliases=job.aliases, compiler_params=pltpu.CompilerParams(has_side_effects=True),
    )(*job.ins)


def _call(body, *, name, grid, in_specs, out_specs, out_shape, args, scratch=(), job=None, mid=None):
    params = pltpu.CompilerParams(dimension_semantics=("arbitrary",) * len(grid), vmem_limit_bytes=VMEM_LIMIT,
                                  has_side_effects=job is not None)
    n_in, n_out, n_sc = len(in_specs), len(out_specs), len(scratch)
    if job is None:
        outs = pl.pallas_call(body, name=name, grid=grid, in_specs=in_specs, out_specs=out_specs, out_shape=out_shape,
                              scratch_shapes=list(scratch), compiler_params=params)(*args)
        return list(outs), []
    ji, jo = len(job.ins), len(job.out_shapes)
    assert not job.aliases and (job.middle is None or len(grid) == 1)

    def wrapped(*refs):
        ins, jin = refs[:n_in], refs[n_in : n_in + ji]
        o0 = n_in + ji
        outs, jout = refs[o0 : o0 + n_out], refs[o0 + n_out : o0 + n_out + jo]
        s0 = o0 + n_out + jo
        sc, jsem = refs[s0 : s0 + n_sc], refs[s0 + n_sc :]
        step = pl.program_id(0)
        for d in range(1, len(grid)):
            step = step * grid[d] + pl.program_id(d)
        n_steps = 1
        for g in grid:
            n_steps *= g

        @pl.when(step == 0)
        def _():
            job.start(jin, jout, jsem)

        if job.middle is not None:
            @pl.when(step == mid)
            def _():
                job.middle(jin, jout, jsem)

        body(*ins, *outs, *sc)

        @pl.when(step == n_steps - 1)
        def _():
            job.finish(jin, jout, jsem)

    outs = pl.pallas_call(
        wrapped, name=name, grid=grid, in_specs=list(in_specs) + [ANY] * ji, out_specs=list(out_specs) + [ANY] * jo,
        out_shape=list(out_shape) + job.out_shapes, scratch_shapes=list(scratch) + job.sems, compiler_params=params,
    )(*args, *job.ins)
    return list(outs[:n_out]), list(outs[n_out:])


def gather_job(items):
    n = len(items)

    def src_half(a, ref, h):
        arr, layer = items[a]
        if layer is None:
            return ref.at[h]
        rh = arr.shape[1] // 2
        return ref.at[layer, pl.ds(h * rh, rh)]

    def dst_half(a, ref, chip, h):
        arr, layer = items[a]
        if layer is None:
            return ref.at[chip, h]
        rh = arr.shape[1] // 2
        return ref.at[chip, pl.ds(h * rh, rh)]

    def copies(src, dst, scratch):
        stage, (in_sem, out_sem, ici_s, ici_r, fwd_s, fwd_r) = scratch[:n], scratch[n:]
        x, y, c, own = _mesh_pos()
        local, ici, fwd, got = [], {}, {}, {}
        for a in range(n):
            layer = items[a][1]
            local.append((pltpu.make_async_copy(src[a] if layer is None else src[a].at[layer], stage[a], in_sem.at[a]),
                          pltpu.make_async_copy(stage[a], dst[a].at[own], out_sem.at[a])))
            for r in (1, 2, 3):
                k = 3 * a + r - 1
                ici[a, r] = pltpu.make_async_remote_copy(
                    src_ref=src_half(a, src[a], c), dst_ref=dst_half(a, dst[a], own, c), send_sem=ici_s.at[k],
                    recv_sem=ici_r.at[k], device_id=_peer(x, y, c, r), device_id_type=MESH)
                fwd[a, r] = pltpu.make_async_remote_copy(
                    src_ref=dst_half(a, dst[a], own ^ r, c), dst_ref=dst_half(a, dst[a], own ^ r, c), send_sem=fwd_s.at[k],
                    recv_sem=fwd_r.at[k], device_id=(x, y, 1 - c), device_id_type=MESH)
                got[a, r] = pltpu.make_async_remote_copy(
                    src_ref=dst_half(a, dst[a], own ^ r, 1 - c), dst_ref=dst_half(a, dst[a], own ^ r, 1 - c),
                    send_sem=fwd_s.at[k], recv_sem=fwd_r.at[k], device_id=(x, y, 1 - c), device_id_type=MESH)
        return local, ici, fwd, got

    def start(src, dst, sems):
        local, ici, _, _ = copies(src, dst, sems)
        for a in range(n):
            for r in (1, 2, 3):
                ici[a, r].start()
        for cp_in, _ in local:
            cp_in.start()
        for cp_in, cp_out in local:
            cp_in.wait()
            cp_out.start()

    def middle(src, dst, sems):
        _, ici, fwd, _ = copies(src, dst, sems)
        for r in (1, 2, 3):
            for a in range(n):
                ici[a, r].wait_recv()
                fwd[a, r].start()

    def finish(src, dst, sems):
        local, ici, fwd, got = copies(src, dst, sems)
        for a in range(n):
            for r in (1, 2, 3):
                got[a, r].wait_recv()
        for a in range(n):
            for r in (1, 2, 3):
                ici[a, r].wait_send()
                fwd[a, r].wait_send()
        for _, cp_out in local:
            cp_out.wait()

    out_shapes = [jax.ShapeDtypeStruct((N_CHIPS,) + (arr.shape if layer is None else arr.shape[1:]), arr.dtype)
                  for arr, layer in items]
    stage = [pltpu.VMEM(arr.shape if layer is None else arr.shape[1:], arr.dtype) for arr, layer in items]
    sems = [pltpu.SemaphoreType.DMA((n,))] * 2 + [pltpu.SemaphoreType.DMA((3 * n,))] * 4
    return Job([arr for arr, _ in items], out_shapes, stage + sems, start, middle, finish)


def _mixer_fwd(proj_ref, lg, lb, ws_ref, bsb_ref, cw_ref, mix_ref, xcbuf, halo_xc):
    u = proj_ref[:, 0 * WIDTH : 1 * WIDTH]
    v = proj_ref[:, 1 * WIDTH : 2 * WIDTH]
    za = proj_ref[:, 2 * WIDTH : 3 * WIDTH]
    h = proj_ref[:, 3 * WIDTH : 4 * WIDTH]
    gb = proj_ref[:, 4 * WIDTH : 5 * WIDTH]
    gc = proj_ref[:, 5 * WIDTH : 6 * WIDTH]
    zb = proj_ref[:, 6 * WIDTH : 7 * WIDTH]
    rs, vhat = _ln_stats(v)
    vl = vhat * lg + lb
    vlb = vl.astype(BF16)
    tile = proj_ref.shape[0]
    for j in range(tile // CHUNK):
        rows = slice(j * CHUNK, (j + 1) * CHUNK)
        for hd in range(HEADS):
            cols = slice(hd * CHUNK, (hd + 1) * CHUNK)
            mix_ref[rows, cols] = _mm(ws_ref[hd], vlb[rows, cols]) + bsb_ref[:, cols]
    mixed = mix_ref[...]
    siga = _sigmoid(za)
    sigb = _sigmoid(zb)
    xc = gc * h
    xcbuf[0:HALO, :] = halo_xc
    xcbuf[HALO : HALO + tile, :] = xc
    y = cw_ref[0:1, :] * xcbuf[HALO - 2 : HALO - 2 + tile, :] + cw_ref[1:2, :] * xcbuf[HALO - 1 : HALO - 1 + tile, :]
    y = y + cw_ref[2:3, :] * xc
    return dict(u=u, za=za, h=h, gb=gb, gc=gc, zb=zb, rs=rs, vhat=vhat, vlb=vlb, mixed=mixed, siga=siga, sigb=sigb,
                xc=xc, y=y)


def fwd_layer(layer, x, p, wts, small, job=None, head=None):
    win, wout, wg, wpp = wts
    ng, lg, lb, ws, bsb, cw, pg = small
    n_head = 0 if head is None else 2

    def body(*refs):
        (x_ref, p_ref, win_ref, wout_ref, wg_ref, wpp_ref, ng_ref, lg_ref, lb_ref, ws_ref, bsb_ref, cw_ref,
         pg_ref) = refs[:13]
        head_in = refs[13 : 13 + n_head]
        proj_ref, x2_ref, gate_ref, x3_ref, hnT_ref, catT_ref, hn2T_ref, pT_ref = refs[13 + n_head : 21 + n_head]
        head_out = refs[21 + n_head : 21 + 2 * n_head]
        mix_ref, xcbuf, carry, wcat, wsem = refs[21 + 2 * n_head :]
        i = pl.program_id(0)

        @pl.when(i == 0)
        def _():
            _load_side_by_side(win_ref, wcat, wsem)
            carry[...] = jnp.zeros_like(carry)
            for ref in head_out:
                ref[...] = jnp.zeros_like(ref)

        xv = x_ref[...]
        _, xh = _rms_stats(xv)
        hn = xh * ng_ref[...]
        hnT_ref[...] = hn.T.astype(BF16)
        proj_ref[...] = _mm(hn.astype(BF16), wcat[...])
        m = _mixer_fwd(proj_ref, lg_ref[...], lb_ref[...], ws_ref, bsb_ref, cw_ref, mix_ref, xcbuf, carry[...])
        carry[...] = m["xc"][FWD_TILE - HALO : FWD_TILE, :]
        out_a = (m["u"] * m["mixed"]) * (m["za"] * m["siga"])
        out_b = (m["gb"] * m["y"]) * (m["zb"] * m["sigb"])
        cat = jnp.concatenate([out_a, out_b], axis=-1)
        catT_ref[...] = cat.T.astype(BF16)
        x2 = xv + _mm_rows(cat.astype(BF16), wout_ref)
        x2_ref[...] = x2
        _, xh2 = _rms_stats(x2)
        hn2 = xh2 * pg_ref[...]
        hn2T_ref[...] = hn2.T.astype(BF16)
        gate = _sigmoid(_mm_rows(hn2.astype(BF16), wg_ref))
        gate_ref[...] = gate
        pv = p_ref[...]
        pT_ref[...] = pv.T.astype(BF16)
        x3 = x2 + gate * _ple_proj(pv, wpp_ref)
        if head is None:
            x3_ref[...] = x3
        else:
            t_ref, gf_ref = head_in
            loss_ref, gg_ref = head_out
            r3, xh3 = _rms_stats(x3)
            gf = gf_ref[...]
            err = xh3 * gf - t_ref[...]
            loss_ref[...] += (0.5 / D_MODEL) * jnp.sum(err * err).reshape(1, 1)
            dy = err * (1.0 / D_MODEL)
            gg_ref[...] += jnp.sum(dy * xh3, axis=0, keepdims=True)
            x3_ref[...] = _rms_bwd(dy * gf, xh3, r3)

    tok = lambda w: pl.BlockSpec((FWD_TILE, w), lambda i: (i, 0))
    tokT = lambda w: pl.BlockSpec((None, w, FWD_TILE), lambda i: (i, 0, 0))
    once = lambda a: pl.BlockSpec(a.shape, lambda i: (0,) * a.ndim, pipeline_mode=pl.Buffered(1))
    f32 = lambda w: jax.ShapeDtypeStruct((SEQ, w), F32)
    bfT = lambda w: jax.ShapeDtypeStruct((SEQ // FWD_TILE, w, FWD_TILE), BF16)
    head_specs = [] if head is None else [tok(D_MODEL), _full((1, D_MODEL))]
    head_outs = [] if head is None else [_full((1, 1)), _full((1, D_MODEL))]
    head_shapes = [] if head is None else [jax.ShapeDtypeStruct((1, 1), F32), jax.ShapeDtypeStruct((1, D_MODEL), F32)]
    return _call(
        body, name=f"fwd_layer{layer}", grid=(SEQ // FWD_TILE,),
        in_specs=[tok(D_MODEL), pl.BlockSpec((None, None, FWD_TILE, PLE), lambda i: (layer, 0, i, 0)),
                  ANY, once(wout), once(wg), once(wpp),
                  _const((1, D_MODEL), (layer,)), _const((1, WIDTH), (layer,)), _const((1, WIDTH), (layer,)),
                  _const((HEADS, CHUNK, CHUNK), (layer,)), _const((CHUNK, WIDTH), (layer,)), _const((3, WIDTH), (layer,)),
                  _const((1, D_MODEL), (layer,))] + head_specs,
        out_specs=[tok(PROJ), tok(D_MODEL), tok(D_MODEL), tok(D_MODEL),
                   tokT(D_MODEL), tokT(D_MODEL), tokT(D_MODEL), tokT(PLE)] + head_outs,
        out_shape=[f32(PROJ), f32(D_MODEL), f32(D_MODEL), f32(D_MODEL),
                   bfT(D_MODEL), bfT(D_MODEL), bfT(D_MODEL), bfT(PLE)] + head_shapes,
        scratch=[pltpu.VMEM((FWD_TILE, WIDTH), F32), pltpu.VMEM((HALO + FWD_TILE, WIDTH), F32), pltpu.VMEM((HALO, WIDTH), F32),
                 pltpu.VMEM((D_MODEL, PROJ), BF16), pltpu.SemaphoreType.DMA((N_CHIPS,))],
        args=(x, p, win, wout, wg, wpp, ng, lg, lb, ws, bsb, cw, pg) + (() if head is None else tuple(head)),
        job=job, mid=SEQ // FWD_TILE // 2)


def bwd_layer(layer, dx3, x2, gate, p, x, proj, wts, small, wsT, tril):
    win, wout, wg, wpp = wts
    ng, lg, lb, ws, bsb, cw, pg = small

    def body(dx3_ref, x2_ref, gate_ref, p_ref, x_ref, proj_ref, halo_ref, win_ref, wout_ref, wg_ref, wpp_ref, ng_ref, lg_ref,
             lb_ref, ws_ref, wsT_ref, bsb_ref, cw_ref, pg_ref, tril_ref,
             dx_ref, dproj_ref, dpp_ref, dgl_ref, dx2b_ref, gn_ref, glg_ref, glb_ref, gws_ref, gbs_ref, gcw_ref, gpg_ref,
             mix_ref, xcbuf, dycbuf, dvl_ref, bs_acc, wcat, wsem):
        i = pl.program_id(0)

        @pl.when(i == 0)
        def _():
            _load_side_by_side(win_ref, wcat, wsem)
            for ref in (gn_ref, glg_ref, glb_ref, gws_ref, gcw_ref, gpg_ref, bs_acc):
                ref[...] = jnp.zeros_like(ref)
            dycbuf[TILE : TILE + HALO, :] = jnp.zeros((HALO, WIDTH), F32)

        dx3v = dx3_ref[...]
        gate_v = gate_ref[...]
        dpp_ref[...] = (dx3v * gate_v).astype(BF16)
        dgl = ((dx3v * _ple_proj(p_ref[...], wpp_ref)) * gate_v * (1.0 - gate_v)).astype(BF16)
        dgl_ref[...] = dgl
        dhn2 = _mm_nt_rows(dgl, wg_ref)
        r2, xh2 = _rms_stats(x2_ref[...])
        gpg_ref[...] += jnp.sum(dhn2 * xh2, axis=0, keepdims=True)
        dx2 = dx3v + _rms_bwd(dhn2 * pg_ref[...], xh2, r2)
        dx2b = dx2.astype(BF16)
        dx2b_ref[...] = dx2b
        dcat = _mm_nt_rows(dx2b, wout_ref)
        lgv = lg_ref[...]
        halo_xc = halo_ref[:, 5 * WIDTH : 6 * WIDTH] * halo_ref[:, 3 * WIDTH : 4 * WIDTH]
        halo_xc = jnp.where(i == N_TILES - 1, 0.0, halo_xc)
        m = _mixer_fwd(proj_ref, lgv, lb_ref[...], ws_ref, bsb_ref, cw_ref, mix_ref, xcbuf, halo_xc)
        u, za, h, gb, gc, zb = m["u"], m["za"], m["h"], m["gb"], m["gc"], m["zb"]
        mixed, siga, sigb, xc, y = m["mixed"], m["siga"], m["sigb"], m["xc"], m["y"]
        doa = dcat[:, 0:WIDTH]
        dob = dcat[:, WIDTH : 2 * WIDTH]
        sa = za * siga
        sb = zb * sigb
        doa_sa = doa * sa
        dproj_ref[:, 0 * WIDTH : 1 * WIDTH] = (doa_sa * mixed).astype(BF16)
        dmixed = doa_sa * u
        dza = (doa * (u * mixed)) * (siga * (1.0 + za * (1.0 - siga)))
        dproj_ref[:, 2 * WIDTH : 3 * WIDTH] = dza.astype(BF16)
        dob_sb = dob * sb
        dproj_ref[:, 4 * WIDTH : 5 * WIDTH] = (dob_sb * y).astype(BF16)
        dyc = dob_sb * gb
        dzb = (dob * (gb * y)) * (sigb * (1.0 + zb * (1.0 - sigb)))
        dproj_ref[:, 6 * WIDTH : 7 * WIDTH] = dzb.astype(BF16)
        dycbuf[0:TILE, :] = dyc
        dyc1 = dycbuf[1 : 1 + TILE, :]
        dyc2 = dycbuf[2 : 2 + TILE, :]
        dxc = cw_ref[2:3, :] * dyc + cw_ref[1:2, :] * dyc1 + cw_ref[0:1, :] * dyc2
        gcw_ref[0:1, :] += jnp.sum(xc * dyc2, axis=0, keepdims=True)
        gcw_ref[1:2, :] += jnp.sum(xc * dyc1, axis=0, keepdims=True)
        gcw_ref[2:3, :] += jnp.sum(xc * dyc, axis=0, keepdims=True)
        dycbuf[TILE : TILE + HALO, :] = dyc[0:HALO, :]
        dproj_ref[:, 5 * WIDTH : 6 * WIDTH] = (dxc * h).astype(BF16)
        dproj_ref[:, 3 * WIDTH : 4 * WIDTH] = (dxc * gc).astype(BF16)
        dmb = dmixed.astype(BF16)
        vlb = m["vlb"]
        bsum = jnp.zeros((CHUNK, WIDTH), F32)
        for j in range(TILE // CHUNK):
            rows = slice(j * CHUNK, (j + 1) * CHUNK)
            bsum = bsum + dmixed[rows, :]
            for hd in range(HEADS):
                cols = slice(hd * CHUNK, (hd + 1) * CHUNK)
                gws_ref[hd] += _mm_nt(dmb[rows, cols], vlb[rows, cols])
                dvl_ref[rows, cols] = _mm(wsT_ref[hd], dmb[rows, cols])
        bs_acc[...] += bsum
        dvl = dvl_ref[...]
        vhat = m["vhat"]
        glb_ref[...] += jnp.sum(dvl, axis=0, keepdims=True)
        glg_ref[...] += jnp.sum(dvl * vhat, axis=0, keepdims=True)
        dvh = dvl * lgv
        dv = m["rs"] * (dvh - jnp.mean(dvh, axis=-1, keepdims=True) - vhat * jnp.mean(dvh * vhat, axis=-1, keepdims=True))
        dproj_ref[:, 1 * WIDTH : 2 * WIDTH] = dv.astype(BF16)
        dhn = _mm_nt(dproj_ref[...], wcat[...])
        r1, xh = _rms_stats(x_ref[...])
        gn_ref[...] += jnp.sum(dhn * xh, axis=0, keepdims=True)
        dx_ref[...] = dx2 + _rms_bwd(dhn * ng_ref[...], xh, r1)

        @pl.when(i == N_TILES - 1)
        def _():
            for hd in range(HEADS):
                gws_ref[hd] = gws_ref[hd] * tril_ref[...]
                gbs_ref[hd : hd + 1, :] = jnp.sum(bs_acc[:, hd * CHUNK : (hd + 1) * CHUNK].T, axis=0, keepdims=True)

    rev = lambda w: pl.BlockSpec((TILE, w), lambda i: (N_TILES - 1 - i, 0))
    halo = pl.BlockSpec((HALO, PROJ), lambda i: (jnp.maximum((N_TILES - 1 - i) * (TILE // HALO) - 1, 0), 0))
    once = lambda a: pl.BlockSpec(a.shape, lambda i: (0,) * a.ndim, pipeline_mode=pl.Buffered(1))
    vec = lambda w: jax.ShapeDtypeStruct((1, w), F32)
    b16 = lambda w: jax.ShapeDtypeStruct((SEQ, w), BF16)
    outs, _ = _call(
        body, name=f"bwd_layer{layer}", grid=(N_TILES,),
        in_specs=[rev(D_MODEL), rev(D_MODEL), rev(D_MODEL),
                  pl.BlockSpec((None, None, TILE, PLE), lambda i: (layer, 0, N_TILES - 1 - i, 0)),
                  rev(D_MODEL), rev(PROJ), halo, ANY, once(wout), once(wg), once(wpp),
                  _const((1, D_MODEL), (layer,)), _const((1, WIDTH), (layer,)), _const((1, WIDTH), (layer,)),
                  _const((HEADS, CHUNK, CHUNK), (layer,)), _const((HEADS, CHUNK, CHUNK), (layer,)),
                  _const((CHUNK, WIDTH), (layer,)), _const((3, WIDTH), (layer,)), _const((1, D_MODEL), (layer,)),
                  _full((CHUNK, CHUNK))],
        out_specs=[rev(D_MODEL), rev(PROJ), rev(D_MODEL), rev(D_MODEL), rev(D_MODEL),
                   _full((1, D_MODEL)), _full((1, WIDTH)), _full((1, WIDTH)), _full((HEADS, CHUNK, CHUNK)),
                   _full((HEADS, CHUNK)), _full((3, WIDTH)), _full((1, D_MODEL))],
        out_shape=[jax.ShapeDtypeStruct((SEQ, D_MODEL), F32), b16(PROJ), b16(D_MODEL), b16(D_MODEL), b16(D_MODEL),
                   vec(D_MODEL), vec(WIDTH), vec(WIDTH), jax.ShapeDtypeStruct((HEADS, CHUNK, CHUNK), F32),
                   jax.ShapeDtypeStruct((HEADS, CHUNK), F32), jax.ShapeDtypeStruct((3, WIDTH), F32), vec(D_MODEL)],
        scratch=[pltpu.VMEM((TILE, WIDTH), F32), pltpu.VMEM((HALO + TILE, WIDTH), F32),
                 pltpu.VMEM((TILE + HALO, WIDTH), F32), pltpu.VMEM((TILE, WIDTH), F32), pltpu.VMEM((CHUNK, WIDTH), F32),
                 pltpu.VMEM((D_MODEL, PROJ), BF16), pltpu.SemaphoreType.DMA((N_CHIPS,))],
        args=(dx3, x2, gate, p, x, proj, proj, win, wout, wg, wpp, ng, lg, lb, ws, wsT, bsb, cw, pg, tril))
    return outs


def wgrad(pairs, n_tiles, col_blocked, tk, name, job=None):
    n = len(pairs)
    m_dim, n_dim, at = pairs[0][0].shape[1], pairs[0][1].shape[1], pairs[0][0].shape[2]
    tn = n_dim // n_tiles
    cb, mb = n_dim // N_CHIPS, m_dim // N_CHIPS
    per_tile = N_CHIPS // n_tiles
    assert col_blocked or n_tiles == 1

    def body(*refs):
        for a in range(n):
            a_ref, b_ref, o_ref = refs[2 * a], refs[2 * a + 1], refs[2 * n + a]

            @pl.when(pl.program_id(1) == 0)
            def _():
                o_ref[...] = jnp.zeros_like(o_ref)

            prod = _mm(a_ref[0], b_ref[0:at, :])
            for t in range(1, tk // at):
                prod = prod + _mm(a_ref[t], b_ref[t * at : (t + 1) * at, :])
            if col_blocked:
                for q in range(per_tile):
                    o_ref[q] += prod[:, q * cb : (q + 1) * cb]
            else:
                for q in range(N_CHIPS):
                    o_ref[q] += prod[q * mb : (q + 1) * mb, :]

            @pl.when(pl.program_id(1) == SEQ // tk - 1)
            def _():
                refs[3 * n + a][...] = o_ref[...].astype(BF16)

    if col_blocked:
        shape = (N_CHIPS, m_dim, cb)
        o_spec = pl.BlockSpec((per_tile, m_dim, cb), lambda j, k: (j, 0, 0))
    else:
        shape = (N_CHIPS, mb, n_dim)
        o_spec = pl.BlockSpec((N_CHIPS, mb, n_dim), lambda j, k: (0, 0, 0))
    outs, job_outs = _call(
        body, name=name, grid=(n_tiles, SEQ // tk),
        in_specs=[pl.BlockSpec((tk // at, m_dim, at), lambda j, k: (k, 0, 0)),
                  pl.BlockSpec((tk, tn), lambda j, k: (k, j))] * n,
        out_specs=[o_spec] * (2 * n), out_shape=[jax.ShapeDtypeStruct(shape, F32)] * n + [jax.ShapeDtypeStruct(shape, BF16)] * n,
        args=[t for pair in pairs for t in pair], job=job)
    return outs[:n], outs[n:], job_outs


def to_sibling_job(grads):
    n = len(grads)

    def copies(src, dst, sems):
        x, y, c, _ = _mesh_pos()
        out = []
        for a in range(n):
            rh = grads[a].shape[1] // 2
            out.append(pltpu.make_async_remote_copy(
                src_ref=src[a].at[:, pl.ds((1 - c) * rh, rh), :], dst_ref=dst[a], send_sem=sems[0].at[a],
                recv_sem=sems[1].at[a], device_id=(x, y, 1 - c), device_id_type=MESH))
        return out

    def start(src, dst, sems):
        for cp in copies(src, dst, sems):
            cp.start()

    def finish(src, dst, sems):
        for cp in copies(src, dst, sems):
            cp.wait()

    out_shapes = [jax.ShapeDtypeStruct((N_CHIPS, g.shape[1] // 2, g.shape[2]), g.dtype) for g in grads]
    return Job(grads, out_shapes, [pltpu.SemaphoreType.DMA((n,))] * 2, start, None, finish)


def rs_add_sibling(ids, grads, recvd, name):
    n = len(grads)

    def body(ids_ref, *refs):
        g, r = refs[:n], refs[n : 2 * n]
        pb, pf = refs[2 * n : 3 * n], refs[3 * n :]
        k = pl.program_id(0)
        for a in range(n):
            s = g[a][...] + r[a][...].astype(F32)
            pb[a][...] = s.astype(BF16)

            @pl.when(k == ids_ref[1])
            def _():
                pf[a][...] = s

    in_specs, out_specs, out_shape = [], [], []
    for g in grads:
        rh, cc = g.shape[1] // 2, g.shape[2]
        in_specs.append(pl.BlockSpec((None, rh, cc), lambda k, ids: (k, ids[0], 0)))
    for g in grads:
        rh, cc = g.shape[1] // 2, g.shape[2]
        in_specs.append(pl.BlockSpec((None, rh, cc), lambda k, ids: (k, 0, 0)))
        out_specs.append(pl.BlockSpec((None, rh, cc), lambda k, ids: (k, 0, 0)))
        out_shape.append(jax.ShapeDtypeStruct((N_CHIPS, rh, cc), BF16))
    for g in grads:
        rh, cc = g.shape[1] // 2, g.shape[2]
        out_specs.append(pl.BlockSpec((rh, cc), lambda k, ids: (0, 0)))
        out_shape.append(jax.ShapeDtypeStruct((rh, cc), F32))
    outs = pl.pallas_call(
        body, name=name, out_shape=out_shape,
        grid_spec=pltpu.PrefetchScalarGridSpec(num_scalar_prefetch=1, grid=(N_CHIPS,), in_specs=in_specs,
                                               out_specs=out_specs),
        compiler_params=pltpu.CompilerParams(dimension_semantics=("arbitrary",), vmem_limit_bytes=VMEM_LIMIT),
    )(ids, *grads, *recvd)
    return list(outs[:n]), list(outs[n:])


def to_owners_job(partials):
    n = len(partials)

    def copies(src, dst, sems):
        x, y, c, own = _mesh_pos()
        out = []
        for a in range(n):
            for r in (1, 2, 3):
                out.append(pltpu.make_async_remote_copy(
                    src_ref=src[a].at[own ^ r], dst_ref=dst[a].at[r - 1], send_sem=sems[0].at[3 * a + r - 1],
                    recv_sem=sems[1].at[3 * a + r - 1], device_id=_peer(x, y, c, r), device_id_type=MESH))
        return out

    def start(src, dst, sems):
        for cp in copies(src, dst, sems):
            cp.start()

    def finish(src, dst, sems):
        for cp in copies(src, dst, sems):
            cp.wait()

    out_shapes = [jax.ShapeDtypeStruct((3,) + p.shape[1:], BF16) for p in partials]
    return Job(partials, out_shapes, [pltpu.SemaphoreType.DMA((3 * n,))] * 2, start, None, finish)


def rs_add_owners(layer, ids, own_f32, recvd, prev, name):
    n = len(own_f32)
    nb = 2

    def body(ids_ref, *refs):
        o, r, f = refs[:n], refs[n : 2 * n], refs[-n:]
        for a in range(n):
            f[a][...] = ((o[a][...] + r[a][0].astype(F32)) + r[a][1].astype(F32)) + r[a][2].astype(F32)

    in_specs, out_specs, out_shape = [], [], []
    for o in own_f32:
        in_specs.append(pl.BlockSpec((o.shape[0] // nb, o.shape[1]), lambda j, ids: (j, 0)))
    for o in own_f32:
        in_specs.append(pl.BlockSpec((3, o.shape[0] // nb, o.shape[1]), lambda j, ids: (0, j, 0)))
        out_specs.append(pl.BlockSpec((None, o.shape[0] // nb, o.shape[1]), lambda j, ids: (layer, ids[0] * nb + j, 0)))
        out_shape.append(jax.ShapeDtypeStruct((DEPTH, 2 * o.shape[0], o.shape[1]), F32))
    args, aliases = [ids, *own_f32, *recvd], {}
    if prev is not None:
        in_specs += [ANY] * n
        args += list(prev)
        aliases = {1 + 2 * n + a: a for a in range(n)}
    return pl.pallas_call(
        body, name=name, out_shape=out_shape, input_output_aliases=aliases,
        grid_spec=pltpu.PrefetchScalarGridSpec(num_scalar_prefetch=1, grid=(nb,), in_specs=in_specs, out_specs=out_specs),
        compiler_params=pltpu.CompilerParams(dimension_semantics=("arbitrary",), vmem_limit_bytes=VMEM_LIMIT),
    )(*args)


def exchange_halves_job(layer, full):
    n = len(full)

    def copies(src, dst, sems):
        x, y, c, _ = _mesh_pos()
        out = []
        for a in range(n):
            rh = full[a].shape[1] // 2
            out.append(pltpu.make_async_remote_copy(
                src_ref=src[a].at[layer, pl.ds(c * rh, rh), :], dst_ref=dst[a].at[layer, pl.ds(c * rh, rh), :],
                send_sem=sems[0].at[a], recv_sem=sems[1].at[a], device_id=(x, y, 1 - c), device_id_type=MESH))
        return out

    def start(src, dst, sems):
        for cp in copies(src, dst, sems):
            cp.start()

    def finish(src, dst, sems):
        for cp in copies(src, dst, sems):
            cp.wait()

    out_shapes = [jax.ShapeDtypeStruct(f.shape, F32) for f in full]
    return Job(full, out_shapes, [pltpu.SemaphoreType.DMA((n,))] * 2, start, None, finish, {a: a for a in range(n)})


def _adamw(w, g, m, v):
    m2 = ADAM_B1 * m + (1.0 - ADAM_B1) * g
    v2 = ADAM_B2 * v + (1.0 - ADAM_B2) * (g * g)
    delta = -ADAM_LR * ((m2 / ADAM_C1) / (jnp.sqrt(v2 / ADAM_C2) + ADAM_EPS) + ADAM_WD * w)
    return delta, m2, v2


def adamw_big(ws, gs, ms, vs, name, job=None):
    n = len(ws)
    nb = 4

    def body(*refs):
        for a in range(n):
            w, g, m, v = (refs[j * n + a][...] for j in range(4))
            d, m2, v2 = _adamw(w, g, m, v)
            refs[4 * n + a][...] = d
            refs[5 * n + a][...] = m2
            refs[6 * n + a][...] = v2

    specs = [pl.BlockSpec((None, w.shape[1] // nb, w.shape[2]), lambda l, j: (l, j, 0)) for w in ws]
    shapes = [jax.ShapeDtypeStruct(w.shape, F32) for w in ws]
    outs, job_outs = _call(body, name=name, grid=(DEPTH, nb), in_specs=specs * 4, out_specs=specs * 3,
                           out_shape=shapes * 3, args=(*ws, *gs, *ms, *vs), job=job)
    return (outs[:n], outs[n : 2 * n], outs[2 * n :]), job_outs


def small_to_sibling_job(gs):
    n = len(gs)

    def copies(src, dst, sems):
        x, y, c, _ = _mesh_pos()
        return [pltpu.make_async_remote_copy(src_ref=src[a], dst_ref=dst[a], send_sem=sems[0].at[a], recv_sem=sems[1].at[a],
                                             device_id=(x, y, 1 - c), device_id_type=MESH) for a in range(n)]

    def start(*refs):
        for cp in copies(*refs):
            cp.start()

    def finish(*refs):
        for cp in copies(*refs):
            cp.wait()

    return Job(gs, [jax.ShapeDtypeStruct(g.shape, F32) for g in gs], [pltpu.SemaphoreType.DMA((n,))] * 2, start, None, finish)


def small_chip_sum(gs, recvd):
    n = len(gs)

    def body(*refs):
        for a in range(n):
            refs[2 * n + a][...] = refs[a][...] + refs[n + a][...]

    return pl.pallas_call(body, name="small_chip_sum", out_shape=[jax.ShapeDtypeStruct(g.shape, F32) for g in gs])(*gs, *recvd)


def small_to_chips_job(chip_sums):
    n = len(chip_sums)

    def copies(src, dst, sems):
        x, y, c, _ = _mesh_pos()
        return [pltpu.make_async_remote_copy(src_ref=src[a], dst_ref=dst[a].at[r - 1], send_sem=sems[0].at[3 * a + r - 1],
                                             recv_sem=sems[1].at[3 * a + r - 1], device_id=_peer(x, y, c, r),
                                             device_id_type=MESH) for a in range(n) for r in (1, 2, 3)]

    def start(*refs):
        for cp in copies(*refs):
            cp.start()

    def finish(*refs):
        for cp in copies(*refs):
            cp.wait()

    return Job(chip_sums, [jax.ShapeDtypeStruct((3,) + s.shape, F32) for s in chip_sums],
               [pltpu.SemaphoreType.DMA((3 * n,))] * 2, start, None, finish)


def small_finish(chip_sums, from_chips, ws, ms, vs):
    n, n_w = len(chip_sums), len(ws)

    def body(*refs):
        cs, fc = refs[:n], refs[n : 2 * n]
        w, m, v = (refs[2 * n + j * n_w : 2 * n + (j + 1) * n_w] for j in range(3))
        outs = refs[2 * n + 3 * n_w :]
        own = 2 * lax.axis_index("x") + lax.axis_index("y")
        for a in range(n):
            mine = cs[a][...]

            def of_chip(k):
                r = own ^ k
                return jnp.where(r == 0, mine, jnp.where(r == 1, fc[a][0], jnp.where(r == 2, fc[a][1], fc[a][2])))

            tot = ((of_chip(0) + of_chip(1)) + of_chip(2)) + of_chip(3)
            outs[a][...] = tot
            if a < n_w:
                d, m2, v2 = _adamw(w[a][...], tot, m[a][...], v[a][...])
                outs[n + a][...] = d
                outs[n + n_w + a][...] = m2
                outs[n + 2 * n_w + a][...] = v2

    shapes = [jax.ShapeDtypeStruct(s.shape, F32) for s in chip_sums]
    outs = pl.pallas_call(body, name="small_finish", out_shape=shapes + shapes[:n_w] * 3)(
        *chip_sums, *from_chips, *ws, *ms, *vs)
    return outs[:n], outs[n : n + n_w], outs[n + n_w : n + 2 * n_w], outs[n + 2 * n_w :]


def merge_jobs(*jobs):
    def parts(refs, counts):
        out, pos = [], 0
        for cnt in counts:
            out.append(refs[pos : pos + cnt])
            pos += cnt
        return out

    def run(phase):
        def go(ins, outs, sems):
            for job, i, o, s in zip(jobs, parts(ins, [len(j.ins) for j in jobs]),
                                    parts(outs, [len(j.out_shapes) for j in jobs]), parts(sems, [len(j.sems) for j in jobs])):
                getattr(job, phase)(i, o, s)
        return go

    assert all(j.middle is None and not j.aliases for j in jobs)
    return Job([a for j in jobs for a in j.ins], [a for j in jobs for a in j.out_shapes], [a for j in jobs for a in j.sems],
               run("start"), None, run("finish"))


def adamw_rows(w, g, m, v):
    def body(w_ref, g_ref, m_ref, v_ref, d_ref, m2_ref, v2_ref):
        d, m2, v2 = _adamw(w_ref[...], g_ref[...], m_ref[...], v_ref[...])
        d_ref[...] = d
        m2_ref[...] = m2
        v2_ref[...] = v2

    return pl.pallas_call(body, name="adamw_conv", out_shape=[jax.ShapeDtypeStruct(w.shape, F32)] * 3)(w, g, m, v)


def kernel(x, p, norm_g, w_in, ln_v_g, ln_v_b, w_s, b_s, conv_w, w_out, ple_norm_g, w_ple_gate, w_ple_proj, final_g, loss_target, m_norm_g, m_w_in, m_ln_v_g, m_ln_v_b, m_w_s, m_b_s, m_conv_w, m_w_out, m_ple_norm_g, m_w_ple_gate, m_w_ple_proj, m_final_g, v_norm_g, v_w_in, v_ln_v_g, v_ln_v_b, v_w_s, v_b_s, v_conv_w, v_w_out, v_ple_norm_g, v_w_ple_gate, v_w_ple_proj, v_final_g):
    cx, cy, cc = lax.axis_index("x"), lax.axis_index("y"), lax.axis_index("c")
    own = 2 * cx + cy
    ids = jnp.stack([cc, own]).astype(jnp.int32)

    cw_rows = jnp.transpose(conv_w, (0, 2, 1))
    shards = [w_in.astype(BF16), w_out.astype(BF16), w_ple_gate.astype(BF16), w_ple_proj.astype(BF16)]
    *wts0, g_cw = run_job(gather_job([(s, 0) for s in shards] + [(cw_rows, None)]), "gather_weights0")
    cw_full = jnp.transpose(g_cw, (1, 2, 0, 3)).reshape(DEPTH, 3, WIDTH)
    tril = jnp.tril(jnp.ones((CHUNK, CHUNK), F32))
    ws_masked = w_s * tril[None, None]
    ws_b = ws_masked.astype(BF16)
    wsT_b = jnp.swapaxes(ws_masked, 2, 3).astype(BF16)
    bsb = jnp.repeat(jnp.swapaxes(b_s, 1, 2), CHUNK, axis=2)
    small = (norm_g[:, None, :], ln_v_g[:, None, :], ln_v_b[:, None, :], ws_b, bsb, cw_full, ple_norm_g[:, None, :])

    saved0, wts1 = fwd_layer(0, x[0], p, wts0, small, job=gather_job([(s, 1) for s in shards]))
    saved1, _ = fwd_layer(1, saved0[3], p, wts1, small, head=(loss_target[0], final_g[None, :]))
    saved, xs = [saved0[:8], saved1[:8]], [x[0], saved0[3]]
    dx, loss_part, g_final = saved1[3], saved1[8], saved1[9]

    def chip_sums(arrs, arrs_bf16, tag):
        recvd = run_job(to_sibling_job(arrs_bf16), f"rs_to_sibling{tag}")
        return rs_add_sibling(ids, arrs, recvd, f"rs_add_sibling{tag}")

    small_g = [None] * DEPTH
    proj, x2, gate, _, hnT, catT, hn2T, pT = saved[1]
    dx, dproj_b, dpp_b, dgl_b, dx2_b, *g_sm = bwd_layer(1, dx, x2, gate, p, xs[1], proj, wts1, small, wsT_b, tril)
    small_g[1] = g_sm[:5] + [g_sm[6], g_sm[5]]
    g1_in, b1_in, _ = wgrad([(hnT, dproj_b)], 2, True, 1024, "wgrad_in1")
    g1_og, b1_og, _ = wgrad([(catT, dx2_b), (hn2T, dgl_b)], 1, False, 1024, "wgrad_outgate1")
    g1_pp, b1_pp, _ = wgrad([(pT, dpp_b)], 1, True, 2048, "wgrad_pp1")
    part1 = chip_sums(g1_in + g1_og + g1_pp, b1_in + b1_og + b1_pp, "1")
    proj, x2, gate, _, hnT, catT, hn2T, pT = saved[0]
    dx, dproj_b, dpp_b, dgl_b, dx2_b, *g_sm = bwd_layer(0, dx, x2, gate, p, xs[0], proj, wts0, small, wsT_b, tril)
    small_g[0] = g_sm[:5] + [g_sm[6], g_sm[5]]
    def both(j):
        return jnp.concatenate([small_g[0][j], small_g[1][j]], axis=0)

    g_small = [both(0), both(1), both(2), both(3).reshape(DEPTH * HEADS * CHUNK, CHUNK), both(4), both(5), g_final, both(6),
               jnp.broadcast_to(loss_part, (1, 128))]
    g0_in, b0_in, landed = wgrad([(hnT, dproj_b)], 2, True, 1024, "wgrad_in0",
                                 job=merge_jobs(to_owners_job(part1[0]), small_to_sibling_job(g_small)))
    from1, small_sib = landed[:4], landed[4:]
    full = rs_add_owners(1, ids, part1[1], from1, None, "rs_add_owners1")
    full = run_job(exchange_halves_job(1, full), "rs_exchange_halves1")
    small_chip = small_chip_sum(g_small, small_sib)
    part0b = chip_sums(g0_in, b0_in, "0b")
    g0_og, b0_og, from0b = wgrad([(catT, dx2_b), (hn2T, dgl_b)], 1, False, 1024, "wgrad_outgate0",
                                 job=to_owners_job(part0b[0]))
    full_b = rs_add_owners(0, ids, part0b[1], from0b, full[:1], "rs_add_owners0b")
    g0_pp, b0_pp, small_chips = wgrad([(pT, dpp_b)], 1, True, 2048, "wgrad_pp0", job=small_to_chips_job(small_chip))
    part0a = chip_sums(g0_og + g0_pp, b0_og + b0_pp, "0a")
    from0a = run_job(to_owners_job(part0a[0]), "rs_to_owners0a")
    full_a = rs_add_owners(0, ids, part0a[1], from0a, full[1:], "rs_add_owners0a")
    gw_in, gw_out, gw_gate, gw_pp = run_job(exchange_halves_job(0, list(full_b) + list(full_a)), "rs_exchange_halves0")
    ((d_in, d_out, d_gate, d_pp), (m_in, m_out, m_gate, m_pp), (v_in, v_out, v_gate, v_pp)), _ = adamw_big(
        [w_in, w_out, w_ple_gate, w_ple_proj], [gw_in, gw_out, gw_gate, gw_pp],
        [m_w_in, m_w_out, m_w_ple_gate, m_w_ple_proj], [v_w_in, v_w_out, v_w_ple_gate, v_w_ple_proj], "adamw_big")

    flat = lambda a: a.reshape(-1, a.shape[-1])
    gsum, dsm, msm, vsm = small_finish(
        small_chip, small_chips, [flat(a) for a in (norm_g, ln_v_g, ln_v_b, w_s, b_s, ple_norm_g, final_g[None])],
        [flat(a) for a in (m_norm_g, m_ln_v_g, m_ln_v_b, m_w_s, m_b_s, m_ple_norm_g, m_final_g[None])],
        [flat(a) for a in (v_norm_g, v_ln_v_g, v_ln_v_b, v_w_s, v_b_s, v_ple_norm_g, v_final_g[None])])
    like = [norm_g, ln_v_g, ln_v_b, w_s, b_s, ple_norm_g, final_g]
    gs, ds, m2s, v2s = ([a.reshape(b.shape) for a, b in zip(res, like)] for res in (gsum, dsm, msm, vsm))
    loss = gsum[8][0, 0]
    g_cw_own = lax.dynamic_slice_in_dim(gsum[7].reshape(DEPTH, 3, WIDTH), own * (WIDTH // N_CHIPS), WIDTH // N_CHIPS, axis=2)
    rows2 = lambda a: a.reshape(DEPTH * 3, WIDTH // N_CHIPS)
    d_cw, m_cw, v_cw = adamw_rows(rows2(cw_rows), rows2(g_cw_own), rows2(jnp.transpose(m_conv_w, (0, 2, 1))),
                                  rows2(jnp.transpose(v_conv_w, (0, 2, 1))))
    back = lambda a: jnp.transpose(a.reshape(DEPTH, 3, WIDTH // N_CHIPS), (0, 2, 1))
    g_conv = jnp.transpose(g_cw_own, (0, 2, 1))

    def ordered(sm, cw_v, w_in_v, w_out_v, gate_v, pp_v):
        return [sm[0], w_in_v, sm[1], sm[2], sm[3], sm[4], cw_v, w_out_v, sm[5], gate_v, pp_v, sm[6]]

    grads = ordered(gs, g_conv, gw_in, gw_out, gw_gate, gw_pp)
    deltas = ordered(ds, back(d_cw), d_in, d_out, d_gate, d_pp)
    new_m = ordered(m2s, back(m_cw), m_in, m_out, m_gate, m_pp)
    new_v = ordered(v2s, back(v_cw), v_in, v_out, v_gate, v_pp)
    return (loss, dx[None], *grads, *deltas, *new_m, *new_v)
```

```python
import jax
import jax.numpy as jnp
from jax import lax
from jax.experimental import pallas as pl
from jax.experimental.pallas import tpu as pltpu

F32 = jnp.float32
BF16 = jnp.bfloat16

SEQ = 8192
D_MODEL = 1024
WIDTH = 512
PROJ = 7 * WIDTH
N_CHIPS = 4
COL_BLK = PROJ // N_CHIPS
PLE = 256
HEADS = 4
CHUNK = 128
DEPTH = 2
EPS = 1e-6
TILE = 256
N_TILES = SEQ // TILE
FWD_TILE = 512
HALO = 8
VMEM_LIMIT = 60 * 1024 * 1024

ADAM_LR, ADAM_B1, ADAM_B2, ADAM_EPS, ADAM_WD, ADAM_STEP = 0.001, 0.9, 0.999, 1e-08, 0.01, 10
ADAM_C1 = 1.0 - ADAM_B1**ADAM_STEP
ADAM_C2 = 1.0 - ADAM_B2**ADAM_STEP

MESH = pl.DeviceIdType.MESH
ANY = pl.BlockSpec(memory_space=pl.ANY)


def _mm(a, b):
    return lax.dot_general(a, b, (((1,), (0,)), ((), ())), preferred_element_type=F32)


def _mm_nt(a, b):
    return lax.dot_general(a, b, (((1,), (1,)), ((), ())), preferred_element_type=F32)


def _mm_rows(a, w_ref):
    blk = w_ref.shape[1]
    acc = _mm(a[:, 0:blk], w_ref[0])
    for k in range(1, N_CHIPS):
        acc = acc + _mm(a[:, k * blk : (k + 1) * blk], w_ref[k])
    return acc


def _ple_proj(p_tile, wpp_ref):
    pb = p_tile.astype(BF16)
    return jnp.concatenate([_mm(pb, wpp_ref[k]) for k in range(N_CHIPS)], axis=-1)


def _mm_nt_rows(a, w_ref):
    return jnp.concatenate([_mm_nt(a, w_ref[k]) for k in range(N_CHIPS)], axis=-1)


def _load_side_by_side(w_hbm, w_vmem, sems):
    copies = [pltpu.make_async_copy(w_hbm.at[k], w_vmem.at[:, pl.ds(k * COL_BLK, COL_BLK)], sems.at[k])
              for k in range(N_CHIPS)]
    for cp in copies:
        cp.start()
    for cp in copies:
        cp.wait()


def _sigmoid(z):
    return 1.0 / (1.0 + jnp.exp(-z))


def _rms_stats(x):
    r = lax.rsqrt(jnp.mean(x * x, axis=-1, keepdims=True) + EPS)
    return r, x * r


def _rms_bwd(dyg, xh, r):
    return r * (dyg - xh * jnp.mean(dyg * xh, axis=-1, keepdims=True))


def _ln_stats(v):
    mu = jnp.mean(v, axis=-1, keepdims=True)
    vc = v - mu
    rs = lax.rsqrt(jnp.mean(vc * vc, axis=-1, keepdims=True) + EPS)
    return rs, vc * rs


def _mesh_pos():
    x, y, c = lax.axis_index("x"), lax.axis_index("y"), lax.axis_index("c")
    return x, y, c, 2 * x + y


def _peer(x, y, c, r):
    return ((1 - x) if (r >> 1) else x, (1 - y) if (r & 1) else y, c)


def _full(shape):
    return pl.BlockSpec(shape, lambda *_: (0,) * len(shape))


def _const(shape, pos):
    return pl.BlockSpec((None,) * len(pos) + tuple(shape), lambda *_: tuple(pos) + (0,) * len(shape))


class Job:
    def __init__(self, ins, out_shapes, sems, start, middle, finish, aliases=None):
        self.ins, self.out_shapes, self.sems = list(ins), list(out_shapes), list(sems)
        self.start, self.middle, self.finish = start, middle, finish
        self.aliases = aliases or {}


def run_job(job, name):
    ni, no = len(job.ins), len(job.out_shapes)

    def body(*refs):
        parts = (refs[:ni], refs[ni : ni + no], refs[ni + no :])
        job.start(*parts)
        if job.middle is not None:
            job.middle(*parts)
        job.finish(*parts)

    return pl.pallas_call(
        body, name=name, out_shape=job.out_shapes, in_specs=[ANY] * ni, out_specs=[ANY] * no, scratch_shapes=job.sems,
        input_output_aliases=job.aliases, compiler_params=pltpu.CompilerParams(has_side_effects=True),
    )(*job.ins)


def _call(body, *, name, grid, in_specs, out_specs, out_shape, args, scratch=(), job=None, mid=None):
    params = pltpu.CompilerParams(dimension_semantics=("arbitrary",) * len(grid), vmem_limit_bytes=VMEM_LIMIT,
                                  has_side_effects=job is not None)
    n_in, n_out, n_sc = len(in_specs), len(out_specs), len(scratch)
    if job is None:
        outs = pl.pallas_call(body, name=name, grid=grid, in_specs=in_specs, out_specs=out_specs, out_shape=out_shape,
                              scratch_shapes=list(scratch), compiler_params=params)(*args)
        return list(outs), []
    ji, jo = len(job.ins), len(job.out_shapes)
    assert not job.aliases and (job.middle is None or len(grid) == 1)

    def wrapped(*refs):
        ins, jin = refs[:n_in], refs[n_in : n_in + ji]
        o0 = n_in + ji
        outs, jout = refs[o0 : o0 + n_out], refs[o0 + n_out : o0 + n_out + jo]
        s0 = o0 + n_out + jo
        sc, jsem = refs[s0 : s0 + n_sc], refs[s0 + n_sc :]
        step = pl.program_id(0)
        for d in range(1, len(grid)):
            step = step * grid[d] + pl.program_id(d)
        n_steps = 1
        for g in grid:
            n_steps *= g

        @pl.when(step == 0)
        def _():
            job.start(jin, jout, jsem)

        if job.middle is not None:
            @pl.when(step == mid)
            def _():
                job.middle(jin, jout, jsem)

        body(*ins, *outs, *sc)

        @pl.when(step == n_steps - 1)
        def _():
            job.finish(jin, jout, jsem)

    outs = pl.pallas_call(
        wrapped, name=name, grid=grid, in_specs=list(in_specs) + [ANY] * ji, out_specs=list(out_specs) + [ANY] * jo,
        out_shape=list(out_shape) + job.out_shapes, scratch_shapes=list(scratch) + job.sems, compiler_params=params,
    )(*args, *job.ins)
    return list(outs[:n_out]), list(outs[n_out:])


def gather_job(items):
    n = len(items)

    def src_half(a, ref, h):
        arr, layer = items[a]
        if layer is None:
            return ref.at[h]
        rh = arr.shape[1] // 2
        return ref.at[layer, pl.ds(h * rh, rh)]

    def dst_half(a, ref, chip, h):
        arr, layer = items[a]
        if layer is None:
            return ref.at[chip, h]
        rh = arr.shape[1] // 2
        return ref.at[chip, pl.ds(h * rh, rh)]

    def copies(src, dst, scratch):
        stage, (in_sem, out_sem, ici_s, ici_r, fwd_s, fwd_r) = scratch[:n], scratch[n:]
        x, y, c, own = _mesh_pos()
        local, ici, fwd, got = [], {}, {}, {}
        for a in range(n):
            layer = items[a][1]
            local.append((pltpu.make_async_copy(src[a] if layer is None else src[a].at[layer], stage[a], in_sem.at[a]),
                          pltpu.make_async_copy(stage[a], dst[a].at[own], out_sem.at[a])))
            for r in (1, 2, 3):
                k = 3 * a + r - 1
                ici[a, r] = pltpu.make_async_remote_copy(
                    src_ref=src_half(a, src[a], c), dst_ref=dst_half(a, dst[a], own, c), send_sem=ici_s.at[k],
                    recv_sem=ici_r.at[k], device_id=_peer(x, y, c, r), device_id_type=MESH)
                fwd[a, r] = pltpu.make_async_remote_copy(
                    src_ref=dst_half(a, dst[a], own ^ r, c), dst_ref=dst_half(a, dst[a], own ^ r, c), send_sem=fwd_s.at[k],
                    recv_sem=fwd_r.at[k], device_id=(x, y, 1 - c), device_id_type=MESH)
                got[a, r] = pltpu.make_async_remote_copy(
                    src_ref=dst_half(a, dst[a], own ^ r, 1 - c), dst_ref=dst_half(a, dst[a], own ^ r, 1 - c),
                    send_sem=fwd_s.at[k], recv_sem=fwd_r.at[k], device_id=(x, y, 1 - c), device_id_type=MESH)
        return local, ici, fwd, got

    def start(src, dst, sems):
        local, ici, _, _ = copies(src, dst, sems)
        for a in range(n):
            for r in (1, 2, 3):
                ici[a, r].start()
        for cp_in, _ in local:
            cp_in.start()
        for cp_in, cp_out in local:
            cp_in.wait()
            cp_out.start()

    def middle(src, dst, sems):
        _, ici, fwd, _ = copies(src, dst, sems)
        for r in (1, 2, 3):
            for a in range(n):
                ici[a, r].wait_recv()
                fwd[a, r].start()

    def finish(src, dst, sems):
        local, ici, fwd, got = copies(src, dst, sems)
        for a in range(n):
            for r in (1, 2, 3):
                got[a, r].wait_recv()
        for a in range(n):
            for r in (1, 2, 3):
                ici[a, r].wait_send()
                fwd[a, r].wait_send()
        for _, cp_out in local:
            cp_out.wait()

    out_shapes = [jax.ShapeDtypeStruct((N_CHIPS,) + (arr.shape if layer is None else arr.shape[1:]), arr.dtype)
                  for arr, layer in items]
    stage = [pltpu.VMEM(arr.shape if layer is None else arr.shape[1:], arr.dtype) for arr, layer in items]
    sems = [pltpu.SemaphoreType.DMA((n,))] * 2 + [pltpu.SemaphoreType.DMA((3 * n,))] * 4
    return Job([arr for arr, _ in items], out_shapes, stage + sems, start, middle, finish)


def _mixer_fwd(proj_ref, lg, lb, ws_ref, bsb_ref, cw_ref, mix_ref, xcbuf, halo_xc):
    u = proj_ref[:, 0 * WIDTH : 1 * WIDTH]
    v = proj_ref[:, 1 * WIDTH : 2 * WIDTH]
    za = proj_ref[:, 2 * WIDTH : 3 * WIDTH]
    h = proj_ref[:, 3 * WIDTH : 4 * WIDTH]
    gb = proj_ref[:, 4 * WIDTH : 5 * WIDTH]
    gc = proj_ref[:, 5 * WIDTH : 6 * WIDTH]
    zb = proj_ref[:, 6 * WIDTH : 7 * WIDTH]
    rs, vhat = _ln_stats(v)
    vl = vhat * lg + lb
    vlb = vl.astype(BF16)
    tile = proj_ref.shape[0]
    for j in range(tile // CHUNK):
        rows = slice(j * CHUNK, (j + 1) * CHUNK)
        for hd in range(HEADS):
            cols = slice(hd * CHUNK, (hd + 1) * CHUNK)
            mix_ref[rows, cols] = _mm(ws_ref[hd], vlb[rows, cols]) + bsb_ref[:, cols]
    mixed = mix_ref[...]
    siga = _sigmoid(za)
    sigb = _sigmoid(zb)
    xc = gc * h
    xcbuf[0:HALO, :] = halo_xc
    xcbuf[HALO : HALO + tile, :] = xc
    y = cw_ref[0:1, :] * xcbuf[HALO - 2 : HALO - 2 + tile, :] + cw_ref[1:2, :] * xcbuf[HALO - 1 : HALO - 1 + tile, :]
    y = y + cw_ref[2:3, :] * xc
    return dict(u=u, za=za, h=h, gb=gb, gc=gc, zb=zb, rs=rs, vhat=vhat, vlb=vlb, mixed=mixed, siga=siga, sigb=sigb,
                xc=xc, y=y)


def fwd_layer(layer, x, p, wts, small, job=None, head=None):
    win, wout, wg, wpp = wts
    ng, lg, lb, ws, bsb, cw, pg = small
    n_head = 0 if head is None else 2

    def body(*refs):
        (x_ref, p_ref, win_ref, wout_ref, wg_ref, wpp_ref, ng_ref, lg_ref, lb_ref, ws_ref, bsb_ref, cw_ref,
         pg_ref) = refs[:13]
        head_in = refs[13 : 13 + n_head]
        proj_ref, x2_ref, gate_ref, x3_ref, hnT_ref, catT_ref, hn2T_ref, pT_ref = refs[13 + n_head : 21 + n_head]
        head_out = refs[21 + n_head : 21 + 2 * n_head]
        mix_ref, xcbuf, carry, wcat, wsem = refs[21 + 2 * n_head :]
        i = pl.program_id(0)

        @pl.when(i == 0)
        def _():
            _load_side_by_side(win_ref, wcat, wsem)
            carry[...] = jnp.zeros_like(carry)
            for ref in head_out:
                ref[...] = jnp.zeros_like(ref)

        xv = x_ref[...]
        _, xh = _rms_stats(xv)
        hn = xh * ng_ref[...]
        hnT_ref[...] = hn.T.astype(BF16)
        proj_ref[...] = _mm(hn.astype(BF16), wcat[...])
        m = _mixer_fwd(proj_ref, lg_ref[...], lb_ref[...], ws_ref, bsb_ref, cw_ref, mix_ref, xcbuf, carry[...])
        carry[...] = m["xc"][FWD_TILE - HALO : FWD_TILE, :]
        out_a = (m["u"] * m["mixed"]) * (m["za"] * m["siga"])
        out_b = (m["gb"] * m["y"]) * (m["zb"] * m["sigb"])
        cat = jnp.concatenate([out_a, out_b], axis=-1)
        catT_ref[...] = cat.T.astype(BF16)
        x2 = xv + _mm_rows(cat.astype(BF16), wout_ref)
        x2_ref[...] = x2
        _, xh2 = _rms_stats(x2)
        hn2 = xh2 * pg_ref[...]
        hn2T_ref[...] = hn2.T.astype(BF16)
        gate = _sigmoid(_mm_rows(hn2.astype(BF16), wg_ref))
        gate_ref[...] = gate
        pv = p_ref[...]
        pT_ref[...] = pv.T.astype(BF16)
        x3 = x2 + gate * _ple_proj(pv, wpp_ref)
        if head is None:
            x3_ref[...] = x3
        else:
            t_ref, gf_ref = head_in
            loss_ref, gg_ref = head_out
            r3, xh3 = _rms_stats(x3)
            gf = gf_ref[...]
            err = xh3 * gf - t_ref[...]
            loss_ref[...] += (0.5 / D_MODEL) * jnp.sum(err * err).reshape(1, 1)
            dy = err * (1.0 / D_MODEL)
            gg_ref[...] += jnp.sum(dy * xh3, axis=0, keepdims=True)
            x3_ref[...] = _rms_bwd(dy * gf, xh3, r3)

    tok = lambda w: pl.BlockSpec((FWD_TILE, w), lambda i: (i, 0))
    tokT = lambda w: pl.BlockSpec((None, w, FWD_TILE), lambda i: (i, 0, 0))
    once = lambda a: pl.BlockSpec(a.shape, lambda i: (0,) * a.ndim, pipeline_mode=pl.Buffered(1))
    f32 = lambda w: jax.ShapeDtypeStruct((SEQ, w), F32)
    bfT = lambda w: jax.ShapeDtypeStruct((SEQ // FWD_TILE, w, FWD_TILE), BF16)
    head_specs = [] if head is None else [tok(D_MODEL), _full((1, D_MODEL))]
    head_outs = [] if head is None else [_full((1, 1)), _full((1, D_MODEL))]
    head_shapes = [] if head is None else [jax.ShapeDtypeStruct((1, 1), F32), jax.ShapeDtypeStruct((1, D_MODEL), F32)]
    return _call(
        body, name=f"fwd_layer{layer}", grid=(SEQ // FWD_TILE,),
        in_specs=[tok(D_MODEL), pl.BlockSpec((None, None, FWD_TILE, PLE), lambda i: (layer, 0, i, 0)),
                  ANY, once(wout), once(wg), once(wpp),
                  _const((1, D_MODEL), (layer,)), _const((1, WIDTH), (layer,)), _const((1, WIDTH), (layer,)),
                  _const((HEADS, CHUNK, CHUNK), (layer,)), _const((CHUNK, WIDTH), (layer,)), _const((3, WIDTH), (layer,)),
                  _const((1, D_MODEL), (layer,))] + head_specs,
        out_specs=[tok(PROJ), tok(D_MODEL), tok(D_MODEL), tok(D_MODEL),
                   tokT(D_MODEL), tokT(D_MODEL), tokT(D_MODEL), tokT(PLE)] + head_outs,
        out_shape=[f32(PROJ), f32(D_MODEL), f32(D_MODEL), f32(D_MODEL),
                   bfT(D_MODEL), bfT(D_MODEL), bfT(D_MODEL), bfT(PLE)] + head_shapes,
        scratch=[pltpu.VMEM((FWD_TILE, WIDTH), F32), pltpu.VMEM((HALO + FWD_TILE, WIDTH), F32), pltpu.VMEM((HALO, WIDTH), F32),
                 pltpu.VMEM((D_MODEL, PROJ), BF16), pltpu.SemaphoreType.DMA((N_CHIPS,))],
        args=(x, p, win, wout, wg, wpp, ng, lg, lb, ws, bsb, cw, pg) + (() if head is None else tuple(head)),
        job=job, mid=SEQ // FWD_TILE // 2)


def bwd_layer(layer, dx3, x2, gate, p, x, proj, wts, small, wsT, tril):
    win, wout, wg, wpp = wts
    ng, lg, lb, ws, bsb, cw, pg = small

    def body(dx3_ref, x2_ref, gate_ref, p_ref, x_ref, proj_ref, halo_ref, win_ref, wout_ref, wg_ref, wpp_ref, ng_ref, lg_ref,
             lb_ref, ws_ref, wsT_ref, bsb_ref, cw_ref, pg_ref, tril_ref,
             dx_ref, dproj_ref, dpp_ref, dgl_ref, dx2b_ref, gn_ref, glg_ref, glb_ref, gws_ref, gbs_ref, gcw_ref, gpg_ref,
             mix_ref, xcbuf, dycbuf, dvl_ref, bs_acc, wcat, wsem):
        i = pl.program_id(0)

        @pl.when(i == 0)
        def _():
            _load_side_by_side(win_ref, wcat, wsem)
            for ref in (gn_ref, glg_ref, glb_ref, gws_ref, gcw_ref, gpg_ref, bs_acc):
                ref[...] = jnp.zeros_like(ref)
            dycbuf[TILE : TILE + HALO, :] = jnp.zeros((HALO, WIDTH), F32)

        dx3v = dx3_ref[...]
        gate_v = gate_ref[...]
        dpp_ref[...] = (dx3v * gate_v).astype(BF16)
        dgl = ((dx3v * _ple_proj(p_ref[...], wpp_ref)) * gate_v * (1.0 - gate_v)).astype(BF16)
        dgl_ref[...] = dgl
        dhn2 = _mm_nt_rows(dgl, wg_ref)
        r2, xh2 = _rms_stats(x2_ref[...])
        gpg_ref[...] += jnp.sum(dhn2 * xh2, axis=0, keepdims=True)
        dx2 = dx3v + _rms_bwd(dhn2 * pg_ref[...], xh2, r2)
        dx2b = dx2.astype(BF16)
        dx2b_ref[...] = dx2b
        dcat = _mm_nt_rows(dx2b, wout_ref)
        lgv = lg_ref[...]
        halo_xc = halo_ref[:, 5 * WIDTH : 6 * WIDTH] * halo_ref[:, 3 * WIDTH : 4 * WIDTH]
        halo_xc = jnp.where(i == N_TILES - 1, 0.0, halo_xc)
        m = _mixer_fwd(proj_ref, lgv, lb_ref[...], ws_ref, bsb_ref, cw_ref, mix_ref, xcbuf, halo_xc)
        u, za, h, gb, gc, zb = m["u"], m["za"], m["h"], m["gb"], m["gc"], m["zb"]
        mixed, siga, sigb, xc, y = m["mixed"], m["siga"], m["sigb"], m["xc"], m["y"]
        doa = dcat[:, 0:WIDTH]
        dob = dcat[:, WIDTH : 2 * WIDTH]
        sa = za * siga
        sb = zb * sigb
        doa_sa = doa * sa
        dproj_ref[:, 0 * WIDTH : 1 * WIDTH] = (doa_sa * mixed).astype(BF16)
        dmixed = doa_sa * u
        dza = (doa * (u * mixed)) * (siga * (1.0 + za * (1.0 - siga)))
        dproj_ref[:, 2 * WIDTH : 3 * WIDTH] = dza.astype(BF16)
        dob_sb = dob * sb
        dproj_ref[:, 4 * WIDTH : 5 * WIDTH] = (dob_sb * y).astype(BF16)
        dyc = dob_sb * gb
        dzb = (dob * (gb * y)) * (sigb * (1.0 + zb * (1.0 - sigb)))
        dproj_ref[:, 6 * WIDTH : 7 * WIDTH] = dzb.astype(BF16)
        dycbuf[0:TILE, :] = dyc
        dyc1 = dycbuf[1 : 1 + TILE, :]
        dyc2 = dycbuf[2 : 2 + TILE, :]
        dxc = cw_ref[2:3, :] * dyc + cw_ref[1:2, :] * dyc1 + cw_ref[0:1, :] * dyc2
        gcw_ref[0:1, :] += jnp.sum(xc * dyc2, axis=0, keepdims=True)
        gcw_ref[1:2, :] += jnp.sum(xc * dyc1, axis=0, keepdims=True)
        gcw_ref[2:3, :] += jnp.sum(xc * dyc, axis=0, keepdims=True)
        dycbuf[TILE : TILE + HALO, :] = dyc[0:HALO, :]
        dproj_ref[:, 5 * WIDTH : 6 * WIDTH] = (dxc * h).astype(BF16)
        dproj_ref[:, 3 * WIDTH : 4 * WIDTH] = (dxc * gc).astype(BF16)
        dmb = dmixed.astype(BF16)
        vlb = m["vlb"]
        bsum = jnp.zeros((CHUNK, WIDTH), F32)
        for j in range(TILE // CHUNK):
            rows = slice(j * CHUNK, (j + 1) * CHUNK)
            bsum = bsum + dmixed[rows, :]
            for hd in range(HEADS):
                cols = slice(hd * CHUNK, (hd + 1) * CHUNK)
                gws_ref[hd] += _mm_nt(dmb[rows, cols], vlb[rows, cols])
                dvl_ref[rows, cols] = _mm(wsT_ref[hd], dmb[rows, cols])
        bs_acc[...] += bsum
        dvl = dvl_ref[...]
        vhat = m["vhat"]
        glb_ref[...] += jnp.sum(dvl, axis=0, keepdims=True)
        glg_ref[...] += jnp.sum(dvl * vhat, axis=0, keepdims=True)
        dvh = dvl * lgv
        dv = m["rs"] * (dvh - jnp.mean(dvh, axis=-1, keepdims=True) - vhat * jnp.mean(dvh * vhat, axis=-1, keepdims=True))
        dproj_ref[:, 1 * WIDTH : 2 * WIDTH] = dv.astype(BF16)
        dhn = _mm_nt(dproj_ref[...], wcat[...])
        r1, xh = _rms_stats(x_ref[...])
        gn_ref[...] += jnp.sum(dhn * xh, axis=0, keepdims=True)
        dx_ref[...] = dx2 + _rms_bwd(dhn * ng_ref[...], xh, r1)

        @pl.when(i == N_TILES - 1)
        def _():
            for hd in range(HEADS):
                gws_ref[hd] = gws_ref[hd] * tril_ref[...]
                gbs_ref[hd : hd + 1, :] = jnp.sum(bs_acc[:, hd * CHUNK : (hd + 1) * CHUNK].T, axis=0, keepdims=True)

    rev = lambda w: pl.BlockSpec((TILE, w), lambda i: (N_TILES - 1 - i, 0))
    halo = pl.BlockSpec((HALO, PROJ), lambda i: (jnp.maximum((N_TILES - 1 - i) * (TILE // HALO) - 1, 0), 0))
    once = lambda a: pl.BlockSpec(a.shape, lambda i: (0,) * a.ndim, pipeline_mode=pl.Buffered(1))
    vec = lambda w: jax.ShapeDtypeStruct((1, w), F32)
    b16 = lambda w: jax.ShapeDtypeStruct((SEQ, w), BF16)
    outs, _ = _call(
        body, name=f"bwd_layer{layer}", grid=(N_TILES,),
        in_specs=[rev(D_MODEL), rev(D_MODEL), rev(D_MODEL),
                  pl.BlockSpec((None, None, TILE, PLE), lambda i: (layer, 0, N_TILES - 1 - i, 0)),
                  rev(D_MODEL), rev(PROJ), halo, ANY, once(wout), once(wg), once(wpp),
                  _const((1, D_MODEL), (layer,)), _const((1, WIDTH), (layer,)), _const((1, WIDTH), (layer,)),
                  _const((HEADS, CHUNK, CHUNK), (layer,)), _const((HEADS, CHUNK, CHUNK), (layer,)),
                  _const((CHUNK, WIDTH), (layer,)), _const((3, WIDTH), (layer,)), _const((1, D_MODEL), (layer,)),
                  _full((CHUNK, CHUNK))],
        out_specs=[rev(D_MODEL), rev(PROJ), rev(D_MODEL), rev(D_MODEL), rev(D_MODEL),
                   _full((1, D_MODEL)), _full((1, WIDTH)), _full((1, WIDTH)), _full((HEADS, CHUNK, CHUNK)),
                   _full((HEADS, CHUNK)), _full((3, WIDTH)), _full((1, D_MODEL))],
        out_shape=[jax.ShapeDtypeStruct((SEQ, D_MODEL), F32), b16(PROJ), b16(D_MODEL), b16(D_MODEL), b16(D_MODEL),
                   vec(D_MODEL), vec(WIDTH), vec(WIDTH), jax.ShapeDtypeStruct((HEADS, CHUNK, CHUNK), F32),
                   jax.ShapeDtypeStruct((HEADS, CHUNK), F32), jax.ShapeDtypeStruct((3, WIDTH), F32), vec(D_MODEL)],
        scratch=[pltpu.VMEM((TILE, WIDTH), F32), pltpu.VMEM((HALO + TILE, WIDTH), F32),
                 pltpu.VMEM((TILE + HALO, WIDTH), F32), pltpu.VMEM((TILE, WIDTH), F32), pltpu.VMEM((CHUNK, WIDTH), F32),
                 pltpu.VMEM((D_MODEL, PROJ), BF16), pltpu.SemaphoreType.DMA((N_CHIPS,))],
        args=(dx3, x2, gate, p, x, proj, proj, win, wout, wg, wpp, ng, lg, lb, ws, wsT, bsb, cw, pg, tril))
    return outs


def wgrad(pairs, n_tiles, col_blocked, tk, name, job=None):
    n = len(pairs)
    m_dim, n_dim, at = pairs[0][0].shape[1], pairs[0][1].shape[1], pairs[0][0].shape[2]
    tn = n_dim // n_tiles
    cb, mb = n_dim // N_CHIPS, m_dim // N_CHIPS
    per_tile = N_CHIPS // n_tiles
    assert col_blocked or n_tiles == 1

    def body(*refs):
        for a in range(n):
            a_ref, b_ref, o_ref = refs[2 * a], refs[2 * a + 1], refs[2 * n + a]

            @pl.when(pl.program_id(1) == 0)
            def _():
                o_ref[...] = jnp.zeros_like(o_ref)

            prod = _mm(a_ref[0], b_ref[0:at, :])
            for t in range(1, tk // at):
                prod = prod + _mm(a_ref[t], b_ref[t * at : (t + 1) * at, :])
            if col_blocked:
                for q in range(per_tile):
                    o_ref[q] += prod[:, q * cb : (q + 1) * cb]
            else:
                for q in range(N_CHIPS):
                    o_ref[q] += prod[q * mb : (q + 1) * mb, :]

            @pl.when(pl.program_id(1) == SEQ // tk - 1)
            def _():
                refs[3 * n + a][...] = o_ref[...].astype(BF16)

    mode = dict(pipeline_mode=pl.Buffered(1)) if n_tiles == 1 else {}
    if col_blocked:
        shape = (N_CHIPS, m_dim, cb)
        o_spec = pl.BlockSpec((per_tile, m_dim, cb), lambda j, k: (j, 0, 0), **mode)
    else:
        shape = (N_CHIPS, mb, n_dim)
        o_spec = pl.BlockSpec((N_CHIPS, mb, n_dim), lambda j, k: (0, 0, 0), **mode)
    outs, job_outs = _call(
        body, name=name, grid=(n_tiles, SEQ // tk),
        in_specs=[pl.BlockSpec((tk // at, m_dim, at), lambda j, k: (k, 0, 0)),
                  pl.BlockSpec((tk, tn), lambda j, k: (k, j))] * n,
        out_specs=[o_spec] * (2 * n), out_shape=[jax.ShapeDtypeStruct(shape, F32)] * n + [jax.ShapeDtypeStruct(shape, BF16)] * n,
        args=[t for pair in pairs for t in pair], job=job)
    return outs[:n], outs[n:], job_outs


def to_sibling_job(grads):
    n = len(grads)

    def copies(src, dst, sems):
        x, y, c, _ = _mesh_pos()
        out = []
        for a in range(n):
            rh = grads[a].shape[1] // 2
            out.append(pltpu.make_async_remote_copy(
                src_ref=src[a].at[:, pl.ds((1 - c) * rh, rh), :], dst_ref=dst[a], send_sem=sems[0].at[a],
                recv_sem=sems[1].at[a], device_id=(x, y, 1 - c), device_id_type=MESH))
        return out

    def start(src, dst, sems):
        for cp in copies(src, dst, sems):
            cp.start()

    def finish(src, dst, sems):
        for cp in copies(src, dst, sems):
            cp.wait()

    out_shapes = [jax.ShapeDtypeStruct((N_CHIPS, g.shape[1] // 2, g.shape[2]), g.dtype) for g in grads]
    return Job(grads, out_shapes, [pltpu.SemaphoreType.DMA((n,))] * 2, start, None, finish)


def rs_add_sibling(ids, grads, recvd, name):
    n = len(grads)

    def body(ids_ref, *refs):
        g, r = refs[:n], refs[n : 2 * n]
        pb, pf = refs[2 * n : 3 * n], refs[3 * n :]
        k = pl.program_id(0)
        for a in range(n):
            s = g[a][...] + r[a][...].astype(F32)
            pb[a][...] = s.astype(BF16)

            @pl.when(k == ids_ref[1])
            def _():
                pf[a][...] = s

    in_specs, out_specs, out_shape = [], [], []
    for g in grads:
        rh, cc = g.shape[1] // 2, g.shape[2]
        in_specs.append(pl.BlockSpec((None, rh, cc), lambda k, ids: (k, ids[0], 0)))
    for g in grads:
        rh, cc = g.shape[1] // 2, g.shape[2]
        in_specs.append(pl.BlockSpec((None, rh, cc), lambda k, ids: (k, 0, 0)))
        out_specs.append(pl.BlockSpec((None, rh, cc), lambda k, ids: (k, 0, 0)))
        out_shape.append(jax.ShapeDtypeStruct((N_CHIPS, rh, cc), BF16))
    for g in grads:
        rh, cc = g.shape[1] // 2, g.shape[2]
        out_specs.append(pl.BlockSpec((rh, cc), lambda k, ids: (0, 0)))
        out_shape.append(jax.ShapeDtypeStruct((rh, cc), F32))
    outs = pl.pallas_call(
        body, name=name, out_shape=out_shape,
        grid_spec=pltpu.PrefetchScalarGridSpec(num_scalar_prefetch=1, grid=(N_CHIPS,), in_specs=in_specs,
                                               out_specs=out_specs),
        compiler_params=pltpu.CompilerParams(dimension_semantics=("arbitrary",), vmem_limit_bytes=VMEM_LIMIT),
    )(ids, *grads, *recvd)
    return list(outs[:n]), list(outs[n:])


def to_owners_job(partials):
    n = len(partials)

    def copies(src, dst, sems):
        x, y, c, own = _mesh_pos()
        out = []
        for a in range(n):
            for r in (1, 2, 3):
                out.append(pltpu.make_async_remote_copy(
                    src_ref=src[a].at[own ^ r], dst_ref=dst[a].at[r - 1], send_sem=sems[0].at[3 * a + r - 1],
                    recv_sem=sems[1].at[3 * a + r - 1], device_id=_peer(x, y, c, r), device_id_type=MESH))
        return out

    def start(src, dst, sems):
        for cp in copies(src, dst, sems):
            cp.start()

    def finish(src, dst, sems):
        for cp in copies(src, dst, sems):
            cp.wait()

    out_shapes = [jax.ShapeDtypeStruct((3,) + p.shape[1:], BF16) for p in partials]
    return Job(partials, out_shapes, [pltpu.SemaphoreType.DMA((3 * n,))] * 2, start, None, finish)


def rs_add_owners(layer, ids, own_f32, recvd, prev, name):
    n = len(own_f32)
    nb = 2

    def body(ids_ref, *refs):
        o, r, f = refs[:n], refs[n : 2 * n], refs[-n:]
        for a in range(n):
            f[a][...] = ((o[a][...] + r[a][0].astype(F32)) + r[a][1].astype(F32)) + r[a][2].astype(F32)

    in_specs, out_specs, out_shape = [], [], []
    for o in own_f32:
        in_specs.append(pl.BlockSpec((o.shape[0] // nb, o.shape[1]), lambda j, ids: (j, 0)))
    for o in own_f32:
        in_specs.append(pl.BlockSpec((3, o.shape[0] // nb, o.shape[1]), lambda j, ids: (0, j, 0)))
        out_specs.append(pl.BlockSpec((None, o.shape[0] // nb, o.shape[1]), lambda j, ids: (layer, ids[0] * nb + j, 0)))
        out_shape.append(jax.ShapeDtypeStruct((DEPTH, 2 * o.shape[0], o.shape[1]), F32))
    args, aliases = [ids, *own_f32, *recvd], {}
    if prev is not None:
        in_specs += [ANY] * n
        args += list(prev)
        aliases = {1 + 2 * n + a: a for a in range(n)}
    return pl.pallas_call(
        body, name=name, out_shape=out_shape, input_output_aliases=aliases,
        grid_spec=pltpu.PrefetchScalarGridSpec(num_scalar_prefetch=1, grid=(nb,), in_specs=in_specs, out_specs=out_specs),
        compiler_params=pltpu.CompilerParams(dimension_semantics=("arbitrary",), vmem_limit_bytes=VMEM_LIMIT),
    )(*args)


def exchange_halves_job(layer, full):
    n = len(full)

    def copies(src, dst, sems):
        x, y, c, _ = _mesh_pos()
        out = []
        for a in range(n):
            rh = full[a].shape[1] // 2
            out.append(pltpu.make_async_remote_copy(
                src_ref=src[a].at[layer, pl.ds(c * rh, rh), :], dst_ref=dst[a].at[layer, pl.ds(c * rh, rh), :],
                send_sem=sems[0].at[a], recv_sem=sems[1].at[a], device_id=(x, y, 1 - c), device_id_type=MESH))
        return out

    def start(src, dst, sems):
        for cp in copies(src, dst, sems):
            cp.start()

    def finish(src, dst, sems):
        for cp in copies(src, dst, sems):
            cp.wait()

    out_shapes = [jax.ShapeDtypeStruct(f.shape, F32) for f in full]
    return Job(full, out_shapes, [pltpu.SemaphoreType.DMA((n,))] * 2, start, None, finish, {a: a for a in range(n)})


def _adamw(w, g, m, v):
    m2 = ADAM_B1 * m + (1.0 - ADAM_B1) * g
    v2 = ADAM_B2 * v + (1.0 - ADAM_B2) * (g * g)
    delta = -ADAM_LR * ((m2 / ADAM_C1) / (jnp.sqrt(v2 / ADAM_C2) + ADAM_EPS) + ADAM_WD * w)
    return delta, m2, v2


def adamw_big(ws, gs, ms, vs, name, job=None):
    n = len(ws)
    nb = 4

    def body(*refs):
        for a in range(n):
            w, g, m, v = (refs[j * n + a][...] for j in range(4))
            d, m2, v2 = _adamw(w, g, m, v)
            refs[4 * n + a][...] = d
            refs[5 * n + a][...] = m2
            refs[6 * n + a][...] = v2

    specs = [pl.BlockSpec((None, w.shape[1] // nb, w.shape[2]), lambda l, j: (l, j, 0)) for w in ws]
    shapes = [jax.ShapeDtypeStruct(w.shape, F32) for w in ws]
    outs, job_outs = _call(body, name=name, grid=(DEPTH, nb), in_specs=specs * 4, out_specs=specs * 3,
                           out_shape=shapes * 3, args=(*ws, *gs, *ms, *vs), job=job)
    return (outs[:n], outs[n : 2 * n], outs[2 * n :]), job_outs


def small_to_sibling_job(gs):
    n = len(gs)

    def copies(src, dst, sems):
        x, y, c, _ = _mesh_pos()
        return [pltpu.make_async_remote_copy(src_ref=src[a], dst_ref=dst[a], send_sem=sems[0].at[a], recv_sem=sems[1].at[a],
                                             device_id=(x, y, 1 - c), device_id_type=MESH) for a in range(n)]

    def start(*refs):
        for cp in copies(*refs):
            cp.start()

    def finish(*refs):
        for cp in copies(*refs):
            cp.wait()

    return Job(gs, [jax.ShapeDtypeStruct(g.shape, F32) for g in gs], [pltpu.SemaphoreType.DMA((n,))] * 2, start, None, finish)


def small_chip_sum(gs, recvd):
    n = len(gs)

    def body(*refs):
        for a in range(n):
            refs[2 * n + a][...] = refs[a][...] + refs[n + a][...]

    return pl.pallas_call(body, name="small_chip_sum", out_shape=[jax.ShapeDtypeStruct(g.shape, F32) for g in gs])(*gs, *recvd)


def small_to_chips_job(chip_sums):
    n = len(chip_sums)

    def copies(src, dst, sems):
        x, y, c, _ = _mesh_pos()
        return [pltpu.make_async_remote_copy(src_ref=src[a], dst_ref=dst[a].at[r - 1], send_sem=sems[0].at[3 * a + r - 1],
                                             recv_sem=sems[1].at[3 * a + r - 1], device_id=_peer(x, y, c, r),
                                             device_id_type=MESH) for a in range(n) for r in (1, 2, 3)]

    def start(*refs):
        for cp in copies(*refs):
            cp.start()

    def finish(*refs):
        for cp in copies(*refs):
            cp.wait()

    return Job(chip_sums, [jax.ShapeDtypeStruct((3,) + s.shape, F32) for s in chip_sums],
               [pltpu.SemaphoreType.DMA((3 * n,))] * 2, start, None, finish)


def small_finish(chip_sums, from_chips, ws, ms, vs):
    n, n_w = len(chip_sums), len(ws)

    def body(*refs):
        cs, fc = refs[:n], refs[n : 2 * n]
        w, m, v = (refs[2 * n + j * n_w : 2 * n + (j + 1) * n_w] for j in range(3))
        outs = refs[2 * n + 3 * n_w :]
        own = 2 * lax.axis_index("x") + lax.axis_index("y")
        for a in range(n):
            mine = cs[a][...]

            def of_chip(k):
                r = own ^ k
                return jnp.where(r == 0, mine, jnp.where(r == 1, fc[a][0], jnp.where(r == 2, fc[a][1], fc[a][2])))

            tot = ((of_chip(0) + of_chip(1)) + of_chip(2)) + of_chip(3)
            outs[a][...] = tot
            if a < n_w:
                d, m2, v2 = _adamw(w[a][...], tot, m[a][...], v[a][...])
                outs[n + a][...] = d
                outs[n + n_w + a][...] = m2
                outs[n + 2 * n_w + a][...] = v2

    shapes = [jax.ShapeDtypeStruct(s.shape, F32) for s in chip_sums]
    outs = pl.pallas_call(body, name="small_finish", out_shape=shapes + shapes[:n_w] * 3)(
        *chip_sums, *from_chips, *ws, *ms, *vs)
    return outs[:n], outs[n : n + n_w], outs[n + n_w : n + 2 * n_w], outs[n + 2 * n_w :]


def merge_jobs(*jobs):
    def parts(refs, counts):
        out, pos = [], 0
        for cnt in counts:
            out.append(refs[pos : pos + cnt])
            pos += cnt
        return out

    def run(phase):
        def go(ins, outs, sems):
            for job, i, o, s in zip(jobs, parts(ins, [len(j.ins) for j in jobs]),
                                    parts(outs, [len(j.out_shapes) for j in jobs]), parts(sems, [len(j.sems) for j in jobs])):
                getattr(job, phase)(i, o, s)
        return go

    assert all(j.middle is None and not j.aliases for j in jobs)
    return Job([a for j in jobs for a in j.ins], [a for j in jobs for a in j.out_shapes], [a for j in jobs for a in j.sems],
               run("start"), None, run("finish"))


def adamw_rows(w, g, m, v):
    def body(w_ref, g_ref, m_ref, v_ref, d_ref, m2_ref, v2_ref):
        d, m2, v2 = _adamw(w_ref[...], g_ref[...], m_ref[...], v_ref[...])
        d_ref[...] = d
        m2_ref[...] = m2
        v2_ref[...] = v2

    return pl.pallas_call(body, name="adamw_conv", out_shape=[jax.ShapeDtypeStruct(w.shape, F32)] * 3)(w, g, m, v)


def kernel(x, p, norm_g, w_in, ln_v_g, ln_v_b, w_s, b_s, conv_w, w_out, ple_norm_g, w_ple_gate, w_ple_proj, final_g, loss_target, m_norm_g, m_w_in, m_ln_v_g, m_ln_v_b, m_w_s, m_b_s, m_conv_w, m_w_out, m_ple_norm_g, m_w_ple_gate, m_w_ple_proj, m_final_g, v_norm_g, v_w_in, v_ln_v_g, v_ln_v_b, v_w_s, v_b_s, v_conv_w, v_w_out, v_ple_norm_g, v_w_ple_gate, v_w_ple_proj, v_final_g):
    cx, cy, cc = lax.axis_index("x"), lax.axis_index("y"), lax.axis_index("c")
    own = 2 * cx + cy
    ids = jnp.stack([cc, own]).astype(jnp.int32)

    cw_rows = jnp.transpose(conv_w, (0, 2, 1))
    shards = [w_in.astype(BF16), w_out.astype(BF16), w_ple_gate.astype(BF16), w_ple_proj.astype(BF16)]
    *wts0, g_cw = run_job(gather_job([(s, 0) for s in shards] + [(cw_rows, None)]), "gather_weights0")
    cw_full = jnp.transpose(g_cw, (1, 2, 0, 3)).reshape(DEPTH, 3, WIDTH)
    tril = jnp.tril(jnp.ones((CHUNK, CHUNK), F32))
    ws_masked = w_s * tril[None, None]
    ws_b = ws_masked.astype(BF16)
    wsT_b = jnp.swapaxes(ws_masked, 2, 3).astype(BF16)
    bsb = jnp.repeat(jnp.swapaxes(b_s, 1, 2), CHUNK, axis=2)
    small = (norm_g[:, None, :], ln_v_g[:, None, :], ln_v_b[:, None, :], ws_b, bsb, cw_full, ple_norm_g[:, None, :])

    saved0, wts1 = fwd_layer(0, x[0], p, wts0, small, job=gather_job([(s, 1) for s in shards]))
    saved1, _ = fwd_layer(1, saved0[3], p, wts1, small, head=(loss_target[0], final_g[None, :]))
    saved, xs = [saved0[:8], saved1[:8]], [x[0], saved0[3]]
    dx, loss_part, g_final = saved1[3], saved1[8], saved1[9]

    def chip_sums(arrs, arrs_bf16, tag):
        recvd = run_job(to_sibling_job(arrs_bf16), f"rs_to_sibling{tag}")
        return rs_add_sibling(ids, arrs, recvd, f"rs_add_sibling{tag}")

    small_g = [None] * DEPTH
    proj, x2, gate, _, hnT, catT, hn2T, pT = saved[1]
    dx, dproj_b, dpp_b, dgl_b, dx2_b, *g_sm = bwd_layer(1, dx, x2, gate, p, xs[1], proj, wts1, small, wsT_b, tril)
    small_g[1] = g_sm[:5] + [g_sm[6], g_sm[5]]
    g1_in, b1_in, _ = wgrad([(hnT, dproj_b)], 1, True, 1024, "wgrad_in1")
    g1_og, b1_og, _ = wgrad([(catT, dx2_b), (hn2T, dgl_b)], 1, False, 1024, "wgrad_outgate1")
    g1_pp, b1_pp, _ = wgrad([(pT, dpp_b)], 1, True, 2048, "wgrad_pp1")
    part1 = chip_sums(g1_in + g1_og + g1_pp, b1_in + b1_og + b1_pp, "1")
    proj, x2, gate, _, hnT, catT, hn2T, pT = saved[0]
    dx, dproj_b, dpp_b, dgl_b, dx2_b, *g_sm = bwd_layer(0, dx, x2, gate, p, xs[0], proj, wts0, small, wsT_b, tril)
    small_g[0] = g_sm[:5] + [g_sm[6], g_sm[5]]
    def both(j):
        return jnp.concatenate([small_g[0][j], small_g[1][j]], axis=0)

    g_small = [both(0), both(1), both(2), both(3).reshape(DEPTH * HEADS * CHUNK, CHUNK), both(4), both(5), g_final, both(6),
               jnp.broadcast_to(loss_part, (1, 128))]
    g0_in, b0_in, landed = wgrad([(hnT, dproj_b)], 1, True, 1024, "wgrad_in0",
                                 job=merge_jobs(to_owners_job(part1[0]), small_to_sibling_job(g_small)))
    from1, small_sib = landed[:4], landed[4:]
    full = rs_add_owners(1, ids, part1[1], from1, None, "rs_add_owners1")
    full = run_job(exchange_halves_job(1, full), "rs_exchange_halves1")
    small_chip = small_chip_sum(g_small, small_sib)
    part0b = chip_sums(g0_in, b0_in, "0b")
    g0_og, b0_og, from0b = wgrad([(catT, dx2_b), (hn2T, dgl_b)], 1, False, 1024, "wgrad_outgate0",
                                 job=to_owners_job(part0b[0]))
    full_b = rs_add_owners(0, ids, part0b[1], from0b, full[:1], "rs_add_owners0b")
    g0_pp, b0_pp, small_chips = wgrad([(pT, dpp_b)], 1, True, 2048, "wgrad_pp0", job=small_to_chips_job(small_chip))
    part0a = chip_sums(g0_og + g0_pp, b0_og + b0_pp, "0a")
    from0a = run_job(to_owners_job(part0a[0]), "rs_to_owners0a")
    full_a = rs_add_owners(0, ids, part0a[1], from0a, full[1:], "rs_add_owners0a")
    gw_in, gw_out, gw_gate, gw_pp = run_job(exchange_halves_job(0, list(full_b) + list(full_a)), "rs_exchange_halves0")
    ((d_in, d_out, d_gate, d_pp), (m_in, m_out, m_gate, m_pp), (v_in, v_out, v_gate, v_pp)), _ = adamw_big(
        [w_in, w_out, w_ple_gate, w_ple_proj], [gw_in, gw_out, gw_gate, gw_pp],
        [m_w_in, m_w_out, m_w_ple_gate, m_w_ple_proj], [v_w_in, v_w_out, v_w_ple_gate, v_w_ple_proj], "adamw_big")

    flat = lambda a: a.reshape(-1, a.shape[-1])
    gsum, dsm, msm, vsm = small_finish(
        small_chip, small_chips, [flat(a) for a in (norm_g, ln_v_g, ln_v_b, w_s, b_s, ple_norm_g, final_g[None])],
        [flat(a) for a in (m_norm_g, m_ln_v_g, m_ln_v_b, m_w_s, m_b_s, m_ple_norm_g, m_final_g[None])],
        [flat(a) for a in (v_norm_g, v_ln_v_g, v_ln_v_b, v_w_s, v_b_s, v_ple_norm_g, v_final_g[None])])
    like = [norm_g, ln_v_g, ln_v_b, w_s, b_s, ple_norm_g, final_g]
    gs, ds, m2s, v2s = ([a.reshape(b.shape) for a, b in zip(res, like)] for res in (gsum, dsm, msm, vsm))
    loss = gsum[8][0, 0]
    g_cw_own = lax.dynamic_slice_in_dim(gsum[7].reshape(DEPTH, 3, WIDTH), own * (WIDTH // N_CHIPS), WIDTH // N_CHIPS, axis=2)
    rows2 = lambda a: a.reshape(DEPTH * 3, WIDTH // N_CHIPS)
    d_cw, m_cw, v_cw = adamw_rows(rows2(cw_rows), rows2(g_cw_own), rows2(jnp.transpose(m_conv_w, (0, 2, 1))),
                                  rows2(jnp.transpose(v_conv_w, (0, 2, 1))))
    back = lambda a: jnp.transpose(a.reshape(DEPTH, 3, WIDTH // N_CHIPS), (0, 2, 1))
    g_conv = jnp.transpose(g_cw_own, (0, 2, 1))

    def ordered(sm, cw_v, w_in_v, w_out_v, gate_v, pp_v):
        return [sm[0], w_in_v, sm[1], sm[2], sm[3], sm[4], cw_v, w_out_v, sm[5], gate_v, pp_v, sm[6]]

    grads = ordered(gs, g_conv, gw_in, gw_out, gw_gate, gw_pp)
    deltas = ordered(ds, back(d_cw), d_in, d_out, d_gate, d_pp)
    new_m = ordered(m2s, back(m_cw), m_in, m_out, m_gate, m_pp)
    new_v = ordered(v2s, back(v_cw), v_in, v_out, v_gate, v_pp)
    return (loss, dx[None], *grads, *deltas, *new_m, *new_v)
```

```python
import jax
import jax.numpy as jnp
from jax import lax
from jax.experimental import pallas as pl
from jax.experimental.pallas import tpu as pltpu

F32 = jnp.float32
BF16 = jnp.bfloat16

SEQ = 8192
D_MODEL = 1024
WIDTH = 512
PROJ = 7 * WIDTH
N_CHIPS = 4
COL_BLK = PROJ // N_CHIPS
PLE = 256
HEADS = 4
CHUNK = 128
DEPTH = 2
EPS = 1e-6
TILE = 256
N_TILES = SEQ // TILE
FWD_TILE = 512
HALO = 8
VMEM_LIMIT = 60 * 1024 * 1024

ADAM_LR, ADAM_B1, ADAM_B2, ADAM_EPS, ADAM_WD, ADAM_STEP = 0.001, 0.9, 0.999, 1e-08, 0.01, 10
ADAM_C1 = 1.0 - ADAM_B1**ADAM_STEP
ADAM_C2 = 1.0 - ADAM_B2**ADAM_STEP

MESH = pl.DeviceIdType.MESH
ANY = pl.BlockSpec(memory_space=pl.ANY)


def _mm(a, b):
    return lax.dot_general(a, b, (((1,), (0,)), ((), ())), preferred_element_type=F32)


def _mm_nt(a, b):
    return lax.dot_general(a, b, (((1,), (1,)), ((), ())), preferred_element_type=F32)


def _mm_rows(a, w_ref):
    blk = w_ref.shape[1]
    acc = _mm(a[:, 0:blk], w_ref[0])
    for k in range(1, N_CHIPS):
        acc = acc + _mm(a[:, k * blk : (k + 1) * blk], w_ref[k])
    return acc


def _ple_proj(p_tile, wpp_ref):
    pb = p_tile.astype(BF16)
    return jnp.concatenate([_mm(pb, wpp_ref[k]) for k in range(N_CHIPS)], axis=-1)


def _mm_nt_rows(a, w_ref):
    return jnp.concatenate([_mm_nt(a, w_ref[k]) for k in range(N_CHIPS)], axis=-1)


def _load_side_by_side(w_hbm, w_vmem, sems):
    copies = [pltpu.make_async_copy(w_hbm.at[k], w_vmem.at[:, pl.ds(k * COL_BLK, COL_BLK)], sems.at[k])
              for k in range(N_CHIPS)]
    for cp in copies:
        cp.start()
    for cp in copies:
        cp.wait()


def _sigmoid(z):
    return 1.0 / (1.0 + jnp.exp(-z))


def _rms_stats(x):
    r = lax.rsqrt(jnp.mean(x * x, axis=-1, keepdims=True) + EPS)
    return r, x * r


def _rms_bwd(dyg, xh, r):
    return r * (dyg - xh * jnp.mean(dyg * xh, axis=-1, keepdims=True))


def _ln_stats(v):
    mu = jnp.mean(v, axis=-1, keepdims=True)
    vc = v - mu
    rs = lax.rsqrt(jnp.mean(vc * vc, axis=-1, keepdims=True) + EPS)
    return rs, vc * rs


def _mesh_pos():
    x, y, c = lax.axis_index("x"), lax.axis_index("y"), lax.axis_index("c")
    return x, y, c, 2 * x + y


def _peer(x, y, c, r):
    return ((1 - x) if (r >> 1) else x, (1 - y) if (r & 1) else y, c)


def _full(shape):
    return pl.BlockSpec(shape, lambda *_: (0,) * len(shape))


def _const(shape, pos):
    return pl.BlockSpec((None,) * len(pos) + tuple(shape), lambda *_: tuple(pos) + (0,) * len(shape))


class Job:
    def __init__(self, ins, out_shapes, sems, start, middle, finish, aliases=None):
        self.ins, self.out_shapes, self.sems = list(ins), list(out_shapes), list(sems)
        self.start, self.middle, self.finish = start, middle, finish
        self.aliases = aliases or {}


def run_job(job, name):
    ni, no = len(job.ins), len(job.out_shapes)

    def body(*refs):
        parts = (refs[:ni], refs[ni : ni + no], refs[ni + no :])
        job.start(*parts)
        if job.middle is not None:
            job.middle(*parts)
        job.finish(*parts)

    return pl.pallas_call(
        body, name=name, out_shape=job.out_shapes, in_specs=[ANY] * ni, out_specs=[ANY] * no, scratch_shapes=job.sems,
        input_output_aliases=job.aliases, compiler_params=pltpu.CompilerParams(has_side_effects=True),
    )(*job.ins)


def _call(body, *, name, grid, in_specs, out_specs, out_shape, args, scratch=(), job=None, mid=None):
    params = pltpu.CompilerParams(dimension_semantics=("arbitrary",) * len(grid), vmem_limit_bytes=VMEM_LIMIT,
                                  has_side_effects=job is not None)
    n_in, n_out, n_sc = len(in_specs), len(out_specs), len(scratch)
    if job is None:
        outs = pl.pallas_call(body, name=name, grid=grid, in_specs=in_specs, out_specs=out_specs, out_shape=out_shape,
                              scratch_shapes=list(scratch), compiler_params=params)(*args)
        return list(outs), []
    ji, jo = len(job.ins), len(job.out_shapes)
    assert not job.aliases and (job.middle is None or len(grid) == 1)

    def wrapped(*refs):
        ins, jin = refs[:n_in], refs[n_in : n_in + ji]
        o0 = n_in + ji
        outs, jout = refs[o0 : o0 + n_out], refs[o0 + n_out : o0 + n_out + jo]
        s0 = o0 + n_out + jo
        sc, jsem = refs[s0 : s0 + n_sc], refs[s0 + n_sc :]
        step = pl.program_id(0)
        for d in range(1, len(grid)):
            step = step * grid[d] + pl.program_id(d)
        n_steps = 1
        for g in grid:
            n_steps *= g

        @pl.when(step == 0)
        def _():
            job.start(jin, jout, jsem)

        if job.middle is not None:
            @pl.when(step == mid)
            def _():
                job.middle(jin, jout, jsem)

        body(*ins, *outs, *sc)

        @pl.when(step == n_steps - 1)
        def _():
            job.finish(jin, jout, jsem)

    outs = pl.pallas_call(
        wrapped, name=name, grid=grid, in_specs=list(in_specs) + [ANY] * ji, out_specs=list(out_specs) + [ANY] * jo,
        out_shape=list(out_shape) + job.out_shapes, scratch_shapes=list(scratch) + job.sems, compiler_params=params,
    )(*args, *job.ins)
    return list(outs[:n_out]), list(outs[n_out:])


def gather_job(items, relay=False):
    n = len(items)
    relayed = [relay and layer is not None for _, layer in items]
    sends = [(a, q, r) for a in range(n) for r in ((1, 2) if relayed[a] else (1, 2, 3))
             for q in (((r - 1), 2 - r) if relayed[a] else (0,))]
    hops = [(a, q) for a in range(n) if relayed[a] for q in (0, 1)]
    lands = sends + [(a, q, 3) for a, q in hops]

    def region(a, ref, lead, h, q):
        arr, layer = items[a]
        if layer is None:
            return ref.at[lead + (h,)]
        rh = arr.shape[1] // 2
        if relayed[a]:
            return ref.at[lead + (pl.ds(h * rh + q * (rh // 2), rh // 2),)]
        return ref.at[lead + (pl.ds(h * rh, rh),)]

    def copies(src, dst, scratch):
        stage, (in_sem, out_sem, ici_s, ici_r, hop_s, hop_r, fwd_s, fwd_r) = scratch[:n], scratch[n:]
        x, y, c, own = _mesh_pos()
        sib = (x, y, 1 - c)
        local = []
        for a in range(n):
            layer = items[a][1]
            local.append((pltpu.make_async_copy(src[a] if layer is None else src[a].at[layer], stage[a], in_sem.at[a]),
                          pltpu.make_async_copy(stage[a], dst[a].at[own], out_sem.at[a])))
        ici, hop, hop_in, fwd, got = {}, {}, {}, {}, {}
        for k, (a, q, r) in enumerate(sends):
            layer = items[a][1]
            ici[a, q, r] = pltpu.make_async_remote_copy(
                src_ref=region(a, src[a], () if layer is None else (layer,), c, q), dst_ref=region(a, dst[a], (own,), c, q),
                send_sem=ici_s.at[k], recv_sem=ici_r.at[k], device_id=_peer(x, y, c, r), device_id_type=MESH)
        for k, (a, q) in enumerate(hops):
            there = region(a, dst[a], (own ^ (q + 1),), c, q)
            hop[a, q] = pltpu.make_async_remote_copy(src_ref=there, dst_ref=there, send_sem=hop_s.at[k], recv_sem=hop_r.at[k],
                                                     device_id=_peer(x, y, c, 2 - q), device_id_type=MESH)
            here = region(a, dst[a], (own ^ 3,), c, q)
            hop_in[a, q] = pltpu.make_async_remote_copy(src_ref=here, dst_ref=here, send_sem=hop_s.at[k], recv_sem=hop_r.at[k],
                                                        device_id=_peer(x, y, c, 2 - q), device_id_type=MESH)
        for k, (a, q, r) in enumerate(lands):
            mine, theirs = region(a, dst[a], (own ^ r,), c, q), region(a, dst[a], (own ^ r,), 1 - c, q)
            fwd[a, q, r] = pltpu.make_async_remote_copy(src_ref=mine, dst_ref=mine, send_sem=fwd_s.at[k], recv_sem=fwd_r.at[k],
                                                        device_id=sib, device_id_type=MESH)
            got[a, q, r] = pltpu.make_async_remote_copy(src_ref=theirs, dst_ref=theirs, send_sem=fwd_s.at[k],
                                                        recv_sem=fwd_r.at[k], device_id=sib, device_id_type=MESH)
        return local, ici, hop, hop_in, fwd, got

    def start(src, dst, sems):
        local, ici, _, _, _, _ = copies(src, dst, sems)
        for slot in sends:
            ici[slot].start()
        for cp_in, _ in local:
            cp_in.start()
        for cp_in, cp_out in local:
            cp_in.wait()
            cp_out.start()

    def middle(src, dst, sems):
        _, ici, hop, hop_in, fwd, _ = copies(src, dst, sems)
        for a, q, r in sends:
            ici[a, q, r].wait_recv()
            if relayed[a] and q == r - 1:
                hop[a, q].start()
            fwd[a, q, r].start()
        for a, q in hops:
            hop_in[a, q].wait_recv()
            fwd[a, q, 3].start()

    def finish(src, dst, sems):
        local, ici, hop, _, fwd, got = copies(src, dst, sems)
        for slot in lands:
            got[slot].wait_recv()
        for slot in sends:
            ici[slot].wait_send()
        for slot in hops:
            hop[slot].wait_send()
        for slot in lands:
            fwd[slot].wait_send()
        for _, cp_out in local:
            cp_out.wait()

    out_shapes = [jax.ShapeDtypeStruct((N_CHIPS,) + (arr.shape if layer is None else arr.shape[1:]), arr.dtype)
                  for arr, layer in items]
    stage = [pltpu.VMEM(arr.shape if layer is None else arr.shape[1:], arr.dtype) for arr, layer in items]
    sems = ([pltpu.SemaphoreType.DMA((n,))] * 2 + [pltpu.SemaphoreType.DMA((len(sends),))] * 2
            + [pltpu.SemaphoreType.DMA((max(len(hops), 1),))] * 2 + [pltpu.SemaphoreType.DMA((len(lands),))] * 2)
    return Job([arr for arr, _ in items], out_shapes, stage + sems, start, middle, finish)


def _mixer_fwd(proj_ref, lg, lb, ws_ref, bsb_ref, cw_ref, mix_ref, xcbuf, halo_xc):
    u = proj_ref[:, 0 * WIDTH : 1 * WIDTH]
    v = proj_ref[:, 1 * WIDTH : 2 * WIDTH]
    za = proj_ref[:, 2 * WIDTH : 3 * WIDTH]
    h = proj_ref[:, 3 * WIDTH : 4 * WIDTH]
    gb = proj_ref[:, 4 * WIDTH : 5 * WIDTH]
    gc = proj_ref[:, 5 * WIDTH : 6 * WIDTH]
    zb = proj_ref[:, 6 * WIDTH : 7 * WIDTH]
    rs, vhat = _ln_stats(v)
    vl = vhat * lg + lb
    vlb = vl.astype(BF16)
    tile = proj_ref.shape[0]
    for j in range(tile // CHUNK):
        rows = slice(j * CHUNK, (j + 1) * CHUNK)
        for hd in range(HEADS):
            cols = slice(hd * CHUNK, (hd + 1) * CHUNK)
            mix_ref[rows, cols] = _mm(ws_ref[hd], vlb[rows, cols]) + bsb_ref[:, cols]
    mixed = mix_ref[...]
    siga = _sigmoid(za)
    sigb = _sigmoid(zb)
    xc = gc * h
    xcbuf[0:HALO, :] = halo_xc
    xcbuf[HALO : HALO + tile, :] = xc
    y = cw_ref[0:1, :] * xcbuf[HALO - 2 : HALO - 2 + tile, :] + cw_ref[1:2, :] * xcbuf[HALO - 1 : HALO - 1 + tile, :]
    y = y + cw_ref[2:3, :] * xc
    return dict(u=u, za=za, h=h, gb=gb, gc=gc, zb=zb, rs=rs, vhat=vhat, vlb=vlb, mixed=mixed, siga=siga, sigb=sigb,
                xc=xc, y=y)


def fwd_layer(layer, x, p, wts, small, job=None, head=None):
    win, wout, wg, wpp = wts
    ng, lg, lb, ws, bsb, cw, pg = small
    n_head = 0 if head is None else 2

    def body(*refs):
        (x_ref, p_ref, win_ref, wout_ref, wg_ref, wpp_ref, ng_ref, lg_ref, lb_ref, ws_ref, bsb_ref, cw_ref,
         pg_ref) = refs[:13]
        head_in = refs[13 : 13 + n_head]
        proj_ref, x2_ref, gate_ref, x3_ref, hnT_ref, catT_ref, hn2T_ref, pT_ref = refs[13 + n_head : 21 + n_head]
        head_out = refs[21 + n_head : 21 + 2 * n_head]
        mix_ref, xcbuf, carry, wcat, wsem = refs[21 + 2 * n_head :]
        i = pl.program_id(0)

        @pl.when(i == 0)
        def _():
            _load_side_by_side(win_ref, wcat, wsem)
            carry[...] = jnp.zeros_like(carry)
            for ref in head_out:
                ref[...] = jnp.zeros_like(ref)

        xv = x_ref[...]
        _, xh = _rms_stats(xv)
        hn = xh * ng_ref[...]
        hnT_ref[...] = hn.T.astype(BF16)
        proj_ref[...] = _mm(hn.astype(BF16), wcat[...])
        m = _mixer_fwd(proj_ref, lg_ref[...], lb_ref[...], ws_ref, bsb_ref, cw_ref, mix_ref, xcbuf, carry[...])
        carry[...] = m["xc"][FWD_TILE - HALO : FWD_TILE, :]
        out_a = (m["u"] * m["mixed"]) * (m["za"] * m["siga"])
        out_b = (m["gb"] * m["y"]) * (m["zb"] * m["sigb"])
        cat = jnp.concatenate([out_a, out_b], axis=-1)
        catT_ref[...] = cat.T.astype(BF16)
        x2 = xv + _mm_rows(cat.astype(BF16), wout_ref)
        x2_ref[...] = x2
        _, xh2 = _rms_stats(x2)
        hn2 = xh2 * pg_ref[...]
        hn2T_ref[...] = hn2.T.astype(BF16)
        gate = _sigmoid(_mm_rows(hn2.astype(BF16), wg_ref))
        gate_ref[...] = gate
        pv = p_ref[...]
        pT_ref[...] = pv.T.astype(BF16)
        x3 = x2 + gate * _ple_proj(pv, wpp_ref)
        if head is None:
            x3_ref[...] = x3
        else:
            t_ref, gf_ref = head_in
            loss_ref, gg_ref = head_out
            r3, xh3 = _rms_stats(x3)
            gf = gf_ref[...]
            err = xh3 * gf - t_ref[...]
            loss_ref[...] += (0.5 / D_MODEL) * jnp.sum(err * err).reshape(1, 1)
            dy = err * (1.0 / D_MODEL)
            gg_ref[...] += jnp.sum(dy * xh3, axis=0, keepdims=True)
            x3_ref[...] = _rms_bwd(dy * gf, xh3, r3)

    tok = lambda w: pl.BlockSpec((FWD_TILE, w), lambda i: (i, 0))
    tokT = lambda w: pl.BlockSpec((None, w, FWD_TILE), lambda i: (i, 0, 0))
    once = lambda a: pl.BlockSpec(a.shape, lambda i: (0,) * a.ndim, pipeline_mode=pl.Buffered(1))
    f32 = lambda w: jax.ShapeDtypeStruct((SEQ, w), F32)
    bfT = lambda w: jax.ShapeDtypeStruct((SEQ // FWD_TILE, w, FWD_TILE), BF16)
    head_specs = [] if head is None else [tok(D_MODEL), _full((1, D_MODEL))]
    head_outs = [] if head is None else [_full((1, 1)), _full((1, D_MODEL))]
    head_shapes = [] if head is None else [jax.ShapeDtypeStruct((1, 1), F32), jax.ShapeDtypeStruct((1, D_MODEL), F32)]
    return _call(
        body, name=f"fwd_layer{layer}", grid=(SEQ // FWD_TILE,),
        in_specs=[tok(D_MODEL), pl.BlockSpec((None, None, FWD_TILE, PLE), lambda i: (layer, 0, i, 0)),
                  ANY, once(wout), once(wg), once(wpp),
                  _const((1, D_MODEL), (layer,)), _const((1, WIDTH), (layer,)), _const((1, WIDTH), (layer,)),
                  _const((HEADS, CHUNK, CHUNK), (layer,)), _const((CHUNK, WIDTH), (layer,)), _const((3, WIDTH), (layer,)),
                  _const((1, D_MODEL), (layer,))] + head_specs,
        out_specs=[tok(PROJ), tok(D_MODEL), tok(D_MODEL), tok(D_MODEL),
                   tokT(D_MODEL), tokT(D_MODEL), tokT(D_MODEL), tokT(PLE)] + head_outs,
        out_shape=[f32(PROJ), f32(D_MODEL), f32(D_MODEL), f32(D_MODEL),
                   bfT(D_MODEL), bfT(D_MODEL), bfT(D_MODEL), bfT(PLE)] + head_shapes,
        scratch=[pltpu.VMEM((FWD_TILE, WIDTH), F32), pltpu.VMEM((HALO + FWD_TILE, WIDTH), F32), pltpu.VMEM((HALO, WIDTH), F32),
                 pltpu.VMEM((D_MODEL, PROJ), BF16), pltpu.SemaphoreType.DMA((N_CHIPS,))],
        args=(x, p, win, wout, wg, wpp, ng, lg, lb, ws, bsb, cw, pg) + (() if head is None else tuple(head)),
        job=job, mid=SEQ // FWD_TILE // 2)


def bwd_layer(layer, dx3, x2, gate, p, x, proj, wts, small, wsT, tril):
    win, wout, wg, wpp = wts
    ng, lg, lb, ws, bsb, cw, pg = small

    def body(dx3_ref, x2_ref, gate_ref, p_ref, x_ref, proj_ref, halo_ref, win_ref, wout_ref, wg_ref, wpp_ref, ng_ref, lg_ref,
             lb_ref, ws_ref, wsT_ref, bsb_ref, cw_ref, pg_ref, tril_ref,
             dx_ref, dproj_ref, dpp_ref, dgl_ref, dx2b_ref, gn_ref, glg_ref, glb_ref, gws_ref, gbs_ref, gcw_ref, gpg_ref,
             mix_ref, xcbuf, dycbuf, dvl_ref, bs_acc, wcat, wsem):
        i = pl.program_id(0)

        @pl.when(i == 0)
        def _():
            _load_side_by_side(win_ref, wcat, wsem)
            for ref in (gn_ref, glg_ref, glb_ref, gws_ref, gcw_ref, gpg_ref, bs_acc):
                ref[...] = jnp.zeros_like(ref)
            dycbuf[TILE : TILE + HALO, :] = jnp.zeros((HALO, WIDTH), F32)

        dx3v = dx3_ref[...]
        gate_v = gate_ref[...]
        dpp_ref[...] = (dx3v * gate_v).astype(BF16)
        dgl = ((dx3v * _ple_proj(p_ref[...], wpp_ref)) * gate_v * (1.0 - gate_v)).astype(BF16)
        dgl_ref[...] = dgl
        dhn2 = _mm_nt_rows(dgl, wg_ref)
        r2, xh2 = _rms_stats(x2_ref[...])
        gpg_ref[...] += jnp.sum(dhn2 * xh2, axis=0, keepdims=True)
        dx2 = dx3v + _rms_bwd(dhn2 * pg_ref[...], xh2, r2)
        dx2b = dx2.astype(BF16)
        dx2b_ref[...] = dx2b
        dcat = _mm_nt_rows(dx2b, wout_ref)
        lgv = lg_ref[...]
        halo_xc = halo_ref[:, 5 * WIDTH : 6 * WIDTH] * halo_ref[:, 3 * WIDTH : 4 * WIDTH]
        halo_xc = jnp.where(i == N_TILES - 1, 0.0, halo_xc)
        m = _mixer_fwd(proj_ref, lgv, lb_ref[...], ws_ref, bsb_ref, cw_ref, mix_ref, xcbuf, halo_xc)
        u, za, h, gb, gc, zb = m["u"], m["za"], m["h"], m["gb"], m["gc"], m["zb"]
        mixed, siga, sigb, xc, y = m["mixed"], m["siga"], m["sigb"], m["xc"], m["y"]
        doa = dcat[:, 0:WIDTH]
        dob = dcat[:, WIDTH : 2 * WIDTH]
        sa = za * siga
        sb = zb * sigb
        doa_sa = doa * sa
        dproj_ref[:, 0 * WIDTH : 1 * WIDTH] = (doa_sa * mixed).astype(BF16)
        dmixed = doa_sa * u
        dza = (doa * (u * mixed)) * (siga * (1.0 + za * (1.0 - siga)))
        dproj_ref[:, 2 * WIDTH : 3 * WIDTH] = dza.astype(BF16)
        dob_sb = dob * sb
        dproj_ref[:, 4 * WIDTH : 5 * WIDTH] = (dob_sb * y).astype(BF16)
        dyc = dob_sb * gb
        dzb = (dob * (gb * y)) * (sigb * (1.0 + zb * (1.0 - sigb)))
        dproj_ref[:, 6 * WIDTH : 7 * WIDTH] = dzb.astype(BF16)
        dycbuf[0:TILE, :] = dyc
        dyc1 = dycbuf[1 : 1 + TILE, :]
        dyc2 = dycbuf[2 : 2 + TILE, :]
        dxc = cw_ref[2:3, :] * dyc + cw_ref[1:2, :] * dyc1 + cw_ref[0:1, :] * dyc2
        gcw_ref[0:1, :] += jnp.sum(xc * dyc2, axis=0, keepdims=True)
        gcw_ref[1:2, :] += jnp.sum(xc * dyc1, axis=0, keepdims=True)
        gcw_ref[2:3, :] += jnp.sum(xc * dyc, axis=0, keepdims=True)
        dycbuf[TILE : TILE + HALO, :] = dyc[0:HALO, :]
        dproj_ref[:, 5 * WIDTH : 6 * WIDTH] = (dxc * h).astype(BF16)
        dproj_ref[:, 3 * WIDTH : 4 * WIDTH] = (dxc * gc).astype(BF16)
        dmb = dmixed.astype(BF16)
        vlb = m["vlb"]
        bsum = jnp.zeros((CHUNK, WIDTH), F32)
        for j in range(TILE // CHUNK):
            rows = slice(j * CHUNK, (j + 1) * CHUNK)
            bsum = bsum + dmixed[rows, :]
            for hd in range(HEADS):
                cols = slice(hd * CHUNK, (hd + 1) * CHUNK)
                gws_ref[hd] += _mm_nt(dmb[rows, cols], vlb[rows, cols])
                dvl_ref[rows, cols] = _mm(wsT_ref[hd], dmb[rows, cols])
        bs_acc[...] += bsum
        dvl = dvl_ref[...]
        vhat = m["vhat"]
        glb_ref[...] += jnp.sum(dvl, axis=0, keepdims=True)
        glg_ref[...] += jnp.sum(dvl * vhat, axis=0, keepdims=True)
        dvh = dvl * lgv
        dv = m["rs"] * (dvh - jnp.mean(dvh, axis=-1, keepdims=True) - vhat * jnp.mean(dvh * vhat, axis=-1, keepdims=True))
        dproj_ref[:, 1 * WIDTH : 2 * WIDTH] = dv.astype(BF16)
        dhn = _mm_nt(dproj_ref[...], wcat[...])
        r1, xh = _rms_stats(x_ref[...])
        gn_ref[...] += jnp.sum(dhn * xh, axis=0, keepdims=True)
        dx_ref[...] = dx2 + _rms_bwd(dhn * ng_ref[...], xh, r1)

        @pl.when(i == N_TILES - 1)
        def _():
            for hd in range(HEADS):
                gws_ref[hd] = gws_ref[hd] * tril_ref[...]
                gbs_ref[hd : hd + 1, :] = jnp.sum(bs_acc[:, hd * CHUNK : (hd + 1) * CHUNK].T, axis=0, keepdims=True)

    rev = lambda w: pl.BlockSpec((TILE, w), lambda i: (N_TILES - 1 - i, 0))
    halo = pl.BlockSpec((HALO, PROJ), lambda i: (jnp.maximum((N_TILES - 1 - i) * (TILE // HALO) - 1, 0), 0))
    once = lambda a: pl.BlockSpec(a.shape, lambda i: (0,) * a.ndim, pipeline_mode=pl.Buffered(1))
    vec = lambda w: jax.ShapeDtypeStruct((1, w), F32)
    b16 = lambda w: jax.ShapeDtypeStruct((SEQ, w), BF16)
    outs, _ = _call(
        body, name=f"bwd_layer{layer}", grid=(N_TILES,),
        in_specs=[rev(D_MODEL), rev(D_MODEL), rev(D_MODEL),
                  pl.BlockSpec((None, None, TILE, PLE), lambda i: (layer, 0, N_TILES - 1 - i, 0)),
                  rev(D_MODEL), rev(PROJ), halo, ANY, once(wout), once(wg), once(wpp),
                  _const((1, D_MODEL), (layer,)), _const((1, WIDTH), (layer,)), _const((1, WIDTH), (layer,)),
                  _const((HEADS, CHUNK, CHUNK), (layer,)), _const((HEADS, CHUNK, CHUNK), (layer,)),
                  _const((CHUNK, WIDTH), (layer,)), _const((3, WIDTH), (layer,)), _const((1, D_MODEL), (layer,)),
                  _full((CHUNK, CHUNK))],
        out_specs=[rev(D_MODEL), rev(PROJ), rev(D_MODEL), rev(D_MODEL), rev(D_MODEL),
                   _full((1, D_MODEL)), _full((1, WIDTH)), _full((1, WIDTH)), _full((HEADS, CHUNK, CHUNK)),
                   _full((HEADS, CHUNK)), _full((3, WIDTH)), _full((1, D_MODEL))],
        out_shape=[jax.ShapeDtypeStruct((SEQ, D_MODEL), F32), b16(PROJ), b16(D_MODEL), b16(D_MODEL), b16(D_MODEL),
                   vec(D_MODEL), vec(WIDTH), vec(WIDTH), jax.ShapeDtypeStruct((HEADS, CHUNK, CHUNK), F32),
                   jax.ShapeDtypeStruct((HEADS, CHUNK), F32), jax.ShapeDtypeStruct((3, WIDTH), F32), vec(D_MODEL)],
        scratch=[pltpu.VMEM((TILE, WIDTH), F32), pltpu.VMEM((HALO + TILE, WIDTH), F32),
                 pltpu.VMEM((TILE + HALO, WIDTH), F32), pltpu.VMEM((TILE, WIDTH), F32), pltpu.VMEM((CHUNK, WIDTH), F32),
                 pltpu.VMEM((D_MODEL, PROJ), BF16), pltpu.SemaphoreType.DMA((N_CHIPS,))],
        args=(dx3, x2, gate, p, x, proj, proj, win, wout, wg, wpp, ng, lg, lb, ws, wsT, bsb, cw, pg, tril))
    return outs


def wgrad(pairs, n_tiles, col_blocked, tk, name, job=None):
    n = len(pairs)
    m_dim, n_dim, at = pairs[0][0].shape[1], pairs[0][1].shape[1], pairs[0][0].shape[2]
    tn = n_dim // n_tiles
    cb, mb = n_dim // N_CHIPS, m_dim // N_CHIPS
    per_tile = N_CHIPS // n_tiles
    assert col_blocked or n_tiles == 1

    def body(*refs):
        for a in range(n):
            a_ref, b_ref, o_ref = refs[2 * a], refs[2 * a + 1], refs[2 * n + a]

            @pl.when(pl.program_id(1) == 0)
            def _():
                o_ref[...] = jnp.zeros_like(o_ref)

            prod = _mm(a_ref[0], b_ref[0:at, :])
            for t in range(1, tk // at):
                prod = prod + _mm(a_ref[t], b_ref[t * at : (t + 1) * at, :])
            if col_blocked:
                for q in range(per_tile):
                    o_ref[q] += prod[:, q * cb : (q + 1) * cb]
            else:
                for q in range(N_CHIPS):
                    o_ref[q] += prod[q * mb : (q + 1) * mb, :]

            @pl.when(pl.program_id(1) == SEQ // tk - 1)
            def _():
                refs[3 * n + a][...] = o_ref[...].astype(BF16)

    mode = dict(pipeline_mode=pl.Buffered(1)) if n_tiles == 1 else {}
    if col_blocked:
        shape = (N_CHIPS, m_dim, cb)
        o_spec = pl.BlockSpec((per_tile, m_dim, cb), lambda j, k: (j, 0, 0), **mode)
    else:
        shape = (N_CHIPS, mb, n_dim)
        o_spec = pl.BlockSpec((N_CHIPS, mb, n_dim), lambda j, k: (0, 0, 0), **mode)
    outs, job_outs = _call(
        body, name=name, grid=(n_tiles, SEQ // tk),
        in_specs=[pl.BlockSpec((tk // at, m_dim, at), lambda j, k: (k, 0, 0)),
                  pl.BlockSpec((tk, tn), lambda j, k: (k, j))] * n,
        out_specs=[o_spec] * (2 * n), out_shape=[jax.ShapeDtypeStruct(shape, F32)] * n + [jax.ShapeDtypeStruct(shape, BF16)] * n,
        args=[t for pair in pairs for t in pair], job=job)
    return outs[:n], outs[n:], job_outs


def to_sibling_job(grads):
    n = len(grads)

    def copies(src, dst, sems):
        x, y, c, _ = _mesh_pos()
        out = []
        for a in range(n):
            rh = grads[a].shape[1] // 2
            out.append(pltpu.make_async_remote_copy(
                src_ref=src[a].at[:, pl.ds((1 - c) * rh, rh), :], dst_ref=dst[a], send_sem=sems[0].at[a],
                recv_sem=sems[1].at[a], device_id=(x, y, 1 - c), device_id_type=MESH))
        return out

    def start(src, dst, sems):
        for cp in copies(src, dst, sems):
            cp.start()

    def finish(src, dst, sems):
        for cp in copies(src, dst, sems):
            cp.wait()

    out_shapes = [jax.ShapeDtypeStruct((N_CHIPS, g.shape[1] // 2, g.shape[2]), g.dtype) for g in grads]
    return Job(grads, out_shapes, [pltpu.SemaphoreType.DMA((n,))] * 2, start, None, finish)


def rs_add_sibling(ids, grads, recvd, name):
    n = len(grads)

    def body(ids_ref, *refs):
        g, r = refs[:n], refs[n : 2 * n]
        pb, pf = refs[2 * n : 3 * n], refs[3 * n :]
        k = pl.program_id(0)
        for a in range(n):
            s = g[a][...] + r[a][...].astype(F32)
            pb[a][...] = s.astype(BF16)

            @pl.when(k == ids_ref[1])
            def _():
                pf[a][...] = s

    in_specs, out_specs, out_shape = [], [], []
    for g in grads:
        rh, cc = g.shape[1] // 2, g.shape[2]
        in_specs.append(pl.BlockSpec((None, rh, cc), lambda k, ids: (k, ids[0], 0)))
    for g in grads:
        rh, cc = g.shape[1] // 2, g.shape[2]
        in_specs.append(pl.BlockSpec((None, rh, cc), lambda k, ids: (k, 0, 0)))
        out_specs.append(pl.BlockSpec((None, rh, cc), lambda k, ids: (k, 0, 0)))
        out_shape.append(jax.ShapeDtypeStruct((N_CHIPS, rh, cc), BF16))
    for g in grads:
        rh, cc = g.shape[1] // 2, g.shape[2]
        out_specs.append(pl.BlockSpec((rh, cc), lambda k, ids: (0, 0)))
        out_shape.append(jax.ShapeDtypeStruct((rh, cc), F32))
    outs = pl.pallas_call(
        body, name=name, out_shape=out_shape,
        grid_spec=pltpu.PrefetchScalarGridSpec(num_scalar_prefetch=1, grid=(N_CHIPS,), in_specs=in_specs,
                                               out_specs=out_specs),
        compiler_params=pltpu.CompilerParams(dimension_semantics=("arbitrary",), vmem_limit_bytes=VMEM_LIMIT),
    )(ids, *grads, *recvd)
    return list(outs[:n]), list(outs[n:])


def to_owners_job(partials):
    n = len(partials)

    def copies(src, dst, sems):
        x, y, c, own = _mesh_pos()
        out = []
        for a in range(n):
            for r in (1, 2, 3):
                out.append(pltpu.make_async_remote_copy(
                    src_ref=src[a].at[own ^ r], dst_ref=dst[a].at[r - 1], send_sem=sems[0].at[3 * a + r - 1],
                    recv_sem=sems[1].at[3 * a + r - 1], device_id=_peer(x, y, c, r), device_id_type=MESH))
        return out

    def start(src, dst, sems):
        for cp in copies(src, dst, sems):
            cp.start()

    def finish(src, dst, sems):
        for cp in copies(src, dst, sems):
            cp.wait()

    out_shapes = [jax.ShapeDtypeStruct((3,) + p.shape[1:], BF16) for p in partials]
    return Job(partials, out_shapes, [pltpu.SemaphoreType.DMA((3 * n,))] * 2, start, None, finish)


def rs_add_owners(layer, ids, own_f32, recvd, prev, name):
    n = len(own_f32)
    nb = 2

    def body(ids_ref, *refs):
        o, r, f = refs[:n], refs[n : 2 * n], refs[-n:]
        for a in range(n):
            f[a][...] = ((o[a][...] + r[a][0].astype(F32)) + r[a][1].astype(F32)) + r[a][2].astype(F32)

    in_specs, out_specs, out_shape = [], [], []
    for o in own_f32:
        in_specs.append(pl.BlockSpec((o.shape[0] // nb, o.shape[1]), lambda j, ids: (j, 0)))
    for o in own_f32:
        in_specs.append(pl.BlockSpec((3, o.shape[0] // nb, o.shape[1]), lambda j, ids: (0, j, 0)))
        out_specs.append(pl.BlockSpec((None, o.shape[0] // nb, o.shape[1]), lambda j, ids: (layer, ids[0] * nb + j, 0)))
        out_shape.append(jax.ShapeDtypeStruct((DEPTH, 2 * o.shape[0], o.shape[1]), F32))
    args, aliases = [ids, *own_f32, *recvd], {}
    if prev is not None:
        in_specs += [ANY] * n
        args += list(prev)
        aliases = {1 + 2 * n + a: a for a in range(n)}
    return pl.pallas_call(
        body, name=name, out_shape=out_shape, input_output_aliases=aliases,
        grid_spec=pltpu.PrefetchScalarGridSpec(num_scalar_prefetch=1, grid=(nb,), in_specs=in_specs, out_specs=out_specs),
        compiler_params=pltpu.CompilerParams(dimension_semantics=("arbitrary",), vmem_limit_bytes=VMEM_LIMIT),
    )(*args)


def exchange_halves_job(layer, full):
    n = len(full)

    def copies(src, dst, sems):
        x, y, c, _ = _mesh_pos()
        out = []
        for a in range(n):
            rh = full[a].shape[1] // 2
            out.append(pltpu.make_async_remote_copy(
                src_ref=src[a].at[layer, pl.ds(c * rh, rh), :], dst_ref=dst[a].at[layer, pl.ds(c * rh, rh), :],
                send_sem=sems[0].at[a], recv_sem=sems[1].at[a], device_id=(x, y, 1 - c), device_id_type=MESH))
        return out

    def start(src, dst, sems):
        for cp in copies(src, dst, sems):
            cp.start()

    def finish(src, dst, sems):
        for cp in copies(src, dst, sems):
            cp.wait()

    out_shapes = [jax.ShapeDtypeStruct(f.shape, F32) for f in full]
    return Job(full, out_shapes, [pltpu.SemaphoreType.DMA((n,))] * 2, start, None, finish, {a: a for a in range(n)})


def _adamw(w, g, m, v):
    m2 = ADAM_B1 * m + (1.0 - ADAM_B1) * g
    v2 = ADAM_B2 * v + (1.0 - ADAM_B2) * (g * g)
    delta = -ADAM_LR * ((m2 / ADAM_C1) / (jnp.sqrt(v2 / ADAM_C2) + ADAM_EPS) + ADAM_WD * w)
    return delta, m2, v2


def adamw_big(ws, gs, ms, vs, name, job=None):
    n = len(ws)
    nb = 4

    def body(*refs):
        for a in range(n):
            w, g, m, v = (refs[j * n + a][...] for j in range(4))
            d, m2, v2 = _adamw(w, g, m, v)
            refs[4 * n + a][...] = d
            refs[5 * n + a][...] = m2
            refs[6 * n + a][...] = v2

    specs = [pl.BlockSpec((None, w.shape[1] // nb, w.shape[2]), lambda l, j: (l, j, 0)) for w in ws]
    shapes = [jax.ShapeDtypeStruct(w.shape, F32) for w in ws]
    outs, job_outs = _call(body, name=name, grid=(DEPTH, nb), in_specs=specs * 4, out_specs=specs * 3,
                           out_shape=shapes * 3, args=(*ws, *gs, *ms, *vs), job=job)
    return (outs[:n], outs[n : 2 * n], outs[2 * n :]), job_outs


def small_to_sibling_job(gs):
    n = len(gs)

    def copies(src, dst, sems):
        x, y, c, _ = _mesh_pos()
        return [pltpu.make_async_remote_copy(src_ref=src[a], dst_ref=dst[a], send_sem=sems[0].at[a], recv_sem=sems[1].at[a],
                                             device_id=(x, y, 1 - c), device_id_type=MESH) for a in range(n)]

    def start(*refs):
        for cp in copies(*refs):
            cp.start()

    def finish(*refs):
        for cp in copies(*refs):
            cp.wait()

    return Job(gs, [jax.ShapeDtypeStruct(g.shape, F32) for g in gs], [pltpu.SemaphoreType.DMA((n,))] * 2, start, None, finish)


def small_chip_sum(gs, recvd):
    n = len(gs)

    def body(*refs):
        for a in range(n):
            refs[2 * n + a][...] = refs[a][...] + refs[n + a][...]

    return pl.pallas_call(body, name="small_chip_sum", out_shape=[jax.ShapeDtypeStruct(g.shape, F32) for g in gs])(*gs, *recvd)


def small_to_chips_job(chip_sums):
    n = len(chip_sums)

    def copies(src, dst, sems):
        x, y, c, _ = _mesh_pos()
        return [pltpu.make_async_remote_copy(src_ref=src[a], dst_ref=dst[a].at[r - 1], send_sem=sems[0].at[3 * a + r - 1],
                                             recv_sem=sems[1].at[3 * a + r - 1], device_id=_peer(x, y, c, r),
                                             device_id_type=MESH) for a in range(n) for r in (1, 2, 3)]

    def start(*refs):
        for cp in copies(*refs):
            cp.start()

    def finish(*refs):
        for cp in copies(*refs):
            cp.wait()

    return Job(chip_sums, [jax.ShapeDtypeStruct((3,) + s.shape, F32) for s in chip_sums],
               [pltpu.SemaphoreType.DMA((3 * n,))] * 2, start, None, finish)


def small_finish(chip_sums, from_chips, ws, ms, vs):
    n, n_w = len(chip_sums), len(ws)

    def body(*refs):
        cs, fc = refs[:n], refs[n : 2 * n]
        w, m, v = (refs[2 * n + j * n_w : 2 * n + (j + 1) * n_w] for j in range(3))
        outs = refs[2 * n + 3 * n_w :]
        own = 2 * lax.axis_index("x") + lax.axis_index("y")
        for a in range(n):
            mine = cs[a][...]

            def of_chip(k):
                r = own ^ k
                return jnp.where(r == 0, mine, jnp.where(r == 1, fc[a][0], jnp.where(r == 2, fc[a][1], fc[a][2])))

            tot = ((of_chip(0) + of_chip(1)) + of_chip(2)) + of_chip(3)
            outs[a][...] = tot
            if a < n_w:
                d, m2, v2 = _adamw(w[a][...], tot, m[a][...], v[a][...])
                outs[n + a][...] = d
                outs[n + n_w + a][...] = m2
                outs[n + 2 * n_w + a][...] = v2

    shapes = [jax.ShapeDtypeStruct(s.shape, F32) for s in chip_sums]
    outs = pl.pallas_call(body, name="small_finish", out_shape=shapes + shapes[:n_w] * 3)(
        *chip_sums, *from_chips, *ws, *ms, *vs)
    return outs[:n], outs[n : n + n_w], outs[n + n_w : n + 2 * n_w], outs[n + 2 * n_w :]


def merge_jobs(*jobs):
    def parts(refs, counts):
        out, pos = [], 0
        for cnt in counts:
            out.append(refs[pos : pos + cnt])
            pos += cnt
        return out

    def run(phase):
        def go(ins, outs, sems):
            for job, i, o, s in zip(jobs, parts(ins, [len(j.ins) for j in jobs]),
                                    parts(outs, [len(j.out_shapes) for j in jobs]), parts(sems, [len(j.sems) for j in jobs])):
                getattr(job, phase)(i, o, s)
        return go

    assert all(j.middle is None and not j.aliases for j in jobs)
    return Job([a for j in jobs for a in j.ins], [a for j in jobs for a in j.out_shapes], [a for j in jobs for a in j.sems],
               run("start"), None, run("finish"))


def adamw_rows(w, g, m, v):
    def body(w_ref, g_ref, m_ref, v_ref, d_ref, m2_ref, v2_ref):
        d, m2, v2 = _adamw(w_ref[...], g_ref[...], m_ref[...], v_ref[...])
        d_ref[...] = d
        m2_ref[...] = m2
        v2_ref[...] = v2

    return pl.pallas_call(body, name="adamw_conv", out_shape=[jax.ShapeDtypeStruct(w.shape, F32)] * 3)(w, g, m, v)


def kernel(x, p, norm_g, w_in, ln_v_g, ln_v_b, w_s, b_s, conv_w, w_out, ple_norm_g, w_ple_gate, w_ple_proj, final_g, loss_target, m_norm_g, m_w_in, m_ln_v_g, m_ln_v_b, m_w_s, m_b_s, m_conv_w, m_w_out, m_ple_norm_g, m_w_ple_gate, m_w_ple_proj, m_final_g, v_norm_g, v_w_in, v_ln_v_g, v_ln_v_b, v_w_s, v_b_s, v_conv_w, v_w_out, v_ple_norm_g, v_w_ple_gate, v_w_ple_proj, v_final_g):
    cx, cy, cc = lax.axis_index("x"), lax.axis_index("y"), lax.axis_index("c")
    own = 2 * cx + cy
    ids = jnp.stack([cc, own]).astype(jnp.int32)

    cw_rows = jnp.transpose(conv_w, (0, 2, 1))
    shards = [w_in.astype(BF16), w_out.astype(BF16), w_ple_gate.astype(BF16), w_ple_proj.astype(BF16)]
    *wts0, g_cw = run_job(gather_job([(s, 0) for s in shards] + [(cw_rows, None)], relay=True), "gather_weights0")
    cw_full = jnp.transpose(g_cw, (1, 2, 0, 3)).reshape(DEPTH, 3, WIDTH)
    tril = jnp.tril(jnp.ones((CHUNK, CHUNK), F32))
    ws_masked = w_s * tril[None, None]
    ws_b = ws_masked.astype(BF16)
    wsT_b = jnp.swapaxes(ws_masked, 2, 3).astype(BF16)
    bsb = jnp.repeat(jnp.swapaxes(b_s, 1, 2), CHUNK, axis=2)
    small = (norm_g[:, None, :], ln_v_g[:, None, :], ln_v_b[:, None, :], ws_b, bsb, cw_full, ple_norm_g[:, None, :])

    saved0, wts1 = fwd_layer(0, x[0], p, wts0, small, job=gather_job([(s, 1) for s in shards]))
    saved1, _ = fwd_layer(1, saved0[3], p, wts1, small, head=(loss_target[0], final_g[None, :]))
    saved, xs = [saved0[:8], saved1[:8]], [x[0], saved0[3]]
    dx, loss_part, g_final = saved1[3], saved1[8], saved1[9]

    def chip_sums(arrs, arrs_bf16, tag):
        recvd = run_job(to_sibling_job(arrs_bf16), f"rs_to_sibling{tag}")
        return rs_add_sibling(ids, arrs, recvd, f"rs_add_sibling{tag}")

    small_g = [None] * DEPTH
    proj, x2, gate, _, hnT, catT, hn2T, pT = saved[1]
    dx, dproj_b, dpp_b, dgl_b, dx2_b, *g_sm = bwd_layer(1, dx, x2, gate, p, xs[1], proj, wts1, small, wsT_b, tril)
    small_g[1] = g_sm[:5] + [g_sm[6], g_sm[5]]
    g1_in, b1_in, _ = wgrad([(hnT, dproj_b)], 1, True, 1024, "wgrad_in1")
    g1_og, b1_og, _ = wgrad([(catT, dx2_b), (hn2T, dgl_b)], 1, False, 1024, "wgrad_outgate1")
    g1_pp, b1_pp, _ = wgrad([(pT, dpp_b)], 1, True, 2048, "wgrad_pp1")
    part1 = chip_sums(g1_in + g1_og + g1_pp, b1_in + b1_og + b1_pp, "1")
    proj, x2, gate, _, hnT, catT, hn2T, pT = saved[0]
    dx, dproj_b, dpp_b, dgl_b, dx2_b, *g_sm = bwd_layer(0, dx, x2, gate, p, xs[0], proj, wts0, small, wsT_b, tril)
    small_g[0] = g_sm[:5] + [g_sm[6], g_sm[5]]
    def both(j):
        return jnp.concatenate([small_g[0][j], small_g[1][j]], axis=0)

    g_small = [both(0), both(1), both(2), both(3).reshape(DEPTH * HEADS * CHUNK, CHUNK), both(4), both(5), g_final, both(6),
               jnp.broadcast_to(loss_part, (1, 128))]
    g0_in, b0_in, landed = wgrad([(hnT, dproj_b)], 1, True, 1024, "wgrad_in0",
                                 job=merge_jobs(to_owners_job(part1[0]), small_to_sibling_job(g_small)))
    from1, small_sib = landed[:4], landed[4:]
    full = rs_add_owners(1, ids, part1[1], from1, None, "rs_add_owners1")
    full = run_job(exchange_halves_job(1, full), "rs_exchange_halves1")
    small_chip = small_chip_sum(g_small, small_sib)
    part0b = chip_sums(g0_in, b0_in, "0b")
    g0_og, b0_og, from0b = wgrad([(catT, dx2_b), (hn2T, dgl_b)], 1, False, 1024, "wgrad_outgate0",
                                 job=to_owners_job(part0b[0]))
    full_b = rs_add_owners(0, ids, part0b[1], from0b, full[:1], "rs_add_owners0b")
    g0_pp, b0_pp, small_chips = wgrad([(pT, dpp_b)], 1, True, 2048, "wgrad_pp0", job=small_to_chips_job(small_chip))
    part0a = chip_sums(g0_og + g0_pp, b0_og + b0_pp, "0a")
    from0a = run_job(to_owners_job(part0a[0]), "rs_to_owners0a")
    full_a = rs_add_owners(0, ids, part0a[1], from0a, full[1:], "rs_add_owners0a")
    gw_in, gw_out, gw_gate, gw_pp = run_job(exchange_halves_job(0, list(full_b) + list(full_a)), "rs_exchange_halves0")
    ((d_in, d_out, d_gate, d_pp), (m_in, m_out, m_gate, m_pp), (v_in, v_out, v_gate, v_pp)), _ = adamw_big(
        [w_in, w_out, w_ple_gate, w_ple_proj], [gw_in, gw_out, gw_gate, gw_pp],
        [m_w_in, m_w_out, m_w_ple_gate, m_w_ple_proj], [v_w_in, v_w_out, v_w_ple_gate, v_w_ple_proj], "adamw_big")

    flat = lambda a: a.reshape(-1, a.shape[-1])
    gsum, dsm, msm, vsm = small_finish(
        small_chip, small_chips, [flat(a) for a in (norm_g, ln_v_g, ln_v_b, w_s, b_s, ple_norm_g, final_g[None])],
        [flat(a) for a in (m_norm_g, m_ln_v_g, m_ln_v_b, m_w_s, m_b_s, m_ple_norm_g, m_final_g[None])],
        [flat(a) for a in (v_norm_g, v_ln_v_g, v_ln_v_b, v_w_s, v_b_s, v_ple_norm_g, v_final_g[None])])
    like = [norm_g, ln_v_g, ln_v_b, w_s, b_s, ple_norm_g, final_g]
    gs, ds, m2s, v2s = ([a.reshape(b.shape) for a, b in zip(res, like)] for res in (gsum, dsm, msm, vsm))
    loss = gsum[8][0, 0]
    g_cw_own = lax.dynamic_slice_in_dim(gsum[7].reshape(DEPTH, 3, WIDTH), own * (WIDTH // N_CHIPS), WIDTH // N_CHIPS, axis=2)
    rows2 = lambda a: a.reshape(DEPTH * 3, WIDTH // N_CHIPS)
    d_cw, m_cw, v_cw = adamw_rows(rows2(cw_rows), rows2(g_cw_own), rows2(jnp.transpose(m_conv_w, (0, 2, 1))),
                                  rows2(jnp.transpose(v_conv_w, (0, 2, 1))))
    back = lambda a: jnp.transpose(a.reshape(DEPTH, 3, WIDTH // N_CHIPS), (0, 2, 1))
    g_conv = jnp.transpose(g_cw_own, (0, 2, 1))

    def ordered(sm, cw_v, w_in_v, w_out_v, gate_v, pp_v):
        return [sm[0], w_in_v, sm[1], sm[2], sm[3], sm[4], cw_v, w_out_v, sm[5], gate_v, pp_v, sm[6]]

    grads = ordered(gs, g_conv, gw_in, gw_out, gw_gate, gw_pp)
    deltas = ordered(ds, back(d_cw), d_in, d_out, d_gate, d_pp)
    new_m = ordered(m2s, back(m_cw), m_in, m_out, m_gate, m_pp)
    new_v = ordered(v2s, back(v_cw), v_in, v_out, v_gate, v_pp)
    return (loss, dx[None], *grads, *deltas, *new_m, *new_v)
```

```python
import jax
import jax.numpy as jnp
from jax import lax
from jax.experimental import pallas as pl
from jax.experimental.pallas import tpu as pltpu

F32 = jnp.float32
BF16 = jnp.bfloat16

SEQ = 8192
D_MODEL = 1024
WIDTH = 512
PROJ = 7 * WIDTH
N_CHIPS = 4
COL_BLK = PROJ // N_CHIPS
PLE = 256
HEADS = 4
CHUNK = 128
DEPTH = 2
EPS = 1e-6
TILE = 256
N_TILES = SEQ // TILE
FWD_TILE = 512
HALO = 8
VMEM_LIMIT = 60 * 1024 * 1024

ADAM_LR, ADAM_B1, ADAM_B2, ADAM_EPS, ADAM_WD, ADAM_STEP = 0.001, 0.9, 0.999, 1e-08, 0.01, 10
ADAM_C1 = 1.0 - ADAM_B1**ADAM_STEP
ADAM_C2 = 1.0 - ADAM_B2**ADAM_STEP

MESH = pl.DeviceIdType.MESH
ANY = pl.BlockSpec(memory_space=pl.ANY)


def _mm(a, b):
    return lax.dot_general(a, b, (((1,), (0,)), ((), ())), preferred_element_type=F32)


def _mm_nt(a, b):
    return lax.dot_general(a, b, (((1,), (1,)), ((), ())), preferred_element_type=F32)


def _mm_rows(a, w_ref):
    blk = w_ref.shape[1]
    acc = _mm(a[:, 0:blk], w_ref[0])
    for k in range(1, N_CHIPS):
        acc = acc + _mm(a[:, k * blk : (k + 1) * blk], w_ref[k])
    return acc


def _ple_proj(p_tile, wpp_ref):
    pb = p_tile.astype(BF16)
    return jnp.concatenate([_mm(pb, wpp_ref[k]) for k in range(N_CHIPS)], axis=-1)


def _mm_nt_rows(a, w_ref):
    return jnp.concatenate([_mm_nt(a, w_ref[k]) for k in range(N_CHIPS)], axis=-1)


def _load_side_by_side(w_hbm, w_vmem, sems):
    copies = [pltpu.make_async_copy(w_hbm.at[k], w_vmem.at[:, pl.ds(k * COL_BLK, COL_BLK)], sems.at[k])
              for k in range(N_CHIPS)]
    for cp in copies:
        cp.start()
    for cp in copies:
        cp.wait()


def _sigmoid(z):
    return 1.0 / (1.0 + jnp.exp(-z))


def _rms_stats(x):
    r = lax.rsqrt(jnp.mean(x * x, axis=-1, keepdims=True) + EPS)
    return r, x * r


def _rms_bwd(dyg, xh, r):
    return r * (dyg - xh * jnp.mean(dyg * xh, axis=-1, keepdims=True))


def _ln_stats(v):
    mu = jnp.mean(v, axis=-1, keepdims=True)
    vc = v - mu
    rs = lax.rsqrt(jnp.mean(vc * vc, axis=-1, keepdims=True) + EPS)
    return rs, vc * rs


def _mesh_pos():
    x, y, c = lax.axis_index("x"), lax.axis_index("y"), lax.axis_index("c")
    return x, y, c, 2 * x + y


def _peer(x, y, c, r):
    return ((1 - x) if (r >> 1) else x, (1 - y) if (r & 1) else y, c)


def _full(shape):
    return pl.BlockSpec(shape, lambda *_: (0,) * len(shape))


def _const(shape, pos):
    return pl.BlockSpec((None,) * len(pos) + tuple(shape), lambda *_: tuple(pos) + (0,) * len(shape))


class Job:
    def __init__(self, ins, out_shapes, sems, start, middle, finish, aliases=None):
        self.ins, self.out_shapes, self.sems = list(ins), list(out_shapes), list(sems)
        self.start, self.middle, self.finish = start, middle, finish
        self.aliases = aliases or {}


def run_job(job, name):
    ni, no = len(job.ins), len(job.out_shapes)

    def body(*refs):
        parts = (refs[:ni], refs[ni : ni + no], refs[ni + no :])
        job.start(*parts)
        if job.middle is not None:
            job.middle(*parts)
        job.finish(*parts)

    return pl.pallas_call(
        body, name=name, out_shape=job.out_shapes, in_specs=[ANY] * ni, out_specs=[ANY] * no, scratch_shapes=job.sems,
        input_output_aliases=job.aliases, compiler_params=pltpu.CompilerParams(has_side_effects=True),
    )(*job.ins)


def _call(body, *, name, grid, in_specs, out_specs, out_shape, args, scratch=(), job=None, mid=None):
    params = pltpu.CompilerParams(dimension_semantics=("arbitrary",) * len(grid), vmem_limit_bytes=VMEM_LIMIT,
                                  has_side_effects=job is not None)
    n_in, n_out, n_sc = len(in_specs), len(out_specs), len(scratch)
    if job is None:
        outs = pl.pallas_call(body, name=name, grid=grid, in_specs=in_specs, out_specs=out_specs, out_shape=out_shape,
                              scratch_shapes=list(scratch), compiler_params=params)(*args)
        return list(outs), []
    ji, jo = len(job.ins), len(job.out_shapes)
    assert not job.aliases and (job.middle is None or len(grid) == 1)

    def wrapped(*refs):
        ins, jin = refs[:n_in], refs[n_in : n_in + ji]
        o0 = n_in + ji
        outs, jout = refs[o0 : o0 + n_out], refs[o0 + n_out : o0 + n_out + jo]
        s0 = o0 + n_out + jo
        sc, jsem = refs[s0 : s0 + n_sc], refs[s0 + n_sc :]
        step = pl.program_id(0)
        for d in range(1, len(grid)):
            step = step * grid[d] + pl.program_id(d)
        n_steps = 1
        for g in grid:
            n_steps *= g

        @pl.when(step == 0)
        def _():
            job.start(jin, jout, jsem)

        if job.middle is not None:
            @pl.when(step == mid)
            def _():
                job.middle(jin, jout, jsem)

        body(*ins, *outs, *sc)

        @pl.when(step == n_steps - 1)
        def _():
            job.finish(jin, jout, jsem)

    outs = pl.pallas_call(
        wrapped, name=name, grid=grid, in_specs=list(in_specs) + [ANY] * ji, out_specs=list(out_specs) + [ANY] * jo,
        out_shape=list(out_shape) + job.out_shapes, scratch_shapes=list(scratch) + job.sems, compiler_params=params,
    )(*args, *job.ins)
    return list(outs[:n_out]), list(outs[n_out:])


def gather_job(items, relay=False):
    n = len(items)
    relayed = [relay and layer is not None for _, layer in items]
    sends = [(a, q, r) for a in range(n) for r in ((1, 2) if relayed[a] else (1, 2, 3))
             for q in (((r - 1), 2 - r) if relayed[a] else (0,))]
    hops = [(a, q) for a in range(n) if relayed[a] for q in (0, 1)]
    lands = sends + [(a, q, 3) for a, q in hops]

    def region(a, ref, lead, h, q):
        arr, layer = items[a]
        if layer is None:
            return ref.at[lead + (h,)]
        rh = arr.shape[1] // 2
        if relayed[a]:
            return ref.at[lead + (pl.ds(h * rh + q * (rh // 2), rh // 2),)]
        return ref.at[lead + (pl.ds(h * rh, rh),)]

    def copies(src, dst, scratch):
        stage, (in_sem, out_sem, ici_s, ici_r, hop_s, hop_r, fwd_s, fwd_r) = scratch[:n], scratch[n:]
        x, y, c, own = _mesh_pos()
        sib = (x, y, 1 - c)
        local = []
        for a in range(n):
            layer = items[a][1]
            local.append((pltpu.make_async_copy(src[a] if layer is None else src[a].at[layer], stage[a], in_sem.at[a]),
                          pltpu.make_async_copy(stage[a], dst[a].at[own], out_sem.at[a])))
        ici, hop, hop_in, fwd, got = {}, {}, {}, {}, {}
        for k, (a, q, r) in enumerate(sends):
            layer = items[a][1]
            ici[a, q, r] = pltpu.make_async_remote_copy(
                src_ref=region(a, src[a], () if layer is None else (layer,), c, q), dst_ref=region(a, dst[a], (own,), c, q),
                send_sem=ici_s.at[k], recv_sem=ici_r.at[k], device_id=_peer(x, y, c, r), device_id_type=MESH)
        for k, (a, q) in enumerate(hops):
            there = region(a, dst[a], (own ^ (q + 1),), c, q)
            hop[a, q] = pltpu.make_async_remote_copy(src_ref=there, dst_ref=there, send_sem=hop_s.at[k], recv_sem=hop_r.at[k],
                                                     device_id=_peer(x, y, c, 2 - q), device_id_type=MESH)
            here = region(a, dst[a], (own ^ 3,), c, q)
            hop_in[a, q] = pltpu.make_async_remote_copy(src_ref=here, dst_ref=here, send_sem=hop_s.at[k], recv_sem=hop_r.at[k],
                                                        device_id=_peer(x, y, c, 2 - q), device_id_type=MESH)
        for k, (a, q, r) in enumerate(lands):
            mine, theirs = region(a, dst[a], (own ^ r,), c, q), region(a, dst[a], (own ^ r,), 1 - c, q)
            fwd[a, q, r] = pltpu.make_async_remote_copy(src_ref=mine, dst_ref=mine, send_sem=fwd_s.at[k], recv_sem=fwd_r.at[k],
                                                        device_id=sib, device_id_type=MESH)
            got[a, q, r] = pltpu.make_async_remote_copy(src_ref=theirs, dst_ref=theirs, send_sem=fwd_s.at[k],
                                                        recv_sem=fwd_r.at[k], device_id=sib, device_id_type=MESH)
        return local, ici, hop, hop_in, fwd, got

    def start(src, dst, sems):
        local, ici, _, _, _, _ = copies(src, dst, sems)
        for slot in sends:
            ici[slot].start()
        for cp_in, _ in local:
            cp_in.start()
        for cp_in, cp_out in local:
            cp_in.wait()
            cp_out.start()

    def middle(src, dst, sems):
        _, ici, hop, hop_in, fwd, _ = copies(src, dst, sems)
        for a, q, r in sends:
            ici[a, q, r].wait_recv()
            if relayed[a] and q == r - 1:
                hop[a, q].start()
            fwd[a, q, r].start()
        for a, q in hops:
            hop_in[a, q].wait_recv()
            fwd[a, q, 3].start()

    def finish(src, dst, sems):
        local, ici, hop, _, fwd, got = copies(src, dst, sems)
        for slot in lands:
            got[slot].wait_recv()
        for slot in sends:
            ici[slot].wait_send()
        for slot in hops:
            hop[slot].wait_send()
        for slot in lands:
            fwd[slot].wait_send()
        for _, cp_out in local:
            cp_out.wait()

    out_shapes = [jax.ShapeDtypeStruct((N_CHIPS,) + (arr.shape if layer is None else arr.shape[1:]), arr.dtype)
                  for arr, layer in items]
    stage = [pltpu.VMEM(arr.shape if layer is None else arr.shape[1:], arr.dtype) for arr, layer in items]
    sems = ([pltpu.SemaphoreType.DMA((n,))] * 2 + [pltpu.SemaphoreType.DMA((len(sends),))] * 2
            + [pltpu.SemaphoreType.DMA((max(len(hops), 1),))] * 2 + [pltpu.SemaphoreType.DMA((len(lands),))] * 2)
    return Job([arr for arr, _ in items], out_shapes, stage + sems, start, middle, finish)


def _mixer_fwd(proj_ref, lg, lb, ws_ref, bsb_ref, cw_ref, mix_ref, xcbuf, halo_xc):
    u = proj_ref[:, 0 * WIDTH : 1 * WIDTH]
    v = proj_ref[:, 1 * WIDTH : 2 * WIDTH]
    za = proj_ref[:, 2 * WIDTH : 3 * WIDTH]
    h = proj_ref[:, 3 * WIDTH : 4 * WIDTH]
    gb = proj_ref[:, 4 * WIDTH : 5 * WIDTH]
    gc = proj_ref[:, 5 * WIDTH : 6 * WIDTH]
    zb = proj_ref[:, 6 * WIDTH : 7 * WIDTH]
    rs, vhat = _ln_stats(v)
    vl = vhat * lg + lb
    vlb = vl.astype(BF16)
    tile = proj_ref.shape[0]
    for j in range(tile // CHUNK):
        rows = slice(j * CHUNK, (j + 1) * CHUNK)
        for hd in range(HEADS):
            cols = slice(hd * CHUNK, (hd + 1) * CHUNK)
            mix_ref[rows, cols] = _mm(ws_ref[hd], vlb[rows, cols]) + bsb_ref[:, cols]
    mixed = mix_ref[...]
    siga = _sigmoid(za)
    sigb = _sigmoid(zb)
    xc = gc * h
    xcbuf[0:HALO, :] = halo_xc
    xcbuf[HALO : HALO + tile, :] = xc
    y = cw_ref[0:1, :] * xcbuf[HALO - 2 : HALO - 2 + tile, :] + cw_ref[1:2, :] * xcbuf[HALO - 1 : HALO - 1 + tile, :]
    y = y + cw_ref[2:3, :] * xc
    return dict(u=u, za=za, h=h, gb=gb, gc=gc, zb=zb, rs=rs, vhat=vhat, vlb=vlb, mixed=mixed, siga=siga, sigb=sigb,
                xc=xc, y=y)


def fwd_layer(layer, x, p, wts, small, job=None, head=None):
    win, wout, wg, wpp = wts
    ng, lg, lb, ws, bsb, cw, pg = small
    n_head = 0 if head is None else 2

    def body(*refs):
        (x_ref, p_ref, win_ref, wout_ref, wg_ref, wpp_ref, ng_ref, lg_ref, lb_ref, ws_ref, bsb_ref, cw_ref,
         pg_ref) = refs[:13]
        head_in = refs[13 : 13 + n_head]
        proj_ref, x2_ref, gate_ref, x3_ref, hnT_ref, catT_ref, hn2T_ref, pT_ref = refs[13 + n_head : 21 + n_head]
        head_out = refs[21 + n_head : 21 + 2 * n_head]
        mix_ref, xcbuf, carry, wcat, wsem = refs[21 + 2 * n_head :]
        i = pl.program_id(0)

        @pl.when(i == 0)
        def _():
            _load_side_by_side(win_ref, wcat, wsem)
            carry[...] = jnp.zeros_like(carry)
            for ref in head_out:
                ref[...] = jnp.zeros_like(ref)

        xv = x_ref[...]
        _, xh = _rms_stats(xv)
        hn = xh * ng_ref[...]
        hnT_ref[...] = hn.T.astype(BF16)
        proj_ref[...] = _mm(hn.astype(BF16), wcat[...])
        m = _mixer_fwd(proj_ref, lg_ref[...], lb_ref[...], ws_ref, bsb_ref, cw_ref, mix_ref, xcbuf, carry[...])
        carry[...] = m["xc"][FWD_TILE - HALO : FWD_TILE, :]
        out_a = (m["u"] * m["mixed"]) * (m["za"] * m["siga"])
        out_b = (m["gb"] * m["y"]) * (m["zb"] * m["sigb"])
        cat = jnp.concatenate([out_a, out_b], axis=-1)
        catT_ref[...] = cat.T.astype(BF16)
        x2 = xv + _mm_rows(cat.astype(BF16), wout_ref)
        x2_ref[...] = x2
        _, xh2 = _rms_stats(x2)
        hn2 = xh2 * pg_ref[...]
        hn2T_ref[...] = hn2.T.astype(BF16)
        gate = _sigmoid(_mm_rows(hn2.astype(BF16), wg_ref))
        gate_ref[...] = gate
        pv = p_ref[...]
        pT_ref[...] = pv.T.astype(BF16)
        x3 = x2 + gate * _ple_proj(pv, wpp_ref)
        if head is None:
            x3_ref[...] = x3
        else:
            t_ref, gf_ref = head_in
            loss_ref, gg_ref = head_out
            r3, xh3 = _rms_stats(x3)
            gf = gf_ref[...]
            err = xh3 * gf - t_ref[...]
            loss_ref[...] += (0.5 / D_MODEL) * jnp.sum(err * err).reshape(1, 1)
            dy = err * (1.0 / D_MODEL)
            gg_ref[...] += jnp.sum(dy * xh3, axis=0, keepdims=True)
            x3_ref[...] = _rms_bwd(dy * gf, xh3, r3)

    tok = lambda w: pl.BlockSpec((FWD_TILE, w), lambda i: (i, 0))
    tokT = lambda w: pl.BlockSpec((None, w, FWD_TILE), lambda i: (i, 0, 0))
    once = lambda a: pl.BlockSpec(a.shape, lambda i: (0,) * a.ndim, pipeline_mode=pl.Buffered(1))
    f32 = lambda w: jax.ShapeDtypeStruct((SEQ, w), F32)
    bfT = lambda w: jax.ShapeDtypeStruct((SEQ // FWD_TILE, w, FWD_TILE), BF16)
    head_specs = [] if head is None else [tok(D_MODEL), _full((1, D_MODEL))]
    head_outs = [] if head is None else [_full((1, 1)), _full((1, D_MODEL))]
    head_shapes = [] if head is None else [jax.ShapeDtypeStruct((1, 1), F32), jax.ShapeDtypeStruct((1, D_MODEL), F32)]
    return _call(
        body, name=f"fwd_layer{layer}", grid=(SEQ // FWD_TILE,),
        in_specs=[tok(D_MODEL), pl.BlockSpec((None, None, FWD_TILE, PLE), lambda i: (layer, 0, i, 0)),
                  ANY, once(wout), once(wg), once(wpp),
                  _const((1, D_MODEL), (layer,)), _const((1, WIDTH), (layer,)), _const((1, WIDTH), (layer,)),
                  _const((HEADS, CHUNK, CHUNK), (layer,)), _const((CHUNK, WIDTH), (layer,)), _const((3, WIDTH), (layer,)),
                  _const((1, D_MODEL), (layer,))] + head_specs,
        out_specs=[tok(PROJ), tok(D_MODEL), tok(D_MODEL), tok(D_MODEL),
                   tokT(D_MODEL), tokT(D_MODEL), tokT(D_MODEL), tokT(PLE)] + head_outs,
        out_shape=[f32(PROJ), f32(D_MODEL), f32(D_MODEL), f32(D_MODEL),
                   bfT(D_MODEL), bfT(D_MODEL), bfT(D_MODEL), bfT(PLE)] + head_shapes,
        scratch=[pltpu.VMEM((FWD_TILE, WIDTH), F32), pltpu.VMEM((HALO + FWD_TILE, WIDTH), F32), pltpu.VMEM((HALO, WIDTH), F32),
                 pltpu.VMEM((D_MODEL, PROJ), BF16), pltpu.SemaphoreType.DMA((N_CHIPS,))],
        args=(x, p, win, wout, wg, wpp, ng, lg, lb, ws, bsb, cw, pg) + (() if head is None else tuple(head)),
        job=job, mid=SEQ // FWD_TILE // 2)


def bwd_layer(layer, dx3, x2, gate, p, x, proj, wts, small, wsT, tril):
    win, wout, wg, wpp = wts
    ng, lg, lb, ws, bsb, cw, pg = small

    def body(dx3_ref, x2_ref, gate_ref, p_ref, x_ref, proj_ref, halo_ref, win_ref, wout_ref, wg_ref, wpp_ref, ng_ref, lg_ref,
             lb_ref, ws_ref, wsT_ref, bsb_ref, cw_ref, pg_ref, tril_ref,
             dx_ref, dproj_ref, dpp_ref, dgl_ref, dx2b_ref, gn_ref, glg_ref, glb_ref, gws_ref, gbs_ref, gcw_ref, gpg_ref,
             mix_ref, xcbuf, dycbuf, dvl_ref, bs_acc, wcat, wsem):
        i = pl.program_id(0)

        @pl.when(i == 0)
        def _():
            _load_side_by_side(win_ref, wcat, wsem)
            for ref in (gn_ref, glg_ref, glb_ref, gws_ref, gcw_ref, gpg_ref, bs_acc):
                ref[...] = jnp.zeros_like(ref)
            dycbuf[TILE : TILE + HALO, :] = jnp.zeros((HALO, WIDTH), F32)

        dx3v = dx3_ref[...]
        gate_v = gate_ref[...]
        dpp_ref[...] = (dx3v * gate_v).astype(BF16)
        dgl = ((dx3v * _ple_proj(p_ref[...], wpp_ref)) * gate_v * (1.0 - gate_v)).astype(BF16)
        dgl_ref[...] = dgl
        dhn2 = _mm_nt_rows(dgl, wg_ref)
        r2, xh2 = _rms_stats(x2_ref[...])
        gpg_ref[...] += jnp.sum(dhn2 * xh2, axis=0, keepdims=True)
        dx2 = dx3v + _rms_bwd(dhn2 * pg_ref[...], xh2, r2)
        dx2b = dx2.astype(BF16)
        dx2b_ref[...] = dx2b
        dcat = _mm_nt_rows(dx2b, wout_ref)
        lgv = lg_ref[...]
        halo_xc = halo_ref[:, 5 * WIDTH : 6 * WIDTH] * halo_ref[:, 3 * WIDTH : 4 * WIDTH]
        halo_xc = jnp.where(i == N_TILES - 1, 0.0, halo_xc)
        m = _mixer_fwd(proj_ref, lgv, lb_ref[...], ws_ref, bsb_ref, cw_ref, mix_ref, xcbuf, halo_xc)
        u, za, h, gb, gc, zb = m["u"], m["za"], m["h"], m["gb"], m["gc"], m["zb"]
        mixed, siga, sigb, xc, y = m["mixed"], m["siga"], m["sigb"], m["xc"], m["y"]
        doa = dcat[:, 0:WIDTH]
        dob = dcat[:, WIDTH : 2 * WIDTH]
        sa = za * siga
        sb = zb * sigb
        doa_sa = doa * sa
        dproj_ref[:, 0 * WIDTH : 1 * WIDTH] = (doa_sa * mixed).astype(BF16)
        dmixed = doa_sa * u
        dza = (doa * (u * mixed)) * (siga * (1.0 + za * (1.0 - siga)))
        dproj_ref[:, 2 * WIDTH : 3 * WIDTH] = dza.astype(BF16)
        dob_sb = dob * sb
        dproj_ref[:, 4 * WIDTH : 5 * WIDTH] = (dob_sb * y).astype(BF16)
        dyc = dob_sb * gb
        dzb = (dob * (gb * y)) * (sigb * (1.0 + zb * (1.0 - sigb)))
        dproj_ref[:, 6 * WIDTH : 7 * WIDTH] = dzb.astype(BF16)
        dycbuf[0:TILE, :] = dyc
        dyc1 = dycbuf[1 : 1 + TILE, :]
        dyc2 = dycbuf[2 : 2 + TILE, :]
        dxc = cw_ref[2:3, :] * dyc + cw_ref[1:2, :] * dyc1 + cw_ref[0:1, :] * dyc2
        gcw_ref[0:1, :] += jnp.sum(xc * dyc2, axis=0, keepdims=True)
        gcw_ref[1:2, :] += jnp.sum(xc * dyc1, axis=0, keepdims=True)
        gcw_ref[2:3, :] += jnp.sum(xc * dyc, axis=0, keepdims=True)
        dycbuf[TILE : TILE + HALO, :] = dyc[0:HALO, :]
        dproj_ref[:, 5 * WIDTH : 6 * WIDTH] = (dxc * h).astype(BF16)
        dproj_ref[:, 3 * WIDTH : 4 * WIDTH] = (dxc * gc).astype(BF16)
        dmb = dmixed.astype(BF16)
        vlb = m["vlb"]
        bsum = jnp.zeros((CHUNK, WIDTH), F32)
        for j in range(TILE // CHUNK):
            rows = slice(j * CHUNK, (j + 1) * CHUNK)
            bsum = bsum + dmixed[rows, :]
            for hd in range(HEADS):
                cols = slice(hd * CHUNK, (hd + 1) * CHUNK)
                gws_ref[hd] += _mm_nt(dmb[rows, cols], vlb[rows, cols])
                dvl_ref[rows, cols] = _mm(wsT_ref[hd], dmb[rows, cols])
        bs_acc[...] += bsum
        dvl = dvl_ref[...]
        vhat = m["vhat"]
        glb_ref[...] += jnp.sum(dvl, axis=0, keepdims=True)
        glg_ref[...] += jnp.sum(dvl * vhat, axis=0, keepdims=True)
        dvh = dvl * lgv
        dv = m["rs"] * (dvh - jnp.mean(dvh, axis=-1, keepdims=True) - vhat * jnp.mean(dvh * vhat, axis=-1, keepdims=True))
        dproj_ref[:, 1 * WIDTH : 2 * WIDTH] = dv.astype(BF16)
        dhn = _mm_nt(dproj_ref[...], wcat[...])
        r1, xh = _rms_stats(x_ref[...])
        gn_ref[...] += jnp.sum(dhn * xh, axis=0, keepdims=True)
        dx_ref[...] = dx2 + _rms_bwd(dhn * ng_ref[...], xh, r1)

        @pl.when(i == N_TILES - 1)
        def _():
            for hd in range(HEADS):
                gws_ref[hd] = gws_ref[hd] * tril_ref[...]
                gbs_ref[hd : hd + 1, :] = jnp.sum(bs_acc[:, hd * CHUNK : (hd + 1) * CHUNK].T, axis=0, keepdims=True)

    rev = lambda w: pl.BlockSpec((TILE, w), lambda i: (N_TILES - 1 - i, 0))
    halo = pl.BlockSpec((HALO, PROJ), lambda i: (jnp.maximum((N_TILES - 1 - i) * (TILE // HALO) - 1, 0), 0))
    once = lambda a: pl.BlockSpec(a.shape, lambda i: (0,) * a.ndim, pipeline_mode=pl.Buffered(1))
    vec = lambda w: jax.ShapeDtypeStruct((1, w), F32)
    b16 = lambda w: jax.ShapeDtypeStruct((SEQ, w), BF16)
    outs, _ = _call(
        body, name=f"bwd_layer{layer}", grid=(N_TILES,),
        in_specs=[rev(D_MODEL), rev(D_MODEL), rev(D_MODEL),
                  pl.BlockSpec((None, None, TILE, PLE), lambda i: (layer, 0, N_TILES - 1 - i, 0)),
                  rev(D_MODEL), rev(PROJ), halo, ANY, once(wout), once(wg), once(wpp),
                  _const((1, D_MODEL), (layer,)), _const((1, WIDTH), (layer,)), _const((1, WIDTH), (layer,)),
                  _const((HEADS, CHUNK, CHUNK), (layer,)), _const((HEADS, CHUNK, CHUNK), (layer,)),
                  _const((CHUNK, WIDTH), (layer,)), _const((3, WIDTH), (layer,)), _const((1, D_MODEL), (layer,)),
                  _full((CHUNK, CHUNK))],
        out_specs=[rev(D_MODEL), rev(PROJ), rev(D_MODEL), rev(D_MODEL), rev(D_MODEL),
                   _full((1, D_MODEL)), _full((1, WIDTH)), _full((1, WIDTH)), _full((HEADS, CHUNK, CHUNK)),
                   _full((HEADS, CHUNK)), _full((3, WIDTH)), _full((1, D_MODEL))],
        out_shape=[jax.ShapeDtypeStruct((SEQ, D_MODEL), F32), b16(PROJ), b16(D_MODEL), b16(D_MODEL), b16(D_MODEL),
                   vec(D_MODEL), vec(WIDTH), vec(WIDTH), jax.ShapeDtypeStruct((HEADS, CHUNK, CHUNK), F32),
                   jax.ShapeDtypeStruct((HEADS, CHUNK), F32), jax.ShapeDtypeStruct((3, WIDTH), F32), vec(D_MODEL)],
        scratch=[pltpu.VMEM((TILE, WIDTH), F32), pltpu.VMEM((HALO + TILE, WIDTH), F32),
                 pltpu.VMEM((TILE + HALO, WIDTH), F32), pltpu.VMEM((TILE, WIDTH), F32), pltpu.VMEM((CHUNK, WIDTH), F32),
                 pltpu.VMEM((D_MODEL, PROJ), BF16), pltpu.SemaphoreType.DMA((N_CHIPS,))],
        args=(dx3, x2, gate, p, x, proj, proj, win, wout, wg, wpp, ng, lg, lb, ws, wsT, bsb, cw, pg, tril))
    return outs


def wgrad(pairs, n_tiles, col_blocked, tk, name, job=None):
    n = len(pairs)
    m_dim, n_dim, at = pairs[0][0].shape[1], pairs[0][1].shape[1], pairs[0][0].shape[2]
    tn = n_dim // n_tiles
    cb, mb = n_dim // N_CHIPS, m_dim // N_CHIPS
    per_tile = N_CHIPS // n_tiles
    assert col_blocked or n_tiles == 1

    def body(*refs):
        for a in range(n):
            a_ref, b_ref, o_ref = refs[2 * a], refs[2 * a + 1], refs[2 * n + a]

            @pl.when(pl.program_id(1) == 0)
            def _():
                o_ref[...] = jnp.zeros_like(o_ref)

            prod = _mm(a_ref[0], b_ref[0:at, :])
            for t in range(1, tk // at):
                prod = prod + _mm(a_ref[t], b_ref[t * at : (t + 1) * at, :])
            if col_blocked:
                for q in range(per_tile):
                    o_ref[q] += prod[:, q * cb : (q + 1) * cb]
            else:
                for q in range(N_CHIPS):
                    o_ref[q] += prod[q * mb : (q + 1) * mb, :]

            @pl.when(pl.program_id(1) == SEQ // tk - 1)
            def _():
                refs[3 * n + a][...] = o_ref[...].astype(BF16)

    mode = dict(pipeline_mode=pl.Buffered(1)) if n_tiles == 1 else {}
    if col_blocked:
        shape = (N_CHIPS, m_dim, cb)
        o_spec = pl.BlockSpec((per_tile, m_dim, cb), lambda j, k: (j, 0, 0), **mode)
    else:
        shape = (N_CHIPS, mb, n_dim)
        o_spec = pl.BlockSpec((N_CHIPS, mb, n_dim), lambda j, k: (0, 0, 0), **mode)
    outs, job_outs = _call(
        body, name=name, grid=(n_tiles, SEQ // tk),
        in_specs=[pl.BlockSpec((tk // at, m_dim, at), lambda j, k: (k, 0, 0)),
                  pl.BlockSpec((tk, tn), lambda j, k: (k, j))] * n,
        out_specs=[o_spec] * (2 * n), out_shape=[jax.ShapeDtypeStruct(shape, F32)] * n + [jax.ShapeDtypeStruct(shape, BF16)] * n,
        args=[t for pair in pairs for t in pair], job=job)
    return outs[:n], outs[n:], job_outs


def to_sibling_job(grads):
    n = len(grads)

    def copies(src, dst, sems):
        x, y, c, _ = _mesh_pos()
        out = []
        for a in range(n):
            rh = grads[a].shape[1] // 2
            out.append(pltpu.make_async_remote_copy(
                src_ref=src[a].at[:, pl.ds((1 - c) * rh, rh), :], dst_ref=dst[a], send_sem=sems[0].at[a],
                recv_sem=sems[1].at[a], device_id=(x, y, 1 - c), device_id_type=MESH))
        return out

    def start(src, dst, sems):
        for cp in copies(src, dst, sems):
            cp.start()

    def finish(src, dst, sems):
        for cp in copies(src, dst, sems):
            cp.wait()

    out_shapes = [jax.ShapeDtypeStruct((N_CHIPS, g.shape[1] // 2, g.shape[2]), g.dtype) for g in grads]
    return Job(grads, out_shapes, [pltpu.SemaphoreType.DMA((n,))] * 2, start, None, finish)


def rs_add_sibling(ids, grads, recvd, name):
    n = len(grads)

    def body(ids_ref, *refs):
        g, r = refs[:n], refs[n : 2 * n]
        pb, pf = refs[2 * n : 3 * n], refs[3 * n :]
        k = pl.program_id(0)
        for a in range(n):
            s = g[a][...] + r[a][...].astype(F32)
            pb[a][...] = s.astype(BF16)

            @pl.when(k == ids_ref[1])
            def _():
                pf[a][...] = s

    in_specs, out_specs, out_shape = [], [], []
    for g in grads:
        rh, cc = g.shape[1] // 2, g.shape[2]
        in_specs.append(pl.BlockSpec((None, rh, cc), lambda k, ids: (k, ids[0], 0)))
    for g in grads:
        rh, cc = g.shape[1] // 2, g.shape[2]
        in_specs.append(pl.BlockSpec((None, rh, cc), lambda k, ids: (k, 0, 0)))
        out_specs.append(pl.BlockSpec((None, rh, cc), lambda k, ids: (k, 0, 0)))
        out_shape.append(jax.ShapeDtypeStruct((N_CHIPS, rh, cc), BF16))
    for g in grads:
        rh, cc = g.shape[1] // 2, g.shape[2]
        out_specs.append(pl.BlockSpec((rh, cc), lambda k, ids: (0, 0)))
        out_shape.append(jax.ShapeDtypeStruct((rh, cc), F32))
    outs = pl.pallas_call(
        body, name=name, out_shape=out_shape,
        grid_spec=pltpu.PrefetchScalarGridSpec(num_scalar_prefetch=1, grid=(N_CHIPS,), in_specs=in_specs,
                                               out_specs=out_specs),
        compiler_params=pltpu.CompilerParams(dimension_semantics=("arbitrary",), vmem_limit_bytes=VMEM_LIMIT),
    )(ids, *grads, *recvd)
    return list(outs[:n]), list(outs[n:])


def to_owners_job(partials):
    n = len(partials)

    def copies(src, dst, sems):
        x, y, c, own = _mesh_pos()
        out = []
        for a in range(n):
            for r in (1, 2, 3):
                out.append(pltpu.make_async_remote_copy(
                    src_ref=src[a].at[own ^ r], dst_ref=dst[a].at[r - 1], send_sem=sems[0].at[3 * a + r - 1],
                    recv_sem=sems[1].at[3 * a + r - 1], device_id=_peer(x, y, c, r), device_id_type=MESH))
        return out

    def start(src, dst, sems):
        for cp in copies(src, dst, sems):
            cp.start()

    def finish(src, dst, sems):
        for cp in copies(src, dst, sems):
            cp.wait()

    out_shapes = [jax.ShapeDtypeStruct((3,) + p.shape[1:], BF16) for p in partials]
    return Job(partials, out_shapes, [pltpu.SemaphoreType.DMA((3 * n,))] * 2, start, None, finish)


def rs_add_owners(layer, ids, own_f32, recvd, prev, name):
    n = len(own_f32)
    nb = 2

    def body(ids_ref, *refs):
        o, r, f = refs[:n], refs[n : 2 * n], refs[-n:]
        for a in range(n):
            f[a][...] = ((o[a][...] + r[a][0].astype(F32)) + r[a][1].astype(F32)) + r[a][2].astype(F32)

    in_specs, out_specs, out_shape = [], [], []
    for o in own_f32:
        in_specs.append(pl.BlockSpec((o.shape[0] // nb, o.shape[1]), lambda j, ids: (j, 0)))
    for o in own_f32:
        in_specs.append(pl.BlockSpec((3, o.shape[0] // nb, o.shape[1]), lambda j, ids: (0, j, 0)))
        out_specs.append(pl.BlockSpec((None, o.shape[0] // nb, o.shape[1]), lambda j, ids: (layer, ids[0] * nb + j, 0)))
        out_shape.append(jax.ShapeDtypeStruct((DEPTH, 2 * o.shape[0], o.shape[1]), F32))
    args, aliases = [ids, *own_f32, *recvd], {}
    if prev is not None:
        in_specs += [ANY] * n
        args += list(prev)
        aliases = {1 + 2 * n + a: a for a in range(n)}
    return pl.pallas_call(
        body, name=name, out_shape=out_shape, input_output_aliases=aliases,
        grid_spec=pltpu.PrefetchScalarGridSpec(num_scalar_prefetch=1, grid=(nb,), in_specs=in_specs, out_specs=out_specs),
        compiler_params=pltpu.CompilerParams(dimension_semantics=("arbitrary",), vmem_limit_bytes=VMEM_LIMIT),
    )(*args)


def exchange_halves_job(layer, full):
    n = len(full)

    def copies(src, dst, sems):
        x, y, c, _ = _mesh_pos()
        out = []
        for a in range(n):
            rh = full[a].shape[1] // 2
            out.append(pltpu.make_async_remote_copy(
                src_ref=src[a].at[layer, pl.ds(c * rh, rh), :], dst_ref=dst[a].at[layer, pl.ds(c * rh, rh), :],
                send_sem=sems[0].at[a], recv_sem=sems[1].at[a], device_id=(x, y, 1 - c), device_id_type=MESH))
        return out

    def start(src, dst, sems):
        for cp in copies(src, dst, sems):
            cp.start()

    def finish(src, dst, sems):
        for cp in copies(src, dst, sems):
            cp.wait()

    out_shapes = [jax.ShapeDtypeStruct(f.shape, F32) for f in full]
    return Job(full, out_shapes, [pltpu.SemaphoreType.DMA((n,))] * 2, start, None, finish, {a: a for a in range(n)})


def _adamw(w, g, m, v):
    m2 = ADAM_B1 * m + (1.0 - ADAM_B1) * g
    v2 = ADAM_B2 * v + (1.0 - ADAM_B2) * (g * g)
    delta = -ADAM_LR * ((m2 / ADAM_C1) / (jnp.sqrt(v2 / ADAM_C2) + ADAM_EPS) + ADAM_WD * w)
    return delta, m2, v2


def adamw_big(ws, gs, ms, vs, name):
    n = len(ws)
    nb = 4

    def body(*refs):
        for a in range(n):
            w, g, m, v = (refs[j * n + a][...] for j in range(4))
            d, m2, v2 = _adamw(w, g, m, v)
            refs[4 * n + a][...] = g
            refs[5 * n + a][...] = d
            refs[6 * n + a][...] = m2
            refs[7 * n + a][...] = v2

    specs = [pl.BlockSpec((None, w.shape[1] // nb, w.shape[2]), lambda l, j: (l, j, 0)) for w in ws]
    shapes = [jax.ShapeDtypeStruct(w.shape, F32) for w in ws]
    outs, _ = _call(body, name=name, grid=(DEPTH, nb), in_specs=specs * 4, out_specs=specs * 4, out_shape=shapes * 4,
                    args=(*ws, *gs, *ms, *vs))
    return outs[:n], outs[n : 2 * n], outs[2 * n : 3 * n], outs[3 * n :]


def small_to_sibling_job(gs):
    n = len(gs)

    def copies(src, dst, sems):
        x, y, c, _ = _mesh_pos()
        return [pltpu.make_async_remote_copy(src_ref=src[a], dst_ref=dst[a], send_sem=sems[0].at[a], recv_sem=sems[1].at[a],
                                             device_id=(x, y, 1 - c), device_id_type=MESH) for a in range(n)]

    def start(*refs):
        for cp in copies(*refs):
            cp.start()

    def finish(*refs):
        for cp in copies(*refs):
            cp.wait()

    return Job(gs, [jax.ShapeDtypeStruct(g.shape, F32) for g in gs], [pltpu.SemaphoreType.DMA((n,))] * 2, start, None, finish)


def small_chip_sum(gs, recvd):
    n = len(gs)

    def body(*refs):
        for a in range(n):
            refs[2 * n + a][...] = refs[a][...] + refs[n + a][...]

    return pl.pallas_call(body, name="small_chip_sum", out_shape=[jax.ShapeDtypeStruct(g.shape, F32) for g in gs])(*gs, *recvd)


def small_to_chips_job(chip_sums):
    n = len(chip_sums)

    def copies(src, dst, sems):
        x, y, c, _ = _mesh_pos()
        return [pltpu.make_async_remote_copy(src_ref=src[a], dst_ref=dst[a].at[r - 1], send_sem=sems[0].at[3 * a + r - 1],
                                             recv_sem=sems[1].at[3 * a + r - 1], device_id=_peer(x, y, c, r),
                                             device_id_type=MESH) for a in range(n) for r in (1, 2, 3)]

    def start(*refs):
        for cp in copies(*refs):
            cp.start()

    def finish(*refs):
        for cp in copies(*refs):
            cp.wait()

    return Job(chip_sums, [jax.ShapeDtypeStruct((3,) + s.shape, F32) for s in chip_sums],
               [pltpu.SemaphoreType.DMA((3 * n,))] * 2, start, None, finish)


def small_finish(chip_sums, from_chips, ws, ms, vs):
    n, n_w = len(chip_sums), len(ws)

    def body(*refs):
        cs, fc = refs[:n], refs[n : 2 * n]
        w, m, v = (refs[2 * n + j * n_w : 2 * n + (j + 1) * n_w] for j in range(3))
        outs = refs[2 * n + 3 * n_w :]
        own = 2 * lax.axis_index("x") + lax.axis_index("y")
        for a in range(n):
            mine = cs[a][...]

            def of_chip(k):
                r = own ^ k
                return jnp.where(r == 0, mine, jnp.where(r == 1, fc[a][0], jnp.where(r == 2, fc[a][1], fc[a][2])))

            tot = ((of_chip(0) + of_chip(1)) + of_chip(2)) + of_chip(3)
            outs[a][...] = tot
            if a < n_w:
                d, m2, v2 = _adamw(w[a][...], tot, m[a][...], v[a][...])
                outs[n + a][...] = d
                outs[n + n_w + a][...] = m2
                outs[n + 2 * n_w + a][...] = v2

    shapes = [jax.ShapeDtypeStruct(s.shape, F32) for s in chip_sums]
    outs = pl.pallas_call(body, name="small_finish", out_shape=shapes + shapes[:n_w] * 3)(
        *chip_sums, *from_chips, *ws, *ms, *vs)
    return outs[:n], outs[n : n + n_w], outs[n + n_w : n + 2 * n_w], outs[n + 2 * n_w :]


def merge_jobs(*jobs):
    def parts(refs, counts):
        out, pos = [], 0
        for cnt in counts:
            out.append(refs[pos : pos + cnt])
            pos += cnt
        return out

    def run(phase):
        def go(ins, outs, sems):
            for job, i, o, s in zip(jobs, parts(ins, [len(j.ins) for j in jobs]),
                                    parts(outs, [len(j.out_shapes) for j in jobs]), parts(sems, [len(j.sems) for j in jobs])):
                getattr(job, phase)(i, o, s)
        return go

    assert all(j.middle is None and not j.aliases for j in jobs)
    return Job([a for j in jobs for a in j.ins], [a for j in jobs for a in j.out_shapes], [a for j in jobs for a in j.sems],
               run("start"), None, run("finish"))


def adamw_rows(w, g, m, v):
    def body(w_ref, g_ref, m_ref, v_ref, d_ref, m2_ref, v2_ref):
        d, m2, v2 = _adamw(w_ref[...], g_ref[...], m_ref[...], v_ref[...])
        d_ref[...] = d
        m2_ref[...] = m2
        v2_ref[...] = v2

    return pl.pallas_call(body, name="adamw_conv", out_shape=[jax.ShapeDtypeStruct(w.shape, F32)] * 3)(w, g, m, v)


def kernel(x, p, norm_g, w_in, ln_v_g, ln_v_b, w_s, b_s, conv_w, w_out, ple_norm_g, w_ple_gate, w_ple_proj, final_g, loss_target, m_norm_g, m_w_in, m_ln_v_g, m_ln_v_b, m_w_s, m_b_s, m_conv_w, m_w_out, m_ple_norm_g, m_w_ple_gate, m_w_ple_proj, m_final_g, v_norm_g, v_w_in, v_ln_v_g, v_ln_v_b, v_w_s, v_b_s, v_conv_w, v_w_out, v_ple_norm_g, v_w_ple_gate, v_w_ple_proj, v_final_g):
    cx, cy, cc = lax.axis_index("x"), lax.axis_index("y"), lax.axis_index("c")
    own = 2 * cx + cy
    ids = jnp.stack([cc, own]).astype(jnp.int32)

    cw_rows = jnp.transpose(conv_w, (0, 2, 1))
    shards = [w_in.astype(BF16), w_out.astype(BF16), w_ple_gate.astype(BF16), w_ple_proj.astype(BF16)]
    *wts0, g_cw = run_job(gather_job([(s, 0) for s in shards] + [(cw_rows, None)], relay=True), "gather_weights0")
    cw_full = jnp.transpose(g_cw, (1, 2, 0, 3)).reshape(DEPTH, 3, WIDTH)
    tril = jnp.tril(jnp.ones((CHUNK, CHUNK), F32))
    ws_masked = w_s * tril[None, None]
    ws_b = ws_masked.astype(BF16)
    wsT_b = jnp.swapaxes(ws_masked, 2, 3).astype(BF16)
    bsb = jnp.repeat(jnp.swapaxes(b_s, 1, 2), CHUNK, axis=2)
    small = (norm_g[:, None, :], ln_v_g[:, None, :], ln_v_b[:, None, :], ws_b, bsb, cw_full, ple_norm_g[:, None, :])

    saved0, wts1 = fwd_layer(0, x[0], p, wts0, small, job=gather_job([(s, 1) for s in shards]))
    saved1, _ = fwd_layer(1, saved0[3], p, wts1, small, head=(loss_target[0], final_g[None, :]))
    saved, xs = [saved0[:8], saved1[:8]], [x[0], saved0[3]]
    dx, loss_part, g_final = saved1[3], saved1[8], saved1[9]

    def chip_sums(arrs, arrs_bf16, tag):
        recvd = run_job(to_sibling_job(arrs_bf16), f"rs_to_sibling{tag}")
        return rs_add_sibling(ids, arrs, recvd, f"rs_add_sibling{tag}")

    small_g = [None] * DEPTH
    proj, x2, gate, _, hnT, catT, hn2T, pT = saved[1]
    dx, dproj_b, dpp_b, dgl_b, dx2_b, *g_sm = bwd_layer(1, dx, x2, gate, p, xs[1], proj, wts1, small, wsT_b, tril)
    small_g[1] = g_sm[:5] + [g_sm[6], g_sm[5]]
    g1_in, b1_in, _ = wgrad([(hnT, dproj_b)], 1, True, 1024, "wgrad_in1")
    g1_og, b1_og, _ = wgrad([(catT, dx2_b), (hn2T, dgl_b)], 1, False, 1024, "wgrad_outgate1")
    g1_pp, b1_pp, _ = wgrad([(pT, dpp_b)], 1, True, 2048, "wgrad_pp1")
    part1 = chip_sums(g1_in + g1_og + g1_pp, b1_in + b1_og + b1_pp, "1")
    proj, x2, gate, _, hnT, catT, hn2T, pT = saved[0]
    dx, dproj_b, dpp_b, dgl_b, dx2_b, *g_sm = bwd_layer(0, dx, x2, gate, p, xs[0], proj, wts0, small, wsT_b, tril)
    small_g[0] = g_sm[:5] + [g_sm[6], g_sm[5]]
    def both(j):
        return jnp.concatenate([small_g[0][j], small_g[1][j]], axis=0)

    g_small = [both(0), both(1), both(2), both(3).reshape(DEPTH * HEADS * CHUNK, CHUNK), both(4), both(5), g_final, both(6),
               jnp.broadcast_to(loss_part, (1, 128))]
    g0_in, b0_in, landed = wgrad([(hnT, dproj_b)], 1, True, 1024, "wgrad_in0",
                                 job=merge_jobs(to_owners_job(part1[0]), small_to_sibling_job(g_small)))
    from1, small_sib = landed[:4], landed[4:]
    full = rs_add_owners(1, ids, part1[1], from1, None, "rs_add_owners1")
    full = run_job(exchange_halves_job(1, full), "rs_exchange_halves1")
    small_chip = small_chip_sum(g_small, small_sib)
    part0b = chip_sums(g0_in, b0_in, "0b")
    g0_og, b0_og, from0b = wgrad([(catT, dx2_b), (hn2T, dgl_b)], 1, False, 1024, "wgrad_outgate0",
                                 job=to_owners_job(part0b[0]))
    full_b = rs_add_owners(0, ids, part0b[1], from0b, full[:1], "rs_add_owners0b")
    g0_pp, b0_pp, small_chips = wgrad([(pT, dpp_b)], 1, True, 2048, "wgrad_pp0", job=small_to_chips_job(small_chip))
    part0a = chip_sums(g0_og + g0_pp, b0_og + b0_pp, "0a")
    from0a = run_job(to_owners_job(part0a[0]), "rs_to_owners0a")
    full_a = rs_add_owners(0, ids, part0a[1], from0a, full[1:], "rs_add_owners0a")
    reduced = run_job(exchange_halves_job(0, list(full_b) + list(full_a)), "rs_exchange_halves0")
    (gw_in, gw_out, gw_gate, gw_pp), (d_in, d_out, d_gate, d_pp), (m_in, m_out, m_gate, m_pp), (v_in, v_out, v_gate, v_pp) = adamw_big(
        [w_in, w_out, w_ple_gate, w_ple_proj], reduced,
        [m_w_in, m_w_out, m_w_ple_gate, m_w_ple_proj], [v_w_in, v_w_out, v_w_ple_gate, v_w_ple_proj], "adamw_big")

    flat = lambda a: a.reshape(-1, a.shape[-1])
    gsum, dsm, msm, vsm = small_finish(
        small_chip, small_chips, [flat(a) for a in (norm_g, ln_v_g, ln_v_b, w_s, b_s, ple_norm_g, final_g[None])],
        [flat(a) for a in (m_norm_g, m_ln_v_g, m_ln_v_b, m_w_s, m_b_s, m_ple_norm_g, m_final_g[None])],
        [flat(a) for a in (v_norm_g, v_ln_v_g, v_ln_v_b, v_w_s, v_b_s, v_ple_norm_g, v_final_g[None])])
    like = [norm_g, ln_v_g, ln_v_b, w_s, b_s, ple_norm_g, final_g]
    gs, ds, m2s, v2s = ([a.reshape(b.shape) for a, b in zip(res, like)] for res in (gsum, dsm, msm, vsm))
    loss = gsum[8][0, 0]
    g_cw_own = lax.dynamic_slice_in_dim(gsum[7].reshape(DEPTH, 3, WIDTH), own * (WIDTH // N_CHIPS), WIDTH // N_CHIPS, axis=2)
    rows2 = lambda a: a.reshape(DEPTH * 3, WIDTH // N_CHIPS)
    d_cw, m_cw, v_cw = adamw_rows(rows2(cw_rows), rows2(g_cw_own), rows2(jnp.transpose(m_conv_w, (0, 2, 1))),
                                  rows2(jnp.transpose(v_conv_w, (0, 2, 1))))
    back = lambda a: jnp.transpose(a.reshape(DEPTH, 3, WIDTH // N_CHIPS), (0, 2, 1))
    g_conv = jnp.transpose(g_cw_own, (0, 2, 1))

    def ordered(sm, cw_v, w_in_v, w_out_v, gate_v, pp_v):
        return [sm[0], w_in_v, sm[1], sm[2], sm[3], sm[4], cw_v, w_out_v, sm[5], gate_v, pp_v, sm[6]]

    grads = ordered(gs, g_conv, gw_in, gw_out, gw_gate, gw_pp)
    deltas = ordered(ds, back(d_cw), d_in, d_out, d_gate, d_pp)
    new_m = ordered(m2s, back(m_cw), m_in, m_out, m_gate, m_pp)
    new_v = ordered(v2s, back(v_cw), v_in, v_out, v_gate, v_pp)
    return (loss, dx[None], *grads, *deltas, *new_m, *new_v)
```

```python
import jax
import jax.numpy as jnp
from jax import lax
from jax.experimental import pallas as pl
from jax.experimental.pallas import tpu as pltpu

F32 = jnp.float32
BF16 = jnp.bfloat16

SEQ = 8192
D_MODEL = 1024
WIDTH = 512
PROJ = 7 * WIDTH
N_CHIPS = 4
COL_BLK = PROJ // N_CHIPS
PLE = 256
HEADS = 4
CHUNK = 128
DEPTH = 2
EPS = 1e-6
TILE = 256
N_TILES = SEQ // TILE
FWD_TILE = 512
HALO = 8
VMEM_LIMIT = 60 * 1024 * 1024

ADAM_LR, ADAM_B1, ADAM_B2, ADAM_EPS, ADAM_WD, ADAM_STEP = 0.001, 0.9, 0.999, 1e-08, 0.01, 10
ADAM_C1 = 1.0 - ADAM_B1**ADAM_STEP
ADAM_C2 = 1.0 - ADAM_B2**ADAM_STEP

MESH = pl.DeviceIdType.MESH
ANY = pl.BlockSpec(memory_space=pl.ANY)


def _mm(a, b):
    return lax.dot_general(a, b, (((1,), (0,)), ((), ())), preferred_element_type=F32)


def _mm_nt(a, b):
    return lax.dot_general(a, b, (((1,), (1,)), ((), ())), preferred_element_type=F32)


def _mm_rows(a, w_ref):
    blk = w_ref.shape[1]
    acc = _mm(a[:, 0:blk], w_ref[0])
    for k in range(1, N_CHIPS):
        acc = acc + _mm(a[:, k * blk : (k + 1) * blk], w_ref[k])
    return acc


def _ple_proj(p_tile, wpp_ref):
    pb = p_tile.astype(BF16)
    return jnp.concatenate([_mm(pb, wpp_ref[k]) for k in range(N_CHIPS)], axis=-1)


def _mm_nt_rows(a, w_ref):
    return jnp.concatenate([_mm_nt(a, w_ref[k]) for k in range(N_CHIPS)], axis=-1)


def _load_side_by_side(w_hbm, w_vmem, sems):
    copies = [pltpu.make_async_copy(w_hbm.at[k], w_vmem.at[:, pl.ds(k * COL_BLK, COL_BLK)], sems.at[k])
              for k in range(N_CHIPS)]
    for cp in copies:
        cp.start()
    for cp in copies:
        cp.wait()


def _sigmoid(z):
    return 1.0 / (1.0 + jnp.exp(-z))


def _rms_stats(x):
    r = lax.rsqrt(jnp.mean(x * x, axis=-1, keepdims=True) + EPS)
    return r, x * r


def _rms_bwd(dyg, xh, r):
    return r * (dyg - xh * jnp.mean(dyg * xh, axis=-1, keepdims=True))


def _ln_stats(v):
    mu = jnp.mean(v, axis=-1, keepdims=True)
    vc = v - mu
    rs = lax.rsqrt(jnp.mean(vc * vc, axis=-1, keepdims=True) + EPS)
    return rs, vc * rs


def _mesh_pos():
    x, y, c = lax.axis_index("x"), lax.axis_index("y"), lax.axis_index("c")
    return x, y, c, 2 * x + y


def _peer(x, y, c, r):
    return ((1 - x) if (r >> 1) else x, (1 - y) if (r & 1) else y, c)


def _full(shape):
    return pl.BlockSpec(shape, lambda *_: (0,) * len(shape))


def _const(shape, pos):
    return pl.BlockSpec((None,) * len(pos) + tuple(shape), lambda *_: tuple(pos) + (0,) * len(shape))


class Job:
    def __init__(self, ins, out_shapes, sems, start, middle, finish, aliases=None):
        self.ins, self.out_shapes, self.sems = list(ins), list(out_shapes), list(sems)
        self.start, self.middle, self.finish = start, middle, finish
        self.aliases = aliases or {}


def run_job(job, name):
    ni, no = len(job.ins), len(job.out_shapes)

    def body(*refs):
        parts = (refs[:ni], refs[ni : ni + no], refs[ni + no :])
        job.start(*parts)
        if job.middle is not None:
            job.middle(*parts)
        job.finish(*parts)

    return pl.pallas_call(
        body, name=name, out_shape=job.out_shapes, in_specs=[ANY] * ni, out_specs=[ANY] * no, scratch_shapes=job.sems,
        input_output_aliases=job.aliases, compiler_params=pltpu.CompilerParams(has_side_effects=True),
    )(*job.ins)


def _call(body, *, name, grid, in_specs, out_specs, out_shape, args, scratch=(), job=None, mid=None):
    params = pltpu.CompilerParams(dimension_semantics=("arbitrary",) * len(grid), vmem_limit_bytes=VMEM_LIMIT,
                                  has_side_effects=job is not None)
    n_in, n_out, n_sc = len(in_specs), len(out_specs), len(scratch)
    if job is None:
        outs = pl.pallas_call(body, name=name, grid=grid, in_specs=in_specs, out_specs=out_specs, out_shape=out_shape,
                              scratch_shapes=list(scratch), compiler_params=params)(*args)
        return list(outs), []
    ji, jo = len(job.ins), len(job.out_shapes)
    assert not job.aliases and (job.middle is None or len(grid) == 1)

    def wrapped(*refs):
        ins, jin = refs[:n_in], refs[n_in : n_in + ji]
        o0 = n_in + ji
        outs, jout = refs[o0 : o0 + n_out], refs[o0 + n_out : o0 + n_out + jo]
        s0 = o0 + n_out + jo
        sc, jsem = refs[s0 : s0 + n_sc], refs[s0 + n_sc :]
        step = pl.program_id(0)
        for d in range(1, len(grid)):
            step = step * grid[d] + pl.program_id(d)
        n_steps = 1
        for g in grid:
            n_steps *= g

        @pl.when(step == 0)
        def _():
            job.start(jin, jout, jsem)

        if job.middle is not None:
            @pl.when(step == mid)
            def _():
                job.middle(jin, jout, jsem)

        body(*ins, *outs, *sc)

        @pl.when(step == n_steps - 1)
        def _():
            job.finish(jin, jout, jsem)

    outs = pl.pallas_call(
        wrapped, name=name, grid=grid, in_specs=list(in_specs) + [ANY] * ji, out_specs=list(out_specs) + [ANY] * jo,
        out_shape=list(out_shape) + job.out_shapes, scratch_shapes=list(scratch) + job.sems, compiler_params=params,
    )(*args, *job.ins)
    return list(outs[:n_out]), list(outs[n_out:])


def gather_job(items, relay=False):
    n = len(items)
    relayed = [relay and layer is not None for _, layer in items]
    sends = [(a, q, r) for a in range(n) for r in ((1, 2) if relayed[a] else (1, 2, 3))
             for q in (((r - 1), 2 - r) if relayed[a] else (0,))]
    hops = [(a, q) for a in range(n) if relayed[a] for q in (0, 1)]
    lands = sends + [(a, q, 3) for a, q in hops]

    def region(a, ref, lead, h, q):
        arr, layer = items[a]
        if layer is None:
            return ref.at[lead + (h,)]
        rh = arr.shape[1] // 2
        if relayed[a]:
            return ref.at[lead + (pl.ds(h * rh + q * (rh // 2), rh // 2),)]
        return ref.at[lead + (pl.ds(h * rh, rh),)]

    def copies(src, dst, scratch):
        stage, (in_sem, out_sem, ici_s, ici_r, hop_s, hop_r, fwd_s, fwd_r) = scratch[:n], scratch[n:]
        x, y, c, own = _mesh_pos()
        sib = (x, y, 1 - c)
        local = []
        for a in range(n):
            layer = items[a][1]
            local.append((pltpu.make_async_copy(src[a] if layer is None else src[a].at[layer], stage[a], in_sem.at[a]),
                          pltpu.make_async_copy(stage[a], dst[a].at[own], out_sem.at[a])))
        ici, hop, hop_in, fwd, got = {}, {}, {}, {}, {}
        for k, (a, q, r) in enumerate(sends):
            layer = items[a][1]
            ici[a, q, r] = pltpu.make_async_remote_copy(
                src_ref=region(a, src[a], () if layer is None else (layer,), c, q), dst_ref=region(a, dst[a], (own,), c, q),
                send_sem=ici_s.at[k], recv_sem=ici_r.at[k], device_id=_peer(x, y, c, r), device_id_type=MESH)
        for k, (a, q) in enumerate(hops):
            there = region(a, dst[a], (own ^ (q + 1),), c, q)
            hop[a, q] = pltpu.make_async_remote_copy(src_ref=there, dst_ref=there, send_sem=hop_s.at[k], recv_sem=hop_r.at[k],
                                                     device_id=_peer(x, y, c, 2 - q), device_id_type=MESH)
            here = region(a, dst[a], (own ^ 3,), c, q)
            hop_in[a, q] = pltpu.make_async_remote_copy(src_ref=here, dst_ref=here, send_sem=hop_s.at[k], recv_sem=hop_r.at[k],
                                                        device_id=_peer(x, y, c, 2 - q), device_id_type=MESH)
        for k, (a, q, r) in enumerate(lands):
            mine, theirs = region(a, dst[a], (own ^ r,), c, q), region(a, dst[a], (own ^ r,), 1 - c, q)
            fwd[a, q, r] = pltpu.make_async_remote_copy(src_ref=mine, dst_ref=mine, send_sem=fwd_s.at[k], recv_sem=fwd_r.at[k],
                                                        device_id=sib, device_id_type=MESH)
            got[a, q, r] = pltpu.make_async_remote_copy(src_ref=theirs, dst_ref=theirs, send_sem=fwd_s.at[k],
                                                        recv_sem=fwd_r.at[k], device_id=sib, device_id_type=MESH)
        return local, ici, hop, hop_in, fwd, got

    def start(src, dst, sems):
        local, ici, _, _, _, _ = copies(src, dst, sems)
        for slot in sends:
            ici[slot].start()
        for cp_in, _ in local:
            cp_in.start()
        for cp_in, cp_out in local:
            cp_in.wait()
            cp_out.start()

    def middle(src, dst, sems):
        _, ici, hop, hop_in, fwd, _ = copies(src, dst, sems)
        for a, q, r in sends:
            ici[a, q, r].wait_recv()
            if relayed[a] and q == r - 1:
                hop[a, q].start()
            fwd[a, q, r].start()
        for a, q in hops:
            hop_in[a, q].wait_recv()
            fwd[a, q, 3].start()

    def finish(src, dst, sems):
        local, ici, hop, _, fwd, got = copies(src, dst, sems)
        for slot in lands:
            got[slot].wait_recv()
        for slot in sends:
            ici[slot].wait_send()
        for slot in hops:
            hop[slot].wait_send()
        for slot in lands:
            fwd[slot].wait_send()
        for _, cp_out in local:
            cp_out.wait()

    out_shapes = [jax.ShapeDtypeStruct((N_CHIPS,) + (arr.shape if layer is None else arr.shape[1:]), arr.dtype)
                  for arr, layer in items]
    stage = [pltpu.VMEM(arr.shape if layer is None else arr.shape[1:], arr.dtype) for arr, layer in items]
    sems = ([pltpu.SemaphoreType.DMA((n,))] * 2 + [pltpu.SemaphoreType.DMA((len(sends),))] * 2
            + [pltpu.SemaphoreType.DMA((max(len(hops), 1),))] * 2 + [pltpu.SemaphoreType.DMA((len(lands),))] * 2)
    return Job([arr for arr, _ in items], out_shapes, stage + sems, start, middle, finish)


def _mixer_fwd(proj_ref, lg, lb, ws_ref, bsb_ref, cw_ref, mix_ref, xcbuf, halo_xc):
    u = proj_ref[:, 0 * WIDTH : 1 * WIDTH]
    v = proj_ref[:, 1 * WIDTH : 2 * WIDTH]
    za = proj_ref[:, 2 * WIDTH : 3 * WIDTH]
    h = proj_ref[:, 3 * WIDTH : 4 * WIDTH]
    gb = proj_ref[:, 4 * WIDTH : 5 * WIDTH]
    gc = proj_ref[:, 5 * WIDTH : 6 * WIDTH]
    zb = proj_ref[:, 6 * WIDTH : 7 * WIDTH]
    rs, vhat = _ln_stats(v)
    vl = vhat * lg + lb
    vlb = vl.astype(BF16)
    tile = proj_ref.shape[0]
    for j in range(tile // CHUNK):
        rows = slice(j * CHUNK, (j + 1) * CHUNK)
        for hd in range(HEADS):
            cols = slice(hd * CHUNK, (hd + 1) * CHUNK)
            mix_ref[rows, cols] = _mm(ws_ref[hd], vlb[rows, cols]) + bsb_ref[:, cols]
    mixed = mix_ref[...]
    siga = _sigmoid(za)
    sigb = _sigmoid(zb)
    xc = gc * h
    xcbuf[0:HALO, :] = halo_xc
    xcbuf[HALO : HALO + tile, :] = xc
    y = cw_ref[0:1, :] * xcbuf[HALO - 2 : HALO - 2 + tile, :] + cw_ref[1:2, :] * xcbuf[HALO - 1 : HALO - 1 + tile, :]
    y = y + cw_ref[2:3, :] * xc
    return dict(u=u, za=za, h=h, gb=gb, gc=gc, zb=zb, rs=rs, vhat=vhat, vlb=vlb, mixed=mixed, siga=siga, sigb=sigb,
                xc=xc, y=y)


def fwd_layer(layer, x, p, wts, small, job=None, head=None):
    win, wout, wg, wpp = wts
    ng, lg, lb, ws, bsb, cw, pg = small
    n_head = 0 if head is None else 2

    def body(*refs):
        (x_ref, p_ref, win_ref, wout_ref, wg_ref, wpp_ref, ng_ref, lg_ref, lb_ref, ws_ref, bsb_ref, cw_ref,
         pg_ref) = refs[:13]
        head_in = refs[13 : 13 + n_head]
        proj_ref, x2_ref, gate_ref, x3_ref, catT_ref = refs[13 + n_head : 18 + n_head]
        head_out = refs[18 + n_head : 18 + 2 * n_head]
        mix_ref, xcbuf, carry, wcat, wsem = refs[18 + 2 * n_head :]
        i = pl.program_id(0)

        @pl.when(i == 0)
        def _():
            _load_side_by_side(win_ref, wcat, wsem)
            carry[...] = jnp.zeros_like(carry)
            for ref in head_out:
                ref[...] = jnp.zeros_like(ref)

        xv = x_ref[...]
        _, xh = _rms_stats(xv)
        hn = xh * ng_ref[...]
        proj_ref[...] = _mm(hn.astype(BF16), wcat[...])
        m = _mixer_fwd(proj_ref, lg_ref[...], lb_ref[...], ws_ref, bsb_ref, cw_ref, mix_ref, xcbuf, carry[...])
        carry[...] = m["xc"][FWD_TILE - HALO : FWD_TILE, :]
        out_a = (m["u"] * m["mixed"]) * (m["za"] * m["siga"])
        out_b = (m["gb"] * m["y"]) * (m["zb"] * m["sigb"])
        cat = jnp.concatenate([out_a, out_b], axis=-1)
        catT_ref[...] = cat.T.astype(BF16)
        x2 = xv + _mm_rows(cat.astype(BF16), wout_ref)
        x2_ref[...] = x2
        _, xh2 = _rms_stats(x2)
        hn2 = xh2 * pg_ref[...]
        gate = _sigmoid(_mm_rows(hn2.astype(BF16), wg_ref))
        gate_ref[...] = gate
        x3 = x2 + gate * _ple_proj(p_ref[...], wpp_ref)
        if head is None:
            x3_ref[...] = x3
        else:
            t_ref, gf_ref = head_in
            loss_ref, gg_ref = head_out
            r3, xh3 = _rms_stats(x3)
            gf = gf_ref[...]
            err = xh3 * gf - t_ref[...]
            loss_ref[...] += (0.5 / D_MODEL) * jnp.sum(err * err).reshape(1, 1)
            dy = err * (1.0 / D_MODEL)
            gg_ref[...] += jnp.sum(dy * xh3, axis=0, keepdims=True)
            x3_ref[...] = _rms_bwd(dy * gf, xh3, r3)

    tok = lambda w: pl.BlockSpec((FWD_TILE, w), lambda i: (i, 0))
    tokT = lambda w: pl.BlockSpec((None, w, FWD_TILE), lambda i: (i, 0, 0))
    once = lambda a: pl.BlockSpec(a.shape, lambda i: (0,) * a.ndim, pipeline_mode=pl.Buffered(1))
    f32 = lambda w: jax.ShapeDtypeStruct((SEQ, w), F32)
    bfT = lambda w: jax.ShapeDtypeStruct((SEQ // FWD_TILE, w, FWD_TILE), BF16)
    head_specs = [] if head is None else [tok(D_MODEL), _full((1, D_MODEL))]
    head_outs = [] if head is None else [_full((1, 1)), _full((1, D_MODEL))]
    head_shapes = [] if head is None else [jax.ShapeDtypeStruct((1, 1), F32), jax.ShapeDtypeStruct((1, D_MODEL), F32)]
    return _call(
        body, name=f"fwd_layer{layer}", grid=(SEQ // FWD_TILE,),
        in_specs=[tok(D_MODEL), pl.BlockSpec((None, None, FWD_TILE, PLE), lambda i: (layer, 0, i, 0)),
                  ANY, once(wout), once(wg), once(wpp),
                  _const((1, D_MODEL), (layer,)), _const((1, WIDTH), (layer,)), _const((1, WIDTH), (layer,)),
                  _const((HEADS, CHUNK, CHUNK), (layer,)), _const((CHUNK, WIDTH), (layer,)), _const((3, WIDTH), (layer,)),
                  _const((1, D_MODEL), (layer,))] + head_specs,
        out_specs=[tok(PROJ), tok(D_MODEL), tok(D_MODEL), tok(D_MODEL), tokT(D_MODEL)] + head_outs,
        out_shape=[f32(PROJ), f32(D_MODEL), f32(D_MODEL), f32(D_MODEL), bfT(D_MODEL)] + head_shapes,
        scratch=[pltpu.VMEM((FWD_TILE, WIDTH), F32), pltpu.VMEM((HALO + FWD_TILE, WIDTH), F32), pltpu.VMEM((HALO, WIDTH), F32),
                 pltpu.VMEM((D_MODEL, PROJ), BF16), pltpu.SemaphoreType.DMA((N_CHIPS,))],
        args=(x, p, win, wout, wg, wpp, ng, lg, lb, ws, bsb, cw, pg) + (() if head is None else tuple(head)),
        job=job, mid=SEQ // FWD_TILE // 2)


def bwd_layer(layer, dx3, x2, gate, p, x, proj, wts, small, wsT, tril):
    win, wout, wg, wpp = wts
    ng, lg, lb, ws, bsb, cw, pg = small

    def body(dx3_ref, x2_ref, gate_ref, p_ref, x_ref, proj_ref, halo_ref, win_ref, wout_ref, wg_ref, wpp_ref, ng_ref, lg_ref,
             lb_ref, ws_ref, wsT_ref, bsb_ref, cw_ref, pg_ref, tril_ref,
             dx_ref, dproj_ref, dpp_ref, dgl_ref, dx2b_ref, gn_ref, glg_ref, glb_ref, gws_ref, gbs_ref, gcw_ref, gpg_ref,
             mix_ref, xcbuf, dycbuf, dvl_ref, bs_acc, wcat, wsem):
        i = pl.program_id(0)

        @pl.when(i == 0)
        def _():
            _load_side_by_side(win_ref, wcat, wsem)
            for ref in (gn_ref, glg_ref, glb_ref, gws_ref, gcw_ref, gpg_ref, bs_acc):
                ref[...] = jnp.zeros_like(ref)
            dycbuf[TILE : TILE + HALO, :] = jnp.zeros((HALO, WIDTH), F32)

        dx3v = dx3_ref[...]
        gate_v = gate_ref[...]
        dpp_ref[...] = (dx3v * gate_v).astype(BF16)
        dgl = ((dx3v * _ple_proj(p_ref[...], wpp_ref)) * gate_v * (1.0 - gate_v)).astype(BF16)
        dgl_ref[...] = dgl
        dhn2 = _mm_nt_rows(dgl, wg_ref)
        r2, xh2 = _rms_stats(x2_ref[...])
        gpg_ref[...] += jnp.sum(dhn2 * xh2, axis=0, keepdims=True)
        dx2 = dx3v + _rms_bwd(dhn2 * pg_ref[...], xh2, r2)
        dx2b = dx2.astype(BF16)
        dx2b_ref[...] = dx2b
        dcat = _mm_nt_rows(dx2b, wout_ref)
        lgv = lg_ref[...]
        halo_xc = halo_ref[:, 5 * WIDTH : 6 * WIDTH] * halo_ref[:, 3 * WIDTH : 4 * WIDTH]
        halo_xc = jnp.where(i == N_TILES - 1, 0.0, halo_xc)
        m = _mixer_fwd(proj_ref, lgv, lb_ref[...], ws_ref, bsb_ref, cw_ref, mix_ref, xcbuf, halo_xc)
        u, za, h, gb, gc, zb = m["u"], m["za"], m["h"], m["gb"], m["gc"], m["zb"]
        mixed, siga, sigb, xc, y = m["mixed"], m["siga"], m["sigb"], m["xc"], m["y"]
        doa = dcat[:, 0:WIDTH]
        dob = dcat[:, WIDTH : 2 * WIDTH]
        sa = za * siga
        sb = zb * sigb
        doa_sa = doa * sa
        dproj_ref[:, 0 * WIDTH : 1 * WIDTH] = (doa_sa * mixed).astype(BF16)
        dmixed = doa_sa * u
        dza = (doa * (u * mixed)) * (siga * (1.0 + za * (1.0 - siga)))
        dproj_ref[:, 2 * WIDTH : 3 * WIDTH] = dza.astype(BF16)
        dob_sb = dob * sb
        dproj_ref[:, 4 * WIDTH : 5 * WIDTH] = (dob_sb * y).astype(BF16)
        dyc = dob_sb * gb
        dzb = (dob * (gb * y)) * (sigb * (1.0 + zb * (1.0 - sigb)))
        dproj_ref[:, 6 * WIDTH : 7 * WIDTH] = dzb.astype(BF16)
        dycbuf[0:TILE, :] = dyc
        dyc1 = dycbuf[1 : 1 + TILE, :]
        dyc2 = dycbuf[2 : 2 + TILE, :]
        dxc = cw_ref[2:3, :] * dyc + cw_ref[1:2, :] * dyc1 + cw_ref[0:1, :] * dyc2
        gcw_ref[0:1, :] += jnp.sum(xc * dyc2, axis=0, keepdims=True)
        gcw_ref[1:2, :] += jnp.sum(xc * dyc1, axis=0, keepdims=True)
        gcw_ref[2:3, :] += jnp.sum(xc * dyc, axis=0, keepdims=True)
        dycbuf[TILE : TILE + HALO, :] = dyc[0:HALO, :]
        dproj_ref[:, 5 * WIDTH : 6 * WIDTH] = (dxc * h).astype(BF16)
        dproj_ref[:, 3 * WIDTH : 4 * WIDTH] = (dxc * gc).astype(BF16)
        dmb = dmixed.astype(BF16)
        vlb = m["vlb"]
        bsum = jnp.zeros((CHUNK, WIDTH), F32)
        for j in range(TILE // CHUNK):
            rows = slice(j * CHUNK, (j + 1) * CHUNK)
            bsum = bsum + dmixed[rows, :]
            for hd in range(HEADS):
                cols = slice(hd * CHUNK, (hd + 1) * CHUNK)
                gws_ref[hd] += _mm_nt(dmb[rows, cols], vlb[rows, cols])
                dvl_ref[rows, cols] = _mm(wsT_ref[hd], dmb[rows, cols])
        bs_acc[...] += bsum
        dvl = dvl_ref[...]
        vhat = m["vhat"]
        glb_ref[...] += jnp.sum(dvl, axis=0, keepdims=True)
        glg_ref[...] += jnp.sum(dvl * vhat, axis=0, keepdims=True)
        dvh = dvl * lgv
        dv = m["rs"] * (dvh - jnp.mean(dvh, axis=-1, keepdims=True) - vhat * jnp.mean(dvh * vhat, axis=-1, keepdims=True))
        dproj_ref[:, 1 * WIDTH : 2 * WIDTH] = dv.astype(BF16)
        dhn = _mm_nt(dproj_ref[...], wcat[...])
        r1, xh = _rms_stats(x_ref[...])
        gn_ref[...] += jnp.sum(dhn * xh, axis=0, keepdims=True)
        dx_ref[...] = dx2 + _rms_bwd(dhn * ng_ref[...], xh, r1)

        @pl.when(i == N_TILES - 1)
        def _():
            for hd in range(HEADS):
                gws_ref[hd] = gws_ref[hd] * tril_ref[...]
                gbs_ref[hd : hd + 1, :] = jnp.sum(bs_acc[:, hd * CHUNK : (hd + 1) * CHUNK].T, axis=0, keepdims=True)

    rev = lambda w: pl.BlockSpec((TILE, w), lambda i: (N_TILES - 1 - i, 0))
    halo = pl.BlockSpec((HALO, PROJ), lambda i: (jnp.maximum((N_TILES - 1 - i) * (TILE // HALO) - 1, 0), 0))
    once = lambda a: pl.BlockSpec(a.shape, lambda i: (0,) * a.ndim, pipeline_mode=pl.Buffered(1))
    vec = lambda w: jax.ShapeDtypeStruct((1, w), F32)
    b16 = lambda w: jax.ShapeDtypeStruct((SEQ, w), BF16)
    outs, _ = _call(
        body, name=f"bwd_layer{layer}", grid=(N_TILES,),
        in_specs=[rev(D_MODEL), rev(D_MODEL), rev(D_MODEL),
                  pl.BlockSpec((None, None, TILE, PLE), lambda i: (layer, 0, N_TILES - 1 - i, 0)),
                  rev(D_MODEL), rev(PROJ), halo, ANY, once(wout), once(wg), once(wpp),
                  _const((1, D_MODEL), (layer,)), _const((1, WIDTH), (layer,)), _const((1, WIDTH), (layer,)),
                  _const((HEADS, CHUNK, CHUNK), (layer,)), _const((HEADS, CHUNK, CHUNK), (layer,)),
                  _const((CHUNK, WIDTH), (layer,)), _const((3, WIDTH), (layer,)), _const((1, D_MODEL), (layer,)),
                  _full((CHUNK, CHUNK))],
        out_specs=[rev(D_MODEL), rev(PROJ), rev(D_MODEL), rev(D_MODEL), rev(D_MODEL),
                   _full((1, D_MODEL)), _full((1, WIDTH)), _full((1, WIDTH)), _full((HEADS, CHUNK, CHUNK)),
                   _full((HEADS, CHUNK)), _full((3, WIDTH)), _full((1, D_MODEL))],
        out_shape=[jax.ShapeDtypeStruct((SEQ, D_MODEL), F32), b16(PROJ), b16(D_MODEL), b16(D_MODEL), b16(D_MODEL),
                   vec(D_MODEL), vec(WIDTH), vec(WIDTH), jax.ShapeDtypeStruct((HEADS, CHUNK, CHUNK), F32),
                   jax.ShapeDtypeStruct((HEADS, CHUNK), F32), jax.ShapeDtypeStruct((3, WIDTH), F32), vec(D_MODEL)],
        scratch=[pltpu.VMEM((TILE, WIDTH), F32), pltpu.VMEM((HALO + TILE, WIDTH), F32),
                 pltpu.VMEM((TILE + HALO, WIDTH), F32), pltpu.VMEM((TILE, WIDTH), F32), pltpu.VMEM((CHUNK, WIDTH), F32),
                 pltpu.VMEM((D_MODEL, PROJ), BF16), pltpu.SemaphoreType.DMA((N_CHIPS,))],
        args=(dx3, x2, gate, p, x, proj, proj, win, wout, wg, wpp, ng, lg, lb, ws, wsT, bsb, cw, pg, tril))
    return outs


def wgrad(pairs, n_tiles, col_blocked, tk, name, job=None):
    n = len(pairs)
    raw = [isinstance(a, tuple) for a, _ in pairs]
    m_dim = pairs[0][0][0].shape[-1] if raw[0] else pairs[0][0].shape[1]
    n_dim = pairs[0][1].shape[1]
    tn = n_dim // n_tiles
    cb, mb = n_dim // N_CHIPS, m_dim // N_CHIPS
    per_tile = N_CHIPS // n_tiles
    assert col_blocked or n_tiles == 1
    in_specs, args, first = [], [], []
    for (a, b), is_raw in zip(pairs, raw):
        first.append(len(args))
        if is_raw:
            x, gain, layer = a
            lead = x.ndim - 2
            in_specs.append(pl.BlockSpec((None,) * lead + (tk, m_dim), lambda j, k, lead=lead, layer=layer: (layer, 0)[:lead] + (k, 0)))
            args.append(x)
            if gain is not None:
                in_specs.append(_const((1, m_dim), (layer,)))
                args.append(gain)
        else:
            in_specs.append(pl.BlockSpec((tk // a.shape[2], m_dim, a.shape[2]), lambda j, k: (k, 0, 0)))
            args.append(a)
        in_specs.append(pl.BlockSpec((tk, tn), lambda j, k: (k, j)))
        args.append(b)
    n_in = len(args)

    def body(*refs):
        for a in range(n):
            o_ref, ob_ref = refs[n_in + a], refs[n_in + n + a]

            @pl.when(pl.program_id(1) == 0)
            def _():
                o_ref[...] = jnp.zeros_like(o_ref)

            if raw[a]:
                gain = pairs[a][0][1]
                b_ref = refs[first[a] + (1 if gain is None else 2)]
                act = refs[first[a]][...]
                if gain is not None:
                    act = _rms_stats(act)[1] * refs[first[a] + 1][...]
                prod = _mm(act.T.astype(BF16), b_ref[...])
            else:
                a_ref, b_ref = refs[first[a]], refs[first[a] + 1]
                at = a_ref.shape[2]
                prod = _mm(a_ref[0], b_ref[0:at, :])
                for t in range(1, tk // at):
                    prod = prod + _mm(a_ref[t], b_ref[t * at : (t + 1) * at, :])
            if col_blocked:
                for q in range(per_tile):
                    o_ref[q] += prod[:, q * cb : (q + 1) * cb]
            else:
                for q in range(N_CHIPS):
                    o_ref[q] += prod[q * mb : (q + 1) * mb, :]

            @pl.when(pl.program_id(1) == SEQ // tk - 1)
            def _():
                ob_ref[...] = o_ref[...].astype(BF16)

    mode = dict(pipeline_mode=pl.Buffered(1)) if n_tiles == 1 else {}
    if col_blocked:
        shape = (N_CHIPS, m_dim, cb)
        o_spec = pl.BlockSpec((per_tile, m_dim, cb), lambda j, k: (j, 0, 0), **mode)
    else:
        shape = (N_CHIPS, mb, n_dim)
        o_spec = pl.BlockSpec((N_CHIPS, mb, n_dim), lambda j, k: (0, 0, 0), **mode)
    outs, job_outs = _call(
        body, name=name, grid=(n_tiles, SEQ // tk), in_specs=in_specs, out_specs=[o_spec] * (2 * n),
        out_shape=[jax.ShapeDtypeStruct(shape, F32)] * n + [jax.ShapeDtypeStruct(shape, BF16)] * n, args=args, job=job)
    return outs[:n], outs[n:], job_outs


def to_sibling_job(grads):
    n = len(grads)

    def copies(src, dst, sems):
        x, y, c, _ = _mesh_pos()
        out = []
        for a in range(n):
            rh = grads[a].shape[1] // 2
            out.append(pltpu.make_async_remote_copy(
                src_ref=src[a].at[:, pl.ds((1 - c) * rh, rh), :], dst_ref=dst[a], send_sem=sems[0].at[a],
                recv_sem=sems[1].at[a], device_id=(x, y, 1 - c), device_id_type=MESH))
        return out

    def start(src, dst, sems):
        for cp in copies(src, dst, sems):
            cp.start()

    def finish(src, dst, sems):
        for cp in copies(src, dst, sems):
            cp.wait()

    out_shapes = [jax.ShapeDtypeStruct((N_CHIPS, g.shape[1] // 2, g.shape[2]), g.dtype) for g in grads]
    return Job(grads, out_shapes, [pltpu.SemaphoreType.DMA((n,))] * 2, start, None, finish)


def rs_add_sibling(ids, grads, recvd, name):
    n = len(grads)

    def body(ids_ref, *refs):
        g, r = refs[:n], refs[n : 2 * n]
        pb, pf = refs[2 * n : 3 * n], refs[3 * n :]
        k = pl.program_id(0)
        for a in range(n):
            s = g[a][...] + r[a][...].astype(F32)
            pb[a][...] = s.astype(BF16)

            @pl.when(k == ids_ref[1])
            def _():
                pf[a][...] = s

    in_specs, out_specs, out_shape = [], [], []
    for g in grads:
        rh, cc = g.shape[1] // 2, g.shape[2]
        in_specs.append(pl.BlockSpec((None, rh, cc), lambda k, ids: (k, ids[0], 0)))
    for g in grads:
        rh, cc = g.shape[1] // 2, g.shape[2]
        in_specs.append(pl.BlockSpec((None, rh, cc), lambda k, ids: (k, 0, 0)))
        out_specs.append(pl.BlockSpec((None, rh, cc), lambda k, ids: (k, 0, 0)))
        out_shape.append(jax.ShapeDtypeStruct((N_CHIPS, rh, cc), BF16))
    for g in grads:
        rh, cc = g.shape[1] // 2, g.shape[2]
        out_specs.append(pl.BlockSpec((rh, cc), lambda k, ids: (0, 0)))
        out_shape.append(jax.ShapeDtypeStruct((rh, cc), F32))
    outs = pl.pallas_call(
        body, name=name, out_shape=out_shape,
        grid_spec=pltpu.PrefetchScalarGridSpec(num_scalar_prefetch=1, grid=(N_CHIPS,), in_specs=in_specs,
                                               out_specs=out_specs),
        compiler_params=pltpu.CompilerParams(dimension_semantics=("arbitrary",), vmem_limit_bytes=VMEM_LIMIT),
    )(ids, *grads, *recvd)
    return list(outs[:n]), list(outs[n:])


def to_owners_job(partials):
    n = len(partials)

    def copies(src, dst, sems):
        x, y, c, own = _mesh_pos()
        out = []
        for a in range(n):
            for r in (1, 2, 3):
                out.append(pltpu.make_async_remote_copy(
                    src_ref=src[a].at[own ^ r], dst_ref=dst[a].at[r - 1], send_sem=sems[0].at[3 * a + r - 1],
                    recv_sem=sems[1].at[3 * a + r - 1], device_id=_peer(x, y, c, r), device_id_type=MESH))
        return out

    def start(src, dst, sems):
        for cp in copies(src, dst, sems):
            cp.start()

    def finish(src, dst, sems):
        for cp in copies(src, dst, sems):
            cp.wait()

    out_shapes = [jax.ShapeDtypeStruct((3,) + p.shape[1:], BF16) for p in partials]
    return Job(partials, out_shapes, [pltpu.SemaphoreType.DMA((3 * n,))] * 2, start, None, finish)


def rs_add_owners(layer, ids, own_f32, recvd, prev, name):
    n = len(own_f32)
    nb = 2

    def body(ids_ref, *refs):
        o, r, f = refs[:n], refs[n : 2 * n], refs[-n:]
        for a in range(n):
            f[a][...] = ((o[a][...] + r[a][0].astype(F32)) + r[a][1].astype(F32)) + r[a][2].astype(F32)

    in_specs, out_specs, out_shape = [], [], []
    for o in own_f32:
        in_specs.append(pl.BlockSpec((o.shape[0] // nb, o.shape[1]), lambda j, ids: (j, 0)))
    for o in own_f32:
        in_specs.append(pl.BlockSpec((3, o.shape[0] // nb, o.shape[1]), lambda j, ids: (0, j, 0)))
        out_specs.append(pl.BlockSpec((None, o.shape[0] // nb, o.shape[1]), lambda j, ids: (layer, ids[0] * nb + j, 0)))
        out_shape.append(jax.ShapeDtypeStruct((DEPTH, 2 * o.shape[0], o.shape[1]), F32))
    args, aliases = [ids, *own_f32, *recvd], {}
    if prev is not None:
        in_specs += [ANY] * n
        args += list(prev)
        aliases = {1 + 2 * n + a: a for a in range(n)}
    return pl.pallas_call(
        body, name=name, out_shape=out_shape, input_output_aliases=aliases,
        grid_spec=pltpu.PrefetchScalarGridSpec(num_scalar_prefetch=1, grid=(nb,), in_specs=in_specs, out_specs=out_specs),
        compiler_params=pltpu.CompilerParams(dimension_semantics=("arbitrary",), vmem_limit_bytes=VMEM_LIMIT),
    )(*args)


def exchange_halves_job(layer, full):
    n = len(full)

    def copies(src, dst, sems):
        x, y, c, _ = _mesh_pos()
        out = []
        for a in range(n):
            rh = full[a].shape[1] // 2
            out.append(pltpu.make_async_remote_copy(
                src_ref=src[a].at[layer, pl.ds(c * rh, rh), :], dst_ref=dst[a].at[layer, pl.ds(c * rh, rh), :],
                send_sem=sems[0].at[a], recv_sem=sems[1].at[a], device_id=(x, y, 1 - c), device_id_type=MESH))
        return out

    def start(src, dst, sems):
        for cp in copies(src, dst, sems):
            cp.start()

    def finish(src, dst, sems):
        for cp in copies(src, dst, sems):
            cp.wait()

    out_shapes = [jax.ShapeDtypeStruct(f.shape, F32) for f in full]
    return Job(full, out_shapes, [pltpu.SemaphoreType.DMA((n,))] * 2, start, None, finish, {a: a for a in range(n)})


def _adamw(w, g, m, v):
    m2 = ADAM_B1 * m + (1.0 - ADAM_B1) * g
    v2 = ADAM_B2 * v + (1.0 - ADAM_B2) * (g * g)
    delta = -ADAM_LR * ((m2 / ADAM_C1) / (jnp.sqrt(v2 / ADAM_C2) + ADAM_EPS) + ADAM_WD * w)
    return delta, m2, v2


def adamw_big(ws, gs, ms, vs, name):
    n = len(ws)
    nb = 4

    def body(*refs):
        for a in range(n):
            w, g, m, v = (refs[j * n + a][...] for j in range(4))
            d, m2, v2 = _adamw(w, g, m, v)
            refs[4 * n + a][...] = g
            refs[5 * n + a][...] = d
            refs[6 * n + a][...] = m2
            refs[7 * n + a][...] = v2

    specs = [pl.BlockSpec((None, w.shape[1] // nb, w.shape[2]), lambda l, j: (l, j, 0)) for w in ws]
    shapes = [jax.ShapeDtypeStruct(w.shape, F32) for w in ws]
    outs, _ = _call(body, name=name, grid=(DEPTH, nb), in_specs=specs * 4, out_specs=specs * 4, out_shape=shapes * 4,
                    args=(*ws, *gs, *ms, *vs))
    return outs[:n], outs[n : 2 * n], outs[2 * n : 3 * n], outs[3 * n :]


def small_to_sibling_job(gs):
    n = len(gs)

    def copies(src, dst, sems):
        x, y, c, _ = _mesh_pos()
        return [pltpu.make_async_remote_copy(src_ref=src[a], dst_ref=dst[a], send_sem=sems[0].at[a], recv_sem=sems[1].at[a],
                                             device_id=(x, y, 1 - c), device_id_type=MESH) for a in range(n)]

    def start(*refs):
        for cp in copies(*refs):
            cp.start()

    def finish(*refs):
        for cp in copies(*refs):
            cp.wait()

    return Job(gs, [jax.ShapeDtypeStruct(g.shape, F32) for g in gs], [pltpu.SemaphoreType.DMA((n,))] * 2, start, None, finish)


def small_chip_sum(gs, recvd):
    n = len(gs)

    def body(*refs):
        for a in range(n):
            refs[2 * n + a][...] = refs[a][...] + refs[n + a][...]

    return pl.pallas_call(body, name="small_chip_sum", out_shape=[jax.ShapeDtypeStruct(g.shape, F32) for g in gs])(*gs, *recvd)


def small_to_chips_job(chip_sums):
    n = len(chip_sums)

    def copies(src, dst, sems):
        x, y, c, _ = _mesh_pos()
        return [pltpu.make_async_remote_copy(src_ref=src[a], dst_ref=dst[a].at[r - 1], send_sem=sems[0].at[3 * a + r - 1],
                                             recv_sem=sems[1].at[3 * a + r - 1], device_id=_peer(x, y, c, r),
                                             device_id_type=MESH) for a in range(n) for r in (1, 2, 3)]

    def start(*refs):
        for cp in copies(*refs):
            cp.start()

    def finish(*refs):
        for cp in copies(*refs):
            cp.wait()

    return Job(chip_sums, [jax.ShapeDtypeStruct((3,) + s.shape, F32) for s in chip_sums],
               [pltpu.SemaphoreType.DMA((3 * n,))] * 2, start, None, finish)


def small_finish(chip_sums, from_chips, ws, ms, vs):
    n, n_w = len(chip_sums), len(ws)

    def body(*refs):
        cs, fc = refs[:n], refs[n : 2 * n]
        w, m, v = (refs[2 * n + j * n_w : 2 * n + (j + 1) * n_w] for j in range(3))
        outs = refs[2 * n + 3 * n_w :]
        own = 2 * lax.axis_index("x") + lax.axis_index("y")
        for a in range(n):
            mine = cs[a][...]

            def of_chip(k):
                r = own ^ k
                return jnp.where(r == 0, mine, jnp.where(r == 1, fc[a][0], jnp.where(r == 2, fc[a][1], fc[a][2])))

            tot = ((of_chip(0) + of_chip(1)) + of_chip(2)) + of_chip(3)
            outs[a][...] = tot
            if a < n_w:
                d, m2, v2 = _adamw(w[a][...], tot, m[a][...], v[a][...])
                outs[n + a][...] = d
                outs[n + n_w + a][...] = m2
                outs[n + 2 * n_w + a][...] = v2

    shapes = [jax.ShapeDtypeStruct(s.shape, F32) for s in chip_sums]
    outs = pl.pallas_call(body, name="small_finish", out_shape=shapes + shapes[:n_w] * 3)(
        *chip_sums, *from_chips, *ws, *ms, *vs)
    return outs[:n], outs[n : n + n_w], outs[n + n_w : n + 2 * n_w], outs[n + 2 * n_w :]


def merge_jobs(*jobs):
    def parts(refs, counts):
        out, pos = [], 0
        for cnt in counts:
            out.append(refs[pos : pos + cnt])
            pos += cnt
        return out

    def run(phase):
        def go(ins, outs, sems):
            for job, i, o, s in zip(jobs, parts(ins, [len(j.ins) for j in jobs]),
                                    parts(outs, [len(j.out_shapes) for j in jobs]), parts(sems, [len(j.sems) for j in jobs])):
                getattr(job, phase)(i, o, s)
        return go

    assert all(j.middle is None and not j.aliases for j in jobs)
    return Job([a for j in jobs for a in j.ins], [a for j in jobs for a in j.out_shapes], [a for j in jobs for a in j.sems],
               run("start"), None, run("finish"))


def adamw_rows(w, g, m, v):
    def body(w_ref, g_ref, m_ref, v_ref, d_ref, m2_ref, v2_ref):
        d, m2, v2 = _adamw(w_ref[...], g_ref[...], m_ref[...], v_ref[...])
        d_ref[...] = d
        m2_ref[...] = m2
        v2_ref[...] = v2

    return pl.pallas_call(body, name="adamw_conv", out_shape=[jax.ShapeDtypeStruct(w.shape, F32)] * 3)(w, g, m, v)


def kernel(x, p, norm_g, w_in, ln_v_g, ln_v_b, w_s, b_s, conv_w, w_out, ple_norm_g, w_ple_gate, w_ple_proj, final_g, loss_target, m_norm_g, m_w_in, m_ln_v_g, m_ln_v_b, m_w_s, m_b_s, m_conv_w, m_w_out, m_ple_norm_g, m_w_ple_gate, m_w_ple_proj, m_final_g, v_norm_g, v_w_in, v_ln_v_g, v_ln_v_b, v_w_s, v_b_s, v_conv_w, v_w_out, v_ple_norm_g, v_w_ple_gate, v_w_ple_proj, v_final_g):
    cx, cy, cc = lax.axis_index("x"), lax.axis_index("y"), lax.axis_index("c")
    own = 2 * cx + cy
    ids = jnp.stack([cc, own]).astype(jnp.int32)

    cw_rows = jnp.transpose(conv_w, (0, 2, 1))
    shards = [w_in.astype(BF16), w_out.astype(BF16), w_ple_gate.astype(BF16), w_ple_proj.astype(BF16)]
    *wts0, g_cw = run_job(gather_job([(s, 0) for s in shards] + [(cw_rows, None)], relay=True), "gather_weights0")
    cw_full = jnp.transpose(g_cw, (1, 2, 0, 3)).reshape(DEPTH, 3, WIDTH)
    tril = jnp.tril(jnp.ones((CHUNK, CHUNK), F32))
    ws_masked = w_s * tril[None, None]
    ws_b = ws_masked.astype(BF16)
    wsT_b = jnp.swapaxes(ws_masked, 2, 3).astype(BF16)
    bsb = jnp.repeat(jnp.swapaxes(b_s, 1, 2), CHUNK, axis=2)
    small = (norm_g[:, None, :], ln_v_g[:, None, :], ln_v_b[:, None, :], ws_b, bsb, cw_full, ple_norm_g[:, None, :])

    saved0, wts1 = fwd_layer(0, x[0], p, wts0, small, job=gather_job([(s, 1) for s in shards]))
    saved1, _ = fwd_layer(1, saved0[3], p, wts1, small, head=(loss_target[0], final_g[None, :]))
    saved, xs = [saved0[:5], saved1[:5]], [x[0], saved0[3]]
    dx, loss_part, g_final = saved1[3], saved1[5], saved1[6]

    def chip_sums(arrs, arrs_bf16, tag):
        recvd = run_job(to_sibling_job(arrs_bf16), f"rs_to_sibling{tag}")
        return rs_add_sibling(ids, arrs, recvd, f"rs_add_sibling{tag}")

    small_g = [None] * DEPTH
    proj, x2, gate, _, catT = saved[1]
    dx, dproj_b, dpp_b, dgl_b, dx2_b, *g_sm = bwd_layer(1, dx, x2, gate, p, xs[1], proj, wts1, small, wsT_b, tril)
    small_g[1] = g_sm[:5] + [g_sm[6], g_sm[5]]
    g1_in, b1_in, _ = wgrad([((xs[1], small[0], 1), dproj_b)], 1, True, 1024, "wgrad_in1")
    g1_og, b1_og, _ = wgrad([(catT, dx2_b), ((x2, small[6], 1), dgl_b)], 1, False, 1024, "wgrad_outgate1")
    g1_pp, b1_pp, _ = wgrad([((p, None, 1), dpp_b)], 1, True, 2048, "wgrad_pp1")
    part1 = chip_sums(g1_in + g1_og + g1_pp, b1_in + b1_og + b1_pp, "1")
    proj, x2, gate, _, catT = saved[0]
    dx, dproj_b, dpp_b, dgl_b, dx2_b, *g_sm = bwd_layer(0, dx, x2, gate, p, xs[0], proj, wts0, small, wsT_b, tril)
    small_g[0] = g_sm[:5] + [g_sm[6], g_sm[5]]
    def both(j):
        return jnp.concatenate([small_g[0][j], small_g[1][j]], axis=0)

    g_small = [both(0), both(1), both(2), both(3).reshape(DEPTH * HEADS * CHUNK, CHUNK), both(4), both(5), g_final, both(6),
               jnp.broadcast_to(loss_part, (1, 128))]
    g0_in, b0_in, landed = wgrad([((xs[0], small[0], 0), dproj_b)], 1, True, 1024, "wgrad_in0",
                                 job=merge_jobs(to_owners_job(part1[0]), small_to_sibling_job(g_small)))
    from1, small_sib = landed[:4], landed[4:]
    full = rs_add_owners(1, ids, part1[1], from1, None, "rs_add_owners1")
    full = run_job(exchange_halves_job(1, full), "rs_exchange_halves1")
    small_chip = small_chip_sum(g_small, small_sib)
    part0b = chip_sums(g0_in, b0_in, "0b")
    g0_og, b0_og, from0b = wgrad([(catT, dx2_b), ((x2, small[6], 0), dgl_b)], 1, False, 1024, "wgrad_outgate0",
                                 job=to_owners_job(part0b[0]))
    full_b = rs_add_owners(0, ids, part0b[1], from0b, full[:1], "rs_add_owners0b")
    g0_pp, b0_pp, small_chips = wgrad([((p, None, 0), dpp_b)], 1, True, 2048, "wgrad_pp0", job=small_to_chips_job(small_chip))
    part0a = chip_sums(g0_og + g0_pp, b0_og + b0_pp, "0a")
    from0a = run_job(to_owners_job(part0a[0]), "rs_to_owners0a")
    full_a = rs_add_owners(0, ids, part0a[1], from0a, full[1:], "rs_add_owners0a")
    reduced = run_job(exchange_halves_job(0, list(full_b) + list(full_a)), "rs_exchange_halves0")
    (gw_in, gw_out, gw_gate, gw_pp), (d_in, d_out, d_gate, d_pp), (m_in, m_out, m_gate, m_pp), (v_in, v_out, v_gate, v_pp) = adamw_big(
        [w_in, w_out, w_ple_gate, w_ple_proj], reduced,
        [m_w_in, m_w_out, m_w_ple_gate, m_w_ple_proj], [v_w_in, v_w_out, v_w_ple_gate, v_w_ple_proj], "adamw_big")

    flat = lambda a: a.reshape(-1, a.shape[-1])
    gsum, dsm, msm, vsm = small_finish(
        small_chip, small_chips, [flat(a) for a in (norm_g, ln_v_g, ln_v_b, w_s, b_s, ple_norm_g, final_g[None])],
        [flat(a) for a in (m_norm_g, m_ln_v_g, m_ln_v_b, m_w_s, m_b_s, m_ple_norm_g, m_final_g[None])],
        [flat(a) for a in (v_norm_g, v_ln_v_g, v_ln_v_b, v_w_s, v_b_s, v_ple_norm_g, v_final_g[None])])
    like = [norm_g, ln_v_g, ln_v_b, w_s, b_s, ple_norm_g, final_g]
    gs, ds, m2s, v2s = ([a.reshape(b.shape) for a, b in zip(res, like)] for res in (gsum, dsm, msm, vsm))
    loss = gsum[8][0, 0]
    g_cw_own = lax.dynamic_slice_in_dim(gsum[7].reshape(DEPTH, 3, WIDTH), own * (WIDTH // N_CHIPS), WIDTH // N_CHIPS, axis=2)
    rows2 = lambda a: a.reshape(DEPTH * 3, WIDTH // N_CHIPS)
    d_cw, m_cw, v_cw = adamw_rows(rows2(cw_rows), rows2(g_cw_own), rows2(jnp.transpose(m_conv_w, (0, 2, 1))),
                                  rows2(jnp.transpose(v_conv_w, (0, 2, 1))))
    back = lambda a: jnp.transpose(a.reshape(DEPTH, 3, WIDTH // N_CHIPS), (0, 2, 1))
    g_conv = jnp.transpose(g_cw_own, (0, 2, 1))

    def ordered(sm, cw_v, w_in_v, w_out_v, gate_v, pp_v):
        return [sm[0], w_in_v, sm[1], sm[2], sm[3], sm[4], cw_v, w_out_v, sm[5], gate_v, pp_v, sm[6]]

    grads = ordered(gs, g_conv, gw_in, gw_out, gw_gate, gw_pp)
    deltas = ordered(ds, back(d_cw), d_in, d_out, d_gate, d_pp)
    new_m = ordered(m2s, back(m_cw), m_in, m_out, m_gate, m_pp)
    new_v = ordered(v2s, back(v_cw), v_in, v_out, v_gate, v_pp)
    return (loss, dx[None], *grads, *deltas, *new_m, *new_v)
```

```python
import jax
import jax.numpy as jnp
from jax import lax
from jax.experimental import pallas as pl
from jax.experimental.pallas import tpu as pltpu

F32 = jnp.float32
BF16 = jnp.bfloat16

SEQ = 8192
D_MODEL = 1024
WIDTH = 512
PROJ = 7 * WIDTH
N_CHIPS = 4
COL_BLK = PROJ // N_CHIPS
PLE = 256
HEADS = 4
CHUNK = 128
DEPTH = 2
EPS = 1e-6
TILE = 256
N_TILES = SEQ // TILE
FWD_TILE = 512
HALO = 8
VMEM_LIMIT = 60 * 1024 * 1024

ADAM_LR, ADAM_B1, ADAM_B2, ADAM_EPS, ADAM_WD, ADAM_STEP = 0.001, 0.9, 0.999, 1e-08, 0.01, 10
ADAM_C1 = 1.0 - ADAM_B1**ADAM_STEP
ADAM_C2 = 1.0 - ADAM_B2**ADAM_STEP

MESH = pl.DeviceIdType.MESH
ANY = pl.BlockSpec(memory_space=pl.ANY)


def _mm(a, b):
    return lax.dot_general(a, b, (((1,), (0,)), ((), ())), preferred_element_type=F32)


def _mm_nt(a, b):
    return lax.dot_general(a, b, (((1,), (1,)), ((), ())), preferred_element_type=F32)


def _mm_rows(a, w_ref):
    blk = w_ref.shape[1]
    acc = _mm(a[:, 0:blk], w_ref[0])
    for k in range(1, N_CHIPS):
        acc = acc + _mm(a[:, k * blk : (k + 1) * blk], w_ref[k])
    return acc


def _ple_proj(p_tile, wpp_ref):
    pb = p_tile.astype(BF16)
    return jnp.concatenate([_mm(pb, wpp_ref[k]) for k in range(N_CHIPS)], axis=-1)


def _mm_nt_rows(a, w_ref):
    return jnp.concatenate([_mm_nt(a, w_ref[k]) for k in range(N_CHIPS)], axis=-1)


def _load_side_by_side(w_hbm, w_vmem, sems):
    copies = [pltpu.make_async_copy(w_hbm.at[k], w_vmem.at[:, pl.ds(k * COL_BLK, COL_BLK)], sems.at[k])
              for k in range(N_CHIPS)]
    for cp in copies:
        cp.start()
    for cp in copies:
        cp.wait()


def _sigmoid(z):
    return 1.0 / (1.0 + jnp.exp(-z))


def _rms_stats(x):
    r = lax.rsqrt(jnp.mean(x * x, axis=-1, keepdims=True) + EPS)
    return r, x * r


def _rms_bwd(dyg, xh, r):
    return r * (dyg - xh * jnp.mean(dyg * xh, axis=-1, keepdims=True))


def _ln_stats(v):
    mu = jnp.mean(v, axis=-1, keepdims=True)
    vc = v - mu
    rs = lax.rsqrt(jnp.mean(vc * vc, axis=-1, keepdims=True) + EPS)
    return rs, vc * rs


def _mesh_pos():
    x, y, c = lax.axis_index("x"), lax.axis_index("y"), lax.axis_index("c")
    return x, y, c, 2 * x + y


def _peer(x, y, c, r):
    return ((1 - x) if (r >> 1) else x, (1 - y) if (r & 1) else y, c)


def _full(shape):
    return pl.BlockSpec(shape, lambda *_: (0,) * len(shape))


def _const(shape, pos):
    return pl.BlockSpec((None,) * len(pos) + tuple(shape), lambda *_: tuple(pos) + (0,) * len(shape))


class Job:
    def __init__(self, ins, out_shapes, sems, start, middle, finish, aliases=None):
        self.ins, self.out_shapes, self.sems = list(ins), list(out_shapes), list(sems)
        self.start, self.middle, self.finish = start, middle, finish
        self.aliases = aliases or {}


def run_job(job, name):
    ni, no = len(job.ins), len(job.out_shapes)

    def body(*refs):
        parts = (refs[:ni], refs[ni : ni + no], refs[ni + no :])
        job.start(*parts)
        if job.middle is not None:
            job.middle(*parts)
        job.finish(*parts)

    return pl.pallas_call(
        body, name=name, out_shape=job.out_shapes, in_specs=[ANY] * ni, out_specs=[ANY] * no, scratch_shapes=job.sems,
        input_output_aliases=job.aliases, compiler_params=pltpu.CompilerParams(has_side_effects=True),
    )(*job.ins)


def _call(body, *, name, grid, in_specs, out_specs, out_shape, args, scratch=(), job=None, mid=None):
    params = pltpu.CompilerParams(dimension_semantics=("arbitrary",) * len(grid), vmem_limit_bytes=VMEM_LIMIT,
                                  has_side_effects=job is not None)
    n_in, n_out, n_sc = len(in_specs), len(out_specs), len(scratch)
    if job is None:
        outs = pl.pallas_call(body, name=name, grid=grid, in_specs=in_specs, out_specs=out_specs, out_shape=out_shape,
                              scratch_shapes=list(scratch), compiler_params=params)(*args)
        return list(outs), []
    ji, jo = len(job.ins), len(job.out_shapes)
    assert not job.aliases and (job.middle is None or len(grid) == 1)

    def wrapped(*refs):
        ins, jin = refs[:n_in], refs[n_in : n_in + ji]
        o0 = n_in + ji
        outs, jout = refs[o0 : o0 + n_out], refs[o0 + n_out : o0 + n_out + jo]
        s0 = o0 + n_out + jo
        sc, jsem = refs[s0 : s0 + n_sc], refs[s0 + n_sc :]
        step = pl.program_id(0)
        for d in range(1, len(grid)):
            step = step * grid[d] + pl.program_id(d)
        n_steps = 1
        for g in grid:
            n_steps *= g

        @pl.when(step == 0)
        def _():
            job.start(jin, jout, jsem)

        if job.middle is not None:
            @pl.when(step == mid)
            def _():
                job.middle(jin, jout, jsem)

        body(*ins, *outs, *sc)

        @pl.when(step == n_steps - 1)
        def _():
            job.finish(jin, jout, jsem)

    outs = pl.pallas_call(
        wrapped, name=name, grid=grid, in_specs=list(in_specs) + [ANY] * ji, out_specs=list(out_specs) + [ANY] * jo,
        out_shape=list(out_shape) + job.out_shapes, scratch_shapes=list(scratch) + job.sems, compiler_params=params,
    )(*args, *job.ins)
    return list(outs[:n_out]), list(outs[n_out:])


def gather_job(items, relay=False):
    n = len(items)
    relayed = [relay and layer is not None for _, layer in items]
    sends = [(a, q, r) for a in range(n) for r in ((1, 2) if relayed[a] else (1, 2, 3))
             for q in (((r - 1), 2 - r) if relayed[a] else (0,))]
    hops = [(a, q) for a in range(n) if relayed[a] for q in (0, 1)]
    lands = sends + [(a, q, 3) for a, q in hops]

    def region(a, ref, lead, h, q):
        arr, layer = items[a]
        if layer is None:
            return ref.at[lead + (h,)]
        rh = arr.shape[1] // 2
        if relayed[a]:
            return ref.at[lead + (pl.ds(h * rh + q * (rh // 2), rh // 2),)]
        return ref.at[lead + (pl.ds(h * rh, rh),)]

    def copies(src, dst, scratch):
        stage, (in_sem, out_sem, ici_s, ici_r, hop_s, hop_r, fwd_s, fwd_r) = scratch[:n], scratch[n:]
        x, y, c, own = _mesh_pos()
        sib = (x, y, 1 - c)
        local = []
        for a in range(n):
            layer = items[a][1]
            local.append((pltpu.make_async_copy(src[a] if layer is None else src[a].at[layer], stage[a], in_sem.at[a]),
                          pltpu.make_async_copy(stage[a], dst[a].at[own], out_sem.at[a])))
        ici, hop, hop_in, fwd, got = {}, {}, {}, {}, {}
        for k, (a, q, r) in enumerate(sends):
            layer = items[a][1]
            ici[a, q, r] = pltpu.make_async_remote_copy(
                src_ref=region(a, src[a], () if layer is None else (layer,), c, q), dst_ref=region(a, dst[a], (own,), c, q),
                send_sem=ici_s.at[k], recv_sem=ici_r.at[k], device_id=_peer(x, y, c, r), device_id_type=MESH)
        for k, (a, q) in enumerate(hops):
            there = region(a, dst[a], (own ^ (q + 1),), c, q)
            hop[a, q] = pltpu.make_async_remote_copy(src_ref=there, dst_ref=there, send_sem=hop_s.at[k], recv_sem=hop_r.at[k],
                                                     device_id=_peer(x, y, c, 2 - q), device_id_type=MESH)
            here = region(a, dst[a], (own ^ 3,), c, q)
            hop_in[a, q] = pltpu.make_async_remote_copy(src_ref=here, dst_ref=here, send_sem=hop_s.at[k], recv_sem=hop_r.at[k],
                                                        device_id=_peer(x, y, c, 2 - q), device_id_type=MESH)
        for k, (a, q, r) in enumerate(lands):
            mine, theirs = region(a, dst[a], (own ^ r,), c, q), region(a, dst[a], (own ^ r,), 1 - c, q)
            fwd[a, q, r] = pltpu.make_async_remote_copy(src_ref=mine, dst_ref=mine, send_sem=fwd_s.at[k], recv_sem=fwd_r.at[k],
                                                        device_id=sib, device_id_type=MESH)
            got[a, q, r] = pltpu.make_async_remote_copy(src_ref=theirs, dst_ref=theirs, send_sem=fwd_s.at[k],
                                                        recv_sem=fwd_r.at[k], device_id=sib, device_id_type=MESH)
        return local, ici, hop, hop_in, fwd, got

    def start(src, dst, sems):
        local, ici, _, _, _, _ = copies(src, dst, sems)
        for slot in sends:
            ici[slot].start()
        for cp_in, _ in local:
            cp_in.start()
        for cp_in, cp_out in local:
            cp_in.wait()
            cp_out.start()

    def middle(src, dst, sems):
        _, ici, hop, hop_in, fwd, _ = copies(src, dst, sems)
        for a, q, r in sends:
            ici[a, q, r].wait_recv()
            if relayed[a] and q == r - 1:
                hop[a, q].start()
            fwd[a, q, r].start()
        for a, q in hops:
            hop_in[a, q].wait_recv()
            fwd[a, q, 3].start()

    def finish(src, dst, sems):
        local, ici, hop, _, fwd, got = copies(src, dst, sems)
        for slot in lands:
            got[slot].wait_recv()
        for slot in sends:
            ici[slot].wait_send()
        for slot in hops:
            hop[slot].wait_send()
        for slot in lands:
            fwd[slot].wait_send()
        for _, cp_out in local:
            cp_out.wait()

    out_shapes = [jax.ShapeDtypeStruct((N_CHIPS,) + (arr.shape if layer is None else arr.shape[1:]), arr.dtype)
                  for arr, layer in items]
    stage = [pltpu.VMEM(arr.shape if layer is None else arr.shape[1:], arr.dtype) for arr, layer in items]
    sems = ([pltpu.SemaphoreType.DMA((n,))] * 2 + [pltpu.SemaphoreType.DMA((len(sends),))] * 2
            + [pltpu.SemaphoreType.DMA((max(len(hops), 1),))] * 2 + [pltpu.SemaphoreType.DMA((len(lands),))] * 2)
    return Job([arr for arr, _ in items], out_shapes, stage + sems, start, middle, finish)


def _mixer_fwd(proj_ref, lg, lb, ws_ref, bsb_ref, cw_ref, mix_ref, xcbuf, halo_xc):
    u = proj_ref[:, 0 * WIDTH : 1 * WIDTH]
    v = proj_ref[:, 1 * WIDTH : 2 * WIDTH]
    za = proj_ref[:, 2 * WIDTH : 3 * WIDTH]
    h = proj_ref[:, 3 * WIDTH : 4 * WIDTH]
    gb = proj_ref[:, 4 * WIDTH : 5 * WIDTH]
    gc = proj_ref[:, 5 * WIDTH : 6 * WIDTH]
    zb = proj_ref[:, 6 * WIDTH : 7 * WIDTH]
    rs, vhat = _ln_stats(v)
    vl = vhat * lg + lb
    vlb = vl.astype(BF16)
    tile = proj_ref.shape[0]
    for j in range(tile // CHUNK):
        rows = slice(j * CHUNK, (j + 1) * CHUNK)
        for hd in range(HEADS):
            cols = slice(hd * CHUNK, (hd + 1) * CHUNK)
            mix_ref[rows, cols] = _mm(ws_ref[hd], vlb[rows, cols]) + bsb_ref[:, cols]
    mixed = mix_ref[...]
    siga = _sigmoid(za)
    sigb = _sigmoid(zb)
    xc = gc * h
    xcbuf[0:HALO, :] = halo_xc
    xcbuf[HALO : HALO + tile, :] = xc
    y = cw_ref[0:1, :] * xcbuf[HALO - 2 : HALO - 2 + tile, :] + cw_ref[1:2, :] * xcbuf[HALO - 1 : HALO - 1 + tile, :]
    y = y + cw_ref[2:3, :] * xc
    return dict(u=u, za=za, h=h, gb=gb, gc=gc, zb=zb, rs=rs, vhat=vhat, vlb=vlb, mixed=mixed, siga=siga, sigb=sigb,
                xc=xc, y=y)


def fwd_layer(layer, x, p, wts, small, job=None, head=None):
    win, wout, wg, wpp = wts
    ng, lg, lb, ws, bsb, cw, pg = small
    n_head = 0 if head is None else 2

    def body(*refs):
        (x_ref, p_ref, win_ref, wout_ref, wg_ref, wpp_ref, ng_ref, lg_ref, lb_ref, ws_ref, bsb_ref, cw_ref,
         pg_ref) = refs[:13]
        head_in = refs[13 : 13 + n_head]
        proj_ref, x2_ref, gate_ref, x3_ref, hnT_ref, catT_ref, hn2T_ref, pT_ref = refs[13 + n_head : 21 + n_head]
        head_out = refs[21 + n_head : 21 + 2 * n_head]
        mix_ref, xcbuf, carry, wcat, wsem = refs[21 + 2 * n_head :]
        i = pl.program_id(0)

        @pl.when(i == 0)
        def _():
            _load_side_by_side(win_ref, wcat, wsem)
            carry[...] = jnp.zeros_like(carry)
            for ref in head_out:
                ref[...] = jnp.zeros_like(ref)

        xv = x_ref[...]
        _, xh = _rms_stats(xv)
        hn = xh * ng_ref[...]
        hnT_ref[...] = hn.T.astype(BF16)
        proj_ref[...] = _mm(hn.astype(BF16), wcat[...])
        m = _mixer_fwd(proj_ref, lg_ref[...], lb_ref[...], ws_ref, bsb_ref, cw_ref, mix_ref, xcbuf, carry[...])
        carry[...] = m["xc"][FWD_TILE - HALO : FWD_TILE, :]
        out_a = (m["u"] * m["mixed"]) * (m["za"] * m["siga"])
        out_b = (m["gb"] * m["y"]) * (m["zb"] * m["sigb"])
        cat = jnp.concatenate([out_a, out_b], axis=-1)
        catT_ref[...] = cat.T.astype(BF16)
        x2 = xv + _mm_rows(cat.astype(BF16), wout_ref)
        x2_ref[...] = x2
        _, xh2 = _rms_stats(x2)
        hn2 = xh2 * pg_ref[...]
        hn2T_ref[...] = hn2.T.astype(BF16)
        gate = _sigmoid(_mm_rows(hn2.astype(BF16), wg_ref))
        gate_ref[...] = gate
        pv = p_ref[...]
        pT_ref[...] = pv.T.astype(BF16)
        x3 = x2 + gate * _ple_proj(pv, wpp_ref)
        if head is None:
            x3_ref[...] = x3
        else:
            t_ref, gf_ref = head_in
            loss_ref, gg_ref = head_out
            r3, xh3 = _rms_stats(x3)
            gf = gf_ref[...]
            err = xh3 * gf - t_ref[...]
            loss_ref[...] += (0.5 / D_MODEL) * jnp.sum(err * err).reshape(1, 1)
            dy = err * (1.0 / D_MODEL)
            gg_ref[...] += jnp.sum(dy * xh3, axis=0, keepdims=True)
            x3_ref[...] = _rms_bwd(dy * gf, xh3, r3)

    tok = lambda w: pl.BlockSpec((FWD_TILE, w), lambda i: (i, 0))
    tokT = lambda w: pl.BlockSpec((None, w, FWD_TILE), lambda i: (i, 0, 0))
    once = lambda a: pl.BlockSpec(a.shape, lambda i: (0,) * a.ndim, pipeline_mode=pl.Buffered(1))
    f32 = lambda w: jax.ShapeDtypeStruct((SEQ, w), F32)
    bfT = lambda w: jax.ShapeDtypeStruct((SEQ // FWD_TILE, w, FWD_TILE), BF16)
    head_specs = [] if head is None else [tok(D_MODEL), _full((1, D_MODEL))]
    head_outs = [] if head is None else [_full((1, 1)), _full((1, D_MODEL))]
    head_shapes = [] if head is None else [jax.ShapeDtypeStruct((1, 1), F32), jax.ShapeDtypeStruct((1, D_MODEL), F32)]
    return _call(
        body, name=f"fwd_layer{layer}", grid=(SEQ // FWD_TILE,),
        in_specs=[tok(D_MODEL), pl.BlockSpec((None, None, FWD_TILE, PLE), lambda i: (layer, 0, i, 0)),
                  ANY, once(wout), once(wg), once(wpp),
                  _const((1, D_MODEL), (layer,)), _const((1, WIDTH), (layer,)), _const((1, WIDTH), (layer,)),
                  _const((HEADS, CHUNK, CHUNK), (layer,)), _const((CHUNK, WIDTH), (layer,)), _const((3, WIDTH), (layer,)),
                  _const((1, D_MODEL), (layer,))] + head_specs,
        out_specs=[tok(PROJ), tok(D_MODEL), tok(D_MODEL), tok(D_MODEL),
                   tokT(D_MODEL), tokT(D_MODEL), tokT(D_MODEL), tokT(PLE)] + head_outs,
        out_shape=[f32(PROJ), f32(D_MODEL), f32(D_MODEL), f32(D_MODEL),
                   bfT(D_MODEL), bfT(D_MODEL), bfT(D_MODEL), bfT(PLE)] + head_shapes,
        scratch=[pltpu.VMEM((FWD_TILE, WIDTH), F32), pltpu.VMEM((HALO + FWD_TILE, WIDTH), F32), pltpu.VMEM((HALO, WIDTH), F32),
                 pltpu.VMEM((D_MODEL, PROJ), BF16), pltpu.SemaphoreType.DMA((N_CHIPS,))],
        args=(x, p, win, wout, wg, wpp, ng, lg, lb, ws, bsb, cw, pg) + (() if head is None else tuple(head)),
        job=job, mid=SEQ // FWD_TILE // 2)


def bwd_layer(layer, dx3, x2, gate, p, x, proj, wts, small, wsT, tril):
    win, wout, wg, wpp = wts
    ng, lg, lb, ws, bsb, cw, pg = small

    def body(dx3_ref, x2_ref, gate_ref, p_ref, x_ref, proj_ref, halo_ref, win_ref, wout_ref, wg_ref, wpp_ref, ng_ref, lg_ref,
             lb_ref, ws_ref, wsT_ref, bsb_ref, cw_ref, pg_ref, tril_ref,
             dx_ref, dproj_ref, dpp_ref, dgl_ref, dx2b_ref, gn_ref, glg_ref, glb_ref, gws_ref, gbs_ref, gcw_ref, gpg_ref,
             mix_ref, xcbuf, dycbuf, dvl_ref, bs_acc, wcat, wsem):
        i = pl.program_id(0)

        @pl.when(i == 0)
        def _():
            _load_side_by_side(win_ref, wcat, wsem)
            for ref in (gn_ref, glg_ref, glb_ref, gws_ref, gcw_ref, gpg_ref, bs_acc):
                ref[...] = jnp.zeros_like(ref)
            dycbuf[TILE : TILE + HALO, :] = jnp.zeros((HALO, WIDTH), F32)

        dx3v = dx3_ref[...]
        gate_v = gate_ref[...]
        dpp_ref[...] = (dx3v * gate_v).astype(BF16)
        dgl = ((dx3v * _ple_proj(p_ref[...], wpp_ref)) * gate_v * (1.0 - gate_v)).astype(BF16)
        dgl_ref[...] = dgl
        dhn2 = _mm_nt_rows(dgl, wg_ref)
        r2, xh2 = _rms_stats(x2_ref[...])
        gpg_ref[...] += jnp.sum(dhn2 * xh2, axis=0, keepdims=True)
        dx2 = dx3v + _rms_bwd(dhn2 * pg_ref[...], xh2, r2)
        dx2b = dx2.astype(BF16)
        dx2b_ref[...] = dx2b
        dcat = _mm_nt_rows(dx2b, wout_ref)
        lgv = lg_ref[...]
        halo_xc = halo_ref[:, 5 * WIDTH : 6 * WIDTH] * halo_ref[:, 3 * WIDTH : 4 * WIDTH]
        halo_xc = jnp.where(i == N_TILES - 1, 0.0, halo_xc)
        m = _mixer_fwd(proj_ref, lgv, lb_ref[...], ws_ref, bsb_ref, cw_ref, mix_ref, xcbuf, halo_xc)
        u, za, h, gb, gc, zb = m["u"], m["za"], m["h"], m["gb"], m["gc"], m["zb"]
        mixed, siga, sigb, xc, y = m["mixed"], m["siga"], m["sigb"], m["xc"], m["y"]
        doa = dcat[:, 0:WIDTH]
        dob = dcat[:, WIDTH : 2 * WIDTH]
        sa = za * siga
        sb = zb * sigb
        doa_sa = doa * sa
        dproj_ref[:, 0 * WIDTH : 1 * WIDTH] = (doa_sa * mixed).astype(BF16)
        dmixed = doa_sa * u
        dza = (doa * (u * mixed)) * (siga * (1.0 + za * (1.0 - siga)))
        dproj_ref[:, 2 * WIDTH : 3 * WIDTH] = dza.astype(BF16)
        dob_sb = dob * sb
        dproj_ref[:, 4 * WIDTH : 5 * WIDTH] = (dob_sb * y).astype(BF16)
        dyc = dob_sb * gb
        dzb = (dob * (gb * y)) * (sigb * (1.0 + zb * (1.0 - sigb)))
        dproj_ref[:, 6 * WIDTH : 7 * WIDTH] = dzb.astype(BF16)
        dycbuf[0:TILE, :] = dyc
        dyc1 = dycbuf[1 : 1 + TILE, :]
        dyc2 = dycbuf[2 : 2 + TILE, :]
        dxc = cw_ref[2:3, :] * dyc + cw_ref[1:2, :] * dyc1 + cw_ref[0:1, :] * dyc2
        gcw_ref[0:1, :] += jnp.sum(xc * dyc2, axis=0, keepdims=True)
        gcw_ref[1:2, :] += jnp.sum(xc * dyc1, axis=0, keepdims=True)
        gcw_ref[2:3, :] += jnp.sum(xc * dyc, axis=0, keepdims=True)
        dycbuf[TILE : TILE + HALO, :] = dyc[0:HALO, :]
        dproj_ref[:, 5 * WIDTH : 6 * WIDTH] = (dxc * h).astype(BF16)
        dproj_ref[:, 3 * WIDTH : 4 * WIDTH] = (dxc * gc).astype(BF16)
        dmb = dmixed.astype(BF16)
        vlb = m["vlb"]
        bsum = jnp.zeros((CHUNK, WIDTH), F32)
        for j in range(TILE // CHUNK):
            rows = slice(j * CHUNK, (j + 1) * CHUNK)
            bsum = bsum + dmixed[rows, :]
            for hd in range(HEADS):
                cols = slice(hd * CHUNK, (hd + 1) * CHUNK)
                gws_ref[hd] += _mm_nt(dmb[rows, cols], vlb[rows, cols])
                dvl_ref[rows, cols] = _mm(wsT_ref[hd], dmb[rows, cols])
        bs_acc[...] += bsum
        dvl = dvl_ref[...]
        vhat = m["vhat"]
        glb_ref[...] += jnp.sum(dvl, axis=0, keepdims=True)
        glg_ref[...] += jnp.sum(dvl * vhat, axis=0, keepdims=True)
        dvh = dvl * lgv
        dv = m["rs"] * (dvh - jnp.mean(dvh, axis=-1, keepdims=True) - vhat * jnp.mean(dvh * vhat, axis=-1, keepdims=True))
        dproj_ref[:, 1 * WIDTH : 2 * WIDTH] = dv.astype(BF16)
        dhn = _mm_nt(dproj_ref[...], wcat[...])
        r1, xh = _rms_stats(x_ref[...])
        gn_ref[...] += jnp.sum(dhn * xh, axis=0, keepdims=True)
        dx_ref[...] = dx2 + _rms_bwd(dhn * ng_ref[...], xh, r1)

        @pl.when(i == N_TILES - 1)
        def _():
            for hd in range(HEADS):
                gws_ref[hd] = gws_ref[hd] * tril_ref[...]
                gbs_ref[hd : hd + 1, :] = jnp.sum(bs_acc[:, hd * CHUNK : (hd + 1) * CHUNK].T, axis=0, keepdims=True)

    rev = lambda w: pl.BlockSpec((TILE, w), lambda i: (N_TILES - 1 - i, 0))
    halo = pl.BlockSpec((HALO, PROJ), lambda i: (jnp.maximum((N_TILES - 1 - i) * (TILE // HALO) - 1, 0), 0))
    once = lambda a: pl.BlockSpec(a.shape, lambda i: (0,) * a.ndim, pipeline_mode=pl.Buffered(1))
    vec = lambda w: jax.ShapeDtypeStruct((1, w), F32)
    b16 = lambda w: jax.ShapeDtypeStruct((SEQ, w), BF16)
    outs, _ = _call(
        body, name=f"bwd_layer{layer}", grid=(N_TILES,),
        in_specs=[rev(D_MODEL), rev(D_MODEL), rev(D_MODEL),
                  pl.BlockSpec((None, None, TILE, PLE), lambda i: (layer, 0, N_TILES - 1 - i, 0)),
                  rev(D_MODEL), rev(PROJ), halo, ANY, once(wout), once(wg), once(wpp),
                  _const((1, D_MODEL), (layer,)), _const((1, WIDTH), (layer,)), _const((1, WIDTH), (layer,)),
                  _const((HEADS, CHUNK, CHUNK), (layer,)), _const((HEADS, CHUNK, CHUNK), (layer,)),
                  _const((CHUNK, WIDTH), (layer,)), _const((3, WIDTH), (layer,)), _const((1, D_MODEL), (layer,)),
                  _full((CHUNK, CHUNK))],
        out_specs=[rev(D_MODEL), rev(PROJ), rev(D_MODEL), rev(D_MODEL), rev(D_MODEL),
                   _full((1, D_MODEL)), _full((1, WIDTH)), _full((1, WIDTH)), _full((HEADS, CHUNK, CHUNK)),
                   _full((HEADS, CHUNK)), _full((3, WIDTH)), _full((1, D_MODEL))],
        out_shape=[jax.ShapeDtypeStruct((SEQ, D_MODEL), F32), b16(PROJ), b16(D_MODEL), b16(D_MODEL), b16(D_MODEL),
                   vec(D_MODEL), vec(WIDTH), vec(WIDTH), jax.ShapeDtypeStruct((HEADS, CHUNK, CHUNK), F32),
                   jax.ShapeDtypeStruct((HEADS, CHUNK), F32), jax.ShapeDtypeStruct((3, WIDTH), F32), vec(D_MODEL)],
        scratch=[pltpu.VMEM((TILE, WIDTH), F32), pltpu.VMEM((HALO + TILE, WIDTH), F32),
                 pltpu.VMEM((TILE + HALO, WIDTH), F32), pltpu.VMEM((TILE, WIDTH), F32), pltpu.VMEM((CHUNK, WIDTH), F32),
                 pltpu.VMEM((D_MODEL, PROJ), BF16), pltpu.SemaphoreType.DMA((N_CHIPS,))],
        args=(dx3, x2, gate, p, x, proj, proj, win, wout, wg, wpp, ng, lg, lb, ws, wsT, bsb, cw, pg, tril))
    return outs


def wgrad(pairs, n_tiles, col_blocked, tk, name, job=None):
    n = len(pairs)
    m_dim, n_dim, at = pairs[0][0].shape[1], pairs[0][1].shape[1], pairs[0][0].shape[2]
    tn = n_dim // n_tiles
    cb, mb = n_dim // N_CHIPS, m_dim // N_CHIPS
    per_tile = N_CHIPS // n_tiles
    assert col_blocked or n_tiles == 1

    def body(*refs):
        for a in range(n):
            a_ref, b_ref, o_ref = refs[2 * a], refs[2 * a + 1], refs[2 * n + a]

            @pl.when(pl.program_id(1) == 0)
            def _():
                o_ref[...] = jnp.zeros_like(o_ref)

            prod = _mm(jnp.concatenate([a_ref[t] for t in range(tk // at)], axis=1), b_ref[...])
            if col_blocked:
                for q in range(per_tile):
                    o_ref[q] += prod[:, q * cb : (q + 1) * cb]
            else:
                for q in range(N_CHIPS):
                    o_ref[q] += prod[q * mb : (q + 1) * mb, :]

            @pl.when(pl.program_id(1) == SEQ // tk - 1)
            def _():
                refs[3 * n + a][...] = o_ref[...].astype(BF16)

    mode = dict(pipeline_mode=pl.Buffered(1)) if n_tiles == 1 else {}
    if col_blocked:
        shape = (N_CHIPS, m_dim, cb)
        o_spec = pl.BlockSpec((per_tile, m_dim, cb), lambda j, k: (j, 0, 0), **mode)
    else:
        shape = (N_CHIPS, mb, n_dim)
        o_spec = pl.BlockSpec((N_CHIPS, mb, n_dim), lambda j, k: (0, 0, 0), **mode)
    outs, job_outs = _call(
        body, name=name, grid=(n_tiles, SEQ // tk),
        in_specs=[pl.BlockSpec((tk // at, m_dim, at), lambda j, k: (k, 0, 0)),
                  pl.BlockSpec((tk, tn), lambda j, k: (k, j))] * n,
        out_specs=[o_spec] * (2 * n), out_shape=[jax.ShapeDtypeStruct(shape, F32)] * n + [jax.ShapeDtypeStruct(shape, BF16)] * n,
        args=[t for pair in pairs for t in pair], job=job)
    return outs[:n], outs[n:], job_outs


def to_sibling_job(grads):
    n = len(grads)

    def copies(src, dst, sems):
        x, y, c, _ = _mesh_pos()
        out = []
        for a in range(n):
            rh = grads[a].shape[1] // 2
            out.append(pltpu.make_async_remote_copy(
                src_ref=src[a].at[:, pl.ds((1 - c) * rh, rh), :], dst_ref=dst[a], send_sem=sems[0].at[a],
                recv_sem=sems[1].at[a], device_id=(x, y, 1 - c), device_id_type=MESH))
        return out

    def start(src, dst, sems):
        for cp in copies(src, dst, sems):
            cp.start()

    def finish(src, dst, sems):
        for cp in copies(src, dst, sems):
            cp.wait()

    out_shapes = [jax.ShapeDtypeStruct((N_CHIPS, g.shape[1] // 2, g.shape[2]), g.dtype) for g in grads]
    return Job(grads, out_shapes, [pltpu.SemaphoreType.DMA((n,))] * 2, start, None, finish)


def rs_add_sibling(ids, grads, recvd, name):
    n = len(grads)

    def body(ids_ref, *refs):
        g, r = refs[:n], refs[n : 2 * n]
        pb, pf = refs[2 * n : 3 * n], refs[3 * n :]
        k = pl.program_id(1)
        for a in range(n):
            s = g[a][...] + r[a][...].astype(F32)
            pb[a][...] = s.astype(BF16)

            @pl.when(k == ids_ref[1])
            def _():
                pf[a][...] = s

    parts = 2
    in_specs, out_specs, out_shape = [], [], []
    for g in grads:
        rp, cc = g.shape[1] // 2 // parts, g.shape[2]
        in_specs.append(pl.BlockSpec((None, rp, cc), lambda j, k, ids: (k, ids[0] * parts + j, 0)))
    for g in grads:
        rp, cc = g.shape[1] // 2 // parts, g.shape[2]
        in_specs.append(pl.BlockSpec((None, rp, cc), lambda j, k, ids: (k, j, 0)))
        out_specs.append(pl.BlockSpec((None, rp, cc), lambda j, k, ids: (k, j, 0)))
        out_shape.append(jax.ShapeDtypeStruct((N_CHIPS, rp * parts, cc), BF16))
    for g in grads:
        rp, cc = g.shape[1] // 2 // parts, g.shape[2]
        out_specs.append(pl.BlockSpec((rp, cc), lambda j, k, ids: (j, 0)))
        out_shape.append(jax.ShapeDtypeStruct((rp * parts, cc), F32))
    outs = pl.pallas_call(
        body, name=name, out_shape=out_shape,
        grid_spec=pltpu.PrefetchScalarGridSpec(num_scalar_prefetch=1, grid=(parts, N_CHIPS), in_specs=in_specs,
                                               out_specs=out_specs),
        compiler_params=pltpu.CompilerParams(dimension_semantics=("arbitrary", "arbitrary"), vmem_limit_bytes=VMEM_LIMIT),
    )(ids, *grads, *recvd)
    return list(outs[:n]), list(outs[n:])


def to_owners_job(partials):
    n = len(partials)

    def copies(src, dst, sems):
        x, y, c, own = _mesh_pos()
        out = []
        for a in range(n):
            for r in (1, 2, 3):
                out.append(pltpu.make_async_remote_copy(
                    src_ref=src[a].at[own ^ r], dst_ref=dst[a].at[r - 1], send_sem=sems[0].at[3 * a + r - 1],
                    recv_sem=sems[1].at[3 * a + r - 1], device_id=_peer(x, y, c, r), device_id_type=MESH))
        return out

    def start(src, dst, sems):
        for cp in copies(src, dst, sems):
            cp.start()

    def finish(src, dst, sems):
        for cp in copies(src, dst, sems):
            cp.wait()

    out_shapes = [jax.ShapeDtypeStruct((3,) + p.shape[1:], BF16) for p in partials]
    return Job(partials, out_shapes, [pltpu.SemaphoreType.DMA((3 * n,))] * 2, start, None, finish)


def rs_add_owners(layer, ids, own_f32, recvd, prev, name):
    n = len(own_f32)
    nb = 2

    def body(ids_ref, *refs):
        o, r, f = refs[:n], refs[n : 2 * n], refs[-n:]
        for a in range(n):
            f[a][...] = ((o[a][...] + r[a][0].astype(F32)) + r[a][1].astype(F32)) + r[a][2].astype(F32)

    in_specs, out_specs, out_shape = [], [], []
    for o in own_f32:
        in_specs.append(pl.BlockSpec((o.shape[0] // nb, o.shape[1]), lambda j, ids: (j, 0)))
    for o in own_f32:
        in_specs.append(pl.BlockSpec((3, o.shape[0] // nb, o.shape[1]), lambda j, ids: (0, j, 0)))
        out_specs.append(pl.BlockSpec((None, o.shape[0] // nb, o.shape[1]), lambda j, ids: (layer, ids[0] * nb + j, 0)))
        out_shape.append(jax.ShapeDtypeStruct((DEPTH, 2 * o.shape[0], o.shape[1]), F32))
    args, aliases = [ids, *own_f32, *recvd], {}
    if prev is not None:
        in_specs += [ANY] * n
        args += list(prev)
        aliases = {1 + 2 * n + a: a for a in range(n)}
    return pl.pallas_call(
        body, name=name, out_shape=out_shape, input_output_aliases=aliases,
        grid_spec=pltpu.PrefetchScalarGridSpec(num_scalar_prefetch=1, grid=(nb,), in_specs=in_specs, out_specs=out_specs),
        compiler_params=pltpu.CompilerParams(dimension_semantics=("arbitrary",), vmem_limit_bytes=VMEM_LIMIT),
    )(*args)


def exchange_halves_job(layer, full):
    n = len(full)

    def copies(src, dst, sems):
        x, y, c, _ = _mesh_pos()
        out = []
        for a in range(n):
            rh = full[a].shape[1] // 2
            out.append(pltpu.make_async_remote_copy(
                src_ref=src[a].at[layer, pl.ds(c * rh, rh), :], dst_ref=dst[a].at[layer, pl.ds(c * rh, rh), :],
                send_sem=sems[0].at[a], recv_sem=sems[1].at[a], device_id=(x, y, 1 - c), device_id_type=MESH))
        return out

    def start(src, dst, sems):
        for cp in copies(src, dst, sems):
            cp.start()

    def finish(src, dst, sems):
        for cp in copies(src, dst, sems):
            cp.wait()

    out_shapes = [jax.ShapeDtypeStruct(f.shape, F32) for f in full]
    return Job(full, out_shapes, [pltpu.SemaphoreType.DMA((n,))] * 2, start, None, finish, {a: a for a in range(n)})


def _adamw(w, g, m, v):
    m2 = ADAM_B1 * m + (1.0 - ADAM_B1) * g
    v2 = ADAM_B2 * v + (1.0 - ADAM_B2) * (g * g)
    delta = -ADAM_LR * ((m2 / ADAM_C1) / (jnp.sqrt(v2 / ADAM_C2) + ADAM_EPS) + ADAM_WD * w)
    return delta, m2, v2


def adamw_big(ws, gs, ms, vs, name):
    n = len(ws)
    nb = 4

    def body(*refs):
        for a in range(n):
            w, g, m, v = (refs[j * n + a][...] for j in range(4))
            d, m2, v2 = _adamw(w, g, m, v)
            refs[4 * n + a][...] = g
            refs[5 * n + a][...] = d
            refs[6 * n + a][...] = m2
            refs[7 * n + a][...] = v2

    specs = [pl.BlockSpec((None, w.shape[1] // nb, w.shape[2]), lambda l, j: (l, j, 0)) for w in ws]
    shapes = [jax.ShapeDtypeStruct(w.shape, F32) for w in ws]
    outs, _ = _call(body, name=name, grid=(DEPTH, nb), in_specs=specs * 4, out_specs=specs * 4, out_shape=shapes * 4,
                    args=(*ws, *gs, *ms, *vs))
    return outs[:n], outs[n : 2 * n], outs[2 * n : 3 * n], outs[3 * n :]


def small_to_sibling_job(gs):
    n = len(gs)

    def copies(src, dst, sems):
        x, y, c, _ = _mesh_pos()
        return [pltpu.make_async_remote_copy(src_ref=src[a], dst_ref=dst[a], send_sem=sems[0].at[a], recv_sem=sems[1].at[a],
                                             device_id=(x, y, 1 - c), device_id_type=MESH) for a in range(n)]

    def start(*refs):
        for cp in copies(*refs):
            cp.start()

    def finish(*refs):
        for cp in copies(*refs):
            cp.wait()

    return Job(gs, [jax.ShapeDtypeStruct(g.shape, F32) for g in gs], [pltpu.SemaphoreType.DMA((n,))] * 2, start, None, finish)


def small_chip_sum(gs, recvd):
    n = len(gs)

    def body(*refs):
        for a in range(n):
            refs[2 * n + a][...] = refs[a][...] + refs[n + a][...]

    return pl.pallas_call(body, name="small_chip_sum", out_shape=[jax.ShapeDtypeStruct(g.shape, F32) for g in gs])(*gs, *recvd)


def small_to_chips_job(chip_sums):
    n = len(chip_sums)

    def copies(src, dst, sems):
        x, y, c, _ = _mesh_pos()
        return [pltpu.make_async_remote_copy(src_ref=src[a], dst_ref=dst[a].at[r - 1], send_sem=sems[0].at[3 * a + r - 1],
                                             recv_sem=sems[1].at[3 * a + r - 1], device_id=_peer(x, y, c, r),
                                             device_id_type=MESH) for a in range(n) for r in (1, 2, 3)]

    def start(*refs):
        for cp in copies(*refs):
            cp.start()

    def finish(*refs):
        for cp in copies(*refs):
            cp.wait()

    return Job(chip_sums, [jax.ShapeDtypeStruct((3,) + s.shape, F32) for s in chip_sums],
               [pltpu.SemaphoreType.DMA((3 * n,))] * 2, start, None, finish)


def small_finish(chip_sums, from_chips, ws, ms, vs):
    n, n_w = len(chip_sums), len(ws)

    def body(*refs):
        cs, fc = refs[:n], refs[n : 2 * n]
        w, m, v = (refs[2 * n + j * n_w : 2 * n + (j + 1) * n_w] for j in range(3))
        outs = refs[2 * n + 3 * n_w :]
        own = 2 * lax.axis_index("x") + lax.axis_index("y")
        for a in range(n):
            mine = cs[a][...]

            def of_chip(k):
                r = own ^ k
                return jnp.where(r == 0, mine, jnp.where(r == 1, fc[a][0], jnp.where(r == 2, fc[a][1], fc[a][2])))

            tot = ((of_chip(0) + of_chip(1)) + of_chip(2)) + of_chip(3)
            outs[a][...] = tot
            if a < n_w:
                d, m2, v2 = _adamw(w[a][...], tot, m[a][...], v[a][...])
                outs[n + a][...] = d
                outs[n + n_w + a][...] = m2
                outs[n + 2 * n_w + a][...] = v2

    shapes = [jax.ShapeDtypeStruct(s.shape, F32) for s in chip_sums]
    outs = pl.pallas_call(body, name="small_finish", out_shape=shapes + shapes[:n_w] * 3)(
        *chip_sums, *from_chips, *ws, *ms, *vs)
    return outs[:n], outs[n : n + n_w], outs[n + n_w : n + 2 * n_w], outs[n + 2 * n_w :]


def merge_jobs(*jobs):
    def parts(refs, counts):
        out, pos = [], 0
        for cnt in counts:
            out.append(refs[pos : pos + cnt])
            pos += cnt
        return out

    def run(phase):
        def go(ins, outs, sems):
            for job, i, o, s in zip(jobs, parts(ins, [len(j.ins) for j in jobs]),
                                    parts(outs, [len(j.out_shapes) for j in jobs]), parts(sems, [len(j.sems) for j in jobs])):
                getattr(job, phase)(i, o, s)
        return go

    assert all(j.middle is None and not j.aliases for j in jobs)
    return Job([a for j in jobs for a in j.ins], [a for j in jobs for a in j.out_shapes], [a for j in jobs for a in j.sems],
               run("start"), None, run("finish"))


def adamw_rows(w, g, m, v):
    def body(w_ref, g_ref, m_ref, v_ref, d_ref, m2_ref, v2_ref):
        d, m2, v2 = _adamw(w_ref[...], g_ref[...], m_ref[...], v_ref[...])
        d_ref[...] = d
        m2_ref[...] = m2
        v2_ref[...] = v2

    return pl.pallas_call(body, name="adamw_conv", out_shape=[jax.ShapeDtypeStruct(w.shape, F32)] * 3)(w, g, m, v)


def kernel(x, p, norm_g, w_in, ln_v_g, ln_v_b, w_s, b_s, conv_w, w_out, ple_norm_g, w_ple_gate, w_ple_proj, final_g, loss_target, m_norm_g, m_w_in, m_ln_v_g, m_ln_v_b, m_w_s, m_b_s, m_conv_w, m_w_out, m_ple_norm_g, m_w_ple_gate, m_w_ple_proj, m_final_g, v_norm_g, v_w_in, v_ln_v_g, v_ln_v_b, v_w_s, v_b_s, v_conv_w, v_w_out, v_ple_norm_g, v_w_ple_gate, v_w_ple_proj, v_final_g):
    cx, cy, cc = lax.axis_index("x"), lax.axis_index("y"), lax.axis_index("c")
    own = 2 * cx + cy
    ids = jnp.stack([cc, own]).astype(jnp.int32)

    cw_rows = jnp.transpose(conv_w, (0, 2, 1))
    shards = [w_in.astype(BF16), w_out.astype(BF16), w_ple_gate.astype(BF16), w_ple_proj.astype(BF16)]
    *wts0, g_cw = run_job(gather_job([(s, 0) for s in shards] + [(cw_rows, None)], relay=True), "gather_weights0")
    cw_full = jnp.transpose(g_cw, (1, 2, 0, 3)).reshape(DEPTH, 3, WIDTH)
    tril = jnp.tril(jnp.ones((CHUNK, CHUNK), F32))
    ws_masked = w_s * tril[None, None]
    ws_b = ws_masked.astype(BF16)
    wsT_b = jnp.swapaxes(ws_masked, 2, 3).astype(BF16)
    bsb = jnp.repeat(jnp.swapaxes(b_s, 1, 2), CHUNK, axis=2)
    small = (norm_g[:, None, :], ln_v_g[:, None, :], ln_v_b[:, None, :], ws_b, bsb, cw_full, ple_norm_g[:, None, :])

    saved0, wts1 = fwd_layer(0, x[0], p, wts0, small, job=gather_job([(s, 1) for s in shards]))
    saved1, _ = fwd_layer(1, saved0[3], p, wts1, small, head=(loss_target[0], final_g[None, :]))
    saved, xs = [saved0[:8], saved1[:8]], [x[0], saved0[3]]
    dx, loss_part, g_final = saved1[3], saved1[8], saved1[9]

    def chip_sums(arrs, arrs_bf16, tag):
        recvd = run_job(to_sibling_job(arrs_bf16), f"rs_to_sibling{tag}")
        return rs_add_sibling(ids, arrs, recvd, f"rs_add_sibling{tag}")

    small_g = [None] * DEPTH
    proj, x2, gate, _, hnT, catT, hn2T, pT = saved[1]
    dx, dproj_b, dpp_b, dgl_b, dx2_b, *g_sm = bwd_layer(1, dx, x2, gate, p, xs[1], proj, wts1, small, wsT_b, tril)
    small_g[1] = g_sm[:5] + [g_sm[6], g_sm[5]]
    g1_in, b1_in, _ = wgrad([(hnT, dproj_b)], 1, True, 1024, "wgrad_in1")
    g1_og, b1_og, _ = wgrad([(catT, dx2_b), (hn2T, dgl_b)], 1, False, 1024, "wgrad_outgate1")
    g1_pp, b1_pp, _ = wgrad([(pT, dpp_b)], 1, True, 2048, "wgrad_pp1")
    part1 = chip_sums(g1_in + g1_og + g1_pp, b1_in + b1_og + b1_pp, "1")
    proj, x2, gate, _, hnT, catT, hn2T, pT = saved[0]
    dx, dproj_b, dpp_b, dgl_b, dx2_b, *g_sm = bwd_layer(0, dx, x2, gate, p, xs[0], proj, wts0, small, wsT_b, tril)
    small_g[0] = g_sm[:5] + [g_sm[6], g_sm[5]]
    def both(j):
        return jnp.concatenate([small_g[0][j], small_g[1][j]], axis=0)

    g_small = [both(0), both(1), both(2), both(3).reshape(DEPTH * HEADS * CHUNK, CHUNK), both(4), both(5), g_final, both(6),
               jnp.broadcast_to(loss_part, (1, 128))]
    g0_in, b0_in, landed = wgrad([(hnT, dproj_b)], 1, True, 1024, "wgrad_in0",
                                 job=merge_jobs(to_owners_job(part1[0]), small_to_sibling_job(g_small)))
    from1, small_sib = landed[:4], landed[4:]
    full = rs_add_owners(1, ids, part1[1], from1, None, "rs_add_owners1")
    full = run_job(exchange_halves_job(1, full), "rs_exchange_halves1")
    small_chip = small_chip_sum(g_small, small_sib)
    part0b = chip_sums(g0_in, b0_in, "0b")
    g0_og, b0_og, from0b = wgrad([(catT, dx2_b), (hn2T, dgl_b)], 1, False, 1024, "wgrad_outgate0",
                                 job=to_owners_job(part0b[0]))
    full_b = rs_add_owners(0, ids, part0b[1], from0b, full[:1], "rs_add_owners0b")
    g0_pp, b0_pp, small_chips = wgrad([(pT, dpp_b)], 1, True, 2048, "wgrad_pp0", job=small_to_chips_job(small_chip))
    part0a = chip_sums(g0_og + g0_pp, b0_og + b0_pp, "0a")
    from0a = run_job(to_owners_job(part0a[0]), "rs_to_owners0a")
    full_a = rs_add_owners(0, ids, part0a[1], from0a, full[1:], "rs_add_owners0a")
    reduced = run_job(exchange_halves_job(0, list(full_b) + list(full_a)), "rs_exchange_halves0")
    (gw_in, gw_out, gw_gate, gw_pp), (d_in, d_out, d_gate, d_pp), (m_in, m_out, m_gate, m_pp), (v_in, v_out, v_gate, v_pp) = adamw_big(
        [w_in, w_out, w_ple_gate, w_ple_proj], reduced,
        [m_w_in, m_w_out, m_w_ple_gate, m_w_ple_proj], [v_w_in, v_w_out, v_w_ple_gate, v_w_ple_proj], "adamw_big")

    flat = lambda a: a.reshape(-1, a.shape[-1])
    gsum, dsm, msm, vsm = small_finish(
        small_chip, small_chips, [flat(a) for a in (norm_g, ln_v_g, ln_v_b, w_s, b_s, ple_norm_g, final_g[None])],
        [flat(a) for a in (m_norm_g, m_ln_v_g, m_ln_v_b, m_w_s, m_b_s, m_ple_norm_g, m_final_g[None])],
        [flat(a) for a in (v_norm_g, v_ln_v_g, v_ln_v_b, v_w_s, v_b_s, v_ple_norm_g, v_final_g[None])])
    like = [norm_g, ln_v_g, ln_v_b, w_s, b_s, ple_norm_g, final_g]
    gs, ds, m2s, v2s = ([a.reshape(b.shape) for a, b in zip(res, like)] for res in (gsum, dsm, msm, vsm))
    loss = gsum[8][0, 0]
    g_cw_own = lax.dynamic_slice_in_dim(gsum[7].reshape(DEPTH, 3, WIDTH), own * (WIDTH // N_CHIPS), WIDTH // N_CHIPS, axis=2)
    rows2 = lambda a: a.reshape(DEPTH * 3, WIDTH // N_CHIPS)
    d_cw, m_cw, v_cw = adamw_rows(rows2(cw_rows), rows2(g_cw_own), rows2(jnp.transpose(m_conv_w, (0, 2, 1))),
                                  rows2(jnp.transpose(v_conv_w, (0, 2, 1))))
    back = lambda a: jnp.transpose(a.reshape(DEPTH, 3, WIDTH // N_CHIPS), (0, 2, 1))
    g_conv = jnp.transpose(g_cw_own, (0, 2, 1))

    def ordered(sm, cw_v, w_in_v, w_out_v, gate_v, pp_v):
        return [sm[0], w_in_v, sm[1], sm[2], sm[3], sm[4], cw_v, w_out_v, sm[5], gate_v, pp_v, sm[6]]

    grads = ordered(gs, g_conv, gw_in, gw_out, gw_gate, gw_pp)
    deltas = ordered(ds, back(d_cw), d_in, d_out, d_gate, d_pp)
    new_m = ordered(m2s, back(m_cw), m_in, m_out, m_gate, m_pp)
    new_v = ordered(v2s, back(v_cw), v_in, v_out, v_gate, v_pp)
    return (loss, dx[None], *grads, *deltas, *new_m, *new_v)
```
